```python
import math
import jax, jax.numpy as jnp
from jax import lax
import numpy as np

D_MODEL = 1024
BATCH = 2
SEQ = 16384
DEPTH = 2

HG_HEADS = 4
HG_KEY_DIM = 128
HG_VAL_DIM = 64
HG_KEY_WIDTH = HG_HEADS * HG_KEY_DIM
HG_WIDTH = HG_HEADS * HG_VAL_DIM
HG_CHUNK = 64
MIN_FORGET = 1e-20

MLA_HEADS = 4
MLA_Q_RANK = 256
MLA_KV_RANK = 128
MLA_NOPE = 128
MLA_ROPE = 64
MLA_V = 128
MLA_WIDTH = MLA_HEADS * MLA_V
ROPE_THETA = 10000.0
ATTN_BLOCK = 128
MASK_VALUE = -1e30

POOL_GROUPS = 4
POOL_WINDOWS = (2, 4, 8, 16)
POOL_WIDTH = 256
POOL_GROUP_DIM = POOL_WIDTH // POOL_GROUPS

D_MIX = HG_WIDTH + MLA_WIDTH + POOL_WIDTH

IN_SIZES = (HG_KEY_WIDTH, HG_KEY_WIDTH, HG_WIDTH, HG_WIDTH,
            MLA_Q_RANK, MLA_KV_RANK, MLA_ROPE, POOL_WIDTH)
D_IN = sum(IN_SIZES)
IN_SPLITS = tuple(int(v) for v in np.cumsum(IN_SIZES)[:-1])

D_FF = 2560
N_EXPERTS = 8
TOP_K = 2
D_FF_EXPERT = 3584
N_DENSE = (DEPTH + 1) // 2
N_MOE = DEPTH // 2
EPS = 1e-6

kernel_name = "hybrid_hgrn2_mla_pool_moe"


def rms_norm(x, g):
    xf = x.astype(jnp.float32)
    y = xf * lax.rsqrt(jnp.mean(xf * xf, axis=-1, keepdims=True) + EPS)
    return (y * g.astype(jnp.float32)).astype(x.dtype)


def swiglu(x, wg, wu, wd):
    return (jax.nn.silu(x @ wg) * (x @ wu)) @ wd


def rope_tables(seq, dim):
    pos = jnp.arange(seq, dtype=jnp.float32)
    inv_freq = 1.0 / (ROPE_THETA ** (jnp.arange(0, dim, 2, dtype=jnp.float32) / dim))
    ang = pos[:, None] * inv_freq[None, :]
    return jnp.cos(ang), jnp.sin(ang)


def apply_rope(x, cos, sin):
    xf = x.astype(jnp.float32)
    x1, x2 = jnp.split(xf, 2, axis=-1)
    out = jnp.concatenate([x1 * cos - x2 * sin, x1 * sin + x2 * cos], axis=-1)
    return out.astype(x.dtype)


def hgrn2_mixer(q, f, i, g, lb, out_norm):
    B, S, _ = q.shape
    dt = q.dtype
    nc = S // HG_CHUNK
    lbf = lb.astype(jnp.float32)
    z = f.astype(jnp.float32)
    forget = lbf + (1.0 - lbf) * jax.nn.sigmoid(z)
    log_f = jnp.log(jnp.maximum(forget, MIN_FORGET))
    k = (1.0 - lbf) * jax.nn.sigmoid(-z)
    qf = jax.nn.silu(q.astype(jnp.float32))
    vf = i.astype(jnp.float32)

    def to_chunks(t, d):
        return t.reshape(B, nc, HG_CHUNK, HG_HEADS, d).transpose(1, 0, 3, 2, 4)

    qc = to_chunks(qf, HG_KEY_DIM)
    kc = to_chunks(k, HG_KEY_DIM)
    gc = to_chunks(log_f, HG_KEY_DIM)
    vc = to_chunks(vf, HG_VAL_DIM)
    tri = jnp.tril(jnp.ones((HG_CHUNK, HG_CHUNK), dtype=bool))

    def step(state, inp):
        qq, kk, vv, lg = inp
        b = jnp.cumsum(lg, axis=2)
        o_inter = jnp.einsum('bhtk,bhkv->bhtv', qq * jnp.exp(b), state)
        diff = b[:, :, :, None, :] - b[:, :, None, :, :]
        decay = jnp.exp(jnp.where(tri[:, :, None], diff, MASK_VALUE))
        attn = jnp.einsum('bhtk,bhsk,bhtsk->bhts', qq, kk, decay)
        o_intra = jnp.einsum('bhts,bhsv->bhtv', attn, vv)
        b_last = b[:, :, -1:, :]
        new_state = (jnp.exp(b_last[:, :, 0, :])[..., None] * state
                     + jnp.einsum('bhsk,bhsv->bhkv', kk * jnp.exp(b_last - b), vv))
        return new_state, o_inter + o_intra

    s0 = jnp.zeros((B, HG_HEADS, HG_KEY_DIM, HG_VAL_DIM), jnp.float32)
    _, o = lax.scan(step, s0, (qc, kc, vc, gc))
    o = o.transpose(1, 0, 3, 2, 4).reshape(B, S, HG_HEADS, HG_VAL_DIM)
    o = rms_norm(o, out_norm.reshape(HG_HEADS, HG_VAL_DIM))
    o = o.reshape(B, S, HG_WIDTH) * jax.nn.silu(g.astype(jnp.float32))
    return o.astype(dt)


def mla_mixer(c_q, c_kv, k_pe, q_norm, w_uq, kv_norm, w_ukv, cos, sin):
    B, S, _ = c_q.shape
    q = (rms_norm(c_q, q_norm) @ w_uq).reshape(B, S, MLA_HEADS, MLA_NOPE + MLA_ROPE)
    q_nope, q_pe = q[..., :MLA_NOPE], q[..., MLA_NOPE:]
    q_pe = apply_rope(q_pe, cos[:, None, :], sin[:, None, :])
    kv = (rms_norm(c_kv, kv_norm) @ w_ukv).reshape(B, S, MLA_HEADS, MLA_NOPE + MLA_V)
    k_nope, v = kv[..., :MLA_NOPE], kv[..., MLA_NOPE:]
    k_pe = apply_rope(k_pe, cos, sin)
    scale = (MLA_NOPE + MLA_ROPE) ** -0.5
    nb = S // ATTN_BLOCK
    qn_b = q_nope.reshape(B, nb, ATTN_BLOCK, MLA_HEADS, MLA_NOPE).transpose(1, 0, 2, 3, 4)
    qp_b = q_pe.reshape(B, nb, ATTN_BLOCK, MLA_HEADS, MLA_ROPE).transpose(1, 0, 2, 3, 4)
    kpos = jnp.arange(S)

    def attend(args):
        qn, qp, blk = args
        s = (jnp.einsum('bqhd,bkhd->bhqk', qn, k_nope)
             + jnp.einsum('bqhd,bkd->bhqk', qp, k_pe)).astype(jnp.float32) * scale
        qpos = blk * ATTN_BLOCK + jnp.arange(ATTN_BLOCK)
        s = jnp.where(qpos[:, None] >= kpos[None, :], s, MASK_VALUE)
        p = jax.nn.softmax(s, axis=-1).astype(v.dtype)
        return jnp.einsum('bhqk,bkhd->bqhd', p, v)

    out = lax.map(attend, (qn_b, qp_b, jnp.arange(nb)))
    return out.transpose(1, 0, 2, 3, 4).reshape(B, S, MLA_WIDTH)


def pool_mixer(xp, w_pool, scale):
    B, S, _ = xp.shape
    xf = xp.astype(jnp.float32)
    cs = jnp.cumsum(xf, axis=1)
    t = jnp.arange(S)
    outs = []
    for gi, w in enumerate(POOL_WINDOWS):
        c = cs[..., gi * POOL_GROUP_DIM:(gi + 1) * POOL_GROUP_DIM]
        lag = jnp.pad(c, ((0, 0), (w, 0), (0, 0)))[:, :S]
        cnt = jnp.minimum(t + 1, w).astype(jnp.float32)[None, :, None]
        outs.append((c - lag) / cnt)
    pooled = (jnp.concatenate(outs, axis=-1) - xf).astype(xp.dtype)
    y = jnp.einsum('bsgc,gcd->bsgd', pooled.reshape(B, S, POOL_GROUPS, POOL_GROUP_DIM), w_pool)
    return y.reshape(B, S, POOL_WIDTH) * scale


def moe_ffn(h, router, wg, wu, wd):
    B, S, D = h.shape
    tok = h.reshape(B * S, D)
    logits = (tok @ router).astype(jnp.float32)
    top_v, top_i = lax.top_k(logits, TOP_K)
    gates = jax.nn.softmax(top_v, axis=-1)
    combine = jnp.sum(jax.nn.one_hot(top_i, N_EXPERTS, dtype=jnp.float32) * gates[..., None], axis=1)
    out = jnp.zeros((B * S, D), jnp.float32)
    for e in range(N_EXPERTS):
        y = swiglu(tok, wg[e], wu[e], wd[e]).astype(jnp.float32)
        out = out + combine[:, e:e + 1] * y
    return out.reshape(B, S, D).astype(h.dtype)


def setup_inputs(seed: int = 0) -> dict:
    key = jax.random.key(seed)
    ks = jax.random.split(key, 24)
    f32 = jnp.float32

    def nrm(k, shape, fan_in):
        return jax.random.normal(k, shape, f32) * (fan_in ** -0.5)

    def gain(k, shape):
        return 1.0 + 0.1 * jax.random.normal(k, shape, f32)

    return {
        "x": jax.random.normal(ks[0], (BATCH, SEQ, D_MODEL), f32),
        "attn_norm": gain(ks[1], (DEPTH, D_MODEL)),
        "w_in": nrm(ks[2], (DEPTH, D_MODEL, D_IN), D_MODEL),
        "hgrn_lower_bounds": jax.random.normal(ks[3], (DEPTH, HG_KEY_WIDTH), f32),
        "hgrn_out_norm": gain(ks[4], (DEPTH, HG_WIDTH)),
        "mla_q_norm": gain(ks[5], (DEPTH, MLA_Q_RANK)),
        "mla_w_uq": nrm(ks[6], (DEPTH, MLA_Q_RANK, MLA_HEADS * (MLA_NOPE + MLA_ROPE)), MLA_Q_RANK),
        "mla_kv_norm": gain(ks[7], (DEPTH, MLA_KV_RANK)),
        "mla_w_ukv": nrm(ks[8], (DEPTH, MLA_KV_RANK, MLA_HEADS * (MLA_NOPE + MLA_V)), MLA_KV_RANK),
        "pool_w": nrm(ks[9], (DEPTH, POOL_GROUPS, POOL_GROUP_DIM, POOL_GROUP_DIM), POOL_GROUP_DIM),
        "pool_scale": gain(ks[10], (DEPTH, POOL_WIDTH)),
        "w_o": nrm(ks[11], (DEPTH, D_MIX, D_MODEL), D_MIX),
        "ffn_norm": gain(ks[12], (DEPTH, D_MODEL)),
        "dense_w_gate": nrm(ks[13], (N_DENSE, D_MODEL, D_FF), D_MODEL),
        "dense_w_up": nrm(ks[14], (N_DENSE, D_MODEL, D_FF), D_MODEL),
        "dense_w_down": nrm(ks[15], (N_DENSE, D_FF, D_MODEL), D_FF),
        "moe_router": nrm(ks[16], (N_MOE, D_MODEL, N_EXPERTS), D_MODEL),
        "moe_w_gate": nrm(ks[17], (N_MOE, N_EXPERTS, D_MODEL, D_FF_EXPERT), D_MODEL),
        "moe_w_up": nrm(ks[18], (N_MOE, N_EXPERTS, D_MODEL, D_FF_EXPERT), D_MODEL),
        "moe_w_down": nrm(ks[19], (N_MOE, N_EXPERTS, D_FF_EXPERT, D_MODEL), D_FF_EXPERT),
        "final_norm": gain(ks[20], (D_MODEL,)),
    }


def reference(x, attn_norm, w_in, hgrn_lower_bounds, hgrn_out_norm, mla_q_norm, mla_w_uq,
              mla_kv_norm, mla_w_ukv, pool_w, pool_scale, w_o, ffn_norm, dense_w_gate,
              dense_w_up, dense_w_down, moe_router, moe_w_gate, moe_w_up, moe_w_down,
              final_norm):
    S = x.shape[1]
    cos, sin = rope_tables(S, MLA_ROPE)
    p_lb = jax.nn.softmax(hgrn_lower_bounds.astype(jnp.float32), axis=0)
    lbs = jnp.cumsum(p_lb, axis=0) - p_lb[0:1]

    for l in range(DEPTH):
        h = rms_norm(x, attn_norm[l])
        proj = h @ w_in[l]
        hq, hf, hi, hg, c_q, c_kv, k_pe, xp = jnp.split(proj, IN_SPLITS, axis=-1)
        o_a = hgrn2_mixer(hq, hf, hi, hg, lbs[l], hgrn_out_norm[l])
        o_b = mla_mixer(c_q, c_kv, k_pe, mla_q_norm[l], mla_w_uq[l], mla_kv_norm[l],
                        mla_w_ukv[l], cos, sin)
        o_c = pool_mixer(xp, pool_w[l], pool_scale[l])
        x = x + jnp.concatenate([o_a, o_b, o_c], axis=-1) @ w_o[l]
        h = rms_norm(x, ffn_norm[l])
        if l % 2 == 0:
            j = l // 2
            x = x + swiglu(h, dense_w_gate[j], dense_w_up[j], dense_w_down[j])
        else:
            j = l // 2
            x = x + moe_ffn(h, moe_router[j], moe_w_gate[j], moe_w_up[j], moe_w_down[j])
    return rms_norm(x, final_norm)
```

```python
import functools
import math

import jax
import jax.numpy as jnp
import numpy as np
from jax import lax
from jax.experimental import pallas as pl
from jax.experimental.pallas import tpu as pltpu

F32 = jnp.float32
BF16 = jnp.bfloat16

HG_HEADS = 4
HG_KEY_DIM = 128
HG_VAL_DIM = 64
HG_KEY_WIDTH = HG_HEADS * HG_KEY_DIM
HG_WIDTH = HG_HEADS * HG_VAL_DIM
MIN_FORGET = 1e-20
MLA_HEADS = 4
MLA_Q_RANK = 256
MLA_KV_RANK = 128
MLA_NOPE = 128
MLA_ROPE = 64
MLA_V = 128
MLA_WIDTH = MLA_HEADS * MLA_V
ROPE_THETA = 10000.0
MASK_VALUE = -1e30
POOL_GROUPS = 4
POOL_WINDOWS = (2, 4, 8, 16)
POOL_WIDTH = 256
POOL_GROUP_DIM = POOL_WIDTH // POOL_GROUPS
N_EXPERTS = 8
EPS = 1e-6

LANES = 128
SUBLANES = 8
QK_PAD = 256

TOKEN_TILE = 512
HGRN_CHUNK = 256
ATTN_BQ = 512
ATTN_BK = 512
FF_CHUNK = 512
MOE_TILE = 1024
ROUTE_TILE = 512
COMBINE_TILE = 256
POOL_HALO = 16

_C_HG = 0
_C_CQ = 2 * HG_KEY_WIDTH + 2 * HG_WIDTH
_C_CKV = _C_CQ + MLA_Q_RANK
_C_KPE = _C_CKV + MLA_KV_RANK
_C_KPES = _C_KPE + LANES
_C_XP = _C_KPES + LANES
_C_END = _C_XP + POOL_WIDTH
_Q_HEAD_COLS = 3 * LANES


def _rms(x, g):
    return x * lax.rsqrt(jnp.mean(x * x, axis=-1, keepdims=True) + EPS) * g


def _dot(a, b):
    return jnp.dot(a, b, preferred_element_type=F32)


def _dot_nt(a, b):
    return lax.dot_general(a, b, (((1,), (1,)), ((), ())), preferred_element_type=F32)


def _dot_tn(a, b):
    return lax.dot_general(a, b, (((0,), (0,)), ((), ())), preferred_element_type=F32)


def _const_spec(shape):
    nd = len(shape)
    return pl.BlockSpec(shape, lambda *_: (0,) * nd)


def _in_proj_kernel(x_ref, g_ref, w_ref, qn_ref, wuq_ref, kvn_ref, wukv_ref, cpad_ref, spad_ref,
                    hg_ref, xp_ref, q_ref, k_ref, v_ref):
    h = _rms(x_ref[0], g_ref[...]).astype(BF16)
    hg_ref[0] = _dot(h, w_ref[:, _C_HG:_C_CQ])
    xp_ref[0] = _dot(h, w_ref[:, _C_XP:_C_END])
    cpad = cpad_ref[...]
    spad = spad_ref[...]
    scale = (MLA_NOPE + MLA_ROPE) ** -0.5

    cq = _dot(h, w_ref[:, _C_CQ:_C_CKV])
    cqn = _rms(cq, qn_ref[...]).astype(BF16)
    for hd in range(MLA_HEADS):
        qh = _dot(cqn, wuq_ref[:, hd * _Q_HEAD_COLS:(hd + 1) * _Q_HEAD_COLS])
        q_ref[0, hd, :, 0:LANES] = (qh[:, 0:LANES] * scale).astype(BF16)
        pe = qh[:, LANES:2 * LANES] * cpad + qh[:, 2 * LANES:3 * LANES] * spad
        q_ref[0, hd, :, LANES:QK_PAD] = (pe * scale).astype(BF16)

    ckv = _dot(h, w_ref[:, _C_CKV:_C_KPE])
    ckvn = _rms(ckv, kvn_ref[...]).astype(BF16)
    kpe = (_dot(h, w_ref[:, _C_KPE:_C_KPES]) * cpad + _dot(h, w_ref[:, _C_KPES:_C_XP]) * spad).astype(BF16)
    for hd in range(MLA_HEADS):
        kv = _dot(ckvn, wukv_ref[:, hd * 2 * LANES:(hd + 1) * 2 * LANES])
        k_ref[0, hd, :, 0:LANES] = kv[:, 0:LANES].astype(BF16)
        k_ref[0, hd, :, LANES:QK_PAD] = kpe
        v_ref[0, hd] = kv[:, LANES:2 * LANES].astype(BF16)


def _in_proj(x, g, w_ext, qn, wuq_ext, kvn, wukv, cpad, spad):
    B, S, D = x.shape
    tm = min(TOKEN_TILE, S)
    grid = (B, S // tm)
    n_hg = _C_CQ
    out_shape = (
        jax.ShapeDtypeStruct((B, S, n_hg), F32),
        jax.ShapeDtypeStruct((B, S, POOL_WIDTH), F32),
        jax.ShapeDtypeStruct((B, MLA_HEADS, S, QK_PAD), BF16),
        jax.ShapeDtypeStruct((B, MLA_HEADS, S, QK_PAD), BF16),
        jax.ShapeDtypeStruct((B, MLA_HEADS, S, MLA_V), BF16),
    )
    return pl.pallas_call(
        _in_proj_kernel,
        grid=grid,
        in_specs=[
            pl.BlockSpec((1, tm, D), lambda b, i: (b, i, 0)),
            _const_spec((1, D)),
            _const_spec(w_ext.shape),
            _const_spec((1, MLA_Q_RANK)),
            _const_spec(wuq_ext.shape),
            _const_spec((1, MLA_KV_RANK)),
            _const_spec(wukv.shape),
            pl.BlockSpec((tm, LANES), lambda b, i: (i, 0)),
            pl.BlockSpec((tm, LANES), lambda b, i: (i, 0)),
        ],
        out_specs=(
            pl.BlockSpec((1, tm, n_hg), lambda b, i: (b, i, 0)),
            pl.BlockSpec((1, tm, POOL_WIDTH), lambda b, i: (b, i, 0)),
            pl.BlockSpec((1, MLA_HEADS, tm, QK_PAD), lambda b, i: (b, 0, i, 0)),
            pl.BlockSpec((1, MLA_HEADS, tm, QK_PAD), lambda b, i: (b, 0, i, 0)),
            pl.BlockSpec((1, MLA_HEADS, tm, MLA_V), lambda b, i: (b, 0, i, 0)),
        ),
        out_shape=out_shape,
        compiler_params=pltpu.CompilerParams(dimension_semantics=("arbitrary", "arbitrary")),
        name="in_proj",
    )(x, g, w_ext, qn, wuq_ext, kvn, wukv, cpad, spad)


def _split3(x):
    hi = x.astype(BF16)
    r1 = x - hi.astype(F32)
    mid = r1.astype(BF16)
    lo = (r1 - mid.astype(F32)).astype(BF16)
    return hi, mid, lo


def _hgrn_kernel(q_ref, f_ref, i_ref, g_ref, lb_ref, on_ref, o_ref, st_ref):
    C = q_ref.shape[1]
    KW = HG_KEY_WIDTH

    @pl.when(pl.program_id(1) == 0)
    def _():
        st_ref[...] = jnp.zeros_like(st_ref)

    lb = lb_ref[...]
    z = f_ref[0]
    forget = lb + (1.0 - lb) * jax.nn.sigmoid(z)
    lg = jnp.log(jnp.maximum(forget, MIN_FORGET))
    kk = (1.0 - lb) * jax.nn.sigmoid(-z)
    qq = jax.nn.silu(q_ref[0])
    vv = i_ref[0]

    row = lax.broadcasted_iota(jnp.int32, (C, C), 0)
    col = lax.broadcasted_iota(jnp.int32, (C, C), 1)
    tril = (col <= row).astype(BF16)
    b = sum(_dot(tril, part) for part in _split3(lg))

    qts, kts, masks = [], [], []
    sub = lax.broadcasted_iota(jnp.int32, (C, KW), 0)
    half = C // 2
    while half >= 4:
        blk = 2 * half
        b3 = b.reshape(C // blk, blk, KW)
        ref_row = jnp.broadcast_to(b3[:, half - 1:half, :], b3.shape).reshape(C, KW)
        e = jnp.exp(-jnp.abs(b - ref_row))
        upper = (sub & (blk - 1)) >= half
        qts.append(jnp.where(upper, qq * e, 0.0).astype(BF16))
        kts.append(jnp.where(upper, 0.0, kk * e).astype(BF16))
        shift = int(math.log2(blk))
        masks.append((row >> shift) == (col >> shift))
        half //= 2
    b3 = b.reshape(C // 8, 8, KW)
    mid_lo = 0.5 * (b3[:, 0:1, :] + b3[:, 3:4, :])
    mid_hi = 0.5 * (b3[:, 4:5, :] + b3[:, 7:8, :])
    sub8 = lax.broadcasted_iota(jnp.int32, b3.shape, 1)
    mid = jnp.where(sub8 < 4, mid_lo, mid_hi).reshape(C, KW)
    qts.append((qq * jnp.exp(b - mid)).astype(BF16))
    kts.append((kk * jnp.exp(mid - b)).astype(BF16))
    masks.append(((row >> 2) == (col >> 2)) & (col <= row))

    lane_v = lax.broadcasted_iota(jnp.int32, (C, HG_WIDTH), 1)
    o = _dot_nt((qq * jnp.exp(b)).astype(BF16), st_ref[...].astype(BF16))
    for hd in range(HG_HEADS):
        ks = slice(hd * HG_KEY_DIM, (hd + 1) * HG_KEY_DIM)
        a = jnp.zeros((C, C), F32)
        for qt, kt, m in zip(qts, kts, masks):
            a = a + jnp.where(m, _dot_nt(qt[:, ks], kt[:, ks]), 0.0)
        v_h = jnp.where((lane_v >> 6) == hd, vv, 0.0).astype(BF16)
        o = o + _dot(a.astype(BF16), v_h)

    b_last = b[C - 1:C, :]
    khat = (kk * jnp.exp(b_last - b)).astype(BF16)
    st_row = lax.broadcasted_iota(jnp.int32, (HG_WIDTH, KW), 0)
    st_col = lax.broadcasted_iota(jnp.int32, (HG_WIDTH, KW), 1)
    new_st = st_ref[...] * jnp.exp(b_last) + _dot_tn(vv.astype(BF16), khat)
    st_ref[...] = jnp.where((st_row >> 6) == (st_col >> 7), new_st, 0.0)

    gi = lax.broadcasted_iota(jnp.int32, (HG_WIDTH, HG_WIDTH), 0)
    gj = lax.broadcasted_iota(jnp.int32, (HG_WIDTH, HG_WIDTH), 1)
    grp = ((gi >> 6) == (gj >> 6)).astype(BF16)
    ssq = sum(_dot(part, grp) for part in _split3(o * o))
    on = o * lax.rsqrt(ssq * (1.0 / HG_VAL_DIM) + EPS) * on_ref[...]
    o_ref[0] = (on * jax.nn.silu(g_ref[0])).astype(o_ref.dtype)


def _hgrn(hg, lb, out_norm):
    B, S, _ = hg.shape
    C = min(HGRN_CHUNK, S)
    kb = HG_KEY_WIDTH // HG_KEY_WIDTH
    return pl.pallas_call(
        _hgrn_kernel,
        grid=(B, S // C),
        in_specs=[
            pl.BlockSpec((1, C, HG_KEY_WIDTH), lambda b, c: (b, c, 0)),
            pl.BlockSpec((1, C, HG_KEY_WIDTH), lambda b, c: (b, c, kb)),
            pl.BlockSpec((1, C, HG_WIDTH), lambda b, c: (b, c, 2 * HG_KEY_WIDTH // HG_WIDTH)),
            pl.BlockSpec((1, C, HG_WIDTH), lambda b, c: (b, c, 2 * HG_KEY_WIDTH // HG_WIDTH + 1)),
            _const_spec((1, HG_KEY_WIDTH)),
            _const_spec((1, HG_WIDTH)),
        ],
        out_specs=pl.BlockSpec((1, C, HG_WIDTH), lambda b, c: (b, c, 0)),
        out_shape=jax.ShapeDtypeStruct((B, S, HG_WIDTH), BF16),
        scratch_shapes=[pltpu.VMEM((HG_WIDTH, HG_KEY_WIDTH), F32)],
        compiler_params=pltpu.CompilerParams(dimension_semantics=("arbitrary", "arbitrary")),
        name="hgrn2",
    )(hg, hg, hg, hg, lb, out_norm)


def _attn_kernel(q_ref, k_ref, v_ref, o_ref):
    bq = q_ref.shape[2]
    bk = bq
    i = pl.program_id(2)
    q = q_ref[0, 0]

    def step(j, carry, masked):
        m, l, acc = carry
        start = pl.multiple_of(j * bk, bk)
        kb = k_ref[0, 0, pl.ds(start, bk), :]
        vb = v_ref[0, 0, pl.ds(start, bk), :]
        s = _dot_nt(q, kb)
        if masked:
            r = lax.broadcasted_iota(jnp.int32, (bq, bk), 0)
            c = lax.broadcasted_iota(jnp.int32, (bq, bk), 1)
            s = jnp.where(c <= r, s, MASK_VALUE)
        m_new = jnp.maximum(m, jnp.max(s, axis=-1, keepdims=True))
        p = jnp.exp(s - m_new)
        alpha = jnp.exp(m - m_new)
        l = alpha * l + jnp.sum(p, axis=-1, keepdims=True)
        acc = alpha * acc + _dot(p.astype(BF16), vb)
        return m_new, l, acc

    init = (jnp.full((bq, 1), -jnp.inf, F32), jnp.zeros((bq, 1), F32), jnp.zeros((bq, MLA_V), F32))
    carry = lax.fori_loop(0, i, lambda j, c: step(j, c, False), init)
    m, l, acc = step(i, carry, True)
    o_ref[0] = (acc / l).astype(o_ref.dtype)


def _attention(q, k, v):
    B, H, S, _ = q.shape
    bq = min(ATTN_BQ, S)
    return pl.pallas_call(
        _attn_kernel,
        grid=(B, H, S // bq),
        in_specs=[
            pl.BlockSpec((1, 1, bq, QK_PAD), lambda b, h, i: (b, h, i, 0)),
            pl.BlockSpec((1, 1, S, QK_PAD), lambda b, h, i: (b, h, 0, 0)),
            pl.BlockSpec((1, 1, S, MLA_V), lambda b, h, i: (b, h, 0, 0)),
        ],
        out_specs=pl.BlockSpec((1, bq, MLA_V), lambda b, h, i: (b, i, h)),
        out_shape=jax.ShapeDtypeStruct((B, S, H * MLA_V), BF16),
        compiler_params=pltpu.CompilerParams(dimension_semantics=("arbitrary", "arbitrary", "arbitrary")),
        name="mla_attention",
    )(q, k, v)


def _mix_out_kernel(x_ref, oa_ref, ob_ref, xp_ref, wpool_ref, pscale_ref, wo_ref, fn_ref,
                    xo_ref, h_ref, halo_ref):
    tm = xp_ref.shape[1]
    i = pl.program_id(1)

    @pl.when(i == 0)
    def _():
        halo_ref[...] = jnp.zeros_like(halo_ref)

    xp = xp_ref[0]
    xx = jnp.concatenate([halo_ref[...], xp], axis=0)
    halo_ref[...] = xp[tm - POOL_HALO:, :]

    w2 = xx[1:, :] + xx[:-1, :]
    w4 = w2[2:, :] + w2[:-2, :]
    w8 = w4[4:, :] + w4[:-4, :]
    w16 = w8[8:, :] + w8[:-8, :]
    sums = (w2[POOL_HALO - 1:, :], w4[POOL_HALO - 3:, :], w8[POOL_HALO - 7:, :], w16[POOL_HALO - 15:, :])
    t = i * tm + lax.broadcasted_iota(jnp.int32, (tm, POOL_WIDTH), 0)
    lane = lax.broadcasted_iota(jnp.int32, (tm, POOL_WIDTH), 1)
    pooled = jnp.zeros((tm, POOL_WIDTH), F32)
    for gi, w in enumerate(POOL_WINDOWS):
        cnt = jnp.minimum(t + 1, w).astype(F32)
        pooled = jnp.where((lane >> 6) == gi, sums[gi] / cnt, pooled)
    pooled = pooled - xp
    oc = _dot(pooled.astype(BF16), wpool_ref[...]) * pscale_ref[...]

    y = _dot(oa_ref[0], wo_ref[0:HG_WIDTH, :])
    y = y + _dot(ob_ref[0], wo_ref[HG_WIDTH:HG_WIDTH + MLA_WIDTH, :])
    y = y + _dot(oc.astype(BF16), wo_ref[HG_WIDTH + MLA_WIDTH:, :])
    xn = x_ref[0] + y
    xo_ref[0] = xn
    h_ref[0] = _rms(xn, fn_ref[...]).astype(h_ref.dtype)


def _mix_out(x, oa, ob, xp, wpool_bd, pscale, wo, fnorm, h_dtype):
    B, S, D = x.shape
    tm = min(TOKEN_TILE, S)
    tok = lambda w: pl.BlockSpec((1, tm, w), lambda b, i: (b, i, 0))
    return pl.pallas_call(
        _mix_out_kernel,
        grid=(B, S // tm),
        in_specs=[tok(D), tok(HG_WIDTH), tok(MLA_WIDTH), tok(POOL_WIDTH),
                  _const_spec(wpool_bd.shape), _const_spec((1, POOL_WIDTH)), _const_spec(wo.shape),
                  _const_spec((1, D))],
        out_specs=(tok(D), tok(D)),
        out_shape=(jax.ShapeDtypeStruct((B, S, D), F32), jax.ShapeDtypeStruct((B, S, D), h_dtype)),
        scratch_shapes=[pltpu.VMEM((POOL_HALO, POOL_WIDTH), F32)],
        compiler_params=pltpu.CompilerParams(dimension_semantics=("arbitrary", "arbitrary")),
        name="mix_out",
    )(x, oa, ob, xp, wpool_bd, pscale, wo, fnorm)


def _dense_ffn_kernel(x_ref, h_ref, wg_ref, wu_ref, wd_ref, o_ref):
    h = h_ref[...]
    acc = x_ref[...]
    dff = wg_ref.shape[1]
    for c0 in range(0, dff, FF_CHUNK):
        g = _dot(h, wg_ref[:, c0:c0 + FF_CHUNK])
        u = _dot(h, wu_ref[:, c0:c0 + FF_CHUNK])
        acc = acc + _dot((jax.nn.silu(g) * u).astype(BF16), wd_ref[c0:c0 + FF_CHUNK, :])
    o_ref[...] = acc


def _dense_ffn(x2d, h2d, wg, wu, wd):
    T, D = x2d.shape
    tm = min(TOKEN_TILE, T)
    tok = pl.BlockSpec((tm, D), lambda i: (i, 0))
    return pl.pallas_call(
        _dense_ffn_kernel,
        grid=(T // tm,),
        in_specs=[tok, tok, _const_spec(wg.shape), _const_spec(wu.shape), _const_spec(wd.shape)],
        out_specs=tok,
        out_shape=jax.ShapeDtypeStruct((T, D), F32),
        compiler_params=pltpu.CompilerParams(dimension_semantics=("arbitrary",)),
        name="dense_ffn",
    )(x2d, h2d, wg, wu, wd)


def _router_kernel(h_ref, rt_ref, meta_ref, gate_ref, cnt_ref, run_ref):
    tm = h_ref.shape[0]
    E = N_EXPERTS

    @pl.when(pl.program_id(0) == 0)
    def _():
        run_ref[...] = jnp.zeros_like(run_ref)

    logits = lax.dot_general(rt_ref[...], h_ref[...], (((1,), (1,)), ((), ())),
                             precision=lax.Precision.HIGHEST, preferred_element_type=F32)
    eid = lax.broadcasted_iota(jnp.int32, (E, tm), 0)
    m1 = jnp.max(logits, axis=0, keepdims=True)
    i1 = jnp.min(jnp.where(logits == m1, eid, E), axis=0, keepdims=True)
    rest = jnp.where(eid == i1, -jnp.inf, logits)
    m2 = jnp.max(rest, axis=0, keepdims=True)
    i2 = jnp.min(jnp.where(rest == m2, eid, E), axis=0, keepdims=True)
    e2 = jnp.exp(m2 - m1)
    g1 = 1.0 / (1.0 + e2)
    g2 = e2 / (1.0 + e2)

    sel = ((eid == i1) | (eid == i2))
    r = lax.broadcasted_iota(jnp.int32, (tm, tm), 0)
    c = lax.broadcasted_iota(jnp.int32, (tm, tm), 1)
    before = (r < c).astype(BF16)
    excl = _dot(sel.astype(BF16), before) + run_ref[:, 0:1]
    rank1 = jnp.sum(jnp.where(eid == i1, excl, 0.0), axis=0, keepdims=True).astype(jnp.int32)
    rank2 = jnp.sum(jnp.where(eid == i2, excl, 0.0), axis=0, keepdims=True).astype(jnp.int32)
    run_ref[...] = run_ref[...] + jnp.sum(sel.astype(F32), axis=1, keepdims=True)
    cnt_ref[...] = run_ref[...].astype(jnp.int32)

    zi = jnp.zeros((1, tm), jnp.int32)
    meta_ref[0] = jnp.concatenate([i1, i2, rank1, rank2, zi, zi, zi, zi], axis=0)
    zf = jnp.zeros((1, tm), F32)
    gate_ref[0] = jnp.concatenate([g1, g2, zf, zf, zf, zf, zf, zf], axis=0)


def _router(h2d, router_t):
    T, D = h2d.shape
    tm = min(ROUTE_TILE, T)
    nt = T // tm
    return pl.pallas_call(
        _router_kernel,
        grid=(nt,),
        in_specs=[pl.BlockSpec((tm, D), lambda i: (i, 0)), _const_spec(router_t.shape)],
        out_specs=(pl.BlockSpec((1, SUBLANES, tm), lambda i: (i, 0, 0)),
                   pl.BlockSpec((1, SUBLANES, tm), lambda i: (i, 0, 0)),
                   _const_spec((N_EXPERTS, LANES))),
        out_shape=(jax.ShapeDtypeStruct((nt, SUBLANES, tm), jnp.int32),
                   jax.ShapeDtypeStruct((nt, SUBLANES, tm), F32),
                   jax.ShapeDtypeStruct((N_EXPERTS, LANES), jnp.int32)),
        scratch_shapes=[pltpu.VMEM((N_EXPERTS, LANES), F32)],
        compiler_params=pltpu.CompilerParams(dimension_semantics=("arbitrary",)),
        name="moe_router",
    )(h2d, router_t)


def _scatter_kernel(start_ref, meta_ref, h_hbm, xs_in_hbm, xs_hbm, sem):
    del xs_in_hbm
    tm = meta_ref.shape[2]
    base = pl.program_id(0) * tm

    def row_copy(src_row, dst_row):
        return pltpu.make_async_copy(h_hbm.at[pl.ds(src_row, 1), :], xs_hbm.at[pl.ds(dst_row, 1), :], sem)

    def issue(t, _):
        for k in range(2):
            dst = start_ref[meta_ref[0, k, t]] + meta_ref[0, 2 + k, t]
            row_copy(base + t, dst).start()
        return 0

    lax.fori_loop(0, tm, issue, 0)

    def drain(t, _):
        row_copy(0, 0).wait()
        row_copy(0, 0).wait()
        return 0

    lax.fori_loop(0, tm, drain, 0)


def _scatter_rows(starts, meta, h2d, xs_init):
    T, D = h2d.shape
    nt, _, tm = meta.shape
    grid_spec = pltpu.PrefetchScalarGridSpec(
        num_scalar_prefetch=1,
        grid=(nt,),
        in_specs=[pl.BlockSpec((1, SUBLANES, tm), lambda i, s: (i, 0, 0), memory_space=pltpu.SMEM),
                  pl.BlockSpec(memory_space=pl.ANY),
                  pl.BlockSpec(memory_space=pl.ANY)],
        out_specs=pl.BlockSpec(memory_space=pl.ANY),
        scratch_shapes=[pltpu.SemaphoreType.DMA(())],
    )
    return pl.pallas_call(
        _scatter_kernel,
        grid_spec=grid_spec,
        out_shape=jax.ShapeDtypeStruct(xs_init.shape, xs_init.dtype),
        input_output_aliases={3: 0},
        compiler_params=pltpu.CompilerParams(dimension_semantics=("arbitrary",), has_side_effects=True),
        name="moe_scatter",
    )(starts, meta, h2d, xs_init)


def _expert_kernel(te_ref, nu_ref, x_ref, wg_ref, wu_ref, wd_ref, y_ref, xb_ref, acc_ref):
    m = pl.program_id(0)
    c = pl.program_id(1)
    nc = pl.num_programs(1)

    @pl.when(m < nu_ref[0])
    def _():
        @pl.when(c == 0)
        def _():
            xb_ref[...] = x_ref[...].astype(BF16)
            acc_ref[...] = jnp.zeros_like(acc_ref)

        xb = xb_ref[...]
        g = _dot(xb, wg_ref[0])
        u = _dot(xb, wu_ref[0])
        acc_ref[...] += _dot((jax.nn.silu(g) * u).astype(BF16), wd_ref[0])

        @pl.when(c == nc - 1)
        def _():
            y_ref[...] = acc_ref[...]

    @pl.when((m >= nu_ref[0]) & (c == nc - 1))
    def _():
        y_ref[...] = jnp.zeros_like(y_ref)


def _expert_ffn(tile_expert, n_used, xs, wg, wu, wd):
    R, D = xs.shape
    E, _, F = wg.shape
    tr = MOE_TILE
    n_tiles = R // tr
    fc = min(FF_CHUNK, F)
    ncs = F // fc

    def row_map(m, c, te, nu):
        return (jnp.minimum(m, nu[0] - 1), 0)

    def col(m, c, nu):
        return jnp.where(m < nu[0], c, ncs - 1)

    def exp(m, te, nu):
        return te[jnp.minimum(m, nu[0] - 1)]

    grid_spec = pltpu.PrefetchScalarGridSpec(
        num_scalar_prefetch=2,
        grid=(n_tiles, ncs),
        in_specs=[
            pl.BlockSpec((tr, D), row_map),
            pl.BlockSpec((1, D, fc), lambda m, c, te, nu: (exp(m, te, nu), 0, col(m, c, nu))),
            pl.BlockSpec((1, D, fc), lambda m, c, te, nu: (exp(m, te, nu), 0, col(m, c, nu))),
            pl.BlockSpec((1, fc, D), lambda m, c, te, nu: (exp(m, te, nu), col(m, c, nu), 0)),
        ],
        out_specs=pl.BlockSpec((tr, D), lambda m, c, te, nu: (m, 0)),
        scratch_shapes=[pltpu.VMEM((tr, D), BF16), pltpu.VMEM((tr, D), F32)],
    )
    return pl.pallas_call(
        _expert_kernel,
        grid_spec=grid_spec,
        out_shape=jax.ShapeDtypeStruct((R, D), F32),
        compiler_params=pltpu.CompilerParams(dimension_semantics=("arbitrary", "arbitrary")),
        name="moe_experts",
    )(tile_expert, n_used, xs, wg, wu, wd)


def _combine_kernel(start_ref, meta_ref, x_ref, gate_ref, fn_ref, ys_hbm, o_ref, buf_ref, sem, *, final_norm):
    tm = x_ref.shape[0]

    def row_copy(src_row, k, t):
        return pltpu.make_async_copy(ys_hbm.at[pl.ds(src_row, 1), :], buf_ref.at[k, pl.ds(t, 1), :], sem)

    def issue(t, _):
        for k in range(2):
            src = start_ref[meta_ref[0, k, t]] + meta_ref[0, 2 + k, t]
            row_copy(src, k, t).start()
        return 0

    lax.fori_loop(0, tm, issue, 0)

    def drain(t, _):
        row_copy(0, 0, 0).wait()
        row_copy(0, 1, 0).wait()
        return 0

    lax.fori_loop(0, tm, drain, 0)
    g = gate_ref[...]
    xn = x_ref[...] + g[:, 0:1] * buf_ref[0] + g[:, 1:2] * buf_ref[1]
    o_ref[...] = _rms(xn, fn_ref[...]) if final_norm else xn


def _combine(starts, meta_c, x2d, gates_col, fnorm, ys, final_norm):
    T, D = x2d.shape
    nt, _, tm = meta_c.shape
    grid_spec = pltpu.PrefetchScalarGridSpec(
        num_scalar_prefetch=1,
        grid=(nt,),
        in_specs=[pl.BlockSpec((1, SUBLANES, tm), lambda i, s: (i, 0, 0), memory_space=pltpu.SMEM),
                  pl.BlockSpec((tm, D), lambda i, s: (i, 0)),
                  pl.BlockSpec((tm, SUBLANES), lambda i, s: (i, 0)),
                  pl.BlockSpec((1, D), lambda i, s: (0, 0)),
                  pl.BlockSpec(memory_space=pl.ANY)],
        out_specs=pl.BlockSpec((tm, D), lambda i, s: (i, 0)),
        scratch_shapes=[pltpu.VMEM((2, tm, D), F32), pltpu.SemaphoreType.DMA(())],
    )
    return pl.pallas_call(
        functools.partial(_combine_kernel, final_norm=final_norm),
        grid_spec=grid_spec,
        out_shape=jax.ShapeDtypeStruct((T, D), F32),
        compiler_params=pltpu.CompilerParams(dimension_semantics=("arbitrary",)),
        name="moe_combine",
    )(starts, meta_c, x2d, gates_col, fnorm, ys)


def _swap_halves_cols(w):
    half = w.shape[-1] // 2
    return jnp.concatenate([w[..., half:], w[..., :half]], axis=-1)


def _pad_cols(w, width):
    return jnp.pad(w, [(0, 0)] * (w.ndim - 1) + [(0, width - w.shape[-1])])


def _prep_w_in(w):
    kpe0 = _C_CKV + MLA_KV_RANK - 0
    kpe = w[:, kpe0:kpe0 + MLA_ROPE]
    return jnp.concatenate([
        w[:, :kpe0],
        _pad_cols(kpe, LANES), _pad_cols(_swap_halves_cols(kpe), LANES),
        w[:, kpe0 + MLA_ROPE:],
    ], axis=1).astype(BF16)


def _prep_w_uq(w):
    w = w.reshape(MLA_Q_RANK, MLA_HEADS, MLA_NOPE + MLA_ROPE)
    nope, pe = w[..., :MLA_NOPE], w[..., MLA_NOPE:]
    out = jnp.concatenate([nope, _pad_cols(pe, LANES), _pad_cols(_swap_halves_cols(pe), LANES)], axis=-1)
    return out.reshape(MLA_Q_RANK, MLA_HEADS * _Q_HEAD_COLS).astype(BF16)


def _rope_tables(seq):
    pos = jnp.arange(seq, dtype=F32)
    inv_freq = 1.0 / (ROPE_THETA ** (jnp.arange(0, MLA_ROPE, 2, dtype=F32) / MLA_ROPE))
    ang = pos[:, None] * inv_freq[None, :]
    cos, sin = jnp.cos(ang), jnp.sin(ang)
    cpad = _pad_cols(jnp.concatenate([cos, cos], axis=-1), LANES)
    spad = _pad_cols(jnp.concatenate([-sin, sin], axis=-1), LANES)
    return cpad, spad


def _block_diag_pool(w):
    G, c, _ = w.shape
    eye = jnp.eye(G, dtype=w.dtype)
    return (eye[:, None, :, None] * w[:, :, None, :]).reshape(G * c, G * c).astype(BF16)


def _moe_layout(counts, n_tiles, tile):
    tiles_per = (counts + tile - 1) // tile
    ends = jnp.cumsum(tiles_per)
    starts = (ends - tiles_per) * tile
    tile_expert = jnp.sum((jnp.arange(n_tiles)[:, None] >= ends[None, :]).astype(jnp.int32), axis=1)
    tile_expert = jnp.minimum(tile_expert, N_EXPERTS - 1)
    return starts.astype(jnp.int32), tile_expert.astype(jnp.int32), ends[-1:].astype(jnp.int32)


def kernel(x, attn_norm, w_in, hgrn_lower_bounds, hgrn_out_norm, mla_q_norm, mla_w_uq, mla_kv_norm,
           mla_w_ukv, pool_w, pool_scale, w_o, ffn_norm, dense_w_gate, dense_w_up, dense_w_down,
           moe_router, moe_w_gate, moe_w_up, moe_w_down, final_norm):
    B, S, D = x.shape
    T = B * S
    depth = w_in.shape[0]
    cpad, spad = _rope_tables(S)
    p_lb = jax.nn.softmax(hgrn_lower_bounds.astype(F32), axis=0)
    lbs = jnp.cumsum(p_lb, axis=0) - p_lb[0:1]

    assert depth % 2 == 0, "the final RMSNorm is fused into the last (MoE) layer's combine kernel"
    for l in range(depth):
        hg, xp, q, k, v = _in_proj(
            x, attn_norm[l][None], _prep_w_in(w_in[l]), mla_q_norm[l][None], _prep_w_uq(mla_w_uq[l]),
            mla_kv_norm[l][None], mla_w_ukv[l].astype(BF16), cpad, spad)
        o_a = _hgrn(hg, lbs[l][None], hgrn_out_norm[l][None])
        o_b = _attention(q, k, v)
        moe_layer = (l % 2 == 1)
        x, h = _mix_out(x, o_a, o_b, xp, _block_diag_pool(pool_w[l]), pool_scale[l][None],
                        w_o[l].astype(BF16), ffn_norm[l][None], F32 if moe_layer else BF16)
        j = l // 2
        if not moe_layer:
            x = _dense_ffn(x.reshape(T, D), h.reshape(T, D), dense_w_gate[j].astype(BF16),
                           dense_w_up[j].astype(BF16), dense_w_down[j].astype(BF16)).reshape(B, S, D)
        else:
            h2d = h.reshape(T, D)
            meta, gates, counts = _router(h2d, moe_router[j].T)
            n_tiles = (2 * T) // MOE_TILE + N_EXPERTS
            starts, tile_expert, n_used = _moe_layout(counts[:, 0], n_tiles, MOE_TILE)
            xs = _scatter_rows(starts, meta, h2d, jnp.zeros((n_tiles * MOE_TILE, D), F32))
            ys = _expert_ffn(tile_expert, n_used, xs, moe_w_gate[j].astype(BF16), moe_w_up[j].astype(BF16),
                             moe_w_down[j].astype(BF16))
            ct = min(COMBINE_TILE, T)
            meta_c = meta.transpose(1, 0, 2).reshape(SUBLANES, T // ct, ct).transpose(1, 0, 2)
            gates_col = gates.transpose(0, 2, 1).reshape(T, SUBLANES)
            last = (l == depth - 1)
            y = _combine(starts, meta_c, x.reshape(T, D), gates_col, final_norm[None], ys, last)
            x = y.reshape(B, S, D)
    return x
```

```python
import functools
import math

import jax
import jax.numpy as jnp
import numpy as np
from jax import lax
from jax.experimental import pallas as pl
from jax.experimental.pallas import tpu as pltpu

F32 = jnp.float32
BF16 = jnp.bfloat16

HG_HEADS = 4
HG_KEY_DIM = 128
HG_VAL_DIM = 64
HG_KEY_WIDTH = HG_HEADS * HG_KEY_DIM
HG_WIDTH = HG_HEADS * HG_VAL_DIM
MIN_FORGET = 1e-20
MLA_HEADS = 4
MLA_Q_RANK = 256
MLA_KV_RANK = 128
MLA_NOPE = 128
MLA_ROPE = 64
MLA_V = 128
MLA_WIDTH = MLA_HEADS * MLA_V
ROPE_THETA = 10000.0
MASK_VALUE = -1e30
POOL_GROUPS = 4
POOL_WINDOWS = (2, 4, 8, 16)
POOL_WIDTH = 256
POOL_GROUP_DIM = POOL_WIDTH // POOL_GROUPS
N_EXPERTS = 8
EPS = 1e-6

LANES = 128
SUBLANES = 8
QK_PAD = 256
V_EXT = 256

TOKEN_TILE = 512
HGRN_CHUNK = 256
ATTN_BK = 512
FF_CHUNK = 512
MOE_TILE = 1024
ROUTE_TILE = 512
COMBINE_TILE = 256
POOL_HALO = 16
DMA_UNROLL = 8

_C_HG = 0
_C_CQ = 2 * HG_KEY_WIDTH + 2 * HG_WIDTH
_C_CKV = _C_CQ + MLA_Q_RANK
_C_KPE = _C_CKV + MLA_KV_RANK
_C_KPES = _C_KPE + LANES
_C_XP = _C_KPES + LANES
_C_END = _C_XP + POOL_WIDTH
_Q_HEAD_COLS = 3 * LANES


def _rms(x, g):
    return x * lax.rsqrt(jnp.mean(x * x, axis=-1, keepdims=True) + EPS) * g


def _dot(a, b):
    return jnp.dot(a, b, preferred_element_type=F32)


def _dot_nt(a, b):
    return lax.dot_general(a, b, (((1,), (1,)), ((), ())), preferred_element_type=F32)


def _dot_tn(a, b):
    return lax.dot_general(a, b, (((0,), (0,)), ((), ())), preferred_element_type=F32)


def _const_spec(shape):
    nd = len(shape)
    return pl.BlockSpec(shape, lambda *_: (0,) * nd)


def _in_proj_kernel(x_ref, g_ref, w_ref, qn_ref, wuq_ref, kvn_ref, wukt_ref, wuv_ref, wkpet_ref,
                    cpad_ref, spad_ref, cpadt_ref, spadt_ref,
                    hg_ref, xp_ref, q_ref, kt_ref, v_ref):
    h = _rms(x_ref[0], g_ref[...]).astype(BF16)
    hg_ref[0] = _dot(h, w_ref[:, _C_HG:_C_CQ])
    xp_ref[0] = _dot(h, w_ref[:, _C_XP:_C_END])
    cpad = cpad_ref[...]
    spad = spad_ref[...]
    scale = (MLA_NOPE + MLA_ROPE) ** -0.5

    cq = _dot(h, w_ref[:, _C_CQ:_C_CKV])
    cqn = _rms(cq, qn_ref[...]).astype(BF16)
    for hd in range(MLA_HEADS):
        qh = _dot(cqn, wuq_ref[:, hd * _Q_HEAD_COLS:(hd + 1) * _Q_HEAD_COLS])
        q_ref[0, hd, :, 0:LANES] = (qh[:, 0:LANES] * scale).astype(BF16)
        pe = qh[:, LANES:2 * LANES] * cpad + qh[:, 2 * LANES:3 * LANES] * spad
        q_ref[0, hd, :, LANES:QK_PAD] = (pe * scale).astype(BF16)

    ckv = _dot(h, w_ref[:, _C_CKV:_C_KPE])
    ckvn = _rms(ckv, kvn_ref[...]).astype(BF16)
    kpet = (_dot_nt(wkpet_ref[0:LANES, :], h) * cpadt_ref[...]
            + _dot_nt(wkpet_ref[LANES:2 * LANES, :], h) * spadt_ref[...]).astype(BF16)
    tm = ckvn.shape[0]
    ones_col = (lax.broadcasted_iota(jnp.int32, (tm, LANES), 1) == 0).astype(BF16)
    for hd in range(MLA_HEADS):
        kt_ref[0, hd, 0:LANES, :] = _dot_nt(wukt_ref[hd * LANES:(hd + 1) * LANES, :], ckvn).astype(BF16)
        kt_ref[0, hd, LANES:QK_PAD, :] = kpet
        v_ref[0, hd, :, 0:MLA_V] = _dot(ckvn, wuv_ref[:, hd * MLA_V:(hd + 1) * MLA_V]).astype(BF16)
        v_ref[0, hd, :, MLA_V:V_EXT] = ones_col


def _in_proj(x, g, w_ext, qn, wuq_ext, kvn, wukt, wuv, wkpet, cpad, spad):
    B, S, D = x.shape
    tm = min(TOKEN_TILE, S)
    grid = (B, S // tm)
    n_hg = _C_CQ
    out_shape = (
        jax.ShapeDtypeStruct((B, S, n_hg), F32),
        jax.ShapeDtypeStruct((B, S, POOL_WIDTH), F32),
        jax.ShapeDtypeStruct((B, MLA_HEADS, S, QK_PAD), BF16),
        jax.ShapeDtypeStruct((B, MLA_HEADS, QK_PAD, S), BF16),
        jax.ShapeDtypeStruct((B, MLA_HEADS, S, V_EXT), BF16),
    )
    return pl.pallas_call(
        _in_proj_kernel,
        grid=grid,
        in_specs=[
            pl.BlockSpec((1, tm, D), lambda b, i: (b, i, 0)),
            _const_spec((1, D)),
            _const_spec(w_ext.shape),
            _const_spec((1, MLA_Q_RANK)),
            _const_spec(wuq_ext.shape),
            _const_spec((1, MLA_KV_RANK)),
            _const_spec(wukt.shape),
            _const_spec(wuv.shape),
            _const_spec(wkpet.shape),
            pl.BlockSpec((tm, LANES), lambda b, i: (i, 0)),
            pl.BlockSpec((tm, LANES), lambda b, i: (i, 0)),
            pl.BlockSpec((LANES, tm), lambda b, i: (0, i)),
            pl.BlockSpec((LANES, tm), lambda b, i: (0, i)),
        ],
        out_specs=(
            pl.BlockSpec((1, tm, n_hg), lambda b, i: (b, i, 0)),
            pl.BlockSpec((1, tm, POOL_WIDTH), lambda b, i: (b, i, 0)),
            pl.BlockSpec((1, MLA_HEADS, tm, QK_PAD), lambda b, i: (b, 0, i, 0)),
            pl.BlockSpec((1, MLA_HEADS, QK_PAD, tm), lambda b, i: (b, 0, 0, i)),
            pl.BlockSpec((1, MLA_HEADS, tm, V_EXT), lambda b, i: (b, 0, i, 0)),
        ),
        out_shape=out_shape,
        compiler_params=pltpu.CompilerParams(dimension_semantics=("arbitrary", "arbitrary")),
        name="in_proj",
    )(x, g, w_ext, qn, wuq_ext, kvn, wukt, wuv, wkpet, cpad, spad, cpad.T, spad.T)


def _split3(x):
    hi = x.astype(BF16)
    r1 = x - hi.astype(F32)
    mid = r1.astype(BF16)
    lo = (r1 - mid.astype(F32)).astype(BF16)
    return hi, mid, lo


def _hgrn_kernel(q_ref, f_ref, i_ref, g_ref, lb_ref, on_ref, o_ref, st_ref):
    C = q_ref.shape[1]
    KW = HG_KEY_WIDTH

    @pl.when(pl.program_id(1) == 0)
    def _():
        st_ref[...] = jnp.zeros_like(st_ref)

    lb = lb_ref[...]
    z = f_ref[0]
    forget = lb + (1.0 - lb) * jax.nn.sigmoid(z)
    lg = jnp.log(jnp.maximum(forget, MIN_FORGET))
    kk = (1.0 - lb) * jax.nn.sigmoid(-z)
    qq = jax.nn.silu(q_ref[0])
    vv = i_ref[0]

    row = lax.broadcasted_iota(jnp.int32, (C, C), 0)
    col = lax.broadcasted_iota(jnp.int32, (C, C), 1)
    tril = (col <= row).astype(BF16)
    b = sum(_dot(tril, part) for part in _split3(lg))

    qts, kts, masks = [], [], []
    sub = lax.broadcasted_iota(jnp.int32, (C, KW), 0)
    half = C // 2
    while half >= 4:
        blk = 2 * half
        b3 = b.reshape(C // blk, blk, KW)
        ref_row = jnp.broadcast_to(b3[:, half - 1:half, :], b3.shape).reshape(C, KW)
        e = jnp.exp(-jnp.abs(b - ref_row))
        upper = (sub & (blk - 1)) >= half
        qts.append(jnp.where(upper, qq * e, 0.0).astype(BF16))
        kts.append(jnp.where(upper, 0.0, kk * e).astype(BF16))
        shift = int(math.log2(blk))
        masks.append((row >> shift) == (col >> shift))
        half //= 2
    b3 = b.reshape(C // 8, 8, KW)
    mid_lo = 0.5 * (b3[:, 0:1, :] + b3[:, 3:4, :])
    mid_hi = 0.5 * (b3[:, 4:5, :] + b3[:, 7:8, :])
    sub8 = lax.broadcasted_iota(jnp.int32, b3.shape, 1)
    mid = jnp.where(sub8 < 4, mid_lo, mid_hi).reshape(C, KW)
    qts.append((qq * jnp.exp(b - mid)).astype(BF16))
    kts.append((kk * jnp.exp(mid - b)).astype(BF16))
    masks.append(((row >> 2) == (col >> 2)) & (col <= row))

    lane_v = lax.broadcasted_iota(jnp.int32, (C, HG_WIDTH), 1)
    o = _dot_nt((qq * jnp.exp(b)).astype(BF16), st_ref[...].astype(BF16))
    for hd in range(HG_HEADS):
        ks = slice(hd * HG_KEY_DIM, (hd + 1) * HG_KEY_DIM)
        a = jnp.zeros((C, C), F32)
        for qt, kt, m in zip(qts, kts, masks):
            a = a + jnp.where(m, _dot_nt(qt[:, ks], kt[:, ks]), 0.0)
        v_h = jnp.where((lane_v >> 6) == hd, vv, 0.0).astype(BF16)
        o = o + _dot(a.astype(BF16), v_h)

    b_last = b[C - 1:C, :]
    khat = (kk * jnp.exp(b_last - b)).astype(BF16)
    st_row = lax.broadcasted_iota(jnp.int32, (HG_WIDTH, KW), 0)
    st_col = lax.broadcasted_iota(jnp.int32, (HG_WIDTH, KW), 1)
    new_st = st_ref[...] * jnp.exp(b_last) + _dot_tn(vv.astype(BF16), khat)
    st_ref[...] = jnp.where((st_row >> 6) == (st_col >> 7), new_st, 0.0)

    gi = lax.broadcasted_iota(jnp.int32, (HG_WIDTH, HG_WIDTH), 0)
    gj = lax.broadcasted_iota(jnp.int32, (HG_WIDTH, HG_WIDTH), 1)
    grp = ((gi >> 6) == (gj >> 6)).astype(BF16)
    ssq = sum(_dot(part, grp) for part in _split3(o * o))
    on = o * lax.rsqrt(ssq * (1.0 / HG_VAL_DIM) + EPS) * on_ref[...]
    o_ref[0] = (on * jax.nn.silu(g_ref[0])).astype(o_ref.dtype)


def _hgrn(hg, lb, out_norm):
    B, S, _ = hg.shape
    C = min(HGRN_CHUNK, S)
    kb = HG_KEY_WIDTH // HG_KEY_WIDTH
    return pl.pallas_call(
        _hgrn_kernel,
        grid=(B, S // C),
        in_specs=[
            pl.BlockSpec((1, C, HG_KEY_WIDTH), lambda b, c: (b, c, 0)),
            pl.BlockSpec((1, C, HG_KEY_WIDTH), lambda b, c: (b, c, kb)),
            pl.BlockSpec((1, C, HG_WIDTH), lambda b, c: (b, c, 2 * HG_KEY_WIDTH // HG_WIDTH)),
            pl.BlockSpec((1, C, HG_WIDTH), lambda b, c: (b, c, 2 * HG_KEY_WIDTH // HG_WIDTH + 1)),
            _const_spec((1, HG_KEY_WIDTH)),
            _const_spec((1, HG_WIDTH)),
        ],
        out_specs=pl.BlockSpec((1, C, HG_WIDTH), lambda b, c: (b, c, 0)),
        out_shape=jax.ShapeDtypeStruct((B, S, HG_WIDTH), BF16),
        scratch_shapes=[pltpu.VMEM((HG_WIDTH, HG_KEY_WIDTH), F32)],
        compiler_params=pltpu.CompilerParams(dimension_semantics=("arbitrary", "arbitrary")),
        name="hgrn2",
    )(hg, hg, hg, hg, lb, out_norm)


def _attn_kernel(q_ref, kt_ref, v_ref, o_ref, s0_ref, s1_ref, acc_ref, *, bk):
    bq = q_ref.shape[2]
    i = pl.program_id(2)
    q = q_ref[0, 0]

    def scores(blk, s_ref):
        start = pl.multiple_of(blk * bk, bk)
        s_ref[...] = _dot(q, kt_ref[0, 0, :, pl.ds(start, bk)])

    def softmax_pv(blk, s_ref, m, key0):
        s = s_ref[...]
        if key0 is not None:
            qry = lax.broadcasted_iota(jnp.int32, (bq, bk), 0)
            key = key0 + lax.broadcasted_iota(jnp.int32, (bq, bk), 1)
            s = jnp.where(key <= qry, s, MASK_VALUE)
        m_new = jnp.maximum(m, jnp.max(s, axis=1, keepdims=True))
        p = jnp.exp(s - m_new).astype(BF16)
        start = pl.multiple_of(blk * bk, bk)
        acc_ref[...] = jnp.exp(m - m_new) * acc_ref[...] + _dot(p, v_ref[0, 0, pl.ds(start, bk), :])
        return m_new

    acc_ref[...] = jnp.zeros_like(acc_ref)
    scores(0, s0_ref)

    def pair(t, m):
        scores(2 * t + 1, s1_ref)
        m = softmax_pv(2 * t, s0_ref, m, None)
        scores(2 * t + 2, s0_ref)
        return softmax_pv(2 * t + 1, s1_ref, m, None)

    m = lax.fori_loop(0, i, pair, jnp.full((bq, 1), -jnp.inf, F32))
    scores(2 * i + 1, s1_ref)
    m = softmax_pv(2 * i, s0_ref, m, 0)
    m = softmax_pv(2 * i + 1, s1_ref, m, bk)
    acc = acc_ref[...]
    o_ref[0] = (acc[:, 0:MLA_V] / acc[:, MLA_V:MLA_V + 1]).astype(o_ref.dtype)


def _attention(q, kt, v):
    B, H, S, _ = q.shape
    bk = min(ATTN_BK, S // 2)
    bq = 2 * bk
    return pl.pallas_call(
        functools.partial(_attn_kernel, bk=bk),
        grid=(B, H, S // bq),
        in_specs=[
            pl.BlockSpec((1, 1, bq, QK_PAD), lambda b, h, i: (b, h, i, 0)),
            pl.BlockSpec((1, 1, QK_PAD, S), lambda b, h, i: (b, h, 0, 0)),
            pl.BlockSpec((1, 1, S, V_EXT), lambda b, h, i: (b, h, 0, 0)),
        ],
        out_specs=pl.BlockSpec((1, bq, MLA_V), lambda b, h, i: (b, i, h)),
        out_shape=jax.ShapeDtypeStruct((B, S, H * MLA_V), BF16),
        scratch_shapes=[pltpu.VMEM((bq, bk), F32), pltpu.VMEM((bq, bk), F32), pltpu.VMEM((bq, V_EXT), F32)],
        compiler_params=pltpu.CompilerParams(dimension_semantics=("arbitrary", "arbitrary", "arbitrary")),
        name="mla_attention",
    )(q, kt, v)


def _mix_out_kernel(x_ref, oa_ref, ob_ref, xp_ref, wpool_ref, pscale_ref, wo_ref, fn_ref,
                    xo_ref, h_ref, halo_ref):
    tm = xp_ref.shape[1]
    i = pl.program_id(1)

    @pl.when(i == 0)
    def _():
        halo_ref[...] = jnp.zeros_like(halo_ref)

    xp = xp_ref[0]
    xx = jnp.concatenate([halo_ref[...], xp], axis=0)
    halo_ref[...] = xp[tm - POOL_HALO:, :]

    w2 = xx[1:, :] + xx[:-1, :]
    w4 = w2[2:, :] + w2[:-2, :]
    w8 = w4[4:, :] + w4[:-4, :]
    w16 = w8[8:, :] + w8[:-8, :]
    sums = (w2[POOL_HALO - 1:, :], w4[POOL_HALO - 3:, :], w8[POOL_HALO - 7:, :], w16[POOL_HALO - 15:, :])
    t = i * tm + lax.broadcasted_iota(jnp.int32, (tm, POOL_WIDTH), 0)
    lane = lax.broadcasted_iota(jnp.int32, (tm, POOL_WIDTH), 1)
    pooled = jnp.zeros((tm, POOL_WIDTH), F32)
    for gi, w in enumerate(POOL_WINDOWS):
        cnt = jnp.minimum(t + 1, w).astype(F32)
        pooled = jnp.where((lane >> 6) == gi, sums[gi] / cnt, pooled)
    pooled = pooled - xp
    oc = _dot(pooled.astype(BF16), wpool_ref[...]) * pscale_ref[...]

    y = _dot(oa_ref[0], wo_ref[0:HG_WIDTH, :])
    y = y + _dot(ob_ref[0], wo_ref[HG_WIDTH:HG_WIDTH + MLA_WIDTH, :])
    y = y + _dot(oc.astype(BF16), wo_ref[HG_WIDTH + MLA_WIDTH:, :])
    xn = x_ref[0] + y
    xo_ref[0] = xn
    h_ref[0] = _rms(xn, fn_ref[...]).astype(h_ref.dtype)


def _mix_out(x, oa, ob, xp, wpool_bd, pscale, wo, fnorm, h_dtype):
    B, S, D = x.shape
    tm = min(TOKEN_TILE, S)
    tok = lambda w: pl.BlockSpec((1, tm, w), lambda b, i: (b, i, 0))
    return pl.pallas_call(
        _mix_out_kernel,
        grid=(B, S // tm),
        in_specs=[tok(D), tok(HG_WIDTH), tok(MLA_WIDTH), tok(POOL_WIDTH),
                  _const_spec(wpool_bd.shape), _const_spec((1, POOL_WIDTH)), _const_spec(wo.shape),
                  _const_spec((1, D))],
        out_specs=(tok(D), tok(D)),
        out_shape=(jax.ShapeDtypeStruct((B, S, D), F32), jax.ShapeDtypeStruct((B, S, D), h_dtype)),
        scratch_shapes=[pltpu.VMEM((POOL_HALO, POOL_WIDTH), F32)],
        compiler_params=pltpu.CompilerParams(dimension_semantics=("arbitrary", "arbitrary")),
        name="mix_out",
    )(x, oa, ob, xp, wpool_bd, pscale, wo, fnorm)


def _dense_ffn_kernel(x_ref, h_ref, wg_ref, wu_ref, wd_ref, o_ref):
    h = h_ref[...]
    acc = x_ref[...]
    dff = wg_ref.shape[1]
    for c0 in range(0, dff, FF_CHUNK):
        g = _dot(h, wg_ref[:, c0:c0 + FF_CHUNK])
        u = _dot(h, wu_ref[:, c0:c0 + FF_CHUNK])
        acc = acc + _dot((jax.nn.silu(g) * u).astype(BF16), wd_ref[c0:c0 + FF_CHUNK, :])
    o_ref[...] = acc


def _dense_ffn(x2d, h2d, wg, wu, wd):
    T, D = x2d.shape
    tm = min(TOKEN_TILE, T)
    tok = pl.BlockSpec((tm, D), lambda i: (i, 0))
    return pl.pallas_call(
        _dense_ffn_kernel,
        grid=(T // tm,),
        in_specs=[tok, tok, _const_spec(wg.shape), _const_spec(wu.shape), _const_spec(wd.shape)],
        out_specs=tok,
        out_shape=jax.ShapeDtypeStruct((T, D), F32),
        compiler_params=pltpu.CompilerParams(dimension_semantics=("arbitrary",)),
        name="dense_ffn",
    )(x2d, h2d, wg, wu, wd)


def _router_kernel(h_ref, rt_ref, meta_ref, gate_ref, cnt_ref, run_ref):
    tm = h_ref.shape[0]
    E = N_EXPERTS

    @pl.when(pl.program_id(0) == 0)
    def _():
        run_ref[...] = jnp.zeros_like(run_ref)

    logits = lax.dot_general(rt_ref[...], h_ref[...], (((1,), (1,)), ((), ())),
                             precision=lax.Precision.HIGHEST, preferred_element_type=F32)
    eid = lax.broadcasted_iota(jnp.int32, (E, tm), 0)
    m1 = jnp.max(logits, axis=0, keepdims=True)
    i1 = jnp.min(jnp.where(logits == m1, eid, E), axis=0, keepdims=True)
    rest = jnp.where(eid == i1, -jnp.inf, logits)
    m2 = jnp.max(rest, axis=0, keepdims=True)
    i2 = jnp.min(jnp.where(rest == m2, eid, E), axis=0, keepdims=True)
    e2 = jnp.exp(m2 - m1)
    g1 = 1.0 / (1.0 + e2)
    g2 = e2 / (1.0 + e2)

    sel = ((eid == i1) | (eid == i2))
    r = lax.broadcasted_iota(jnp.int32, (tm, tm), 0)
    c = lax.broadcasted_iota(jnp.int32, (tm, tm), 1)
    before = (r < c).astype(BF16)
    excl = _dot(sel.astype(BF16), before) + run_ref[:, 0:1]
    rank1 = jnp.sum(jnp.where(eid == i1, excl, 0.0), axis=0, keepdims=True).astype(jnp.int32)
    rank2 = jnp.sum(jnp.where(eid == i2, excl, 0.0), axis=0, keepdims=True).astype(jnp.int32)
    run_ref[...] = run_ref[...] + jnp.sum(sel.astype(F32), axis=1, keepdims=True)
    cnt_ref[...] = run_ref[...].astype(jnp.int32)

    zi = jnp.zeros((1, tm), jnp.int32)
    meta_ref[0] = jnp.concatenate([i1, i2, rank1, rank2, zi, zi, zi, zi], axis=0)
    zf = jnp.zeros((1, tm), F32)
    gate_ref[0] = jnp.concatenate([g1, g2, zf, zf, zf, zf, zf, zf], axis=0)


def _router(h2d, router_t):
    T, D = h2d.shape
    tm = min(ROUTE_TILE, T)
    nt = T // tm
    return pl.pallas_call(
        _router_kernel,
        grid=(nt,),
        in_specs=[pl.BlockSpec((tm, D), lambda i: (i, 0)), _const_spec(router_t.shape)],
        out_specs=(pl.BlockSpec((1, SUBLANES, tm), lambda i: (i, 0, 0)),
                   pl.BlockSpec((1, SUBLANES, tm), lambda i: (i, 0, 0)),
                   _const_spec((N_EXPERTS, LANES))),
        out_shape=(jax.ShapeDtypeStruct((nt, SUBLANES, tm), jnp.int32),
                   jax.ShapeDtypeStruct((nt, SUBLANES, tm), F32),
                   jax.ShapeDtypeStruct((N_EXPERTS, LANES), jnp.int32)),
        scratch_shapes=[pltpu.VMEM((N_EXPERTS, LANES), F32)],
        compiler_params=pltpu.CompilerParams(dimension_semantics=("arbitrary",)),
        name="moe_router",
    )(h2d, router_t)


def _scatter_kernel(start_ref, meta_ref, h_ref, xs_in_hbm, xs_hbm, sem):
    del xs_in_hbm
    tm = meta_ref.shape[2]

    def row_copy(src_row, dst_row):
        return pltpu.make_async_copy(h_ref.at[pl.ds(src_row, 1), :], xs_hbm.at[pl.ds(dst_row, 1), :], sem)

    def issue(t, _):
        for k in range(2):
            dst = start_ref[meta_ref[0, k, t]] + meta_ref[0, 2 + k, t]
            row_copy(t, dst).start(priority=k)
        return 0

    lax.fori_loop(0, tm, issue, 0, unroll=DMA_UNROLL)

    def drain(t, _):
        row_copy(0, 0).wait()
        row_copy(0, 0).wait()
        return 0

    lax.fori_loop(0, tm, drain, 0, unroll=DMA_UNROLL)


def _scatter_rows(starts, meta, h2d, xs_init):
    T, D = h2d.shape
    nt, _, tm = meta.shape
    grid_spec = pltpu.PrefetchScalarGridSpec(
        num_scalar_prefetch=1,
        grid=(nt,),
        in_specs=[pl.BlockSpec((1, SUBLANES, tm), lambda i, s: (i, 0, 0), memory_space=pltpu.SMEM),
                  pl.BlockSpec((tm, D), lambda i, s: (i, 0)),
                  pl.BlockSpec(memory_space=pl.ANY)],
        out_specs=pl.BlockSpec(memory_space=pl.ANY),
        scratch_shapes=[pltpu.SemaphoreType.DMA(())],
    )
    return pl.pallas_call(
        _scatter_kernel,
        grid_spec=grid_spec,
        out_shape=jax.ShapeDtypeStruct(xs_init.shape, xs_init.dtype),
        input_output_aliases={3: 0},
        compiler_params=pltpu.CompilerParams(dimension_semantics=("arbitrary",), has_side_effects=True),
        name="moe_scatter",
    )(starts, meta, h2d, xs_init)


def _expert_kernel(te_ref, nu_ref, x_ref, wg_ref, wu_ref, wd_ref, y_ref, xb_ref, acc_ref):
    m = pl.program_id(0)
    c = pl.program_id(1)
    nc = pl.num_programs(1)

    @pl.when(m < nu_ref[0])
    def _():
        @pl.when(c == 0)
        def _():
            xb_ref[...] = x_ref[...].astype(BF16)
            acc_ref[...] = jnp.zeros_like(acc_ref)

        xb = xb_ref[...]
        g = _dot(xb, wg_ref[0])
        u = _dot(xb, wu_ref[0])
        acc_ref[...] += _dot((jax.nn.silu(g) * u).astype(BF16), wd_ref[0])

        @pl.when(c == nc - 1)
        def _():
            y_ref[...] = acc_ref[...]

    @pl.when((m >= nu_ref[0]) & (c == nc - 1))
    def _():
        y_ref[...] = jnp.zeros_like(y_ref)


def _expert_ffn(tile_expert, n_used, xs, wg, wu, wd):
    R, D = xs.shape
    E, _, F = wg.shape
    tr = MOE_TILE
    n_tiles = R // tr
    fc = min(FF_CHUNK, F)
    ncs = F // fc

    def row_map(m, c, te, nu):
        return (jnp.minimum(m, nu[0] - 1), 0)

    def col(m, c, nu):
        return jnp.where(m < nu[0], c, ncs - 1)

    def exp(m, te, nu):
        return te[jnp.minimum(m, nu[0] - 1)]

    grid_spec = pltpu.PrefetchScalarGridSpec(
        num_scalar_prefetch=2,
        grid=(n_tiles, ncs),
        in_specs=[
            pl.BlockSpec((tr, D), row_map),
            pl.BlockSpec((1, D, fc), lambda m, c, te, nu: (exp(m, te, nu), 0, col(m, c, nu))),
            pl.BlockSpec((1, D, fc), lambda m, c, te, nu: (exp(m, te, nu), 0, col(m, c, nu))),
            pl.BlockSpec((1, fc, D), lambda m, c, te, nu: (exp(m, te, nu), col(m, c, nu), 0)),
        ],
        out_specs=pl.BlockSpec((tr, D), lambda m, c, te, nu: (m, 0)),
        scratch_shapes=[pltpu.VMEM((tr, D), BF16), pltpu.VMEM((tr, D), F32)],
    )
    return pl.pallas_call(
        _expert_kernel,
        grid_spec=grid_spec,
        out_shape=jax.ShapeDtypeStruct((R, D), F32),
        compiler_params=pltpu.CompilerParams(dimension_semantics=("arbitrary", "arbitrary")),
        name="moe_experts",
    )(tile_expert, n_used, xs, wg, wu, wd)


def _combine_kernel(start_ref, meta_ref, x_ref, gate_ref, fn_ref, ys_hbm, o_ref, buf_ref, sem, *, final_norm):
    tm = x_ref.shape[0]

    def row_copy(src_row, k, t):
        return pltpu.make_async_copy(ys_hbm.at[pl.ds(src_row, 1), :], buf_ref.at[k, pl.ds(t, 1), :], sem)

    def issue(t, _):
        for k in range(2):
            src = start_ref[meta_ref[0, k, t]] + meta_ref[0, 2 + k, t]
            row_copy(src, k, t).start(priority=k)
        return 0

    lax.fori_loop(0, tm, issue, 0, unroll=DMA_UNROLL)

    def drain(t, _):
        row_copy(0, 0, 0).wait()
        row_copy(0, 1, 0).wait()
        return 0

    lax.fori_loop(0, tm, drain, 0, unroll=DMA_UNROLL)
    g = gate_ref[...]
    xn = x_ref[...] + g[:, 0:1] * buf_ref[0] + g[:, 1:2] * buf_ref[1]
    o_ref[...] = _rms(xn, fn_ref[...]) if final_norm else xn


def _combine(starts, meta_c, x2d, gates_col, fnorm, ys, final_norm):
    T, D = x2d.shape
    nt, _, tm = meta_c.shape
    grid_spec = pltpu.PrefetchScalarGridSpec(
        num_scalar_prefetch=1,
        grid=(nt,),
        in_specs=[pl.BlockSpec((1, SUBLANES, tm), lambda i, s: (i, 0, 0), memory_space=pltpu.SMEM),
                  pl.BlockSpec((tm, D), lambda i, s: (i, 0)),
                  pl.BlockSpec((tm, SUBLANES), lambda i, s: (i, 0)),
                  pl.BlockSpec((1, D), lambda i, s: (0, 0)),
                  pl.BlockSpec(memory_space=pl.ANY)],
        out_specs=pl.BlockSpec((tm, D), lambda i, s: (i, 0)),
        scratch_shapes=[pltpu.VMEM((2, tm, D), F32), pltpu.SemaphoreType.DMA(())],
    )
    return pl.pallas_call(
        functools.partial(_combine_kernel, final_norm=final_norm),
        grid_spec=grid_spec,
        out_shape=jax.ShapeDtypeStruct((T, D), F32),
        compiler_params=pltpu.CompilerParams(dimension_semantics=("arbitrary",)),
        name="moe_combine",
    )(starts, meta_c, x2d, gates_col, fnorm, ys)


def _swap_halves_cols(w):
    half = w.shape[-1] // 2
    return jnp.concatenate([w[..., half:], w[..., :half]], axis=-1)


def _pad_cols(w, width):
    return jnp.pad(w, [(0, 0)] * (w.ndim - 1) + [(0, width - w.shape[-1])])


def _prep_w_in(w):
    kpe0 = _C_CKV + MLA_KV_RANK - 0
    kpe = w[:, kpe0:kpe0 + MLA_ROPE]
    return jnp.concatenate([
        w[:, :kpe0],
        _pad_cols(kpe, LANES), _pad_cols(_swap_halves_cols(kpe), LANES),
        w[:, kpe0 + MLA_ROPE:],
    ], axis=1).astype(BF16)


def _prep_w_uq(w):
    w = w.reshape(MLA_Q_RANK, MLA_HEADS, MLA_NOPE + MLA_ROPE)
    nope, pe = w[..., :MLA_NOPE], w[..., MLA_NOPE:]
    out = jnp.concatenate([nope, _pad_cols(pe, LANES), _pad_cols(_swap_halves_cols(pe), LANES)], axis=-1)
    return out.reshape(MLA_Q_RANK, MLA_HEADS * _Q_HEAD_COLS).astype(BF16)


def _prep_w_ukv(w):
    w = w.reshape(MLA_KV_RANK, MLA_HEADS, MLA_NOPE + MLA_V)
    wukt = jnp.transpose(w[..., :MLA_NOPE], (1, 2, 0)).reshape(MLA_HEADS * MLA_NOPE, MLA_KV_RANK)
    wuv = w[..., MLA_NOPE:].reshape(MLA_KV_RANK, MLA_HEADS * MLA_V)
    return wukt.astype(BF16), wuv.astype(BF16)


def _prep_w_kpe_t(w_ext):
    return w_ext[:, _C_KPE:_C_XP].T


def _rope_tables(seq):
    pos = jnp.arange(seq, dtype=F32)
    inv_freq = 1.0 / (ROPE_THETA ** (jnp.arange(0, MLA_ROPE, 2, dtype=F32) / MLA_ROPE))
    ang = pos[:, None] * inv_freq[None, :]
    cos, sin = jnp.cos(ang), jnp.sin(ang)
    cpad = _pad_cols(jnp.concatenate([cos, cos], axis=-1), LANES)
    spad = _pad_cols(jnp.concatenate([-sin, sin], axis=-1), LANES)
    return cpad, spad


def _block_diag_pool(w):
    G, c, _ = w.shape
    eye = jnp.eye(G, dtype=w.dtype)
    return (eye[:, None, :, None] * w[:, :, None, :]).reshape(G * c, G * c).astype(BF16)


def _moe_layout(counts, n_tiles, tile):
    tiles_per = (counts + tile - 1) // tile
    ends = jnp.cumsum(tiles_per)
    starts = (ends - tiles_per) * tile
    tile_expert = jnp.sum((jnp.arange(n_tiles)[:, None] >= ends[None, :]).astype(jnp.int32), axis=1)
    tile_expert = jnp.minimum(tile_expert, N_EXPERTS - 1)
    return starts.astype(jnp.int32), tile_expert.astype(jnp.int32), ends[-1:].astype(jnp.int32)


def kernel(x, attn_norm, w_in, hgrn_lower_bounds, hgrn_out_norm, mla_q_norm, mla_w_uq, mla_kv_norm,
           mla_w_ukv, pool_w, pool_scale, w_o, ffn_norm, dense_w_gate, dense_w_up, dense_w_down,
           moe_router, moe_w_gate, moe_w_up, moe_w_down, final_norm):
    B, S, D = x.shape
    T = B * S
    depth = w_in.shape[0]
    cpad, spad = _rope_tables(S)
    p_lb = jax.nn.softmax(hgrn_lower_bounds.astype(F32), axis=0)
    lbs = jnp.cumsum(p_lb, axis=0) - p_lb[0:1]

    assert depth % 2 == 0, "the final RMSNorm is fused into the last (MoE) layer's combine kernel"
    for l in range(depth):
        wukt, wuv = _prep_w_ukv(mla_w_ukv[l])
        w_ext = _prep_w_in(w_in[l])
        hg, xp, q, kt, v = _in_proj(
            x, attn_norm[l][None], w_ext, mla_q_norm[l][None], _prep_w_uq(mla_w_uq[l]),
            mla_kv_norm[l][None], wukt, wuv, _prep_w_kpe_t(w_ext), cpad, spad)
        o_a = _hgrn(hg, lbs[l][None], hgrn_out_norm[l][None])
        o_b = _attention(q, kt, v)
        moe_layer = (l % 2 == 1)
        x, h = _mix_out(x, o_a, o_b, xp, _block_diag_pool(pool_w[l]), pool_scale[l][None],
                        w_o[l].astype(BF16), ffn_norm[l][None], F32 if moe_layer else BF16)
        j = l // 2
        if not moe_layer:
            x = _dense_ffn(x.reshape(T, D), h.reshape(T, D), dense_w_gate[j].astype(BF16),
                           dense_w_up[j].astype(BF16), dense_w_down[j].astype(BF16)).reshape(B, S, D)
        else:
            h2d = h.reshape(T, D)
            meta, gates, counts = _router(h2d, moe_router[j].T)
            n_tiles = (2 * T) // MOE_TILE + N_EXPERTS
            starts, tile_expert, n_used = _moe_layout(counts[:, 0], n_tiles, MOE_TILE)
            xs = _scatter_rows(starts, meta, h2d, jnp.zeros((n_tiles * MOE_TILE, D), F32))
            ys = _expert_ffn(tile_expert, n_used, xs, moe_w_gate[j].astype(BF16), moe_w_up[j].astype(BF16),
                             moe_w_down[j].astype(BF16))
            ct = min(COMBINE_TILE, T)
            meta_c = meta.transpose(1, 0, 2).reshape(SUBLANES, T // ct, ct).transpose(1, 0, 2)
            gates_col = gates.transpose(0, 2, 1).reshape(T, SUBLANES)
            last = (l == depth - 1)
            y = _combine(starts, meta_c, x.reshape(T, D), gates_col, final_norm[None], ys, last)
            x = y.reshape(B, S, D)
    return x
```

```python
import functools
import math

import jax
import jax.numpy as jnp
import numpy as np
from jax import lax
from jax.experimental import pallas as pl
from jax.experimental.pallas import tpu as pltpu

F32 = jnp.float32
BF16 = jnp.bfloat16

HG_HEADS = 4
HG_KEY_DIM = 128
HG_VAL_DIM = 64
HG_KEY_WIDTH = HG_HEADS * HG_KEY_DIM
HG_WIDTH = HG_HEADS * HG_VAL_DIM
MIN_FORGET = 1e-20
MLA_HEADS = 4
MLA_Q_RANK = 256
MLA_KV_RANK = 128
MLA_NOPE = 128
MLA_ROPE = 64
MLA_V = 128
MLA_WIDTH = MLA_HEADS * MLA_V
ROPE_THETA = 10000.0
MASK_VALUE = -1e30
POOL_GROUPS = 4
POOL_WINDOWS = (2, 4, 8, 16)
POOL_WIDTH = 256
POOL_GROUP_DIM = POOL_WIDTH // POOL_GROUPS
N_EXPERTS = 8
EPS = 1e-6

LANES = 128
SUBLANES = 8
QK_PAD = 256
V_EXT = 256

TOKEN_TILE = 512
HGRN_CHUNK = 256
ATTN_BK = 512
FF_CHUNK = 512
MOE_FF_CHUNK = 896
MOE_TILE = 1024
ROUTE_TILE = 512
COMBINE_TILE = 256
POOL_HALO = 16
DMA_UNROLL = 8

_C_HG = 0
_C_CQ = 2 * HG_KEY_WIDTH + 2 * HG_WIDTH
_C_CKV = _C_CQ + MLA_Q_RANK
_C_KPE = _C_CKV + MLA_KV_RANK
_C_KPES = _C_KPE + LANES
_C_XP = _C_KPES + LANES
_C_END = _C_XP + POOL_WIDTH
_Q_HEAD_COLS = 3 * LANES


def _rms(x, g):
    return x * lax.rsqrt(jnp.mean(x * x, axis=-1, keepdims=True) + EPS) * g


def _dot(a, b):
    return jnp.dot(a, b, preferred_element_type=F32)


def _dot_nt(a, b):
    return lax.dot_general(a, b, (((1,), (1,)), ((), ())), preferred_element_type=F32)


def _dot_tn(a, b):
    return lax.dot_general(a, b, (((0,), (0,)), ((), ())), preferred_element_type=F32)


def _const_spec(shape):
    nd = len(shape)
    return pl.BlockSpec(shape, lambda *_: (0,) * nd)


def _in_proj_kernel(x_ref, g_ref, w_ref, qn_ref, wuq_ref, kvn_ref, wukt_ref, wuv_ref, wkpet_ref,
                    cpad_ref, spad_ref, cpadt_ref, spadt_ref,
                    hg_ref, xp_ref, q_ref, kt_ref, v_ref):
    h = _rms(x_ref[0], g_ref[...]).astype(BF16)
    hg_ref[0] = _dot(h, w_ref[:, _C_HG:_C_CQ])
    xp_ref[0] = _dot(h, w_ref[:, _C_XP:_C_END])
    cpad = cpad_ref[...]
    spad = spad_ref[...]
    scale = (MLA_NOPE + MLA_ROPE) ** -0.5 * math.log2(math.e)

    cq = _dot(h, w_ref[:, _C_CQ:_C_CKV])
    cqn = _rms(cq, qn_ref[...]).astype(BF16)
    for hd in range(MLA_HEADS):
        qh = _dot(cqn, wuq_ref[:, hd * _Q_HEAD_COLS:(hd + 1) * _Q_HEAD_COLS])
        q_ref[0, hd, :, 0:LANES] = (qh[:, 0:LANES] * scale).astype(BF16)
        pe = qh[:, LANES:2 * LANES] * cpad + qh[:, 2 * LANES:3 * LANES] * spad
        q_ref[0, hd, :, LANES:QK_PAD] = (pe * scale).astype(BF16)

    ckv = _dot(h, w_ref[:, _C_CKV:_C_KPE])
    ckvn = _rms(ckv, kvn_ref[...]).astype(BF16)
    kpet = (_dot_nt(wkpet_ref[0:LANES, :], h) * cpadt_ref[...]
            + _dot_nt(wkpet_ref[LANES:2 * LANES, :], h) * spadt_ref[...]).astype(BF16)
    tm = ckvn.shape[0]
    ones_col = (lax.broadcasted_iota(jnp.int32, (tm, LANES), 1) == 0).astype(BF16)
    for hd in range(MLA_HEADS):
        kt_ref[0, hd, 0:LANES, :] = _dot_nt(wukt_ref[hd * LANES:(hd + 1) * LANES, :], ckvn).astype(BF16)
        kt_ref[0, hd, LANES:QK_PAD, :] = kpet
        v_ref[0, hd, :, 0:MLA_V] = _dot(ckvn, wuv_ref[:, hd * MLA_V:(hd + 1) * MLA_V]).astype(BF16)
        v_ref[0, hd, :, MLA_V:V_EXT] = ones_col


def _in_proj(x, g, w_ext, qn, wuq_ext, kvn, wukt, wuv, wkpet, cpad, spad):
    B, S, D = x.shape
    tm = min(TOKEN_TILE, S)
    grid = (B, S // tm)
    n_hg = _C_CQ
    out_shape = (
        jax.ShapeDtypeStruct((B, S, n_hg), F32),
        jax.ShapeDtypeStruct((B, S, POOL_WIDTH), F32),
        jax.ShapeDtypeStruct((B, MLA_HEADS, S, QK_PAD), BF16),
        jax.ShapeDtypeStruct((B, MLA_HEADS, QK_PAD, S), BF16),
        jax.ShapeDtypeStruct((B, MLA_HEADS, S, V_EXT), BF16),
    )
    return pl.pallas_call(
        _in_proj_kernel,
        grid=grid,
        in_specs=[
            pl.BlockSpec((1, tm, D), lambda b, i: (b, i, 0)),
            _const_spec((1, D)),
            _const_spec(w_ext.shape),
            _const_spec((1, MLA_Q_RANK)),
            _const_spec(wuq_ext.shape),
            _const_spec((1, MLA_KV_RANK)),
            _const_spec(wukt.shape),
            _const_spec(wuv.shape),
            _const_spec(wkpet.shape),
            pl.BlockSpec((tm, LANES), lambda b, i: (i, 0)),
            pl.BlockSpec((tm, LANES), lambda b, i: (i, 0)),
            pl.BlockSpec((LANES, tm), lambda b, i: (0, i)),
            pl.BlockSpec((LANES, tm), lambda b, i: (0, i)),
        ],
        out_specs=(
            pl.BlockSpec((1, tm, n_hg), lambda b, i: (b, i, 0)),
            pl.BlockSpec((1, tm, POOL_WIDTH), lambda b, i: (b, i, 0)),
            pl.BlockSpec((1, MLA_HEADS, tm, QK_PAD), lambda b, i: (b, 0, i, 0)),
            pl.BlockSpec((1, MLA_HEADS, QK_PAD, tm), lambda b, i: (b, 0, 0, i)),
            pl.BlockSpec((1, MLA_HEADS, tm, V_EXT), lambda b, i: (b, 0, i, 0)),
        ),
        out_shape=out_shape,
        compiler_params=pltpu.CompilerParams(dimension_semantics=("arbitrary", "arbitrary")),
        name="in_proj",
    )(x, g, w_ext, qn, wuq_ext, kvn, wukt, wuv, wkpet, cpad, spad, cpad.T, spad.T)


def _split3(x):
    hi = x.astype(BF16)
    r1 = x - hi.astype(F32)
    mid = r1.astype(BF16)
    lo = (r1 - mid.astype(F32)).astype(BF16)
    return hi, mid, lo


def _hgrn_kernel(q_ref, f_ref, i_ref, g_ref, lb_ref, on_ref, o_ref, st_ref):
    C = q_ref.shape[1]
    KW = HG_KEY_WIDTH

    @pl.when(pl.program_id(1) == 0)
    def _():
        st_ref[...] = jnp.zeros_like(st_ref)

    lb = lb_ref[...]
    z = f_ref[0]
    forget = lb + (1.0 - lb) * jax.nn.sigmoid(z)
    lg = jnp.log(jnp.maximum(forget, MIN_FORGET))
    kk = (1.0 - lb) * jax.nn.sigmoid(-z)
    qq = jax.nn.silu(q_ref[0])
    vv = i_ref[0]

    row = lax.broadcasted_iota(jnp.int32, (C, C), 0)
    col = lax.broadcasted_iota(jnp.int32, (C, C), 1)
    tril = (col <= row).astype(BF16)
    b = sum(_dot(tril, part) for part in _split3(lg))

    qts, kts, masks = [], [], []
    sub = lax.broadcasted_iota(jnp.int32, (C, KW), 0)
    half = C // 2
    while half >= 4:
        blk = 2 * half
        b3 = b.reshape(C // blk, blk, KW)
        ref_row = jnp.broadcast_to(b3[:, half - 1:half, :], b3.shape).reshape(C, KW)
        e = jnp.exp(-jnp.abs(b - ref_row))
        upper = (sub & (blk - 1)) >= half
        qts.append(jnp.where(upper, qq * e, 0.0).astype(BF16))
        kts.append(jnp.where(upper, 0.0, kk * e).astype(BF16))
        shift = int(math.log2(blk))
        masks.append((row >> shift) == (col >> shift))
        half //= 2
    b3 = b.reshape(C // 8, 8, KW)
    mid_lo = 0.5 * (b3[:, 0:1, :] + b3[:, 3:4, :])
    mid_hi = 0.5 * (b3[:, 4:5, :] + b3[:, 7:8, :])
    sub8 = lax.broadcasted_iota(jnp.int32, b3.shape, 1)
    mid = jnp.where(sub8 < 4, mid_lo, mid_hi).reshape(C, KW)
    qts.append((qq * jnp.exp(b - mid)).astype(BF16))
    kts.append((kk * jnp.exp(mid - b)).astype(BF16))
    masks.append(((row >> 2) == (col >> 2)) & (col <= row))

    lane_v = lax.broadcasted_iota(jnp.int32, (C, HG_WIDTH), 1)
    o = _dot_nt((qq * jnp.exp(b)).astype(BF16), st_ref[...].astype(BF16))
    for hd in range(HG_HEADS):
        ks = slice(hd * HG_KEY_DIM, (hd + 1) * HG_KEY_DIM)
        a = jnp.zeros((C, C), F32)
        for qt, kt, m in zip(qts, kts, masks):
            a = a + jnp.where(m, _dot_nt(qt[:, ks], kt[:, ks]), 0.0)
        v_h = jnp.where((lane_v >> 6) == hd, vv, 0.0).astype(BF16)
        o = o + _dot(a.astype(BF16), v_h)

    b_last = b[C - 1:C, :]
    khat = (kk * jnp.exp(b_last - b)).astype(BF16)
    st_row = lax.broadcasted_iota(jnp.int32, (HG_WIDTH, KW), 0)
    st_col = lax.broadcasted_iota(jnp.int32, (HG_WIDTH, KW), 1)
    new_st = st_ref[...] * jnp.exp(b_last) + _dot_tn(vv.astype(BF16), khat)
    st_ref[...] = jnp.where((st_row >> 6) == (st_col >> 7), new_st, 0.0)

    gi = lax.broadcasted_iota(jnp.int32, (HG_WIDTH, HG_WIDTH), 0)
    gj = lax.broadcasted_iota(jnp.int32, (HG_WIDTH, HG_WIDTH), 1)
    grp = ((gi >> 6) == (gj >> 6)).astype(BF16)
    ssq = sum(_dot(part, grp) for part in _split3(o * o))
    on = o * lax.rsqrt(ssq * (1.0 / HG_VAL_DIM) + EPS) * on_ref[...]
    o_ref[0] = (on * jax.nn.silu(g_ref[0])).astype(o_ref.dtype)


def _hgrn(hg, lb, out_norm):
    B, S, _ = hg.shape
    C = min(HGRN_CHUNK, S)
    kb = HG_KEY_WIDTH // HG_KEY_WIDTH
    return pl.pallas_call(
        _hgrn_kernel,
        grid=(B, S // C),
        in_specs=[
            pl.BlockSpec((1, C, HG_KEY_WIDTH), lambda b, c: (b, c, 0)),
            pl.BlockSpec((1, C, HG_KEY_WIDTH), lambda b, c: (b, c, kb)),
            pl.BlockSpec((1, C, HG_WIDTH), lambda b, c: (b, c, 2 * HG_KEY_WIDTH // HG_WIDTH)),
            pl.BlockSpec((1, C, HG_WIDTH), lambda b, c: (b, c, 2 * HG_KEY_WIDTH // HG_WIDTH + 1)),
            _const_spec((1, HG_KEY_WIDTH)),
            _const_spec((1, HG_WIDTH)),
        ],
        out_specs=pl.BlockSpec((1, C, HG_WIDTH), lambda b, c: (b, c, 0)),
        out_shape=jax.ShapeDtypeStruct((B, S, HG_WIDTH), BF16),
        scratch_shapes=[pltpu.VMEM((HG_WIDTH, HG_KEY_WIDTH), F32)],
        compiler_params=pltpu.CompilerParams(dimension_semantics=("arbitrary", "arbitrary")),
        name="hgrn2",
    )(hg, hg, hg, hg, lb, out_norm)


def _attn_kernel(q_ref, kt_ref, v_ref, o_ref, s0_ref, s1_ref, acc_ref, *, bk):
    bq = q_ref.shape[2]
    i = pl.program_id(2)
    q = q_ref[0, 0]

    def causal(s, key0):
        qry = lax.broadcasted_iota(jnp.int32, (bq, bk), 0)
        key = key0 + lax.broadcasted_iota(jnp.int32, (bq, bk), 1)
        return jnp.where(key <= qry, s, MASK_VALUE)

    def scores(blk, s_ref, key0=None):
        start = pl.multiple_of(blk * bk, bk)
        s = _dot(q, kt_ref[0, 0, :, pl.ds(start, bk)])
        if key0 is not None:
            s = causal(s, key0)
        s_ref[...] = s
        return jnp.max(s, axis=1, keepdims=True)

    def softmax_pv(blk, s_ref, mx, m, key0=None):
        s = s_ref[...]
        if key0 is not None:
            s = causal(s, key0)
            mx = jnp.max(s, axis=1, keepdims=True)
        m_new = jnp.maximum(m, mx)
        p = jnp.exp2(s - m_new).astype(BF16)
        start = pl.multiple_of(blk * bk, bk)
        acc_ref[...] = jnp.exp2(m - m_new) * acc_ref[...] + _dot(p, v_ref[0, 0, pl.ds(start, bk), :])
        return m_new

    acc_ref[...] = jnp.zeros_like(acc_ref)
    mx0 = scores(0, s0_ref)

    def pair(t, carry):
        m, mx0 = carry
        mx1 = scores(2 * t + 1, s1_ref)
        m = softmax_pv(2 * t, s0_ref, mx0, m)
        mx0 = scores(2 * t + 2, s0_ref)
        return softmax_pv(2 * t + 1, s1_ref, mx1, m), mx0

    m, mx0 = lax.fori_loop(0, i, pair, (jnp.full((bq, 1), -jnp.inf, F32), mx0))
    mx1 = scores(2 * i + 1, s1_ref, bk)
    m = softmax_pv(2 * i, s0_ref, mx0, m, 0)
    m = softmax_pv(2 * i + 1, s1_ref, mx1, m)
    acc = acc_ref[...]
    o_ref[0] = (acc[:, 0:MLA_V] / acc[:, MLA_V:MLA_V + 1]).astype(o_ref.dtype)


def _attention(q, kt, v):
    B, H, S, _ = q.shape
    bk = min(ATTN_BK, S // 2)
    bq = 2 * bk
    return pl.pallas_call(
        functools.partial(_attn_kernel, bk=bk),
        grid=(B, H, S // bq),
        in_specs=[
            pl.BlockSpec((1, 1, bq, QK_PAD), lambda b, h, i: (b, h, i, 0)),
            pl.BlockSpec((1, 1, QK_PAD, S), lambda b, h, i: (b, h, 0, 0)),
            pl.BlockSpec((1, 1, S, V_EXT), lambda b, h, i: (b, h, 0, 0)),
        ],
        out_specs=pl.BlockSpec((1, bq, MLA_V), lambda b, h, i: (b, i, h)),
        out_shape=jax.ShapeDtypeStruct((B, S, H * MLA_V), BF16),
        scratch_shapes=[pltpu.VMEM((bq, bk), F32), pltpu.VMEM((bq, bk), F32), pltpu.VMEM((bq, V_EXT), F32)],
        compiler_params=pltpu.CompilerParams(dimension_semantics=("arbitrary", "arbitrary", "arbitrary")),
        name="mla_attention",
    )(q, kt, v)


def _mix_out_kernel(x_ref, oa_ref, ob_ref, xp_ref, wpool_ref, pscale_ref, wo_ref, fn_ref,
                    xo_ref, h_ref, halo_ref):
    tm = xp_ref.shape[1]
    i = pl.program_id(1)

    @pl.when(i == 0)
    def _():
        halo_ref[...] = jnp.zeros_like(halo_ref)

    xp = xp_ref[0]
    xx = jnp.concatenate([halo_ref[...], xp], axis=0)
    halo_ref[...] = xp[tm - POOL_HALO:, :]

    w2 = xx[1:, :] + xx[:-1, :]
    w4 = w2[2:, :] + w2[:-2, :]
    w8 = w4[4:, :] + w4[:-4, :]
    w16 = w8[8:, :] + w8[:-8, :]
    sums = (w2[POOL_HALO - 1:, :], w4[POOL_HALO - 3:, :], w8[POOL_HALO - 7:, :], w16[POOL_HALO - 15:, :])
    t = i * tm + lax.broadcasted_iota(jnp.int32, (tm, POOL_WIDTH), 0)
    lane = lax.broadcasted_iota(jnp.int32, (tm, POOL_WIDTH), 1)
    pooled = jnp.zeros((tm, POOL_WIDTH), F32)
    for gi, w in enumerate(POOL_WINDOWS):
        cnt = jnp.minimum(t + 1, w).astype(F32)
        pooled = jnp.where((lane >> 6) == gi, sums[gi] / cnt, pooled)
    pooled = pooled - xp
    oc = _dot(pooled.astype(BF16), wpool_ref[...]) * pscale_ref[...]

    y = _dot(oa_ref[0], wo_ref[0:HG_WIDTH, :])
    y = y + _dot(ob_ref[0], wo_ref[HG_WIDTH:HG_WIDTH + MLA_WIDTH, :])
    y = y + _dot(oc.astype(BF16), wo_ref[HG_WIDTH + MLA_WIDTH:, :])
    xn = x_ref[0] + y
    xo_ref[0] = xn
    h_ref[0] = _rms(xn, fn_ref[...]).astype(h_ref.dtype)


def _mix_out(x, oa, ob, xp, wpool_bd, pscale, wo, fnorm, h_dtype):
    B, S, D = x.shape
    tm = min(TOKEN_TILE, S)
    tok = lambda w: pl.BlockSpec((1, tm, w), lambda b, i: (b, i, 0))
    return pl.pallas_call(
        _mix_out_kernel,
        grid=(B, S // tm),
        in_specs=[tok(D), tok(HG_WIDTH), tok(MLA_WIDTH), tok(POOL_WIDTH),
                  _const_spec(wpool_bd.shape), _const_spec((1, POOL_WIDTH)), _const_spec(wo.shape),
                  _const_spec((1, D))],
        out_specs=(tok(D), tok(D)),
        out_shape=(jax.ShapeDtypeStruct((B, S, D), F32), jax.ShapeDtypeStruct((B, S, D), h_dtype)),
        scratch_shapes=[pltpu.VMEM((POOL_HALO, POOL_WIDTH), F32)],
        compiler_params=pltpu.CompilerParams(dimension_semantics=("arbitrary", "arbitrary")),
        name="mix_out",
    )(x, oa, ob, xp, wpool_bd, pscale, wo, fnorm)


def _dense_ffn_kernel(x_ref, h_ref, wg_ref, wu_ref, wd_ref, o_ref):
    h = h_ref[...]
    acc = x_ref[...]
    dff = wg_ref.shape[1]
    for c0 in range(0, dff, FF_CHUNK):
        g = _dot(h, wg_ref[:, c0:c0 + FF_CHUNK])
        u = _dot(h, wu_ref[:, c0:c0 + FF_CHUNK])
        acc = acc + _dot((jax.nn.silu(g) * u).astype(BF16), wd_ref[c0:c0 + FF_CHUNK, :])
    o_ref[...] = acc


def _dense_ffn(x2d, h2d, wg, wu, wd):
    T, D = x2d.shape
    tm = min(TOKEN_TILE, T)
    tok = pl.BlockSpec((tm, D), lambda i: (i, 0))
    return pl.pallas_call(
        _dense_ffn_kernel,
        grid=(T // tm,),
        in_specs=[tok, tok, _const_spec(wg.shape), _const_spec(wu.shape), _const_spec(wd.shape)],
        out_specs=tok,
        out_shape=jax.ShapeDtypeStruct((T, D), F32),
        compiler_params=pltpu.CompilerParams(dimension_semantics=("arbitrary",)),
        name="dense_ffn",
    )(x2d, h2d, wg, wu, wd)


def _router_kernel(h_ref, rt_ref, meta_ref, gate_ref, cnt_ref, run_ref):
    tm = h_ref.shape[0]
    E = N_EXPERTS

    @pl.when(pl.program_id(0) == 0)
    def _():
        run_ref[...] = jnp.zeros_like(run_ref)

    logits = lax.dot_general(rt_ref[...], h_ref[...], (((1,), (1,)), ((), ())),
                             precision=lax.Precision.HIGHEST, preferred_element_type=F32)
    eid = lax.broadcasted_iota(jnp.int32, (E, tm), 0)
    m1 = jnp.max(logits, axis=0, keepdims=True)
    i1 = jnp.min(jnp.where(logits == m1, eid, E), axis=0, keepdims=True)
    rest = jnp.where(eid == i1, -jnp.inf, logits)
    m2 = jnp.max(rest, axis=0, keepdims=True)
    i2 = jnp.min(jnp.where(rest == m2, eid, E), axis=0, keepdims=True)
    e2 = jnp.exp(m2 - m1)
    g1 = 1.0 / (1.0 + e2)
    g2 = e2 / (1.0 + e2)

    sel = ((eid == i1) | (eid == i2))
    r = lax.broadcasted_iota(jnp.int32, (tm, tm), 0)
    c = lax.broadcasted_iota(jnp.int32, (tm, tm), 1)
    before = (r < c).astype(BF16)
    excl = _dot(sel.astype(BF16), before) + run_ref[:, 0:1]
    rank1 = jnp.sum(jnp.where(eid == i1, excl, 0.0), axis=0, keepdims=True).astype(jnp.int32)
    rank2 = jnp.sum(jnp.where(eid == i2, excl, 0.0), axis=0, keepdims=True).astype(jnp.int32)
    run_ref[...] = run_ref[...] + jnp.sum(sel.astype(F32), axis=1, keepdims=True)
    cnt_ref[...] = run_ref[...].astype(jnp.int32)

    zi = jnp.zeros((1, tm), jnp.int32)
    meta_ref[0] = jnp.concatenate([i1, i2, rank1, rank2, zi, zi, zi, zi], axis=0)
    zf = jnp.zeros((1, tm), F32)
    gate_ref[0] = jnp.concatenate([g1, g2, zf, zf, zf, zf, zf, zf], axis=0)


def _router(h2d, router_t):
    T, D = h2d.shape
    tm = min(ROUTE_TILE, T)
    nt = T // tm
    return pl.pallas_call(
        _router_kernel,
        grid=(nt,),
        in_specs=[pl.BlockSpec((tm, D), lambda i: (i, 0)), _const_spec(router_t.shape)],
        out_specs=(pl.BlockSpec((1, SUBLANES, tm), lambda i: (i, 0, 0)),
                   pl.BlockSpec((1, SUBLANES, tm), lambda i: (i, 0, 0)),
                   _const_spec((N_EXPERTS, LANES))),
        out_shape=(jax.ShapeDtypeStruct((nt, SUBLANES, tm), jnp.int32),
                   jax.ShapeDtypeStruct((nt, SUBLANES, tm), F32),
                   jax.ShapeDtypeStruct((N_EXPERTS, LANES), jnp.int32)),
        scratch_shapes=[pltpu.VMEM((N_EXPERTS, LANES), F32)],
        compiler_params=pltpu.CompilerParams(dimension_semantics=("arbitrary",)),
        name="moe_router",
    )(h2d, router_t)


def _dest_kernel(start_ref, meta_ref, dst_ref):
    meta = meta_ref[0]
    rows = []
    for k in range(2):
        e = meta[k:k + 1, :]
        base = jnp.zeros_like(e)
        for ex in range(N_EXPERTS):
            base = jnp.where(e == ex, start_ref[ex], base)
        rows.append(base + meta[2 + k:3 + k, :])
    dst_ref[0] = jnp.concatenate(rows + [jnp.zeros_like(rows[0])] * (SUBLANES - 2), axis=0)


def _dest_rows(starts, meta):
    nt, _, tm = meta.shape
    spec = pl.BlockSpec((1, SUBLANES, tm), lambda i, s: (i, 0, 0))
    return pl.pallas_call(
        _dest_kernel,
        grid_spec=pltpu.PrefetchScalarGridSpec(num_scalar_prefetch=1, grid=(nt,), in_specs=[spec], out_specs=spec),
        out_shape=jax.ShapeDtypeStruct(meta.shape, jnp.int32),
        compiler_params=pltpu.CompilerParams(dimension_semantics=("arbitrary",)),
        name="moe_dest",
    )(starts, meta)


def _scatter_kernel(dst_ref, h_ref, xs_in_hbm, xs_hbm, sem):
    del xs_in_hbm
    tm = dst_ref.shape[2]

    def row_copy(src_row, dst_row):
        return pltpu.make_async_copy(h_ref.at[pl.ds(src_row, 1), :], xs_hbm.at[pl.ds(dst_row, 1), :], sem)

    def issue(t, _):
        for k in range(2):
            row_copy(t, dst_ref[0, k, t]).start(priority=k)
        return 0

    lax.fori_loop(0, tm, issue, 0, unroll=DMA_UNROLL)

    def drain(t, _):
        row_copy(0, 0).wait()
        row_copy(0, 0).wait()
        return 0

    lax.fori_loop(0, tm, drain, 0, unroll=DMA_UNROLL)


def _scatter_rows(dst, h2d, xs_init):
    T, D = h2d.shape
    nt, _, tm = dst.shape
    return pl.pallas_call(
        _scatter_kernel,
        grid=(nt,),
        in_specs=[pl.BlockSpec((1, SUBLANES, tm), lambda i: (i, 0, 0), memory_space=pltpu.SMEM),
                  pl.BlockSpec((tm, D), lambda i: (i, 0)),
                  pl.BlockSpec(memory_space=pl.ANY)],
        out_specs=pl.BlockSpec(memory_space=pl.ANY),
        scratch_shapes=[pltpu.SemaphoreType.DMA(())],
        out_shape=jax.ShapeDtypeStruct(xs_init.shape, xs_init.dtype),
        input_output_aliases={2: 0},
        compiler_params=pltpu.CompilerParams(dimension_semantics=("arbitrary",), has_side_effects=True),
        name="moe_scatter",
    )(dst, h2d, xs_init)


def _expert_kernel(te_ref, nu_ref, x_ref, wg_ref, wu_ref, wd_ref, y_ref, xb_ref, acc_ref):
    m = pl.program_id(0)
    c = pl.program_id(1)
    nc = pl.num_programs(1)

    @pl.when(m < nu_ref[0])
    def _():
        @pl.when(c == 0)
        def _():
            xb_ref[...] = x_ref[...].astype(BF16)
            acc_ref[...] = jnp.zeros_like(acc_ref)

        xb = xb_ref[...]
        g = _dot(xb, wg_ref[0])
        u = _dot(xb, wu_ref[0])
        acc_ref[...] += _dot((jax.nn.silu(g) * u).astype(BF16), wd_ref[0])

        @pl.when(c == nc - 1)
        def _():
            y_ref[...] = acc_ref[...]

    @pl.when((m >= nu_ref[0]) & (c == nc - 1))
    def _():
        y_ref[...] = jnp.zeros_like(y_ref)


def _expert_ffn(tile_expert, n_used, xs, wg, wu, wd):
    R, D = xs.shape
    E, _, F = wg.shape
    tr = MOE_TILE
    n_tiles = R // tr
    fc = min(MOE_FF_CHUNK, F)
    ncs = F // fc

    def row_map(m, c, te, nu):
        return (jnp.minimum(m, nu[0] - 1), 0)

    def col(m, c, nu):
        return jnp.where(m < nu[0], c, ncs - 1)

    def exp(m, te, nu):
        return te[jnp.minimum(m, nu[0] - 1)]

    grid_spec = pltpu.PrefetchScalarGridSpec(
        num_scalar_prefetch=2,
        grid=(n_tiles, ncs),
        in_specs=[
            pl.BlockSpec((tr, D), row_map),
            pl.BlockSpec((1, D, fc), lambda m, c, te, nu: (exp(m, te, nu), 0, col(m, c, nu))),
            pl.BlockSpec((1, D, fc), lambda m, c, te, nu: (exp(m, te, nu), 0, col(m, c, nu))),
            pl.BlockSpec((1, fc, D), lambda m, c, te, nu: (exp(m, te, nu), col(m, c, nu), 0)),
        ],
        out_specs=pl.BlockSpec((tr, D), lambda m, c, te, nu: (m, 0)),
        scratch_shapes=[pltpu.VMEM((tr, D), BF16), pltpu.VMEM((tr, D), F32)],
    )
    return pl.pallas_call(
        _expert_kernel,
        grid_spec=grid_spec,
        out_shape=jax.ShapeDtypeStruct((R, D), F32),
        compiler_params=pltpu.CompilerParams(dimension_semantics=("arbitrary", "arbitrary")),
        name="moe_experts",
    )(tile_expert, n_used, xs, wg, wu, wd)


def _combine_kernel(dcur_ref, dnxt_ref, x_ref, gate_ref, fn_ref, ys_hbm, o_ref, buf_ref, sems, *, final_norm):
    tm = x_ref.shape[0]
    i = pl.program_id(0)
    slot = i % 2

    def row_copy(src_row, s, k, t):
        return pltpu.make_async_copy(ys_hbm.at[pl.ds(src_row, 1), :], buf_ref.at[s, k, pl.ds(t, 1), :], sems.at[s])

    def issue_tile(d_ref, s):
        def issue(t, _):
            for k in range(2):
                row_copy(d_ref[0, k, t], s, k, t).start(priority=k)
            return 0

        lax.fori_loop(0, tm, issue, 0, unroll=DMA_UNROLL)

    @pl.when(i == 0)
    def _():
        issue_tile(dcur_ref, 0)

    @pl.when(i + 1 < pl.num_programs(0))
    def _():
        issue_tile(dnxt_ref, 1 - slot)

    def drain(t, _):
        row_copy(0, slot, 0, 0).wait()
        row_copy(0, slot, 1, 0).wait()
        return 0

    lax.fori_loop(0, tm, drain, 0, unroll=DMA_UNROLL)
    g = gate_ref[...]
    xn = x_ref[...] + g[:, 0:1] * buf_ref[slot, 0] + g[:, 1:2] * buf_ref[slot, 1]
    o_ref[...] = _rms(xn, fn_ref[...]) if final_norm else xn


def _combine(dst_c, x2d, gates_col, fnorm, ys, final_norm):
    T, D = x2d.shape
    nt, _, tm = dst_c.shape
    return pl.pallas_call(
        functools.partial(_combine_kernel, final_norm=final_norm),
        grid=(nt,),
        in_specs=[pl.BlockSpec((1, SUBLANES, tm), lambda i: (i, 0, 0), memory_space=pltpu.SMEM),
                  pl.BlockSpec((1, SUBLANES, tm), lambda i: (jnp.minimum(i + 1, nt - 1), 0, 0),
                               memory_space=pltpu.SMEM),
                  pl.BlockSpec((tm, D), lambda i: (i, 0)),
                  pl.BlockSpec((tm, SUBLANES), lambda i: (i, 0)),
                  pl.BlockSpec((1, D), lambda i: (0, 0)),
                  pl.BlockSpec(memory_space=pl.ANY)],
        out_specs=pl.BlockSpec((tm, D), lambda i: (i, 0)),
        scratch_shapes=[pltpu.VMEM((2, 2, tm, D), F32), pltpu.SemaphoreType.DMA((2,))],
        out_shape=jax.ShapeDtypeStruct((T, D), F32),
        compiler_params=pltpu.CompilerParams(dimension_semantics=("arbitrary",)),
        name="moe_combine",
    )(dst_c, dst_c, x2d, gates_col, fnorm, ys)


def _swap_halves_cols(w):
    half = w.shape[-1] // 2
    return jnp.concatenate([w[..., half:], w[..., :half]], axis=-1)


def _pad_cols(w, width):
    return jnp.pad(w, [(0, 0)] * (w.ndim - 1) + [(0, width - w.shape[-1])])


def _prep_w_in(w):
    kpe0 = _C_CKV + MLA_KV_RANK - 0
    kpe = w[:, kpe0:kpe0 + MLA_ROPE]
    return jnp.concatenate([
        w[:, :kpe0],
        _pad_cols(kpe, LANES), _pad_cols(_swap_halves_cols(kpe), LANES),
        w[:, kpe0 + MLA_ROPE:],
    ], axis=1).astype(BF16)


def _prep_w_uq(w):
    w = w.reshape(MLA_Q_RANK, MLA_HEADS, MLA_NOPE + MLA_ROPE)
    nope, pe = w[..., :MLA_NOPE], w[..., MLA_NOPE:]
    out = jnp.concatenate([nope, _pad_cols(pe, LANES), _pad_cols(_swap_halves_cols(pe), LANES)], axis=-1)
    return out.reshape(MLA_Q_RANK, MLA_HEADS * _Q_HEAD_COLS).astype(BF16)


def _prep_w_ukv(w):
    w = w.reshape(MLA_KV_RANK, MLA_HEADS, MLA_NOPE + MLA_V)
    wukt = jnp.transpose(w[..., :MLA_NOPE], (1, 2, 0)).reshape(MLA_HEADS * MLA_NOPE, MLA_KV_RANK)
    wuv = w[..., MLA_NOPE:].reshape(MLA_KV_RANK, MLA_HEADS * MLA_V)
    return wukt.astype(BF16), wuv.astype(BF16)


def _prep_w_kpe_t(w_ext):
    return w_ext[:, _C_KPE:_C_XP].T


def _rope_tables(seq):
    pos = jnp.arange(seq, dtype=F32)
    inv_freq = 1.0 / (ROPE_THETA ** (jnp.arange(0, MLA_ROPE, 2, dtype=F32) / MLA_ROPE))
    ang = pos[:, None] * inv_freq[None, :]
    cos, sin = jnp.cos(ang), jnp.sin(ang)
    cpad = _pad_cols(jnp.concatenate([cos, cos], axis=-1), LANES)
    spad = _pad_cols(jnp.concatenate([-sin, sin], axis=-1), LANES)
    return cpad, spad


def _block_diag_pool(w):
    G, c, _ = w.shape
    eye = jnp.eye(G, dtype=w.dtype)
    return (eye[:, None, :, None] * w[:, :, None, :]).reshape(G * c, G * c).astype(BF16)


def _moe_layout(counts, n_tiles, tile):
    tiles_per = (counts + tile - 1) // tile
    ends = jnp.cumsum(tiles_per)
    starts = (ends - tiles_per) * tile
    tile_expert = jnp.sum((jnp.arange(n_tiles)[:, None] >= ends[None, :]).astype(jnp.int32), axis=1)
    tile_expert = jnp.minimum(tile_expert, N_EXPERTS - 1)
    return starts.astype(jnp.int32), tile_expert.astype(jnp.int32), ends[-1:].astype(jnp.int32)


def kernel(x, attn_norm, w_in, hgrn_lower_bounds, hgrn_out_norm, mla_q_norm, mla_w_uq, mla_kv_norm,
           mla_w_ukv, pool_w, pool_scale, w_o, ffn_norm, dense_w_gate, dense_w_up, dense_w_down,
           moe_router, moe_w_gate, moe_w_up, moe_w_down, final_norm):
    B, S, D = x.shape
    T = B * S
    depth = w_in.shape[0]
    cpad, spad = _rope_tables(S)
    p_lb = jax.nn.softmax(hgrn_lower_bounds.astype(F32), axis=0)
    lbs = jnp.cumsum(p_lb, axis=0) - p_lb[0:1]

    assert depth % 2 == 0, "the final RMSNorm is fused into the last (MoE) layer's combine kernel"
    for l in range(depth):
        wukt, wuv = _prep_w_ukv(mla_w_ukv[l])
        w_ext = _prep_w_in(w_in[l])
        hg, xp, q, kt, v = _in_proj(
            x, attn_norm[l][None], w_ext, mla_q_norm[l][None], _prep_w_uq(mla_w_uq[l]),
            mla_kv_norm[l][None], wukt, wuv, _prep_w_kpe_t(w_ext), cpad, spad)
        o_a = _hgrn(hg, lbs[l][None], hgrn_out_norm[l][None])
        o_b = _attention(q, kt, v)
        moe_layer = (l % 2 == 1)
        x, h = _mix_out(x, o_a, o_b, xp, _block_diag_pool(pool_w[l]), pool_scale[l][None],
                        w_o[l].astype(BF16), ffn_norm[l][None], F32 if moe_layer else BF16)
        j = l // 2
        if not moe_layer:
            x = _dense_ffn(x.reshape(T, D), h.reshape(T, D), dense_w_gate[j].astype(BF16),
                           dense_w_up[j].astype(BF16), dense_w_down[j].astype(BF16)).reshape(B, S, D)
        else:
            h2d = h.reshape(T, D)
            meta, gates, counts = _router(h2d, moe_router[j].T)
            n_tiles = (2 * T) // MOE_TILE + N_EXPERTS
            starts, tile_expert, n_used = _moe_layout(counts[:, 0], n_tiles, MOE_TILE)
            dst = _dest_rows(starts, meta)
            xs = _scatter_rows(dst, h2d, jnp.zeros((n_tiles * MOE_TILE, D), F32))
            ys = _expert_ffn(tile_expert, n_used, xs, moe_w_gate[j].astype(BF16), moe_w_up[j].astype(BF16),
                             moe_w_down[j].astype(BF16))
            ct = min(COMBINE_TILE, T)
            dst_c = dst.transpose(1, 0, 2).reshape(SUBLANES, T // ct, ct).transpose(1, 0, 2)
            gates_col = gates.transpose(0, 2, 1).reshape(T, SUBLANES)
            last = (l == depth - 1)
            y = _combine(dst_c, x.reshape(T, D), gates_col, final_norm[None], ys, last)
            x = y.reshape(B, S, D)
    return x
```

```python
import functools
import math

import jax
import jax.numpy as jnp
import numpy as np
from jax import lax
from jax.experimental import pallas as pl
from jax.experimental.pallas import tpu as pltpu

F32 = jnp.float32
BF16 = jnp.bfloat16

HG_HEADS = 4
HG_KEY_DIM = 128
HG_VAL_DIM = 64
HG_KEY_WIDTH = HG_HEADS * HG_KEY_DIM
HG_WIDTH = HG_HEADS * HG_VAL_DIM
MIN_FORGET = 1e-20
MLA_HEADS = 4
MLA_Q_RANK = 256
MLA_KV_RANK = 128
MLA_NOPE = 128
MLA_ROPE = 64
MLA_V = 128
MLA_WIDTH = MLA_HEADS * MLA_V
ROPE_THETA = 10000.0
MASK_VALUE = -1e30
POOL_GROUPS = 4
POOL_WINDOWS = (2, 4, 8, 16)
POOL_WIDTH = 256
POOL_GROUP_DIM = POOL_WIDTH // POOL_GROUPS
N_EXPERTS = 8
EPS = 1e-6

LANES = 128
SUBLANES = 8
QK_PAD = 256
V_EXT = 256

TOKEN_TILE = 1024
HGRN_CHUNK = 256
ATTN_BK = 512
FF_CHUNK = 512
MOE_FF_CHUNK = 512
MOE_TILE = 1024
ROUTE_TILE = 512
COMBINE_TILE = 256
POOL_HALO = 16
DMA_UNROLL = 8

_C_HG = 0
_C_CQ = 2 * HG_KEY_WIDTH + 2 * HG_WIDTH
_C_CKV = _C_CQ + MLA_Q_RANK
_C_KPE = _C_CKV + MLA_KV_RANK
_C_KPES = _C_KPE + LANES
_C_XP = _C_KPES + LANES
_C_END = _C_XP + POOL_WIDTH
_Q_HEAD_COLS = 3 * LANES


def _rms(x, g):
    return x * lax.rsqrt(jnp.mean(x * x, axis=-1, keepdims=True) + EPS) * g


def _dot(a, b):
    return jnp.dot(a, b, preferred_element_type=F32)


def _dot_nt(a, b):
    return lax.dot_general(a, b, (((1,), (1,)), ((), ())), preferred_element_type=F32)


def _dot_tn(a, b):
    return lax.dot_general(a, b, (((0,), (0,)), ((), ())), preferred_element_type=F32)


def _const_spec(shape):
    nd = len(shape)
    return pl.BlockSpec(shape, lambda *_: (0,) * nd)


def _in_proj_kernel(x_ref, g_ref, w_ref, qn_ref, wuq_ref, kvn_ref, wukt_ref, wuv_ref, wkpet_ref,
                    cpad_ref, spad_ref, cpadt_ref, spadt_ref,
                    hg_ref, xp_ref, q_ref, kt_ref, v_ref):
    h = _rms(x_ref[0], g_ref[...]).astype(BF16)
    hg_ref[0] = _dot(h, w_ref[:, _C_HG:_C_CQ])
    xp_ref[0] = _dot(h, w_ref[:, _C_XP:_C_END])
    cpad = cpad_ref[...]
    spad = spad_ref[...]
    scale = (MLA_NOPE + MLA_ROPE) ** -0.5 * math.log2(math.e)

    cq = _dot(h, w_ref[:, _C_CQ:_C_CKV])
    cqn = _rms(cq, qn_ref[...]).astype(BF16)
    for hd in range(MLA_HEADS):
        qh = _dot(cqn, wuq_ref[:, hd * _Q_HEAD_COLS:(hd + 1) * _Q_HEAD_COLS])
        q_ref[0, hd, :, 0:LANES] = (qh[:, 0:LANES] * scale).astype(BF16)
        pe = qh[:, LANES:2 * LANES] * cpad + qh[:, 2 * LANES:3 * LANES] * spad
        q_ref[0, hd, :, LANES:QK_PAD] = (pe * scale).astype(BF16)

    ckv = _dot(h, w_ref[:, _C_CKV:_C_KPE])
    ckvn = _rms(ckv, kvn_ref[...]).astype(BF16)
    kpet = (_dot_nt(wkpet_ref[0:LANES, :], h) * cpadt_ref[...]
            + _dot_nt(wkpet_ref[LANES:2 * LANES, :], h) * spadt_ref[...]).astype(BF16)
    tm = ckvn.shape[0]
    ones_col = (lax.broadcasted_iota(jnp.int32, (tm, LANES), 1) == 0).astype(BF16)
    for hd in range(MLA_HEADS):
        kt_ref[0, hd, 0:LANES, :] = _dot_nt(wukt_ref[hd * LANES:(hd + 1) * LANES, :], ckvn).astype(BF16)
        kt_ref[0, hd, LANES:QK_PAD, :] = kpet
        v_ref[0, hd, :, 0:MLA_V] = _dot(ckvn, wuv_ref[:, hd * MLA_V:(hd + 1) * MLA_V]).astype(BF16)
        v_ref[0, hd, :, MLA_V:V_EXT] = ones_col


def _in_proj(x, g, w_ext, qn, wuq_ext, kvn, wukt, wuv, wkpet, cpad, spad):
    B, S, D = x.shape
    tm = min(TOKEN_TILE, S)
    grid = (B, S // tm)
    n_hg = _C_CQ
    out_shape = (
        jax.ShapeDtypeStruct((B, S, n_hg), F32),
        jax.ShapeDtypeStruct((B, S, POOL_WIDTH), F32),
        jax.ShapeDtypeStruct((B, MLA_HEADS, S, QK_PAD), BF16),
        jax.ShapeDtypeStruct((B, MLA_HEADS, QK_PAD, S), BF16),
        jax.ShapeDtypeStruct((B, MLA_HEADS, S, V_EXT), BF16),
    )
    return pl.pallas_call(
        _in_proj_kernel,
        grid=grid,
        in_specs=[
            pl.BlockSpec((1, tm, D), lambda b, i: (b, i, 0)),
            _const_spec((1, D)),
            _const_spec(w_ext.shape),
            _const_spec((1, MLA_Q_RANK)),
            _const_spec(wuq_ext.shape),
            _const_spec((1, MLA_KV_RANK)),
            _const_spec(wukt.shape),
            _const_spec(wuv.shape),
            _const_spec(wkpet.shape),
            pl.BlockSpec((tm, LANES), lambda b, i: (i, 0)),
            pl.BlockSpec((tm, LANES), lambda b, i: (i, 0)),
            pl.BlockSpec((LANES, tm), lambda b, i: (0, i)),
            pl.BlockSpec((LANES, tm), lambda b, i: (0, i)),
        ],
        out_specs=(
            pl.BlockSpec((1, tm, n_hg), lambda b, i: (b, i, 0)),
            pl.BlockSpec((1, tm, POOL_WIDTH), lambda b, i: (b, i, 0)),
            pl.BlockSpec((1, MLA_HEADS, tm, QK_PAD), lambda b, i: (b, 0, i, 0)),
            pl.BlockSpec((1, MLA_HEADS, QK_PAD, tm), lambda b, i: (b, 0, 0, i)),
            pl.BlockSpec((1, MLA_HEADS, tm, V_EXT), lambda b, i: (b, 0, i, 0)),
        ),
        out_shape=out_shape,
        compiler_params=pltpu.CompilerParams(dimension_semantics=("arbitrary", "arbitrary")),
        name="in_proj",
    )(x, g, w_ext, qn, wuq_ext, kvn, wukt, wuv, wkpet, cpad, spad, cpad.T, spad.T)


def _split3(x):
    hi = x.astype(BF16)
    r1 = x - hi.astype(F32)
    mid = r1.astype(BF16)
    lo = (r1 - mid.astype(F32)).astype(BF16)
    return hi, mid, lo


def _hgrn_kernel(q_ref, f_ref, i_ref, g_ref, lb_ref, on_ref, o_ref, st_ref):
    C = q_ref.shape[1]
    KW = HG_KEY_WIDTH

    @pl.when(pl.program_id(1) == 0)
    def _():
        st_ref[...] = jnp.zeros_like(st_ref)

    lb = lb_ref[...]
    z = f_ref[0]
    forget = lb + (1.0 - lb) * jax.nn.sigmoid(z)
    lg = jnp.log2(jnp.maximum(forget, MIN_FORGET))
    kk = (1.0 - lb) * jax.nn.sigmoid(-z)
    qq = jax.nn.silu(q_ref[0])
    vv = i_ref[0]

    row = lax.broadcasted_iota(jnp.int32, (C, C), 0)
    col = lax.broadcasted_iota(jnp.int32, (C, C), 1)
    tril = (col <= row).astype(BF16)
    b = sum(_dot(tril, part) for part in _split3(lg))

    sides, masks = [], []
    sub = lax.broadcasted_iota(jnp.int32, (C, KW), 0)
    half = C // 2
    while half >= 4:
        blk = 2 * half
        b3 = b.reshape(C // blk, blk, KW)
        ref_row = jnp.broadcast_to(b3[:, half - 1:half, :], b3.shape).reshape(C, KW)
        e = jnp.exp2(-jnp.abs(b - ref_row))
        x = (jnp.where((sub & half) != 0, qq, kk) * e).astype(BF16)
        sides.append((x, x))
        shift = int(math.log2(blk))
        masks.append(((row >> shift) == (col >> shift)) & ((row & half) != 0) & ((col & half) == 0))
        half //= 2
    b3 = b.reshape(C // 8, 8, KW)
    mid_lo = 0.5 * (b3[:, 0:1, :] + b3[:, 3:4, :])
    mid_hi = 0.5 * (b3[:, 4:5, :] + b3[:, 7:8, :])
    sub8 = lax.broadcasted_iota(jnp.int32, b3.shape, 1)
    mid = jnp.where(sub8 < 4, mid_lo, mid_hi).reshape(C, KW)
    sides.append(((qq * jnp.exp2(b - mid)).astype(BF16), (kk * jnp.exp2(mid - b)).astype(BF16)))
    masks.append(((row >> 2) == (col >> 2)) & (col <= row))

    lane_v = lax.broadcasted_iota(jnp.int32, (C, HG_WIDTH), 1)
    o = _dot_nt((qq * jnp.exp2(b)).astype(BF16), st_ref[...].astype(BF16))
    for hd in range(HG_HEADS):
        ks = slice(hd * HG_KEY_DIM, (hd + 1) * HG_KEY_DIM)
        a = jnp.zeros((C, C), F32)
        for (qt, kt), m in zip(sides, masks):
            a = jnp.where(m, _dot_nt(qt[:, ks], kt[:, ks]), a)
        v_h = jnp.where((lane_v >> 6) == hd, vv, 0.0).astype(BF16)
        o = o + _dot(a.astype(BF16), v_h)

    b_last = b[C - 1:C, :]
    khat = (kk * jnp.exp2(b_last - b)).astype(BF16)
    st_row = lax.broadcasted_iota(jnp.int32, (HG_WIDTH, KW), 0)
    st_col = lax.broadcasted_iota(jnp.int32, (HG_WIDTH, KW), 1)
    new_st = st_ref[...] * jnp.exp2(b_last) + _dot_tn(vv.astype(BF16), khat)
    st_ref[...] = jnp.where((st_row >> 6) == (st_col >> 7), new_st, 0.0)

    gi = lax.broadcasted_iota(jnp.int32, (HG_WIDTH, HG_WIDTH), 0)
    gj = lax.broadcasted_iota(jnp.int32, (HG_WIDTH, HG_WIDTH), 1)
    grp = ((gi >> 6) == (gj >> 6)).astype(BF16)
    ssq = sum(_dot(part, grp) for part in _split3(o * o))
    on = o * lax.rsqrt(ssq * (1.0 / HG_VAL_DIM) + EPS) * on_ref[...]
    o_ref[0] = (on * jax.nn.silu(g_ref[0])).astype(o_ref.dtype)


def _hgrn(hg, lb, out_norm):
    B, S, _ = hg.shape
    C = min(HGRN_CHUNK, S)
    kb = HG_KEY_WIDTH // HG_KEY_WIDTH
    return pl.pallas_call(
        _hgrn_kernel,
        grid=(B, S // C),
        in_specs=[
            pl.BlockSpec((1, C, HG_KEY_WIDTH), lambda b, c: (b, c, 0)),
            pl.BlockSpec((1, C, HG_KEY_WIDTH), lambda b, c: (b, c, kb)),
            pl.BlockSpec((1, C, HG_WIDTH), lambda b, c: (b, c, 2 * HG_KEY_WIDTH // HG_WIDTH)),
            pl.BlockSpec((1, C, HG_WIDTH), lambda b, c: (b, c, 2 * HG_KEY_WIDTH // HG_WIDTH + 1)),
            _const_spec((1, HG_KEY_WIDTH)),
            _const_spec((1, HG_WIDTH)),
        ],
        out_specs=pl.BlockSpec((1, C, HG_WIDTH), lambda b, c: (b, c, 0)),
        out_shape=jax.ShapeDtypeStruct((B, S, HG_WIDTH), BF16),
        scratch_shapes=[pltpu.VMEM((HG_WIDTH, HG_KEY_WIDTH), F32)],
        compiler_params=pltpu.CompilerParams(dimension_semantics=("arbitrary", "arbitrary")),
        name="hgrn2",
    )(hg, hg, hg, hg, lb, out_norm)


def _attn_kernel(q_ref, kt_ref, v_ref, o_ref, s0_ref, s1_ref, acc_ref, *, bk):
    bq = q_ref.shape[2]
    i = pl.program_id(2)
    q = q_ref[0, 0]

    def causal(s, key0):
        qry = lax.broadcasted_iota(jnp.int32, (bq, bk), 0)
        key = key0 + lax.broadcasted_iota(jnp.int32, (bq, bk), 1)
        return jnp.where(key <= qry, s, MASK_VALUE)

    def scores(blk, s_ref):
        start = pl.multiple_of(blk * bk, bk)
        s_ref[...] = _dot(q, kt_ref[0, 0, :, pl.ds(start, bk)])

    def softmax_pv(blk, s_ref, m, key0=None):
        s = s_ref[...]
        if key0 is not None:
            s = causal(s, key0)
        m_new = jnp.maximum(m, jnp.max(s, axis=1, keepdims=True))
        p = jnp.exp2(s - m_new).astype(BF16)
        start = pl.multiple_of(blk * bk, bk)
        acc_ref[...] = jnp.exp2(m - m_new) * acc_ref[...] + _dot(p, v_ref[0, 0, pl.ds(start, bk), :])
        return m_new

    acc_ref[...] = jnp.zeros_like(acc_ref)
    scores(0, s0_ref)

    def pair(t, m):
        scores(2 * t + 1, s1_ref)
        m = softmax_pv(2 * t, s0_ref, m)
        scores(2 * t + 2, s0_ref)
        return softmax_pv(2 * t + 1, s1_ref, m)

    m = lax.fori_loop(0, i, pair, jnp.full((bq, 1), -jnp.inf, F32))
    scores(2 * i + 1, s1_ref)
    m = softmax_pv(2 * i, s0_ref, m, 0)
    m = softmax_pv(2 * i + 1, s1_ref, m, bk)
    acc = acc_ref[...]
    o_ref[0] = (acc[:, 0:MLA_V] / acc[:, MLA_V:MLA_V + 1]).astype(o_ref.dtype)


def _attention(q, kt, v):
    B, H, S, _ = q.shape
    bk = min(ATTN_BK, S // 2)
    bq = 2 * bk
    return pl.pallas_call(
        functools.partial(_attn_kernel, bk=bk),
        grid=(B, H, S // bq),
        in_specs=[
            pl.BlockSpec((1, 1, bq, QK_PAD), lambda b, h, i: (b, h, i, 0)),
            pl.BlockSpec((1, 1, QK_PAD, S), lambda b, h, i: (b, h, 0, 0)),
            pl.BlockSpec((1, 1, S, V_EXT), lambda b, h, i: (b, h, 0, 0)),
        ],
        out_specs=pl.BlockSpec((1, bq, MLA_V), lambda b, h, i: (b, i, h)),
        out_shape=jax.ShapeDtypeStruct((B, S, H * MLA_V), BF16),
        scratch_shapes=[pltpu.VMEM((bq, bk), F32), pltpu.VMEM((bq, bk), F32), pltpu.VMEM((bq, V_EXT), F32)],
        compiler_params=pltpu.CompilerParams(dimension_semantics=("arbitrary", "arbitrary", "arbitrary")),
        name="mla_attention",
    )(q, kt, v)


def _mix_out_kernel(x_ref, oa_ref, ob_ref, xp_ref, wpool_ref, pscale_ref, wo_ref, fn_ref,
                    xo_ref, h_ref, halo_ref):
    tm = xp_ref.shape[1]
    i = pl.program_id(1)

    @pl.when(i == 0)
    def _():
        halo_ref[...] = jnp.zeros_like(halo_ref)

    xp = xp_ref[0]
    xx = jnp.concatenate([halo_ref[...], xp], axis=0)
    halo_ref[...] = xp[tm - POOL_HALO:, :]

    w2 = xx[1:, :] + xx[:-1, :]
    w4 = w2[2:, :] + w2[:-2, :]
    w8 = w4[4:, :] + w4[:-4, :]
    w16 = w8[8:, :] + w8[:-8, :]
    sums = (w2[POOL_HALO - 1:, :], w4[POOL_HALO - 3:, :], w8[POOL_HALO - 7:, :], w16[POOL_HALO - 15:, :])
    t = i * tm + lax.broadcasted_iota(jnp.int32, (tm, POOL_WIDTH), 0)
    lane = lax.broadcasted_iota(jnp.int32, (tm, POOL_WIDTH), 1)
    pooled = jnp.zeros((tm, POOL_WIDTH), F32)
    for gi, w in enumerate(POOL_WINDOWS):
        cnt = jnp.minimum(t + 1, w).astype(F32)
        pooled = jnp.where((lane >> 6) == gi, sums[gi] / cnt, pooled)
    pooled = pooled - xp
    oc = _dot(pooled.astype(BF16), wpool_ref[...]) * pscale_ref[...]

    y = _dot(oa_ref[0], wo_ref[0:HG_WIDTH, :])
    y = y + _dot(ob_ref[0], wo_ref[HG_WIDTH:HG_WIDTH + MLA_WIDTH, :])
    y = y + _dot(oc.astype(BF16), wo_ref[HG_WIDTH + MLA_WIDTH:, :])
    xn = x_ref[0] + y
    xo_ref[0] = xn
    h_ref[0] = _rms(xn, fn_ref[...]).astype(h_ref.dtype)


def _mix_out(x, oa, ob, xp, wpool_bd, pscale, wo, fnorm, h_dtype):
    B, S, D = x.shape
    tm = min(TOKEN_TILE, S)
    tok = lambda w: pl.BlockSpec((1, tm, w), lambda b, i: (b, i, 0))
    return pl.pallas_call(
        _mix_out_kernel,
        grid=(B, S // tm),
        in_specs=[tok(D), tok(HG_WIDTH), tok(MLA_WIDTH), tok(POOL_WIDTH),
                  _const_spec(wpool_bd.shape), _const_spec((1, POOL_WIDTH)), _const_spec(wo.shape),
                  _const_spec((1, D))],
        out_specs=(tok(D), tok(D)),
        out_shape=(jax.ShapeDtypeStruct((B, S, D), F32), jax.ShapeDtypeStruct((B, S, D), h_dtype)),
        scratch_shapes=[pltpu.VMEM((POOL_HALO, POOL_WIDTH), F32)],
        compiler_params=pltpu.CompilerParams(dimension_semantics=("arbitrary", "arbitrary")),
        name="mix_out",
    )(x, oa, ob, xp, wpool_bd, pscale, wo, fnorm)


def _dense_ffn_kernel(x_ref, h_ref, wg_ref, wu_ref, wd_ref, o_ref):
    h = h_ref[...]
    acc = x_ref[...]
    dff = wg_ref.shape[1]
    for c0 in range(0, dff, FF_CHUNK):
        g = _dot(h, wg_ref[:, c0:c0 + FF_CHUNK])
        u = _dot(h, wu_ref[:, c0:c0 + FF_CHUNK])
        acc = acc + _dot((jax.nn.silu(g) * u).astype(BF16), wd_ref[c0:c0 + FF_CHUNK, :])
    o_ref[...] = acc


def _dense_ffn(x2d, h2d, wg, wu, wd):
    T, D = x2d.shape
    tm = min(TOKEN_TILE, T)
    tok = pl.BlockSpec((tm, D), lambda i: (i, 0))
    return pl.pallas_call(
        _dense_ffn_kernel,
        grid=(T // tm,),
        in_specs=[tok, tok, _const_spec(wg.shape), _const_spec(wu.shape), _const_spec(wd.shape)],
        out_specs=tok,
        out_shape=jax.ShapeDtypeStruct((T, D), F32),
        compiler_params=pltpu.CompilerParams(dimension_semantics=("arbitrary",)),
        name="dense_ffn",
    )(x2d, h2d, wg, wu, wd)


def _router_kernel(h_ref, rt_ref, meta_ref, gate_ref, cnt_ref, run_ref):
    tm = h_ref.shape[0]
    E = N_EXPERTS

    @pl.when(pl.program_id(0) == 0)
    def _():
        run_ref[...] = jnp.zeros_like(run_ref)

    logits = lax.dot_general(rt_ref[...], h_ref[...], (((1,), (1,)), ((), ())),
                             precision=lax.Precision.HIGHEST, preferred_element_type=F32)
    eid = lax.broadcasted_iota(jnp.int32, (E, tm), 0)
    m1 = jnp.max(logits, axis=0, keepdims=True)
    i1 = jnp.min(jnp.where(logits == m1, eid, E), axis=0, keepdims=True)
    rest = jnp.where(eid == i1, -jnp.inf, logits)
    m2 = jnp.max(rest, axis=0, keepdims=True)
    i2 = jnp.min(jnp.where(rest == m2, eid, E), axis=0, keepdims=True)
    e2 = jnp.exp(m2 - m1)
    g1 = 1.0 / (1.0 + e2)
    g2 = e2 / (1.0 + e2)

    sel = ((eid == i1) | (eid == i2))
    r = lax.broadcasted_iota(jnp.int32, (tm, tm), 0)
    c = lax.broadcasted_iota(jnp.int32, (tm, tm), 1)
    before = (r < c).astype(BF16)
    excl = _dot(sel.astype(BF16), before) + run_ref[:, 0:1]
    rank1 = jnp.sum(jnp.where(eid == i1, excl, 0.0), axis=0, keepdims=True).astype(jnp.int32)
    rank2 = jnp.sum(jnp.where(eid == i2, excl, 0.0), axis=0, keepdims=True).astype(jnp.int32)
    run_ref[...] = run_ref[...] + jnp.sum(sel.astype(F32), axis=1, keepdims=True)
    cnt_ref[...] = run_ref[...].astype(jnp.int32)

    zi = jnp.zeros((1, tm), jnp.int32)
    meta_ref[0] = jnp.concatenate([i1, i2, rank1, rank2, zi, zi, zi, zi], axis=0)
    zf = jnp.zeros((1, tm), F32)
    gate_ref[0] = jnp.concatenate([g1, g2, zf, zf, zf, zf, zf, zf], axis=0)


def _router(h2d, router_t):
    T, D = h2d.shape
    tm = min(ROUTE_TILE, T)
    nt = T // tm
    return pl.pallas_call(
        _router_kernel,
        grid=(nt,),
        in_specs=[pl.BlockSpec((tm, D), lambda i: (i, 0)), _const_spec(router_t.shape)],
        out_specs=(pl.BlockSpec((1, SUBLANES, tm), lambda i: (i, 0, 0)),
                   pl.BlockSpec((1, SUBLANES, tm), lambda i: (i, 0, 0)),
                   _const_spec((N_EXPERTS, LANES))),
        out_shape=(jax.ShapeDtypeStruct((nt, SUBLANES, tm), jnp.int32),
                   jax.ShapeDtypeStruct((nt, SUBLANES, tm), F32),
                   jax.ShapeDtypeStruct((N_EXPERTS, LANES), jnp.int32)),
        scratch_shapes=[pltpu.VMEM((N_EXPERTS, LANES), F32)],
        compiler_params=pltpu.CompilerParams(dimension_semantics=("arbitrary",)),
        name="moe_router",
    )(h2d, router_t)


def _dest_kernel(start_ref, meta_ref, dst_ref):
    meta = meta_ref[0]
    rows = []
    for k in range(2):
        e = meta[k:k + 1, :]
        base = jnp.zeros_like(e)
        for ex in range(N_EXPERTS):
            base = jnp.where(e == ex, start_ref[ex], base)
        rows.append(base + meta[2 + k:3 + k, :])
    dst_ref[0] = jnp.concatenate(rows + [jnp.zeros_like(rows[0])] * (SUBLANES - 2), axis=0)


def _dest_rows(starts, meta):
    nt, _, tm = meta.shape
    spec = pl.BlockSpec((1, SUBLANES, tm), lambda i, s: (i, 0, 0))
    return pl.pallas_call(
        _dest_kernel,
        grid_spec=pltpu.PrefetchScalarGridSpec(num_scalar_prefetch=1, grid=(nt,), in_specs=[spec], out_specs=spec),
        out_shape=jax.ShapeDtypeStruct(meta.shape, jnp.int32),
        compiler_params=pltpu.CompilerParams(dimension_semantics=("arbitrary",)),
        name="moe_dest",
    )(starts, meta)


def _scatter_kernel(dst_ref, h_ref, xs_in_hbm, xs_hbm, sem):
    del xs_in_hbm
    tm = dst_ref.shape[2]

    def row_copy(src_row, dst_row):
        return pltpu.make_async_copy(h_ref.at[pl.ds(src_row, 1), :], xs_hbm.at[pl.ds(dst_row, 1), :], sem)

    def issue(t, _):
        for k in range(2):
            row_copy(t, dst_ref[0, k, t]).start(priority=k)
        return 0

    lax.fori_loop(0, tm, issue, 0, unroll=DMA_UNROLL)

    def drain(t, _):
        row_copy(0, 0).wait()
        row_copy(0, 0).wait()
        return 0

    lax.fori_loop(0, tm, drain, 0, unroll=DMA_UNROLL)


def _scatter_rows(dst, h2d, xs_init):
    T, D = h2d.shape
    nt, _, tm = dst.shape
    return pl.pallas_call(
        _scatter_kernel,
        grid=(nt,),
        in_specs=[pl.BlockSpec((1, SUBLANES, tm), lambda i: (i, 0, 0), memory_space=pltpu.SMEM),
                  pl.BlockSpec((tm, D), lambda i: (i, 0)),
                  pl.BlockSpec(memory_space=pl.ANY)],
        out_specs=pl.BlockSpec(memory_space=pl.ANY),
        scratch_shapes=[pltpu.SemaphoreType.DMA(())],
        out_shape=jax.ShapeDtypeStruct(xs_init.shape, xs_init.dtype),
        input_output_aliases={2: 0},
        compiler_params=pltpu.CompilerParams(dimension_semantics=("arbitrary",), has_side_effects=True),
        name="moe_scatter",
    )(dst, h2d, xs_init)


def _expert_kernel(te_ref, nu_ref, x_ref, wg_ref, wu_ref, wd_ref, y_ref, xb_ref, acc_ref):
    m = pl.program_id(0)
    c = pl.program_id(1)
    nc = pl.num_programs(1)

    @pl.when(m < nu_ref[0])
    def _():
        @pl.when(c == 0)
        def _():
            xb_ref[...] = x_ref[...].astype(BF16)
            acc_ref[...] = jnp.zeros_like(acc_ref)

        xb = xb_ref[...]
        g = _dot(xb, wg_ref[0])
        u = _dot(xb, wu_ref[0])
        acc_ref[...] += _dot((jax.nn.silu(g) * u).astype(BF16), wd_ref[0])

        @pl.when(c == nc - 1)
        def _():
            y_ref[...] = acc_ref[...]

    @pl.when((m >= nu_ref[0]) & (c == nc - 1))
    def _():
        y_ref[...] = jnp.zeros_like(y_ref)


def _expert_ffn(tile_expert, n_used, xs, wg, wu, wd):
    R, D = xs.shape
    E, _, F = wg.shape
    tr = MOE_TILE
    n_tiles = R // tr
    fc = min(MOE_FF_CHUNK, F)
    ncs = F // fc

    def row_map(m, c, te, nu):
        return (jnp.minimum(m, nu[0] - 1), 0)

    def col(m, c, nu):
        return jnp.where(m < nu[0], c, ncs - 1)

    def exp(m, te, nu):
        return te[jnp.minimum(m, nu[0] - 1)]

    grid_spec = pltpu.PrefetchScalarGridSpec(
        num_scalar_prefetch=2,
        grid=(n_tiles, ncs),
        in_specs=[
            pl.BlockSpec((tr, D), row_map),
            pl.BlockSpec((1, D, fc), lambda m, c, te, nu: (exp(m, te, nu), 0, col(m, c, nu))),
            pl.BlockSpec((1, D, fc), lambda m, c, te, nu: (exp(m, te, nu), 0, col(m, c, nu))),
            pl.BlockSpec((1, fc, D), lambda m, c, te, nu: (exp(m, te, nu), col(m, c, nu), 0)),
        ],
        out_specs=pl.BlockSpec((tr, D), lambda m, c, te, nu: (m, 0)),
        scratch_shapes=[pltpu.VMEM((tr, D), BF16), pltpu.VMEM((tr, D), F32)],
    )
    return pl.pallas_call(
        _expert_kernel,
        grid_spec=grid_spec,
        out_shape=jax.ShapeDtypeStruct((R, D), F32),
        compiler_params=pltpu.CompilerParams(dimension_semantics=("arbitrary", "arbitrary")),
        name="moe_experts",
    )(tile_expert, n_used, xs, wg, wu, wd)


def _combine_kernel(dcur_ref, dnxt_ref, x_ref, gate_ref, fn_ref, ys_hbm, o_ref, buf_ref, sems, *, final_norm):
    tm = x_ref.shape[0]
    i = pl.program_id(0)
    slot = i % 2

    def row_copy(src_row, s, k, t):
        return pltpu.make_async_copy(ys_hbm.at[pl.ds(src_row, 1), :], buf_ref.at[s, k, pl.ds(t, 1), :], sems.at[s])

    def issue_tile(d_ref, s):
        def issue(t, _):
            for k in range(2):
                row_copy(d_ref[0, k, t], s, k, t).start(priority=k)
            return 0

        lax.fori_loop(0, tm, issue, 0, unroll=DMA_UNROLL)

    @pl.when(i == 0)
    def _():
        issue_tile(dcur_ref, 0)

    @pl.when(i + 1 < pl.num_programs(0))
    def _():
        issue_tile(dnxt_ref, 1 - slot)

    def drain(t, _):
        row_copy(0, slot, 0, 0).wait()
        row_copy(0, slot, 1, 0).wait()
        return 0

    lax.fori_loop(0, tm, drain, 0, unroll=DMA_UNROLL)
    g = gate_ref[...]
    xn = x_ref[...] + g[:, 0:1] * buf_ref[slot, 0] + g[:, 1:2] * buf_ref[slot, 1]
    o_ref[...] = _rms(xn, fn_ref[...]) if final_norm else xn


def _combine(dst_c, x2d, gates_col, fnorm, ys, final_norm):
    T, D = x2d.shape
    nt, _, tm = dst_c.shape
    return pl.pallas_call(
        functools.partial(_combine_kernel, final_norm=final_norm),
        grid=(nt,),
        in_specs=[pl.BlockSpec((1, SUBLANES, tm), lambda i: (i, 0, 0), memory_space=pltpu.SMEM),
                  pl.BlockSpec((1, SUBLANES, tm), lambda i: (jnp.minimum(i + 1, nt - 1), 0, 0),
                               memory_space=pltpu.SMEM),
                  pl.BlockSpec((tm, D), lambda i: (i, 0)),
                  pl.BlockSpec((tm, SUBLANES), lambda i: (i, 0)),
                  pl.BlockSpec((1, D), lambda i: (0, 0)),
                  pl.BlockSpec(memory_space=pl.ANY)],
        out_specs=pl.BlockSpec((tm, D), lambda i: (i, 0)),
        scratch_shapes=[pltpu.VMEM((2, 2, tm, D), F32), pltpu.SemaphoreType.DMA((2,))],
        out_shape=jax.ShapeDtypeStruct((T, D), F32),
        compiler_params=pltpu.CompilerParams(dimension_semantics=("arbitrary",)),
        name="moe_combine",
    )(dst_c, dst_c, x2d, gates_col, fnorm, ys)


def _swap_halves_cols(w):
    half = w.shape[-1] // 2
    return jnp.concatenate([w[..., half:], w[..., :half]], axis=-1)


def _pad_cols(w, width):
    return jnp.pad(w, [(0, 0)] * (w.ndim - 1) + [(0, width - w.shape[-1])])


def _prep_w_in(w):
    kpe0 = _C_CKV + MLA_KV_RANK - 0
    kpe = w[:, kpe0:kpe0 + MLA_ROPE]
    return jnp.concatenate([
        w[:, :kpe0],
        _pad_cols(kpe, LANES), _pad_cols(_swap_halves_cols(kpe), LANES),
        w[:, kpe0 + MLA_ROPE:],
    ], axis=1).astype(BF16)


def _prep_w_uq(w):
    w = w.reshape(MLA_Q_RANK, MLA_HEADS, MLA_NOPE + MLA_ROPE)
    nope, pe = w[..., :MLA_NOPE], w[..., MLA_NOPE:]
    out = jnp.concatenate([nope, _pad_cols(pe, LANES), _pad_cols(_swap_halves_cols(pe), LANES)], axis=-1)
    return out.reshape(MLA_Q_RANK, MLA_HEADS * _Q_HEAD_COLS).astype(BF16)


def _prep_w_ukv(w):
    w = w.reshape(MLA_KV_RANK, MLA_HEADS, MLA_NOPE + MLA_V)
    wukt = jnp.transpose(w[..., :MLA_NOPE], (1, 2, 0)).reshape(MLA_HEADS * MLA_NOPE, MLA_KV_RANK)
    wuv = w[..., MLA_NOPE:].reshape(MLA_KV_RANK, MLA_HEADS * MLA_V)
    return wukt.astype(BF16), wuv.astype(BF16)


def _prep_w_kpe_t(w_ext):
    return w_ext[:, _C_KPE:_C_XP].T


def _rope_tables(seq):
    pos = jnp.arange(seq, dtype=F32)
    inv_freq = 1.0 / (ROPE_THETA ** (jnp.arange(0, MLA_ROPE, 2, dtype=F32) / MLA_ROPE))
    ang = pos[:, None] * inv_freq[None, :]
    cos, sin = jnp.cos(ang), jnp.sin(ang)
    cpad = _pad_cols(jnp.concatenate([cos, cos], axis=-1), LANES)
    spad = _pad_cols(jnp.concatenate([-sin, sin], axis=-1), LANES)
    return cpad, spad


def _block_diag_pool(w):
    G, c, _ = w.shape
    eye = jnp.eye(G, dtype=w.dtype)
    return (eye[:, None, :, None] * w[:, :, None, :]).reshape(G * c, G * c).astype(BF16)


def _moe_layout(counts, n_tiles, tile):
    tiles_per = (counts + tile - 1) // tile
    ends = jnp.cumsum(tiles_per)
    starts = (ends - tiles_per) * tile
    tile_expert = jnp.sum((jnp.arange(n_tiles)[:, None] >= ends[None, :]).astype(jnp.int32), axis=1)
    tile_expert = jnp.minimum(tile_expert, N_EXPERTS - 1)
    return starts.astype(jnp.int32), tile_expert.astype(jnp.int32), ends[-1:].astype(jnp.int32)


def kernel(x, attn_norm, w_in, hgrn_lower_bounds, hgrn_out_norm, mla_q_norm, mla_w_uq, mla_kv_norm,
           mla_w_ukv, pool_w, pool_scale, w_o, ffn_norm, dense_w_gate, dense_w_up, dense_w_down,
           moe_router, moe_w_gate, moe_w_up, moe_w_down, final_norm):
    B, S, D = x.shape
    T = B * S
    depth = w_in.shape[0]
    cpad, spad = _rope_tables(S)
    p_lb = jax.nn.softmax(hgrn_lower_bounds.astype(F32), axis=0)
    lbs = jnp.cumsum(p_lb, axis=0) - p_lb[0:1]

    assert depth % 2 == 0, "the final RMSNorm is fused into the last (MoE) layer's combine kernel"
    for l in range(depth):
        wukt, wuv = _prep_w_ukv(mla_w_ukv[l])
        w_ext = _prep_w_in(w_in[l])
        hg, xp, q, kt, v = _in_proj(
            x, attn_norm[l][None], w_ext, mla_q_norm[l][None], _prep_w_uq(mla_w_uq[l]),
            mla_kv_norm[l][None], wukt, wuv, _prep_w_kpe_t(w_ext), cpad, spad)
        o_a = _hgrn(hg, lbs[l][None], hgrn_out_norm[l][None])
        o_b = _attention(q, kt, v)
        moe_layer = (l % 2 == 1)
        x, h = _mix_out(x, o_a, o_b, xp, _block_diag_pool(pool_w[l]), pool_scale[l][None],
                        w_o[l].astype(BF16), ffn_norm[l][None], F32 if moe_layer else BF16)
        j = l // 2
        if not moe_layer:
            x = _dense_ffn(x.reshape(T, D), h.reshape(T, D), dense_w_gate[j].astype(BF16),
                           dense_w_up[j].astype(BF16), dense_w_down[j].astype(BF16)).reshape(B, S, D)
        else:
            h2d = h.reshape(T, D)
            meta, gates, counts = _router(h2d, moe_router[j].T)
            n_tiles = (2 * T) // MOE_TILE + N_EXPERTS
            starts, tile_expert, n_used = _moe_layout(counts[:, 0], n_tiles, MOE_TILE)
            dst = _dest_rows(starts, meta)
            xs = _scatter_rows(dst, h2d, jnp.zeros((n_tiles * MOE_TILE, D), F32))
            ys = _expert_ffn(tile_expert, n_used, xs, moe_w_gate[j].astype(BF16), moe_w_up[j].astype(BF16),
                             moe_w_down[j].astype(BF16))
            ct = min(COMBINE_TILE, T)
            dst_c = dst.transpose(1, 0, 2).reshape(SUBLANES, T // ct, ct).transpose(1, 0, 2)
            gates_col = gates.transpose(0, 2, 1).reshape(T, SUBLANES)
            last = (l == depth - 1)
            y = _combine(dst_c, x.reshape(T, D), gates_col, final_norm[None], ys, last)
            x = y.reshape(B, S, D)
    return x
```

```python
import functools
import math

import jax
import jax.numpy as jnp
import numpy as np
from jax import lax
from jax.experimental import pallas as pl
from jax.experimental.pallas import tpu as pltpu

F32 = jnp.float32
BF16 = jnp.bfloat16

HG_HEADS = 4
HG_KEY_DIM = 128
HG_VAL_DIM = 64
HG_KEY_WIDTH = HG_HEADS * HG_KEY_DIM
HG_WIDTH = HG_HEADS * HG_VAL_DIM
MIN_FORGET = 1e-20
MLA_HEADS = 4
MLA_Q_RANK = 256
MLA_KV_RANK = 128
MLA_NOPE = 128
MLA_ROPE = 64
MLA_V = 128
MLA_WIDTH = MLA_HEADS * MLA_V
ROPE_THETA = 10000.0
MASK_VALUE = -1e30
POOL_GROUPS = 4
POOL_WINDOWS = (2, 4, 8, 16)
POOL_WIDTH = 256
POOL_GROUP_DIM = POOL_WIDTH // POOL_GROUPS
N_EXPERTS = 8
EPS = 1e-6

LANES = 128
SUBLANES = 8
QK_PAD = 256
V_EXT = 256

TOKEN_TILE = 1024
HGRN_CHUNK = 256
ATTN_BK = 512
FF_CHUNK = 512
MOE_FF_CHUNK = 512
MOE_TILE = 1024
ROUTE_TILE = 512
COMBINE_TILE = 256
POOL_HALO = 16
DMA_UNROLL = 8

_C_HG = 0
_C_CQ = 2 * HG_KEY_WIDTH + 2 * HG_WIDTH
_C_CKV = _C_CQ + MLA_Q_RANK
_C_KPE = _C_CKV + MLA_KV_RANK
_C_KPES = _C_KPE + LANES
_C_XP = _C_KPES + LANES
_C_END = _C_XP + POOL_WIDTH
_Q_HEAD_COLS = 3 * LANES


def _rms(x, g):
    return x * lax.rsqrt(jnp.mean(x * x, axis=-1, keepdims=True) + EPS) * g


def _dot(a, b):
    return jnp.dot(a, b, preferred_element_type=F32)


def _dot_nt(a, b):
    return lax.dot_general(a, b, (((1,), (1,)), ((), ())), preferred_element_type=F32)


def _dot_tn(a, b):
    return lax.dot_general(a, b, (((0,), (0,)), ((), ())), preferred_element_type=F32)


def _const_spec(shape):
    nd = len(shape)
    return pl.BlockSpec(shape, lambda *_: (0,) * nd)


def _load_tile_rows(ref, n):
    return jnp.concatenate([ref[pl.ds(s, n, stride=SUBLANES), :] for s in range(SUBLANES)], axis=1)


def _store_tile_rows(ref, val):
    n = val.shape[0]
    for s in range(SUBLANES):
        ref[pl.ds(s, n, stride=SUBLANES), :] = val[:, s * LANES:(s + 1) * LANES]


def _in_proj_kernel(x_ref, g_ref, w_ref, qn_ref, wuq_ref, kvn_ref, wukt_ref, wuv_ref, wkpet_ref,
                    cpad_ref, spad_ref, cpadt_ref, spadt_ref,
                    hg_ref, xp_ref, q_ref, kt_ref, v_ref):
    h = _rms(x_ref[0], g_ref[...]).astype(BF16)
    hg_ref[0] = _dot(h, w_ref[:, _C_HG:_C_CQ])
    xp_ref[0] = _dot(h, w_ref[:, _C_XP:_C_END])
    cpad = cpad_ref[...]
    spad = spad_ref[...]
    scale = (MLA_NOPE + MLA_ROPE) ** -0.5 * math.log2(math.e)

    cq = _dot(h, w_ref[:, _C_CQ:_C_CKV])
    cqn = _rms(cq, qn_ref[...]).astype(BF16)
    for hd in range(MLA_HEADS):
        qh = _dot(cqn, wuq_ref[:, hd * _Q_HEAD_COLS:(hd + 1) * _Q_HEAD_COLS])
        q_ref[0, hd, :, 0:LANES] = (qh[:, 0:LANES] * scale).astype(BF16)
        pe = qh[:, LANES:2 * LANES] * cpad + qh[:, 2 * LANES:3 * LANES] * spad
        q_ref[0, hd, :, LANES:QK_PAD] = (pe * scale).astype(BF16)

    ckv = _dot(h, w_ref[:, _C_CKV:_C_KPE])
    ckvn = _rms(ckv, kvn_ref[...]).astype(BF16)
    kpet = (_dot_nt(wkpet_ref[0:LANES, :], h) * cpadt_ref[...]
            + _dot_nt(wkpet_ref[LANES:2 * LANES, :], h) * spadt_ref[...]).astype(BF16)
    tm = ckvn.shape[0]
    ones_col = (lax.broadcasted_iota(jnp.int32, (tm, LANES), 1) == 0).astype(BF16)
    for hd in range(MLA_HEADS):
        kt_ref[0, hd, 0:LANES, :] = _dot_nt(wukt_ref[hd * LANES:(hd + 1) * LANES, :], ckvn).astype(BF16)
        kt_ref[0, hd, LANES:QK_PAD, :] = kpet
        v_ref[0, hd, :, 0:MLA_V] = _dot(ckvn, wuv_ref[:, hd * MLA_V:(hd + 1) * MLA_V]).astype(BF16)
        v_ref[0, hd, :, MLA_V:V_EXT] = ones_col


def _in_proj(x, g, w_ext, qn, wuq_ext, kvn, wukt, wuv, wkpet, cpad, spad):
    B, S, D = x.shape
    tm = min(TOKEN_TILE, S)
    grid = (B, S // tm)
    n_hg = _C_CQ
    out_shape = (
        jax.ShapeDtypeStruct((B, S, n_hg), F32),
        jax.ShapeDtypeStruct((B, S, POOL_WIDTH), F32),
        jax.ShapeDtypeStruct((B, MLA_HEADS, S, QK_PAD), BF16),
        jax.ShapeDtypeStruct((B, MLA_HEADS, QK_PAD, S), BF16),
        jax.ShapeDtypeStruct((B, MLA_HEADS, S, V_EXT), BF16),
    )
    return pl.pallas_call(
        _in_proj_kernel,
        grid=grid,
        in_specs=[
            pl.BlockSpec((1, tm, D), lambda b, i: (b, i, 0)),
            _const_spec((1, D)),
            _const_spec(w_ext.shape),
            _const_spec((1, MLA_Q_RANK)),
            _const_spec(wuq_ext.shape),
            _const_spec((1, MLA_KV_RANK)),
            _const_spec(wukt.shape),
            _const_spec(wuv.shape),
            _const_spec(wkpet.shape),
            pl.BlockSpec((tm, LANES), lambda b, i: (i, 0)),
            pl.BlockSpec((tm, LANES), lambda b, i: (i, 0)),
            pl.BlockSpec((LANES, tm), lambda b, i: (0, i)),
            pl.BlockSpec((LANES, tm), lambda b, i: (0, i)),
        ],
        out_specs=(
            pl.BlockSpec((1, tm, n_hg), lambda b, i: (b, i, 0)),
            pl.BlockSpec((1, tm, POOL_WIDTH), lambda b, i: (b, i, 0)),
            pl.BlockSpec((1, MLA_HEADS, tm, QK_PAD), lambda b, i: (b, 0, i, 0)),
            pl.BlockSpec((1, MLA_HEADS, QK_PAD, tm), lambda b, i: (b, 0, 0, i)),
            pl.BlockSpec((1, MLA_HEADS, tm, V_EXT), lambda b, i: (b, 0, i, 0)),
        ),
        out_shape=out_shape,
        compiler_params=pltpu.CompilerParams(dimension_semantics=("arbitrary", "arbitrary")),
        name="in_proj",
    )(x, g, w_ext, qn, wuq_ext, kvn, wukt, wuv, wkpet, cpad, spad, cpad.T, spad.T)


def _split3(x):
    hi = x.astype(BF16)
    r1 = x - hi.astype(F32)
    mid = r1.astype(BF16)
    lo = (r1 - mid.astype(F32)).astype(BF16)
    return hi, mid, lo


def _hgrn_kernel(q_ref, f_ref, i_ref, g_ref, lb_ref, on_ref, o_ref, st_ref):
    C = q_ref.shape[1]
    KW = HG_KEY_WIDTH

    @pl.when(pl.program_id(1) == 0)
    def _():
        st_ref[...] = jnp.zeros_like(st_ref)

    lb = lb_ref[...]
    z = f_ref[0]
    forget = lb + (1.0 - lb) * jax.nn.sigmoid(z)
    lg = jnp.log2(jnp.maximum(forget, MIN_FORGET))
    kk = (1.0 - lb) * jax.nn.sigmoid(-z)
    qq = jax.nn.silu(q_ref[0])
    vv = i_ref[0]

    row = lax.broadcasted_iota(jnp.int32, (C, C), 0)
    col = lax.broadcasted_iota(jnp.int32, (C, C), 1)
    tril = (col <= row).astype(BF16)
    b = sum(_dot(tril, part) for part in _split3(lg))

    sides, masks = [], []
    sub = lax.broadcasted_iota(jnp.int32, (C, KW), 0)
    half = C // 2
    while half >= 4:
        blk = 2 * half
        b3 = b.reshape(C // blk, blk, KW)
        ref_row = jnp.broadcast_to(b3[:, half - 1:half, :], b3.shape).reshape(C, KW)
        e = jnp.exp2(-jnp.abs(b - ref_row))
        x = (jnp.where((sub & half) != 0, qq, kk) * e).astype(BF16)
        sides.append((x, x))
        shift = int(math.log2(blk))
        masks.append(((row >> shift) == (col >> shift)) & ((row & half) != 0) & ((col & half) == 0))
        half //= 2
    b3 = b.reshape(C // 8, 8, KW)
    mid_lo = 0.5 * (b3[:, 0:1, :] + b3[:, 3:4, :])
    mid_hi = 0.5 * (b3[:, 4:5, :] + b3[:, 7:8, :])
    sub8 = lax.broadcasted_iota(jnp.int32, b3.shape, 1)
    mid = jnp.where(sub8 < 4, mid_lo, mid_hi).reshape(C, KW)
    sides.append(((qq * jnp.exp2(b - mid)).astype(BF16), (kk * jnp.exp2(mid - b)).astype(BF16)))
    masks.append(((row >> 2) == (col >> 2)) & (col <= row))

    lane_v = lax.broadcasted_iota(jnp.int32, (C, HG_WIDTH), 1)
    o = _dot_nt((qq * jnp.exp2(b)).astype(BF16), st_ref[...].astype(BF16))
    for hd in range(HG_HEADS):
        ks = slice(hd * HG_KEY_DIM, (hd + 1) * HG_KEY_DIM)
        a = jnp.zeros((C, C), F32)
        for (qt, kt), m in zip(sides, masks):
            a = jnp.where(m, _dot_nt(qt[:, ks], kt[:, ks]), a)
        v_h = jnp.where((lane_v >> 6) == hd, vv, 0.0).astype(BF16)
        o = o + _dot(a.astype(BF16), v_h)

    b_last = b[C - 1:C, :]
    khat = (kk * jnp.exp2(b_last - b)).astype(BF16)
    st_row = lax.broadcasted_iota(jnp.int32, (HG_WIDTH, KW), 0)
    st_col = lax.broadcasted_iota(jnp.int32, (HG_WIDTH, KW), 1)
    new_st = st_ref[...] * jnp.exp2(b_last) + _dot_tn(vv.astype(BF16), khat)
    st_ref[...] = jnp.where((st_row >> 6) == (st_col >> 7), new_st, 0.0)

    gi = lax.broadcasted_iota(jnp.int32, (HG_WIDTH, HG_WIDTH), 0)
    gj = lax.broadcasted_iota(jnp.int32, (HG_WIDTH, HG_WIDTH), 1)
    grp = ((gi >> 6) == (gj >> 6)).astype(BF16)
    ssq = sum(_dot(part, grp) for part in _split3(o * o))
    on = o * lax.rsqrt(ssq * (1.0 / HG_VAL_DIM) + EPS) * on_ref[...]
    o_ref[0] = (on * jax.nn.silu(g_ref[0])).astype(o_ref.dtype)


def _hgrn(hg, lb, out_norm):
    B, S, _ = hg.shape
    C = min(HGRN_CHUNK, S)
    kb = HG_KEY_WIDTH // HG_KEY_WIDTH
    return pl.pallas_call(
        _hgrn_kernel,
        grid=(B, S // C),
        in_specs=[
            pl.BlockSpec((1, C, HG_KEY_WIDTH), lambda b, c: (b, c, 0)),
            pl.BlockSpec((1, C, HG_KEY_WIDTH), lambda b, c: (b, c, kb)),
            pl.BlockSpec((1, C, HG_WIDTH), lambda b, c: (b, c, 2 * HG_KEY_WIDTH // HG_WIDTH)),
            pl.BlockSpec((1, C, HG_WIDTH), lambda b, c: (b, c, 2 * HG_KEY_WIDTH // HG_WIDTH + 1)),
            _const_spec((1, HG_KEY_WIDTH)),
            _const_spec((1, HG_WIDTH)),
        ],
        out_specs=pl.BlockSpec((1, C, HG_WIDTH), lambda b, c: (b, c, 0)),
        out_shape=jax.ShapeDtypeStruct((B, S, HG_WIDTH), BF16),
        scratch_shapes=[pltpu.VMEM((HG_WIDTH, HG_KEY_WIDTH), F32)],
        compiler_params=pltpu.CompilerParams(dimension_semantics=("arbitrary", "arbitrary")),
        name="hgrn2",
    )(hg, hg, hg, hg, lb, out_norm)


def _attn_kernel(q_ref, kt_ref, v_ref, o_ref, s0_ref, s1_ref, acc_ref, *, bk):
    bq = q_ref.shape[2]
    i = pl.program_id(2)
    q = q_ref[0, 0]

    def causal(s, key0):
        qry = lax.broadcasted_iota(jnp.int32, (bq, bk), 0)
        key = key0 + lax.broadcasted_iota(jnp.int32, (bq, bk), 1)
        return jnp.where(key <= qry, s, MASK_VALUE)

    def scores(blk, s_ref):
        start = pl.multiple_of(blk * bk, bk)
        s_ref[...] = _dot(q, kt_ref[0, 0, :, pl.ds(start, bk)])

    def softmax_pv(blk, s_ref, m, key0=None):
        s = s_ref[...]
        if key0 is not None:
            s = causal(s, key0)
        m_new = jnp.maximum(m, jnp.max(s, axis=1, keepdims=True))
        p = jnp.exp2(s - m_new).astype(BF16)
        start = pl.multiple_of(blk * bk, bk)
        acc_ref[...] = jnp.exp2(m - m_new) * acc_ref[...] + _dot(p, v_ref[0, 0, pl.ds(start, bk), :])
        return m_new

    acc_ref[...] = jnp.zeros_like(acc_ref)
    scores(0, s0_ref)

    def pair(t, m):
        scores(2 * t + 1, s1_ref)
        m = softmax_pv(2 * t, s0_ref, m)
        scores(2 * t + 2, s0_ref)
        return softmax_pv(2 * t + 1, s1_ref, m)

    m = lax.fori_loop(0, i, pair, jnp.full((bq, 1), -jnp.inf, F32))
    scores(2 * i + 1, s1_ref)
    m = softmax_pv(2 * i, s0_ref, m, 0)
    m = softmax_pv(2 * i + 1, s1_ref, m, bk)
    acc = acc_ref[...]
    o_ref[0] = (acc[:, 0:MLA_V] / acc[:, MLA_V:MLA_V + 1]).astype(o_ref.dtype)


def _attention(q, kt, v):
    B, H, S, _ = q.shape
    bk = min(ATTN_BK, S // 2)
    bq = 2 * bk
    return pl.pallas_call(
        functools.partial(_attn_kernel, bk=bk),
        grid=(B, H, S // bq),
        in_specs=[
            pl.BlockSpec((1, 1, bq, QK_PAD), lambda b, h, i: (b, h, i, 0)),
            pl.BlockSpec((1, 1, QK_PAD, S), lambda b, h, i: (b, h, 0, 0)),
            pl.BlockSpec((1, 1, S, V_EXT), lambda b, h, i: (b, h, 0, 0)),
        ],
        out_specs=pl.BlockSpec((1, bq, MLA_V), lambda b, h, i: (b, i, h)),
        out_shape=jax.ShapeDtypeStruct((B, S, H * MLA_V), BF16),
        scratch_shapes=[pltpu.VMEM((bq, bk), F32), pltpu.VMEM((bq, bk), F32), pltpu.VMEM((bq, V_EXT), F32)],
        compiler_params=pltpu.CompilerParams(dimension_semantics=("arbitrary", "arbitrary", "arbitrary")),
        name="mla_attention",
    )(q, kt, v)


def _mix_out_kernel(x_ref, oa_ref, ob_ref, xp_ref, wpool_ref, pscale_ref, wo_ref, fn_ref,
                    xo_ref, h_ref, halo_ref, *, tile_rows):
    tm = xp_ref.shape[1]
    i = pl.program_id(1)

    @pl.when(i == 0)
    def _():
        halo_ref[...] = jnp.zeros_like(halo_ref)

    xp = xp_ref[0]
    xx = jnp.concatenate([halo_ref[...], xp], axis=0)
    halo_ref[...] = xp[tm - POOL_HALO:, :]

    w2 = xx[1:, :] + xx[:-1, :]
    w4 = w2[2:, :] + w2[:-2, :]
    w8 = w4[4:, :] + w4[:-4, :]
    w16 = w8[8:, :] + w8[:-8, :]
    sums = (w2[POOL_HALO - 1:, :], w4[POOL_HALO - 3:, :], w8[POOL_HALO - 7:, :], w16[POOL_HALO - 15:, :])
    t = i * tm + lax.broadcasted_iota(jnp.int32, (tm, POOL_WIDTH), 0)
    lane = lax.broadcasted_iota(jnp.int32, (tm, POOL_WIDTH), 1)
    pooled = jnp.zeros((tm, POOL_WIDTH), F32)
    for gi, w in enumerate(POOL_WINDOWS):
        cnt = jnp.minimum(t + 1, w).astype(F32)
        pooled = jnp.where((lane >> 6) == gi, sums[gi] / cnt, pooled)
    pooled = pooled - xp
    oc = _dot(pooled.astype(BF16), wpool_ref[...]) * pscale_ref[...]

    y = _dot(oa_ref[0], wo_ref[0:HG_WIDTH, :])
    y = y + _dot(ob_ref[0], wo_ref[HG_WIDTH:HG_WIDTH + MLA_WIDTH, :])
    y = y + _dot(oc.astype(BF16), wo_ref[HG_WIDTH + MLA_WIDTH:, :])
    xn = x_ref[0] + y
    xo_ref[0] = xn
    h = _rms(xn, fn_ref[...])
    if tile_rows:
        _store_tile_rows(h_ref, h)
    else:
        h_ref[0] = h.astype(h_ref.dtype)


def _mix_out(x, oa, ob, xp, wpool_bd, pscale, wo, fnorm, tile_rows):
    B, S, D = x.shape
    tm = min(TOKEN_TILE, S)
    nt = S // tm
    tok = lambda w: pl.BlockSpec((1, tm, w), lambda b, i: (b, i, 0))
    if tile_rows:
        assert D == SUBLANES * LANES
        h_spec = pl.BlockSpec((tm * SUBLANES, LANES), lambda b, i: (b * nt + i, 0))
        h_shape = jax.ShapeDtypeStruct((B * S * SUBLANES, LANES), F32)
    else:
        h_spec, h_shape = tok(D), jax.ShapeDtypeStruct((B, S, D), BF16)
    return pl.pallas_call(
        functools.partial(_mix_out_kernel, tile_rows=tile_rows),
        grid=(B, nt),
        in_specs=[tok(D), tok(HG_WIDTH), tok(MLA_WIDTH), tok(POOL_WIDTH),
                  _const_spec(wpool_bd.shape), _const_spec((1, POOL_WIDTH)), _const_spec(wo.shape),
                  _const_spec((1, D))],
        out_specs=(tok(D), h_spec),
        out_shape=(jax.ShapeDtypeStruct((B, S, D), F32), h_shape),
        scratch_shapes=[pltpu.VMEM((POOL_HALO, POOL_WIDTH), F32)],
        compiler_params=pltpu.CompilerParams(dimension_semantics=("arbitrary", "arbitrary")),
        name="mix_out",
    )(x, oa, ob, xp, wpool_bd, pscale, wo, fnorm)


def _dense_ffn_kernel(x_ref, h_ref, wg_ref, wu_ref, wd_ref, o_ref):
    h = h_ref[...]
    acc = x_ref[...]
    dff = wg_ref.shape[1]
    for c0 in range(0, dff, FF_CHUNK):
        g = _dot(h, wg_ref[:, c0:c0 + FF_CHUNK])
        u = _dot(h, wu_ref[:, c0:c0 + FF_CHUNK])
        acc = acc + _dot((jax.nn.silu(g) * u).astype(BF16), wd_ref[c0:c0 + FF_CHUNK, :])
    o_ref[...] = acc


def _dense_ffn(x2d, h2d, wg, wu, wd):
    T, D = x2d.shape
    tm = min(TOKEN_TILE, T)
    tok = pl.BlockSpec((tm, D), lambda i: (i, 0))
    return pl.pallas_call(
        _dense_ffn_kernel,
        grid=(T // tm,),
        in_specs=[tok, tok, _const_spec(wg.shape), _const_spec(wu.shape), _const_spec(wd.shape)],
        out_specs=tok,
        out_shape=jax.ShapeDtypeStruct((T, D), F32),
        compiler_params=pltpu.CompilerParams(dimension_semantics=("arbitrary",)),
        name="dense_ffn",
    )(x2d, h2d, wg, wu, wd)


def _router_kernel(h_ref, rt_ref, meta_ref, gate_ref, cnt_ref, run_ref):
    tm = h_ref.shape[0] // SUBLANES
    E = N_EXPERTS

    @pl.when(pl.program_id(0) == 0)
    def _():
        run_ref[...] = jnp.zeros_like(run_ref)

    logits = lax.dot_general(rt_ref[...], _load_tile_rows(h_ref, tm), (((1,), (1,)), ((), ())),
                             precision=lax.Precision.HIGHEST, preferred_element_type=F32)
    eid = lax.broadcasted_iota(jnp.int32, (E, tm), 0)
    m1 = jnp.max(logits, axis=0, keepdims=True)
    i1 = jnp.min(jnp.where(logits == m1, eid, E), axis=0, keepdims=True)
    rest = jnp.where(eid == i1, -jnp.inf, logits)
    m2 = jnp.max(rest, axis=0, keepdims=True)
    i2 = jnp.min(jnp.where(rest == m2, eid, E), axis=0, keepdims=True)
    e2 = jnp.exp(m2 - m1)
    g1 = 1.0 / (1.0 + e2)
    g2 = e2 / (1.0 + e2)

    sel = ((eid == i1) | (eid == i2))
    r = lax.broadcasted_iota(jnp.int32, (tm, tm), 0)
    c = lax.broadcasted_iota(jnp.int32, (tm, tm), 1)
    before = (r < c).astype(BF16)
    excl = _dot(sel.astype(BF16), before) + run_ref[:, 0:1]
    rank1 = jnp.sum(jnp.where(eid == i1, excl, 0.0), axis=0, keepdims=True).astype(jnp.int32)
    rank2 = jnp.sum(jnp.where(eid == i2, excl, 0.0), axis=0, keepdims=True).astype(jnp.int32)
    run_ref[...] = run_ref[...] + jnp.sum(sel.astype(F32), axis=1, keepdims=True)
    cnt_ref[...] = run_ref[...].astype(jnp.int32)

    zi = jnp.zeros((1, tm), jnp.int32)
    meta_ref[0] = jnp.concatenate([i1, i2, rank1, rank2, zi, zi, zi, zi], axis=0)
    zf = jnp.zeros((1, tm), F32)
    gate_ref[0] = jnp.concatenate([g1, g2, zf, zf, zf, zf, zf, zf], axis=0)


def _router(h_rows, router_t):
    T = h_rows.shape[0] // SUBLANES
    tm = min(ROUTE_TILE, T)
    nt = T // tm
    return pl.pallas_call(
        _router_kernel,
        grid=(nt,),
        in_specs=[pl.BlockSpec((tm * SUBLANES, LANES), lambda i: (i, 0)), _const_spec(router_t.shape)],
        out_specs=(pl.BlockSpec((1, SUBLANES, tm), lambda i: (i, 0, 0)),
                   pl.BlockSpec((1, SUBLANES, tm), lambda i: (i, 0, 0)),
                   _const_spec((N_EXPERTS, LANES))),
        out_shape=(jax.ShapeDtypeStruct((nt, SUBLANES, tm), jnp.int32),
                   jax.ShapeDtypeStruct((nt, SUBLANES, tm), F32),
                   jax.ShapeDtypeStruct((N_EXPERTS, LANES), jnp.int32)),
        scratch_shapes=[pltpu.VMEM((N_EXPERTS, LANES), F32)],
        compiler_params=pltpu.CompilerParams(dimension_semantics=("arbitrary",)),
        name="moe_router",
    )(h_rows, router_t)


def _dest_kernel(start_ref, meta_ref, dst_ref):
    meta = meta_ref[0]
    rows = []
    for k in range(2):
        e = meta[k:k + 1, :]
        base = jnp.zeros_like(e)
        for ex in range(N_EXPERTS):
            base = jnp.where(e == ex, start_ref[ex], base)
        rows.append(base + meta[2 + k:3 + k, :])
    dst_ref[0] = jnp.concatenate(rows + [jnp.zeros_like(rows[0])] * (SUBLANES - 2), axis=0)


def _dest_rows(starts, meta):
    nt, _, tm = meta.shape
    spec = pl.BlockSpec((1, SUBLANES, tm), lambda i, s: (i, 0, 0))
    return pl.pallas_call(
        _dest_kernel,
        grid_spec=pltpu.PrefetchScalarGridSpec(num_scalar_prefetch=1, grid=(nt,), in_specs=[spec], out_specs=spec),
        out_shape=jax.ShapeDtypeStruct(meta.shape, jnp.int32),
        compiler_params=pltpu.CompilerParams(dimension_semantics=("arbitrary",)),
        name="moe_dest",
    )(starts, meta)


def _scatter_kernel(dst_ref, h_ref, xs_in_hbm, xs_hbm, sem):
    del xs_in_hbm
    tm = dst_ref.shape[2]

    def row_copy(src_row, dst_row):
        src = pl.multiple_of(src_row * SUBLANES, SUBLANES)
        dst = pl.multiple_of(dst_row * SUBLANES, SUBLANES)
        return pltpu.make_async_copy(h_ref.at[pl.ds(src, SUBLANES), :], xs_hbm.at[pl.ds(dst, SUBLANES), :], sem)

    def issue(t, _):
        for k in range(2):
            row_copy(t, dst_ref[0, k, t]).start(priority=k)
        return 0

    lax.fori_loop(0, tm, issue, 0, unroll=DMA_UNROLL)

    def drain(t, _):
        row_copy(0, 0).wait()
        row_copy(0, 0).wait()
        return 0

    lax.fori_loop(0, tm, drain, 0, unroll=DMA_UNROLL)


def _scatter_rows(dst, h_rows, xs_init):
    nt, _, tm = dst.shape
    return pl.pallas_call(
        _scatter_kernel,
        grid=(nt,),
        in_specs=[pl.BlockSpec((1, SUBLANES, tm), lambda i: (i, 0, 0), memory_space=pltpu.SMEM),
                  pl.BlockSpec((tm * SUBLANES, LANES), lambda i: (i, 0)),
                  pl.BlockSpec(memory_space=pl.ANY)],
        out_specs=pl.BlockSpec(memory_space=pl.ANY),
        scratch_shapes=[pltpu.SemaphoreType.DMA(())],
        out_shape=jax.ShapeDtypeStruct(xs_init.shape, xs_init.dtype),
        input_output_aliases={2: 0},
        compiler_params=pltpu.CompilerParams(dimension_semantics=("arbitrary",), has_side_effects=True),
        name="moe_scatter",
    )(dst, h_rows, xs_init)


def _expert_kernel(te_ref, nu_ref, x_ref, wg_ref, wu_ref, wd_ref, y_ref, xb_ref, acc_ref):
    m = pl.program_id(0)
    c = pl.program_id(1)
    nc = pl.num_programs(1)

    @pl.when(m < nu_ref[0])
    def _():
        @pl.when(c == 0)
        def _():
            xb_ref[...] = _load_tile_rows(x_ref, xb_ref.shape[0]).astype(BF16)
            acc_ref[...] = jnp.zeros_like(acc_ref)

        xb = xb_ref[...]
        g = _dot(xb, wg_ref[0])
        u = _dot(xb, wu_ref[0])
        acc_ref[...] += _dot((jax.nn.silu(g) * u).astype(BF16), wd_ref[0])

        @pl.when(c == nc - 1)
        def _():
            _store_tile_rows(y_ref, acc_ref[...])

    @pl.when((m >= nu_ref[0]) & (c == nc - 1))
    def _():
        y_ref[...] = jnp.zeros_like(y_ref)


def _expert_ffn(tile_expert, n_used, xs, wg, wu, wd):
    E, D, F = wg.shape
    R = xs.shape[0] // SUBLANES
    tr = MOE_TILE
    n_tiles = R // tr
    fc = min(MOE_FF_CHUNK, F)
    ncs = F // fc

    def row_map(m, c, te, nu):
        return (jnp.minimum(m, nu[0] - 1), 0)

    def col(m, c, nu):
        return jnp.where(m < nu[0], c, ncs - 1)

    def exp(m, te, nu):
        return te[jnp.minimum(m, nu[0] - 1)]

    grid_spec = pltpu.PrefetchScalarGridSpec(
        num_scalar_prefetch=2,
        grid=(n_tiles, ncs),
        in_specs=[
            pl.BlockSpec((tr * SUBLANES, LANES), row_map),
            pl.BlockSpec((1, D, fc), lambda m, c, te, nu: (exp(m, te, nu), 0, col(m, c, nu))),
            pl.BlockSpec((1, D, fc), lambda m, c, te, nu: (exp(m, te, nu), 0, col(m, c, nu))),
            pl.BlockSpec((1, fc, D), lambda m, c, te, nu: (exp(m, te, nu), col(m, c, nu), 0)),
        ],
        out_specs=pl.BlockSpec((tr * SUBLANES, LANES), lambda m, c, te, nu: (m, 0)),
        scratch_shapes=[pltpu.VMEM((tr, D), BF16), pltpu.VMEM((tr, D), F32)],
    )
    return pl.pallas_call(
        _expert_kernel,
        grid_spec=grid_spec,
        out_shape=jax.ShapeDtypeStruct(xs.shape, F32),
        compiler_params=pltpu.CompilerParams(dimension_semantics=("arbitrary", "arbitrary")),
        name="moe_experts",
    )(tile_expert, n_used, xs, wg, wu, wd)


def _combine_kernel(dcur_ref, dnxt_ref, x_ref, gate_ref, fn_ref, ys_hbm, o_ref, buf_ref, sems, *, final_norm):
    tm = x_ref.shape[0]
    i = pl.program_id(0)
    slot = i % 2

    def row_copy(src_row, s, k, t):
        src = pl.multiple_of(src_row * SUBLANES, SUBLANES)
        dst = pl.multiple_of(t * SUBLANES, SUBLANES)
        return pltpu.make_async_copy(ys_hbm.at[pl.ds(src, SUBLANES), :],
                                     buf_ref.at[s, k, pl.ds(dst, SUBLANES), :], sems.at[s])

    def issue_tile(d_ref, s):
        def issue(t, _):
            for k in range(2):
                row_copy(d_ref[0, k, t], s, k, t).start(priority=k)
            return 0

        lax.fori_loop(0, tm, issue, 0, unroll=DMA_UNROLL)

    @pl.when(i == 0)
    def _():
        issue_tile(dcur_ref, 0)

    @pl.when(i + 1 < pl.num_programs(0))
    def _():
        issue_tile(dnxt_ref, 1 - slot)

    def drain(t, _):
        row_copy(0, slot, 0, 0).wait()
        row_copy(0, slot, 1, 0).wait()
        return 0

    lax.fori_loop(0, tm, drain, 0, unroll=DMA_UNROLL)
    g = gate_ref[...]
    y0 = _load_tile_rows(buf_ref.at[slot, 0], tm)
    y1 = _load_tile_rows(buf_ref.at[slot, 1], tm)
    xn = x_ref[...] + g[:, 0:1] * y0 + g[:, 1:2] * y1
    o_ref[...] = _rms(xn, fn_ref[...]) if final_norm else xn


def _combine(dst_c, x2d, gates_col, fnorm, ys, final_norm):
    T, D = x2d.shape
    nt, _, tm = dst_c.shape
    return pl.pallas_call(
        functools.partial(_combine_kernel, final_norm=final_norm),
        grid=(nt,),
        in_specs=[pl.BlockSpec((1, SUBLANES, tm), lambda i: (i, 0, 0), memory_space=pltpu.SMEM),
                  pl.BlockSpec((1, SUBLANES, tm), lambda i: (jnp.minimum(i + 1, nt - 1), 0, 0),
                               memory_space=pltpu.SMEM),
                  pl.BlockSpec((tm, D), lambda i: (i, 0)),
                  pl.BlockSpec((tm, SUBLANES), lambda i: (i, 0)),
                  pl.BlockSpec((1, D), lambda i: (0, 0)),
                  pl.BlockSpec(memory_space=pl.ANY)],
        out_specs=pl.BlockSpec((tm, D), lambda i: (i, 0)),
        scratch_shapes=[pltpu.VMEM((2, 2, tm * SUBLANES, LANES), F32), pltpu.SemaphoreType.DMA((2,))],
        out_shape=jax.ShapeDtypeStruct((T, D), F32),
        compiler_params=pltpu.CompilerParams(dimension_semantics=("arbitrary",)),
        name="moe_combine",
    )(dst_c, dst_c, x2d, gates_col, fnorm, ys)


def _swap_halves_cols(w):
    half = w.shape[-1] // 2
    return jnp.concatenate([w[..., half:], w[..., :half]], axis=-1)


def _pad_cols(w, width):
    return jnp.pad(w, [(0, 0)] * (w.ndim - 1) + [(0, width - w.shape[-1])])


def _prep_w_in(w):
    kpe0 = _C_CKV + MLA_KV_RANK - 0
    kpe = w[:, kpe0:kpe0 + MLA_ROPE]
    return jnp.concatenate([
        w[:, :kpe0],
        _pad_cols(kpe, LANES), _pad_cols(_swap_halves_cols(kpe), LANES),
        w[:, kpe0 + MLA_ROPE:],
    ], axis=1).astype(BF16)


def _prep_w_uq(w):
    w = w.reshape(MLA_Q_RANK, MLA_HEADS, MLA_NOPE + MLA_ROPE)
    nope, pe = w[..., :MLA_NOPE], w[..., MLA_NOPE:]
    out = jnp.concatenate([nope, _pad_cols(pe, LANES), _pad_cols(_swap_halves_cols(pe), LANES)], axis=-1)
    return out.reshape(MLA_Q_RANK, MLA_HEADS * _Q_HEAD_COLS).astype(BF16)


def _prep_w_ukv(w):
    w = w.reshape(MLA_KV_RANK, MLA_HEADS, MLA_NOPE + MLA_V)
    wukt = jnp.transpose(w[..., :MLA_NOPE], (1, 2, 0)).reshape(MLA_HEADS * MLA_NOPE, MLA_KV_RANK)
    wuv = w[..., MLA_NOPE:].reshape(MLA_KV_RANK, MLA_HEADS * MLA_V)
    return wukt.astype(BF16), wuv.astype(BF16)


def _prep_w_kpe_t(w_ext):
    return w_ext[:, _C_KPE:_C_XP].T


def _rope_tables(seq):
    pos = jnp.arange(seq, dtype=F32)
    inv_freq = 1.0 / (ROPE_THETA ** (jnp.arange(0, MLA_ROPE, 2, dtype=F32) / MLA_ROPE))
    ang = pos[:, None] * inv_freq[None, :]
    cos, sin = jnp.cos(ang), jnp.sin(ang)
    cpad = _pad_cols(jnp.concatenate([cos, cos], axis=-1), LANES)
    spad = _pad_cols(jnp.concatenate([-sin, sin], axis=-1), LANES)
    return cpad, spad


def _block_diag_pool(w):
    G, c, _ = w.shape
    eye = jnp.eye(G, dtype=w.dtype)
    return (eye[:, None, :, None] * w[:, :, None, :]).reshape(G * c, G * c).astype(BF16)


def _moe_layout(counts, n_tiles, tile):
    tiles_per = (counts + tile - 1) // tile
    ends = jnp.cumsum(tiles_per)
    starts = (ends - tiles_per) * tile
    tile_expert = jnp.sum((jnp.arange(n_tiles)[:, None] >= ends[None, :]).astype(jnp.int32), axis=1)
    tile_expert = jnp.minimum(tile_expert, N_EXPERTS - 1)
    return starts.astype(jnp.int32), tile_expert.astype(jnp.int32), ends[-1:].astype(jnp.int32)


def kernel(x, attn_norm, w_in, hgrn_lower_bounds, hgrn_out_norm, mla_q_norm, mla_w_uq, mla_kv_norm,
           mla_w_ukv, pool_w, pool_scale, w_o, ffn_norm, dense_w_gate, dense_w_up, dense_w_down,
           moe_router, moe_w_gate, moe_w_up, moe_w_down, final_norm):
    B, S, D = x.shape
    T = B * S
    depth = w_in.shape[0]
    cpad, spad = _rope_tables(S)
    p_lb = jax.nn.softmax(hgrn_lower_bounds.astype(F32), axis=0)
    lbs = jnp.cumsum(p_lb, axis=0) - p_lb[0:1]

    assert depth % 2 == 0, "the final RMSNorm is fused into the last (MoE) layer's combine kernel"
    for l in range(depth):
        wukt, wuv = _prep_w_ukv(mla_w_ukv[l])
        w_ext = _prep_w_in(w_in[l])
        hg, xp, q, kt, v = _in_proj(
            x, attn_norm[l][None], w_ext, mla_q_norm[l][None], _prep_w_uq(mla_w_uq[l]),
            mla_kv_norm[l][None], wukt, wuv, _prep_w_kpe_t(w_ext), cpad, spad)
        o_a = _hgrn(hg, lbs[l][None], hgrn_out_norm[l][None])
        o_b = _attention(q, kt, v)
        moe_layer = (l % 2 == 1)
        x, h = _mix_out(x, o_a, o_b, xp, _block_diag_pool(pool_w[l]), pool_scale[l][None],
                        w_o[l].astype(BF16), ffn_norm[l][None], moe_layer)
        j = l // 2
        if not moe_layer:
            x = _dense_ffn(x.reshape(T, D), h.reshape(T, D), dense_w_gate[j].astype(BF16),
                           dense_w_up[j].astype(BF16), dense_w_down[j].astype(BF16)).reshape(B, S, D)
        else:
            meta, gates, counts = _router(h, moe_router[j].T)
            n_tiles = (2 * T) // MOE_TILE + N_EXPERTS
            starts, tile_expert, n_used = _moe_layout(counts[:, 0], n_tiles, MOE_TILE)
            dst = _dest_rows(starts, meta)
            xs = _scatter_rows(dst, h, jnp.zeros((n_tiles * MOE_TILE * SUBLANES, LANES), F32))
            ys = _expert_ffn(tile_expert, n_used, xs, moe_w_gate[j].astype(BF16), moe_w_up[j].astype(BF16),
                             moe_w_down[j].astype(BF16))
            ct = min(COMBINE_TILE, T)
            dst_c = dst.transpose(1, 0, 2).reshape(SUBLANES, T // ct, ct).transpose(1, 0, 2)
            gates_col = gates.transpose(0, 2, 1).reshape(T, SUBLANES)
            last = (l == depth - 1)
            y = _combine(dst_c, x.reshape(T, D), gates_col, final_norm[None], ys, last)
            x = y.reshape(B, S, D)
    return x
```

```python
import functools
import math

import jax
import jax.numpy as jnp
import numpy as np
from jax import lax
from jax.experimental import pallas as pl
from jax.experimental.pallas import tpu as pltpu

F32 = jnp.float32
BF16 = jnp.bfloat16

HG_HEADS = 4
HG_KEY_DIM = 128
HG_VAL_DIM = 64
HG_KEY_WIDTH = HG_HEADS * HG_KEY_DIM
HG_WIDTH = HG_HEADS * HG_VAL_DIM
MIN_FORGET = 1e-20
MLA_HEADS = 4
MLA_Q_RANK = 256
MLA_KV_RANK = 128
MLA_NOPE = 128
MLA_ROPE = 64
MLA_V = 128
MLA_WIDTH = MLA_HEADS * MLA_V
ROPE_THETA = 10000.0
MASK_VALUE = -1e30
POOL_GROUPS = 4
POOL_WINDOWS = (2, 4, 8, 16)
POOL_WIDTH = 256
POOL_GROUP_DIM = POOL_WIDTH // POOL_GROUPS
N_EXPERTS = 8
EPS = 1e-6

LANES = 128
SUBLANES = 8
QK_PAD = 256
V_EXT = 256

TOKEN_TILE = 1024
HGRN_CHUNK = 256
ATTN_BK = 512
FF_CHUNK = 512
MOE_FF_CHUNK = 512
MOE_TILE = 1024
ROUTE_TILE = 512
COMBINE_TILE = 256
POOL_HALO = 16
DMA_UNROLL = 8

_C_HG = 0
_C_CQ = 2 * HG_KEY_WIDTH + 2 * HG_WIDTH
_C_CKV = _C_CQ + MLA_Q_RANK
_C_KPE = _C_CKV + MLA_KV_RANK
_C_KPES = _C_KPE + LANES
_C_XP = _C_KPES + LANES
_C_END = _C_XP + POOL_WIDTH
_Q_HEAD_COLS = 3 * LANES


def _rms(x, g):
    return x * lax.rsqrt(jnp.mean(x * x, axis=-1, keepdims=True) + EPS) * g


def _dot(a, b):
    return jnp.dot(a, b, preferred_element_type=F32)


def _dot_nt(a, b):
    return lax.dot_general(a, b, (((1,), (1,)), ((), ())), preferred_element_type=F32)


def _dot_tn(a, b):
    return lax.dot_general(a, b, (((0,), (0,)), ((), ())), preferred_element_type=F32)


def _const_spec(shape):
    nd = len(shape)
    return pl.BlockSpec(shape, lambda *_: (0,) * nd)


def _load_tile_rows(ref, n):
    return jnp.concatenate([ref[pl.ds(s, n, stride=SUBLANES), :] for s in range(SUBLANES)], axis=1)


def _store_tile_rows(ref, val):
    n = val.shape[0]
    for s in range(SUBLANES):
        ref[pl.ds(s, n, stride=SUBLANES), :] = val[:, s * LANES:(s + 1) * LANES]


def _in_proj_kernel(x_ref, g_ref, w_ref, qn_ref, wuq_ref, kvn_ref, wukt_ref, wuv_ref, wkpet_ref,
                    cpad_ref, spad_ref, cpadt_ref, spadt_ref,
                    hg_ref, xp_ref, q_ref, kt_ref, v_ref):
    h = _rms(x_ref[0], g_ref[...]).astype(BF16)
    hg_ref[0] = _dot(h, w_ref[:, _C_HG:_C_CQ])
    xp_ref[0] = _dot(h, w_ref[:, _C_XP:_C_END])
    cpad = cpad_ref[...]
    spad = spad_ref[...]
    scale = (MLA_NOPE + MLA_ROPE) ** -0.5 * math.log2(math.e)

    cq = _dot(h, w_ref[:, _C_CQ:_C_CKV])
    cqn = _rms(cq, qn_ref[...]).astype(BF16)
    for hd in range(MLA_HEADS):
        qh = _dot(cqn, wuq_ref[:, hd * _Q_HEAD_COLS:(hd + 1) * _Q_HEAD_COLS])
        q_ref[0, hd, :, 0:LANES] = (qh[:, 0:LANES] * scale).astype(BF16)
        pe = qh[:, LANES:2 * LANES] * cpad + qh[:, 2 * LANES:3 * LANES] * spad
        q_ref[0, hd, :, LANES:QK_PAD] = (pe * scale).astype(BF16)

    ckv = _dot(h, w_ref[:, _C_CKV:_C_KPE])
    ckvn = _rms(ckv, kvn_ref[...]).astype(BF16)
    kpet = (_dot_nt(wkpet_ref[0:LANES, :], h) * cpadt_ref[...]
            + _dot_nt(wkpet_ref[LANES:2 * LANES, :], h) * spadt_ref[...]).astype(BF16)
    tm = ckvn.shape[0]
    ones_col = (lax.broadcasted_iota(jnp.int32, (tm, LANES), 1) == 0).astype(BF16)
    for hd in range(MLA_HEADS):
        kt_ref[0, hd, 0:LANES, :] = _dot_nt(wukt_ref[hd * LANES:(hd + 1) * LANES, :], ckvn).astype(BF16)
        kt_ref[0, hd, LANES:QK_PAD, :] = kpet
        v_ref[0, hd, :, 0:MLA_V] = _dot(ckvn, wuv_ref[:, hd * MLA_V:(hd + 1) * MLA_V]).astype(BF16)
        v_ref[0, hd, :, MLA_V:V_EXT] = ones_col


def _in_proj(x, g, w_ext, qn, wuq_ext, kvn, wukt, wuv, wkpet, cpad, spad):
    B, S, D = x.shape
    tm = min(TOKEN_TILE, S)
    grid = (B, S // tm)
    n_hg = _C_CQ
    out_shape = (
        jax.ShapeDtypeStruct((B, S, n_hg), F32),
        jax.ShapeDtypeStruct((B, S, POOL_WIDTH), F32),
        jax.ShapeDtypeStruct((B, MLA_HEADS, S, QK_PAD), BF16),
        jax.ShapeDtypeStruct((B, MLA_HEADS, QK_PAD, S), BF16),
        jax.ShapeDtypeStruct((B, MLA_HEADS, S, V_EXT), BF16),
    )
    return pl.pallas_call(
        _in_proj_kernel,
        grid=grid,
        in_specs=[
            pl.BlockSpec((1, tm, D), lambda b, i: (b, i, 0)),
            _const_spec((1, D)),
            _const_spec(w_ext.shape),
            _const_spec((1, MLA_Q_RANK)),
            _const_spec(wuq_ext.shape),
            _const_spec((1, MLA_KV_RANK)),
            _const_spec(wukt.shape),
            _const_spec(wuv.shape),
            _const_spec(wkpet.shape),
            pl.BlockSpec((tm, LANES), lambda b, i: (i, 0)),
            pl.BlockSpec((tm, LANES), lambda b, i: (i, 0)),
            pl.BlockSpec((LANES, tm), lambda b, i: (0, i)),
            pl.BlockSpec((LANES, tm), lambda b, i: (0, i)),
        ],
        out_specs=(
            pl.BlockSpec((1, tm, n_hg), lambda b, i: (b, i, 0)),
            pl.BlockSpec((1, tm, POOL_WIDTH), lambda b, i: (b, i, 0)),
            pl.BlockSpec((1, MLA_HEADS, tm, QK_PAD), lambda b, i: (b, 0, i, 0)),
            pl.BlockSpec((1, MLA_HEADS, QK_PAD, tm), lambda b, i: (b, 0, 0, i)),
            pl.BlockSpec((1, MLA_HEADS, tm, V_EXT), lambda b, i: (b, 0, i, 0)),
        ),
        out_shape=out_shape,
        compiler_params=pltpu.CompilerParams(dimension_semantics=("arbitrary", "arbitrary")),
        name="in_proj",
    )(x, g, w_ext, qn, wuq_ext, kvn, wukt, wuv, wkpet, cpad, spad, cpad.T, spad.T)


def _split3(x):
    hi = x.astype(BF16)
    r1 = x - hi.astype(F32)
    mid = r1.astype(BF16)
    lo = (r1 - mid.astype(F32)).astype(BF16)
    return hi, mid, lo


def _hgrn_kernel(q_ref, f_ref, i_ref, g_ref, lb_ref, on_ref, o_ref, st_ref):
    C = q_ref.shape[1]
    KW = HG_KEY_WIDTH

    @pl.when(pl.program_id(1) == 0)
    def _():
        st_ref[...] = jnp.zeros_like(st_ref)

    lb = lb_ref[...]
    z = f_ref[0]
    forget = lb + (1.0 - lb) * jax.nn.sigmoid(z)
    lg = jnp.log2(jnp.maximum(forget, MIN_FORGET))
    kk = (1.0 - lb) * jax.nn.sigmoid(-z)
    qq = jax.nn.silu(q_ref[0])
    vv = i_ref[0]

    row = lax.broadcasted_iota(jnp.int32, (C, C), 0)
    col = lax.broadcasted_iota(jnp.int32, (C, C), 1)
    tril = (col <= row).astype(BF16)
    b = sum(_dot(tril, part) for part in _split3(lg))

    sides, masks = [], []
    sub = lax.broadcasted_iota(jnp.int32, (C, KW), 0)
    half = C // 2
    while half >= 4:
        blk = 2 * half
        b3 = b.reshape(C // blk, blk, KW)
        ref_row = jnp.broadcast_to(b3[:, half - 1:half, :], b3.shape).reshape(C, KW)
        e = jnp.exp2(-jnp.abs(b - ref_row))
        x = (jnp.where((sub & half) != 0, qq, kk) * e).astype(BF16)
        sides.append((x, x))
        shift = int(math.log2(blk))
        masks.append(((row >> shift) == (col >> shift)) & ((row & half) != 0) & ((col & half) == 0))
        half //= 2
    b3 = b.reshape(C // 8, 8, KW)
    mid_lo = 0.5 * (b3[:, 0:1, :] + b3[:, 3:4, :])
    mid_hi = 0.5 * (b3[:, 4:5, :] + b3[:, 7:8, :])
    sub8 = lax.broadcasted_iota(jnp.int32, b3.shape, 1)
    mid = jnp.where(sub8 < 4, mid_lo, mid_hi).reshape(C, KW)
    sides.append(((qq * jnp.exp2(b - mid)).astype(BF16), (kk * jnp.exp2(mid - b)).astype(BF16)))
    masks.append(((row >> 2) == (col >> 2)) & (col <= row))

    lane_v = lax.broadcasted_iota(jnp.int32, (C, HG_WIDTH), 1)
    o = _dot_nt((qq * jnp.exp2(b)).astype(BF16), st_ref[...].astype(BF16))
    for hd in range(HG_HEADS):
        ks = slice(hd * HG_KEY_DIM, (hd + 1) * HG_KEY_DIM)
        a = jnp.zeros((C, C), F32)
        for (qt, kt), m in zip(sides, masks):
            a = jnp.where(m, _dot_nt(qt[:, ks], kt[:, ks]), a)
        v_h = jnp.where((lane_v >> 6) == hd, vv, 0.0).astype(BF16)
        o = o + _dot(a.astype(BF16), v_h)

    b_last = b[C - 1:C, :]
    khat = (kk * jnp.exp2(b_last - b)).astype(BF16)
    st_row = lax.broadcasted_iota(jnp.int32, (HG_WIDTH, KW), 0)
    st_col = lax.broadcasted_iota(jnp.int32, (HG_WIDTH, KW), 1)
    new_st = st_ref[...] * jnp.exp2(b_last) + _dot_tn(vv.astype(BF16), khat)
    st_ref[...] = jnp.where((st_row >> 6) == (st_col >> 7), new_st, 0.0)

    gi = lax.broadcasted_iota(jnp.int32, (HG_WIDTH, HG_WIDTH), 0)
    gj = lax.broadcasted_iota(jnp.int32, (HG_WIDTH, HG_WIDTH), 1)
    grp = ((gi >> 6) == (gj >> 6)).astype(BF16)
    ssq = sum(_dot(part, grp) for part in _split3(o * o))
    on = o * lax.rsqrt(ssq * (1.0 / HG_VAL_DIM) + EPS) * on_ref[...]
    o_ref[0] = (on * jax.nn.silu(g_ref[0])).astype(o_ref.dtype)


def _hgrn(hg, lb, out_norm):
    B, S, _ = hg.shape
    C = min(HGRN_CHUNK, S)
    kb = HG_KEY_WIDTH // HG_KEY_WIDTH
    return pl.pallas_call(
        _hgrn_kernel,
        grid=(B, S // C),
        in_specs=[
            pl.BlockSpec((1, C, HG_KEY_WIDTH), lambda b, c: (b, c, 0)),
            pl.BlockSpec((1, C, HG_KEY_WIDTH), lambda b, c: (b, c, kb)),
            pl.BlockSpec((1, C, HG_WIDTH), lambda b, c: (b, c, 2 * HG_KEY_WIDTH // HG_WIDTH)),
            pl.BlockSpec((1, C, HG_WIDTH), lambda b, c: (b, c, 2 * HG_KEY_WIDTH // HG_WIDTH + 1)),
            _const_spec((1, HG_KEY_WIDTH)),
            _const_spec((1, HG_WIDTH)),
        ],
        out_specs=pl.BlockSpec((1, C, HG_WIDTH), lambda b, c: (b, c, 0)),
        out_shape=jax.ShapeDtypeStruct((B, S, HG_WIDTH), BF16),
        scratch_shapes=[pltpu.VMEM((HG_WIDTH, HG_KEY_WIDTH), F32)],
        compiler_params=pltpu.CompilerParams(dimension_semantics=("arbitrary", "arbitrary")),
        name="hgrn2",
    )(hg, hg, hg, hg, lb, out_norm)


def _attn_kernel(q_ref, kt_ref, v_ref, o_ref, s0_ref, s1_ref, acc_ref, *, bk):
    bq = q_ref.shape[2]
    i = pl.program_id(2)
    q = q_ref[0, 0]

    def causal(s, key0):
        qry = lax.broadcasted_iota(jnp.int32, (bq, bk), 0)
        key = key0 + lax.broadcasted_iota(jnp.int32, (bq, bk), 1)
        return jnp.where(key <= qry, s, MASK_VALUE)

    def scores(blk, s_ref):
        start = pl.multiple_of(blk * bk, bk)
        s_ref[...] = _dot(q, kt_ref[0, 0, :, pl.ds(start, bk)])

    def softmax_pv(blk, s_ref, m, key0=None):
        s = s_ref[...]
        if key0 is not None:
            s = causal(s, key0)
        m_new = jnp.maximum(m, jnp.max(s, axis=1, keepdims=True))
        p = jnp.exp2(s - m_new).astype(BF16)
        start = pl.multiple_of(blk * bk, bk)
        acc_ref[...] = jnp.exp2(m - m_new) * acc_ref[...] + _dot(p, v_ref[0, 0, pl.ds(start, bk), :])
        return m_new

    acc_ref[...] = jnp.zeros_like(acc_ref)
    scores(0, s0_ref)

    def pair(t, m):
        scores(2 * t + 1, s1_ref)
        m = softmax_pv(2 * t, s0_ref, m)
        scores(2 * t + 2, s0_ref)
        return softmax_pv(2 * t + 1, s1_ref, m)

    m = lax.fori_loop(0, i, pair, jnp.full((bq, 1), -jnp.inf, F32))
    scores(2 * i + 1, s1_ref)
    m = softmax_pv(2 * i, s0_ref, m, 0)
    m = softmax_pv(2 * i + 1, s1_ref, m, bk)
    acc = acc_ref[...]
    o_ref[0] = (acc[:, 0:MLA_V] / acc[:, MLA_V:MLA_V + 1]).astype(o_ref.dtype)


def _attention(q, kt, v):
    B, H, S, _ = q.shape
    bk = min(ATTN_BK, S // 2)
    bq = 2 * bk
    return pl.pallas_call(
        functools.partial(_attn_kernel, bk=bk),
        grid=(B, H, S // bq),
        in_specs=[
            pl.BlockSpec((1, 1, bq, QK_PAD), lambda b, h, i: (b, h, i, 0)),
            pl.BlockSpec((1, 1, QK_PAD, S), lambda b, h, i: (b, h, 0, 0)),
            pl.BlockSpec((1, 1, S, V_EXT), lambda b, h, i: (b, h, 0, 0)),
        ],
        out_specs=pl.BlockSpec((1, bq, MLA_V), lambda b, h, i: (b, i, h)),
        out_shape=jax.ShapeDtypeStruct((B, S, H * MLA_V), BF16),
        scratch_shapes=[pltpu.VMEM((bq, bk), F32), pltpu.VMEM((bq, bk), F32), pltpu.VMEM((bq, V_EXT), F32)],
        compiler_params=pltpu.CompilerParams(dimension_semantics=("arbitrary", "arbitrary", "arbitrary")),
        name="mla_attention",
    )(q, kt, v)


def _mix_out_kernel(x_ref, oa_ref, ob_ref, xp_ref, wpool_ref, pscale_ref, wo_ref, fn_ref,
                    xo_ref, h_ref, halo_ref, *, tile_rows):
    tm = xp_ref.shape[1]
    i = pl.program_id(1)

    @pl.when(i == 0)
    def _():
        halo_ref[...] = jnp.zeros_like(halo_ref)

    xp = xp_ref[0]
    xx = jnp.concatenate([halo_ref[...], xp], axis=0)
    halo_ref[...] = xp[tm - POOL_HALO:, :]

    w2 = xx[1:, :] + xx[:-1, :]
    w4 = w2[2:, :] + w2[:-2, :]
    w8 = w4[4:, :] + w4[:-4, :]
    w16 = w8[8:, :] + w8[:-8, :]
    sums = (w2[POOL_HALO - 1:, :], w4[POOL_HALO - 3:, :], w8[POOL_HALO - 7:, :], w16[POOL_HALO - 15:, :])
    t = i * tm + lax.broadcasted_iota(jnp.int32, (tm, POOL_WIDTH), 0)
    lane = lax.broadcasted_iota(jnp.int32, (tm, POOL_WIDTH), 1)
    pooled = jnp.zeros((tm, POOL_WIDTH), F32)
    for gi, w in enumerate(POOL_WINDOWS):
        cnt = jnp.minimum(t + 1, w).astype(F32)
        pooled = jnp.where((lane >> 6) == gi, sums[gi] / cnt, pooled)
    pooled = pooled - xp
    oc = _dot(pooled.astype(BF16), wpool_ref[...]) * pscale_ref[...]

    y = _dot(oa_ref[0], wo_ref[0:HG_WIDTH, :])
    y = y + _dot(ob_ref[0], wo_ref[HG_WIDTH:HG_WIDTH + MLA_WIDTH, :])
    y = y + _dot(oc.astype(BF16), wo_ref[HG_WIDTH + MLA_WIDTH:, :])
    xn = x_ref[0] + y
    xo_ref[0] = xn
    h = _rms(xn, fn_ref[...])
    if tile_rows:
        _store_tile_rows(h_ref, h)
    else:
        h_ref[0] = h.astype(h_ref.dtype)


def _mix_out(x, oa, ob, xp, wpool_bd, pscale, wo, fnorm, tile_rows):
    B, S, D = x.shape
    tm = min(TOKEN_TILE, S)
    nt = S // tm
    tok = lambda w: pl.BlockSpec((1, tm, w), lambda b, i: (b, i, 0))
    if tile_rows:
        assert D == SUBLANES * LANES
        h_spec = pl.BlockSpec((tm * SUBLANES, LANES), lambda b, i: (b * nt + i, 0))
        h_shape = jax.ShapeDtypeStruct((B * S * SUBLANES, LANES), F32)
    else:
        h_spec, h_shape = tok(D), jax.ShapeDtypeStruct((B, S, D), BF16)
    return pl.pallas_call(
        functools.partial(_mix_out_kernel, tile_rows=tile_rows),
        grid=(B, nt),
        in_specs=[tok(D), tok(HG_WIDTH), tok(MLA_WIDTH), tok(POOL_WIDTH),
                  _const_spec(wpool_bd.shape), _const_spec((1, POOL_WIDTH)), _const_spec(wo.shape),
                  _const_spec((1, D))],
        out_specs=(tok(D), h_spec),
        out_shape=(jax.ShapeDtypeStruct((B, S, D), F32), h_shape),
        scratch_shapes=[pltpu.VMEM((POOL_HALO, POOL_WIDTH), F32)],
        compiler_params=pltpu.CompilerParams(dimension_semantics=("arbitrary", "arbitrary")),
        name="mix_out",
    )(x, oa, ob, xp, wpool_bd, pscale, wo, fnorm)


def _dense_ffn_kernel(x_ref, h_ref, wg_ref, wu_ref, wd_ref, o_ref):
    h = h_ref[...]
    acc = x_ref[...]
    dff = wg_ref.shape[1]
    for c0 in range(0, dff, FF_CHUNK):
        g = _dot(h, wg_ref[:, c0:c0 + FF_CHUNK])
        u = _dot(h, wu_ref[:, c0:c0 + FF_CHUNK])
        acc = acc + _dot((jax.nn.silu(g) * u).astype(BF16), wd_ref[c0:c0 + FF_CHUNK, :])
    o_ref[...] = acc


def _dense_ffn(x2d, h2d, wg, wu, wd):
    T, D = x2d.shape
    tm = min(TOKEN_TILE, T)
    tok = pl.BlockSpec((tm, D), lambda i: (i, 0))
    return pl.pallas_call(
        _dense_ffn_kernel,
        grid=(T // tm,),
        in_specs=[tok, tok, _const_spec(wg.shape), _const_spec(wu.shape), _const_spec(wd.shape)],
        out_specs=tok,
        out_shape=jax.ShapeDtypeStruct((T, D), F32),
        compiler_params=pltpu.CompilerParams(dimension_semantics=("arbitrary",)),
        name="dense_ffn",
    )(x2d, h2d, wg, wu, wd)


def _router_kernel(h_ref, rt_ref, meta_ref, gate_ref, cnt_ref, run_ref):
    tm = h_ref.shape[0] // SUBLANES
    E = N_EXPERTS

    @pl.when(pl.program_id(0) == 0)
    def _():
        run_ref[...] = jnp.zeros_like(run_ref)

    logits = lax.dot_general(rt_ref[...], _load_tile_rows(h_ref, tm), (((1,), (1,)), ((), ())),
                             precision=lax.Precision.HIGHEST, preferred_element_type=F32)
    eid = lax.broadcasted_iota(jnp.int32, (E, tm), 0)
    m1 = jnp.max(logits, axis=0, keepdims=True)
    i1 = jnp.min(jnp.where(logits == m1, eid, E), axis=0, keepdims=True)
    rest = jnp.where(eid == i1, -jnp.inf, logits)
    m2 = jnp.max(rest, axis=0, keepdims=True)
    i2 = jnp.min(jnp.where(rest == m2, eid, E), axis=0, keepdims=True)
    e2 = jnp.exp(m2 - m1)
    g1 = 1.0 / (1.0 + e2)
    g2 = e2 / (1.0 + e2)

    sel = ((eid == i1) | (eid == i2))
    r = lax.broadcasted_iota(jnp.int32, (tm, tm), 0)
    c = lax.broadcasted_iota(jnp.int32, (tm, tm), 1)
    before = (r < c).astype(BF16)
    excl = _dot(sel.astype(BF16), before) + run_ref[:, 0:1]
    rank1 = jnp.sum(jnp.where(eid == i1, excl, 0.0), axis=0, keepdims=True).astype(jnp.int32)
    rank2 = jnp.sum(jnp.where(eid == i2, excl, 0.0), axis=0, keepdims=True).astype(jnp.int32)
    run_ref[...] = run_ref[...] + jnp.sum(sel.astype(F32), axis=1, keepdims=True)
    cnt_ref[...] = run_ref[...].astype(jnp.int32)

    zi = jnp.zeros((1, tm), jnp.int32)
    meta_ref[0] = jnp.concatenate([i1, i2, rank1, rank2, zi, zi, zi, zi], axis=0)
    zf = jnp.zeros((1, tm), F32)
    gate_ref[0] = jnp.concatenate([g1, g2, zf, zf, zf, zf, zf, zf], axis=0)


def _router(h_rows, router_t):
    T = h_rows.shape[0] // SUBLANES
    tm = min(ROUTE_TILE, T)
    nt = T // tm
    return pl.pallas_call(
        _router_kernel,
        grid=(nt,),
        in_specs=[pl.BlockSpec((tm * SUBLANES, LANES), lambda i: (i, 0)), _const_spec(router_t.shape)],
        out_specs=(pl.BlockSpec((1, SUBLANES, tm), lambda i: (i, 0, 0)),
                   pl.BlockSpec((1, SUBLANES, tm), lambda i: (i, 0, 0)),
                   _const_spec((N_EXPERTS, LANES))),
        out_shape=(jax.ShapeDtypeStruct((nt, SUBLANES, tm), jnp.int32),
                   jax.ShapeDtypeStruct((nt, SUBLANES, tm), F32),
                   jax.ShapeDtypeStruct((N_EXPERTS, LANES), jnp.int32)),
        scratch_shapes=[pltpu.VMEM((N_EXPERTS, LANES), F32)],
        compiler_params=pltpu.CompilerParams(dimension_semantics=("arbitrary",)),
        name="moe_router",
    )(h_rows, router_t)


def _dest_kernel(start_ref, meta_ref, dst_ref):
    meta = meta_ref[0]
    rows = []
    for k in range(2):
        e = meta[k:k + 1, :]
        base = jnp.zeros_like(e)
        for ex in range(N_EXPERTS):
            base = jnp.where(e == ex, start_ref[ex], base)
        rows.append(base + meta[2 + k:3 + k, :])
    dst_ref[0] = jnp.concatenate(rows + [jnp.zeros_like(rows[0])] * (SUBLANES - 2), axis=0)


def _dest_rows(starts, meta):
    nt, _, tm = meta.shape
    spec = pl.BlockSpec((1, SUBLANES, tm), lambda i, s: (i, 0, 0))
    return pl.pallas_call(
        _dest_kernel,
        grid_spec=pltpu.PrefetchScalarGridSpec(num_scalar_prefetch=1, grid=(nt,), in_specs=[spec], out_specs=spec),
        out_shape=jax.ShapeDtypeStruct(meta.shape, jnp.int32),
        compiler_params=pltpu.CompilerParams(dimension_semantics=("arbitrary",)),
        name="moe_dest",
    )(starts, meta)


def _scatter_kernel(zrow_ref, dst_ref, h_ref, xs_hbm, zero_ref, sem, zsem):
    tm = dst_ref.shape[2]

    @pl.when(pl.program_id(0) == 0)
    def _():
        zero_ref[...] = jnp.zeros_like(zero_ref)
        tile_rows = zero_ref.shape[0]

        def fill(row):
            start = pl.multiple_of(row * SUBLANES, SUBLANES)
            return pltpu.make_async_copy(zero_ref, xs_hbm.at[pl.ds(start, tile_rows), :], zsem)

        fills = [fill(zrow_ref[e]) for e in range(N_EXPERTS)]
        for f in fills:
            f.start()
        for f in fills:
            f.wait()
        n_tiles = xs_hbm.shape[0] // tile_rows
        for j in range(N_EXPERTS):
            @pl.when(n_tiles - 1 - j >= zrow_ref[N_EXPERTS])
            def _():
                f = fill((n_tiles - 1 - j) * (tile_rows // SUBLANES))
                f.start()
                f.wait()

    def row_copy(src_row, dst_row):
        src = pl.multiple_of(src_row * SUBLANES, SUBLANES)
        dst = pl.multiple_of(dst_row * SUBLANES, SUBLANES)
        return pltpu.make_async_copy(h_ref.at[pl.ds(src, SUBLANES), :], xs_hbm.at[pl.ds(dst, SUBLANES), :], sem)

    def issue(t, _):
        for k in range(2):
            row_copy(t, dst_ref[0, k, t]).start(priority=k)
        return 0

    lax.fori_loop(0, tm, issue, 0, unroll=DMA_UNROLL)

    def drain(t, _):
        row_copy(0, 0).wait()
        row_copy(0, 0).wait()
        return 0

    lax.fori_loop(0, tm, drain, 0, unroll=DMA_UNROLL)


def _scatter_rows(zero_rows, dst, h_rows, n_rows):
    nt, _, tm = dst.shape
    grid_spec = pltpu.PrefetchScalarGridSpec(
        num_scalar_prefetch=1,
        grid=(nt,),
        in_specs=[pl.BlockSpec((1, SUBLANES, tm), lambda i, z: (i, 0, 0), memory_space=pltpu.SMEM),
                  pl.BlockSpec((tm * SUBLANES, LANES), lambda i, z: (i, 0))],
        out_specs=pl.BlockSpec(memory_space=pl.ANY),
        scratch_shapes=[pltpu.VMEM((MOE_TILE * SUBLANES, LANES), F32), pltpu.SemaphoreType.DMA(()),
                        pltpu.SemaphoreType.DMA(())],
    )
    return pl.pallas_call(
        _scatter_kernel,
        grid_spec=grid_spec,
        out_shape=jax.ShapeDtypeStruct((n_rows * SUBLANES, LANES), F32),
        compiler_params=pltpu.CompilerParams(dimension_semantics=("arbitrary",), has_side_effects=True),
        name="moe_scatter",
    )(zero_rows, dst, h_rows)


def _expert_kernel(te_ref, nu_ref, x_ref, wg_ref, wu_ref, wd_ref, y_ref, *, fc):
    m = pl.program_id(0)
    tr = x_ref.shape[0] // SUBLANES
    dff = wg_ref.shape[2]

    @pl.when(m < nu_ref[0])
    def _():
        xb = _load_tile_rows(x_ref, tr).astype(BF16)
        acc = jnp.zeros((tr, wd_ref.shape[2]), F32)
        for c0 in range(0, dff, fc):
            g = _dot(xb, wg_ref[0, :, c0:c0 + fc])
            u = _dot(xb, wu_ref[0, :, c0:c0 + fc])
            acc = acc + _dot((jax.nn.silu(g) * u).astype(BF16), wd_ref[0, c0:c0 + fc, :])
        _store_tile_rows(y_ref, acc)

    @pl.when(m >= nu_ref[0])
    def _():
        y_ref[...] = jnp.zeros_like(y_ref)


def _expert_ffn(tile_expert, n_used, xs, wg, wu, wd):
    E, D, F = wg.shape
    R = xs.shape[0] // SUBLANES
    tr = MOE_TILE
    n_tiles = R // tr
    fc = min(MOE_FF_CHUNK, F)

    def row_map(m, te, nu):
        return (jnp.minimum(m, nu[0] - 1), 0)

    def w_map(m, te, nu):
        return (te[jnp.minimum(m, nu[0] - 1)], 0, 0)

    resident = dict(pipeline_mode=pl.Buffered(1))
    grid_spec = pltpu.PrefetchScalarGridSpec(
        num_scalar_prefetch=2,
        grid=(n_tiles,),
        in_specs=[
            pl.BlockSpec((tr * SUBLANES, LANES), row_map),
            pl.BlockSpec((1, D, F), w_map, **resident),
            pl.BlockSpec((1, D, F), w_map, **resident),
            pl.BlockSpec((1, F, D), w_map, **resident),
        ],
        out_specs=pl.BlockSpec((tr * SUBLANES, LANES), lambda m, te, nu: (m, 0)),
    )
    return pl.pallas_call(
        functools.partial(_expert_kernel, fc=fc),
        grid_spec=grid_spec,
        out_shape=jax.ShapeDtypeStruct(xs.shape, F32),
        compiler_params=pltpu.CompilerParams(dimension_semantics=("arbitrary",)),
        name="moe_experts",
    )(tile_expert, n_used, xs, wg, wu, wd)


def _combine_kernel(dcur_ref, dnxt_ref, x_ref, gate_ref, fn_ref, ys_hbm, o_ref, buf_ref, sems, *, final_norm):
    tm = x_ref.shape[0]
    i = pl.program_id(0)
    slot = i % 2

    def row_copy(src_row, s, k, t):
        src = pl.multiple_of(src_row * SUBLANES, SUBLANES)
        dst = pl.multiple_of(t * SUBLANES, SUBLANES)
        return pltpu.make_async_copy(ys_hbm.at[pl.ds(src, SUBLANES), :],
                                     buf_ref.at[s, k, pl.ds(dst, SUBLANES), :], sems.at[s])

    def issue_tile(d_ref, s):
        def issue(t, _):
            for k in range(2):
                row_copy(d_ref[0, k, t], s, k, t).start(priority=k)
            return 0

        lax.fori_loop(0, tm, issue, 0, unroll=DMA_UNROLL)

    @pl.when(i == 0)
    def _():
        issue_tile(dcur_ref, 0)

    @pl.when(i + 1 < pl.num_programs(0))
    def _():
        issue_tile(dnxt_ref, 1 - slot)

    def drain(t, _):
        row_copy(0, slot, 0, 0).wait()
        row_copy(0, slot, 1, 0).wait()
        return 0

    lax.fori_loop(0, tm, drain, 0, unroll=DMA_UNROLL)
    g = gate_ref[...]
    y0 = _load_tile_rows(buf_ref.at[slot, 0], tm)
    y1 = _load_tile_rows(buf_ref.at[slot, 1], tm)
    xn = x_ref[...] + g[:, 0:1] * y0 + g[:, 1:2] * y1
    o_ref[...] = _rms(xn, fn_ref[...]) if final_norm else xn


def _combine(dst_c, x2d, gates_col, fnorm, ys, final_norm):
    T, D = x2d.shape
    nt, _, tm = dst_c.shape
    return pl.pallas_call(
        functools.partial(_combine_kernel, final_norm=final_norm),
        grid=(nt,),
        in_specs=[pl.BlockSpec((1, SUBLANES, tm), lambda i: (i, 0, 0), memory_space=pltpu.SMEM),
                  pl.BlockSpec((1, SUBLANES, tm), lambda i: (jnp.minimum(i + 1, nt - 1), 0, 0),
                               memory_space=pltpu.SMEM),
                  pl.BlockSpec((tm, D), lambda i: (i, 0)),
                  pl.BlockSpec((tm, SUBLANES), lambda i: (i, 0)),
                  pl.BlockSpec((1, D), lambda i: (0, 0)),
                  pl.BlockSpec(memory_space=pl.ANY)],
        out_specs=pl.BlockSpec((tm, D), lambda i: (i, 0)),
        scratch_shapes=[pltpu.VMEM((2, 2, tm * SUBLANES, LANES), F32), pltpu.SemaphoreType.DMA((2,))],
        out_shape=jax.ShapeDtypeStruct((T, D), F32),
        compiler_params=pltpu.CompilerParams(dimension_semantics=("arbitrary",)),
        name="moe_combine",
    )(dst_c, dst_c, x2d, gates_col, fnorm, ys)


def _swap_halves_cols(w):
    half = w.shape[-1] // 2
    return jnp.concatenate([w[..., half:], w[..., :half]], axis=-1)


def _pad_cols(w, width):
    return jnp.pad(w, [(0, 0)] * (w.ndim - 1) + [(0, width - w.shape[-1])])


def _prep_w_in(w):
    kpe0 = _C_CKV + MLA_KV_RANK - 0
    kpe = w[:, kpe0:kpe0 + MLA_ROPE]
    return jnp.concatenate([
        w[:, :kpe0],
        _pad_cols(kpe, LANES), _pad_cols(_swap_halves_cols(kpe), LANES),
        w[:, kpe0 + MLA_ROPE:],
    ], axis=1).astype(BF16)


def _prep_w_uq(w):
    w = w.reshape(MLA_Q_RANK, MLA_HEADS, MLA_NOPE + MLA_ROPE)
    nope, pe = w[..., :MLA_NOPE], w[..., MLA_NOPE:]
    out = jnp.concatenate([nope, _pad_cols(pe, LANES), _pad_cols(_swap_halves_cols(pe), LANES)], axis=-1)
    return out.reshape(MLA_Q_RANK, MLA_HEADS * _Q_HEAD_COLS).astype(BF16)


def _prep_w_ukv(w):
    w = w.reshape(MLA_KV_RANK, MLA_HEADS, MLA_NOPE + MLA_V)
    wukt = jnp.transpose(w[..., :MLA_NOPE], (1, 2, 0)).reshape(MLA_HEADS * MLA_NOPE, MLA_KV_RANK)
    wuv = w[..., MLA_NOPE:].reshape(MLA_KV_RANK, MLA_HEADS * MLA_V)
    return wukt.astype(BF16), wuv.astype(BF16)


def _prep_w_kpe_t(w_ext):
    return w_ext[:, _C_KPE:_C_XP].T


def _rope_tables(seq):
    pos = jnp.arange(seq, dtype=F32)
    inv_freq = 1.0 / (ROPE_THETA ** (jnp.arange(0, MLA_ROPE, 2, dtype=F32) / MLA_ROPE))
    ang = pos[:, None] * inv_freq[None, :]
    cos, sin = jnp.cos(ang), jnp.sin(ang)
    cpad = _pad_cols(jnp.concatenate([cos, cos], axis=-1), LANES)
    spad = _pad_cols(jnp.concatenate([-sin, sin], axis=-1), LANES)
    return cpad, spad


def _block_diag_pool(w):
    G, c, _ = w.shape
    eye = jnp.eye(G, dtype=w.dtype)
    return (eye[:, None, :, None] * w[:, :, None, :]).reshape(G * c, G * c).astype(BF16)


def _moe_layout(counts, n_tiles, tile):
    tiles_per = (counts + tile - 1) // tile
    ends = jnp.cumsum(tiles_per)
    starts = (ends - tiles_per) * tile
    tile_expert = jnp.sum((jnp.arange(n_tiles)[:, None] >= ends[None, :]).astype(jnp.int32), axis=1)
    tile_expert = jnp.minimum(tile_expert, N_EXPERTS - 1)
    zero_rows = jnp.minimum(starts + counts, (n_tiles - 1) * tile)
    zero_info = jnp.concatenate([zero_rows, ends[-1:]])
    return (starts.astype(jnp.int32), tile_expert.astype(jnp.int32), ends[-1:].astype(jnp.int32),
            zero_info.astype(jnp.int32))


def kernel(x, attn_norm, w_in, hgrn_lower_bounds, hgrn_out_norm, mla_q_norm, mla_w_uq, mla_kv_norm,
           mla_w_ukv, pool_w, pool_scale, w_o, ffn_norm, dense_w_gate, dense_w_up, dense_w_down,
           moe_router, moe_w_gate, moe_w_up, moe_w_down, final_norm):
    B, S, D = x.shape
    T = B * S
    depth = w_in.shape[0]
    cpad, spad = _rope_tables(S)
    p_lb = jax.nn.softmax(hgrn_lower_bounds.astype(F32), axis=0)
    lbs = jnp.cumsum(p_lb, axis=0) - p_lb[0:1]

    assert depth % 2 == 0, "the final RMSNorm is fused into the last (MoE) layer's combine kernel"
    for l in range(depth):
        wukt, wuv = _prep_w_ukv(mla_w_ukv[l])
        w_ext = _prep_w_in(w_in[l])
        hg, xp, q, kt, v = _in_proj(
            x, attn_norm[l][None], w_ext, mla_q_norm[l][None], _prep_w_uq(mla_w_uq[l]),
            mla_kv_norm[l][None], wukt, wuv, _prep_w_kpe_t(w_ext), cpad, spad)
        o_a = _hgrn(hg, lbs[l][None], hgrn_out_norm[l][None])
        o_b = _attention(q, kt, v)
        moe_layer = (l % 2 == 1)
        x, h = _mix_out(x, o_a, o_b, xp, _block_diag_pool(pool_w[l]), pool_scale[l][None],
                        w_o[l].astype(BF16), ffn_norm[l][None], moe_layer)
        j = l // 2
        if not moe_layer:
            x = _dense_ffn(x.reshape(T, D), h.reshape(T, D), dense_w_gate[j].astype(BF16),
                           dense_w_up[j].astype(BF16), dense_w_down[j].astype(BF16)).reshape(B, S, D)
        else:
            meta, gates, counts = _router(h, moe_router[j].T)
            n_tiles = (2 * T) // MOE_TILE + N_EXPERTS
            starts, tile_expert, n_used, zero_rows = _moe_layout(counts[:, 0], n_tiles, MOE_TILE)
            dst = _dest_rows(starts, meta)
            xs = _scatter_rows(zero_rows, dst, h, n_tiles * MOE_TILE)
            ys = _expert_ffn(tile_expert, n_used, xs, moe_w_gate[j].astype(BF16), moe_w_up[j].astype(BF16),
                             moe_w_down[j].astype(BF16))
            ct = min(COMBINE_TILE, T)
            dst_c = dst.transpose(1, 0, 2).reshape(SUBLANES, T // ct, ct).transpose(1, 0, 2)
            gates_col = gates.transpose(0, 2, 1).reshape(T, SUBLANES)
            last = (l == depth - 1)
            y = _combine(dst_c, x.reshape(T, D), gates_col, final_norm[None], ys, last)
            x = y.reshape(B, S, D)
    return x
```

```python
import functools
import math

import jax
import jax.numpy as jnp
import numpy as np
from jax import lax
from jax.experimental import pallas as pl
from jax.experimental.pallas import tpu as pltpu

F32 = jnp.float32
BF16 = jnp.bfloat16

HG_HEADS = 4
HG_KEY_DIM = 128
HG_VAL_DIM = 64
HG_KEY_WIDTH = HG_HEADS * HG_KEY_DIM
HG_WIDTH = HG_HEADS * HG_VAL_DIM
MIN_FORGET = 1e-20
MLA_HEADS = 4
MLA_Q_RANK = 256
MLA_KV_RANK = 128
MLA_NOPE = 128
MLA_ROPE = 64
MLA_V = 128
MLA_WIDTH = MLA_HEADS * MLA_V
ROPE_THETA = 10000.0
MASK_VALUE = -1e30
POOL_GROUPS = 4
POOL_WINDOWS = (2, 4, 8, 16)
POOL_WIDTH = 256
POOL_GROUP_DIM = POOL_WIDTH // POOL_GROUPS
N_EXPERTS = 8
EPS = 1e-6

LANES = 128
SUBLANES = 8
QK_PAD = 256
V_EXT = 256

TOKEN_TILE = 1024
IN_PROJ_CHAINS = 1
HGRN_CHUNK = 256
ATTN_BK = 512
ATTN_HEADS_PER_STEP = 1
FF_CHUNK = 512
MOE_FF_CHUNK = 512
MOE_TILE = 1024
ROUTE_TILE = 512
COMBINE_TILE = 256
POOL_HALO = 16
DMA_UNROLL = 8

_C_HG = 0
_C_CQ = 2 * HG_KEY_WIDTH + 2 * HG_WIDTH
_C_CKV = _C_CQ + MLA_Q_RANK
_C_KPE = _C_CKV + MLA_KV_RANK
_C_KPES = _C_KPE + LANES
_C_XP = _C_KPES + LANES
_C_END = _C_XP + POOL_WIDTH
_Q_HEAD_COLS = 3 * LANES


def _rms(x, g):
    return x * lax.rsqrt(jnp.mean(x * x, axis=-1, keepdims=True) + EPS) * g


def _dot(a, b):
    return jnp.dot(a, b, preferred_element_type=F32)


def _dot_nt(a, b):
    return lax.dot_general(a, b, (((1,), (1,)), ((), ())), preferred_element_type=F32)


def _dot_tn(a, b):
    return lax.dot_general(a, b, (((0,), (0,)), ((), ())), preferred_element_type=F32)


def _const_spec(shape):
    nd = len(shape)
    return pl.BlockSpec(shape, lambda *_: (0,) * nd)


def _load_tile_rows(ref, n):
    return jnp.concatenate([ref[pl.ds(s, n, stride=SUBLANES), :] for s in range(SUBLANES)], axis=1)


def _store_tile_rows(ref, val):
    n = val.shape[0]
    for s in range(SUBLANES):
        ref[pl.ds(s, n, stride=SUBLANES), :] = val[:, s * LANES:(s + 1) * LANES]


def _in_proj_kernel(x_ref, g_ref, w_ref, qn_ref, wuq_ref, kvn_ref, wukt_ref, wuv_ref, wkpet_ref,
                    cpad_ref, spad_ref, cpadt_ref, spadt_ref,
                    hg_ref, xp_ref, q_ref, kt_ref, v_ref):
    scale = (MLA_NOPE + MLA_ROPE) ** -0.5 * math.log2(math.e)
    tm = x_ref.shape[1]
    rows = tm // IN_PROJ_CHAINS
    ones_col = (lax.broadcasted_iota(jnp.int32, (rows, LANES), 1) == 0).astype(BF16)

    for r0 in range(0, tm, rows):
        rs = slice(r0, r0 + rows)
        h = _rms(x_ref[0, rs, :], g_ref[...]).astype(BF16)
        hg_ref[0, rs, :] = _dot(h, w_ref[:, _C_HG:_C_CQ])
        xp_ref[0, rs, :] = _dot(h, w_ref[:, _C_XP:_C_END])
        cpad = cpad_ref[rs, :]
        spad = spad_ref[rs, :]

        cq = _dot(h, w_ref[:, _C_CQ:_C_CKV])
        cqn = _rms(cq, qn_ref[...]).astype(BF16)
        for hd in range(MLA_HEADS):
            qh = _dot(cqn, wuq_ref[:, hd * _Q_HEAD_COLS:(hd + 1) * _Q_HEAD_COLS])
            q_ref[0, hd, rs, 0:LANES] = (qh[:, 0:LANES] * scale).astype(BF16)
            pe = qh[:, LANES:2 * LANES] * cpad + qh[:, 2 * LANES:3 * LANES] * spad
            q_ref[0, hd, rs, LANES:QK_PAD] = (pe * scale).astype(BF16)

        ckv = _dot(h, w_ref[:, _C_CKV:_C_KPE])
        ckvn = _rms(ckv, kvn_ref[...]).astype(BF16)
        kpet = (_dot_nt(wkpet_ref[0:LANES, :], h) * cpadt_ref[:, rs]
                + _dot_nt(wkpet_ref[LANES:2 * LANES, :], h) * spadt_ref[:, rs]).astype(BF16)
        v_all = _dot(ckvn, wuv_ref[...]).astype(BF16)
        for hd in range(MLA_HEADS):
            kt_ref[0, hd, 0:LANES, rs] = _dot_nt(wukt_ref[hd * LANES:(hd + 1) * LANES, :], ckvn).astype(BF16)
            kt_ref[0, hd, LANES:QK_PAD, rs] = kpet
            v_ref[0, hd, rs, 0:MLA_V] = v_all[:, hd * MLA_V:(hd + 1) * MLA_V]
            v_ref[0, hd, rs, MLA_V:V_EXT] = ones_col


def _in_proj(x, g, w_ext, qn, wuq_ext, kvn, wukt, wuv, wkpet, cpad, spad):
    B, S, D = x.shape
    tm = min(TOKEN_TILE, S)
    grid = (B, S // tm)
    n_hg = _C_CQ
    out_shape = (
        jax.ShapeDtypeStruct((B, S, n_hg), F32),
        jax.ShapeDtypeStruct((B, S, POOL_WIDTH), F32),
        jax.ShapeDtypeStruct((B, MLA_HEADS, S, QK_PAD), BF16),
        jax.ShapeDtypeStruct((B, MLA_HEADS, QK_PAD, S), BF16),
        jax.ShapeDtypeStruct((B, MLA_HEADS, S, V_EXT), BF16),
    )
    return pl.pallas_call(
        _in_proj_kernel,
        grid=grid,
        in_specs=[
            pl.BlockSpec((1, tm, D), lambda b, i: (b, i, 0)),
            _const_spec((1, D)),
            _const_spec(w_ext.shape),
            _const_spec((1, MLA_Q_RANK)),
            _const_spec(wuq_ext.shape),
            _const_spec((1, MLA_KV_RANK)),
            _const_spec(wukt.shape),
            _const_spec(wuv.shape),
            _const_spec(wkpet.shape),
            pl.BlockSpec((tm, LANES), lambda b, i: (i, 0)),
            pl.BlockSpec((tm, LANES), lambda b, i: (i, 0)),
            pl.BlockSpec((LANES, tm), lambda b, i: (0, i)),
            pl.BlockSpec((LANES, tm), lambda b, i: (0, i)),
        ],
        out_specs=(
            pl.BlockSpec((1, tm, n_hg), lambda b, i: (b, i, 0)),
            pl.BlockSpec((1, tm, POOL_WIDTH), lambda b, i: (b, i, 0)),
            pl.BlockSpec((1, MLA_HEADS, tm, QK_PAD), lambda b, i: (b, 0, i, 0)),
            pl.BlockSpec((1, MLA_HEADS, QK_PAD, tm), lambda b, i: (b, 0, 0, i)),
            pl.BlockSpec((1, MLA_HEADS, tm, V_EXT), lambda b, i: (b, 0, i, 0)),
        ),
        out_shape=out_shape,
        compiler_params=pltpu.CompilerParams(dimension_semantics=("arbitrary", "arbitrary")),
        name="in_proj",
    )(x, g, w_ext, qn, wuq_ext, kvn, wukt, wuv, wkpet, cpad, spad, cpad.T, spad.T)


def _split3(x):
    hi = x.astype(BF16)
    r1 = x - hi.astype(F32)
    mid = r1.astype(BF16)
    lo = (r1 - mid.astype(F32)).astype(BF16)
    return hi, mid, lo


def _hgrn_kernel(q_ref, f_ref, i_ref, g_ref, lb_ref, on_ref, tril_ref, grp_ref, o_ref, st_ref):
    C = q_ref.shape[1]
    KW = HG_KEY_WIDTH

    @pl.when(pl.program_id(1) == 0)
    def _():
        st_ref[...] = jnp.zeros_like(st_ref)

    lb = lb_ref[...]
    z = f_ref[0]
    forget = lb + (1.0 - lb) * jax.nn.sigmoid(z)
    lg = jnp.log2(jnp.maximum(forget, MIN_FORGET))
    kk = (1.0 - lb) * jax.nn.sigmoid(-z)
    qq = jax.nn.silu(q_ref[0])
    vv = i_ref[0]

    tril = tril_ref[...]
    b = sum(_dot(tril, part) for part in _split3(lg))

    sides, masks = [], []
    row = lax.broadcasted_iota(jnp.int32, (C, C), 0)
    col = lax.broadcasted_iota(jnp.int32, (C, C), 1)
    sub = lax.broadcasted_iota(jnp.int32, (C, KW), 0)
    half = C // 2
    while half >= 4:
        blk = 2 * half
        b3 = b.reshape(C // blk, blk, KW)
        ref_row = jnp.broadcast_to(b3[:, half - 1:half, :], b3.shape).reshape(C, KW)
        e = jnp.exp2(-jnp.abs(b - ref_row))
        x = (jnp.where((sub & half) != 0, qq, kk) * e).astype(BF16)
        sides.append((x, x))
        shift = int(math.log2(blk))
        masks.append(((row >> shift) == (col >> shift)) & ((row & half) != 0) & ((col & half) == 0))
        half //= 2
    b3 = b.reshape(C // 8, 8, KW)
    mid_lo = 0.5 * (b3[:, 0:1, :] + b3[:, 3:4, :])
    mid_hi = 0.5 * (b3[:, 4:5, :] + b3[:, 7:8, :])
    sub8 = lax.broadcasted_iota(jnp.int32, b3.shape, 1)
    mid = jnp.where(sub8 < 4, mid_lo, mid_hi).reshape(C, KW)
    sides.append(((qq * jnp.exp2(b - mid)).astype(BF16), (kk * jnp.exp2(mid - b)).astype(BF16)))
    masks.append(((row >> 2) == (col >> 2)) & (col <= row))

    lane_v = lax.broadcasted_iota(jnp.int32, (C, HG_WIDTH), 1)
    o = _dot_nt((qq * jnp.exp2(b)).astype(BF16), st_ref[...].astype(BF16))
    for hd in range(HG_HEADS):
        ks = slice(hd * HG_KEY_DIM, (hd + 1) * HG_KEY_DIM)
        a = jnp.zeros((C, C), F32)
        for (qt, kt), m in zip(sides, masks):
            a = jnp.where(m, _dot_nt(qt[:, ks], kt[:, ks]), a)
        v_h = jnp.where((lane_v >> 6) == hd, vv, 0.0).astype(BF16)
        o = o + _dot(a.astype(BF16), v_h)

    b_last = b[C - 1:C, :]
    khat = (kk * jnp.exp2(b_last - b)).astype(BF16)
    st_row = lax.broadcasted_iota(jnp.int32, (HG_WIDTH, KW), 0)
    st_col = lax.broadcasted_iota(jnp.int32, (HG_WIDTH, KW), 1)
    new_st = st_ref[...] * jnp.exp2(b_last) + _dot_tn(vv.astype(BF16), khat)
    st_ref[...] = jnp.where((st_row >> 6) == (st_col >> 7), new_st, 0.0)

    grp = grp_ref[...]
    ssq = sum(_dot(part, grp) for part in _split3(o * o))
    on = o * lax.rsqrt(ssq * (1.0 / HG_VAL_DIM) + EPS) * on_ref[...]
    o_ref[0] = (on * jax.nn.silu(g_ref[0])).astype(o_ref.dtype)


def _hgrn(hg, lb, out_norm):
    B, S, _ = hg.shape
    C = min(HGRN_CHUNK, S)
    kb = HG_KEY_WIDTH // HG_KEY_WIDTH
    tril = jnp.asarray(np.tril(np.ones((C, C), np.float32)), BF16)
    lane_group = np.arange(HG_WIDTH) // HG_VAL_DIM
    grp = jnp.asarray((lane_group[:, None] == lane_group[None, :]).astype(np.float32), BF16)
    return pl.pallas_call(
        _hgrn_kernel,
        grid=(B, S // C),
        in_specs=[
            pl.BlockSpec((1, C, HG_KEY_WIDTH), lambda b, c: (b, c, 0)),
            pl.BlockSpec((1, C, HG_KEY_WIDTH), lambda b, c: (b, c, kb)),
            pl.BlockSpec((1, C, HG_WIDTH), lambda b, c: (b, c, 2 * HG_KEY_WIDTH // HG_WIDTH)),
            pl.BlockSpec((1, C, HG_WIDTH), lambda b, c: (b, c, 2 * HG_KEY_WIDTH // HG_WIDTH + 1)),
            _const_spec((1, HG_KEY_WIDTH)),
            _const_spec((1, HG_WIDTH)),
            _const_spec(tril.shape),
            _const_spec(grp.shape),
        ],
        out_specs=pl.BlockSpec((1, C, HG_WIDTH), lambda b, c: (b, c, 0)),
        out_shape=jax.ShapeDtypeStruct((B, S, HG_WIDTH), BF16),
        scratch_shapes=[pltpu.VMEM((HG_WIDTH, HG_KEY_WIDTH), F32)],
        compiler_params=pltpu.CompilerParams(dimension_semantics=("arbitrary", "arbitrary")),
        name="hgrn2",
    )(hg, hg, hg, hg, lb, out_norm, tril, grp)


def _attn_kernel(q_ref, kt_ref, v_ref, o_ref, s0_ref, s1_ref, acc_ref, *, bk):
    bq = q_ref.shape[2]
    nh = q_ref.shape[1]
    i = pl.program_id(2)
    qs = [q_ref[0, hd] for hd in range(nh)]

    def causal(s, key0):
        qry = lax.broadcasted_iota(jnp.int32, (bq, bk), 0)
        key = key0 + lax.broadcasted_iota(jnp.int32, (bq, bk), 1)
        return jnp.where(key <= qry, s, MASK_VALUE)

    def scores(blk, s_ref):
        start = pl.multiple_of(blk * bk, bk)
        for hd in range(nh):
            s_ref[hd] = _dot(qs[hd], kt_ref[0, hd, :, pl.ds(start, bk)])

    def softmax_pv(blk, s_ref, ms, key0=None):
        start = pl.multiple_of(blk * bk, bk)
        out = []
        for hd in range(nh):
            s = s_ref[hd]
            if key0 is not None:
                s = causal(s, key0)
            m_new = jnp.maximum(ms[hd], jnp.max(s, axis=1, keepdims=True))
            p = jnp.exp2(s - m_new).astype(BF16)
            acc_ref[hd] = jnp.exp2(ms[hd] - m_new) * acc_ref[hd] + _dot(p, v_ref[0, hd, pl.ds(start, bk), :])
            out.append(m_new)
        return tuple(out)

    acc_ref[...] = jnp.zeros_like(acc_ref)
    scores(0, s0_ref)

    def pair(t, ms):
        scores(2 * t + 1, s1_ref)
        ms = softmax_pv(2 * t, s0_ref, ms)
        scores(2 * t + 2, s0_ref)
        return softmax_pv(2 * t + 1, s1_ref, ms)

    ms = lax.fori_loop(0, i, pair, tuple(jnp.full((bq, 1), -jnp.inf, F32) for _ in range(nh)))
    start1 = pl.multiple_of((2 * i + 1) * bk, bk)
    lower = [_dot(qs[hd][bk:, :], kt_ref[0, hd, :, pl.ds(start1, bk)]) for hd in range(nh)]
    ms = softmax_pv(2 * i, s0_ref, ms, 0)
    qry = lax.broadcasted_iota(jnp.int32, (bk, bk), 0)
    key = lax.broadcasted_iota(jnp.int32, (bk, bk), 1)
    for hd in range(nh):
        s = jnp.where(key <= qry, lower[hd], MASK_VALUE)
        m_old = ms[hd][bk:, :]
        m_new = jnp.maximum(m_old, jnp.max(s, axis=1, keepdims=True))
        p = jnp.exp2(s - m_new).astype(BF16)
        acc_ref[hd, bk:, :] = (jnp.exp2(m_old - m_new) * acc_ref[hd, bk:, :]
                               + _dot(p, v_ref[0, hd, pl.ds(start1, bk), :]))
    for hd in range(nh):
        acc = acc_ref[hd]
        o_ref[0, :, hd * MLA_V:(hd + 1) * MLA_V] = (acc[:, 0:MLA_V] / acc[:, MLA_V:MLA_V + 1]).astype(o_ref.dtype)


def _attention(q, kt, v):
    B, H, S, _ = q.shape
    bk = min(ATTN_BK, S // 2)
    bq = 2 * bk
    nh = ATTN_HEADS_PER_STEP
    resident = dict(pipeline_mode=pl.Buffered(1)) if nh > 1 else {}
    return pl.pallas_call(
        functools.partial(_attn_kernel, bk=bk),
        grid=(B, H // nh, S // bq),
        in_specs=[
            pl.BlockSpec((1, nh, bq, QK_PAD), lambda b, h, i: (b, h, i, 0)),
            pl.BlockSpec((1, nh, QK_PAD, S), lambda b, h, i: (b, h, 0, 0), **resident),
            pl.BlockSpec((1, nh, S, V_EXT), lambda b, h, i: (b, h, 0, 0), **resident),
        ],
        out_specs=pl.BlockSpec((1, bq, nh * MLA_V), lambda b, h, i: (b, i, h)),
        out_shape=jax.ShapeDtypeStruct((B, S, H * MLA_V), BF16),
        scratch_shapes=[pltpu.VMEM((nh, bq, bk), F32), pltpu.VMEM((nh, bq, bk), F32),
                        pltpu.VMEM((nh, bq, V_EXT), F32)],
        compiler_params=pltpu.CompilerParams(dimension_semantics=("arbitrary", "arbitrary", "arbitrary")),
        name="mla_attention",
    )(q, kt, v)


def _mix_out_kernel(x_ref, oa_ref, ob_ref, xp_ref, wpool_ref, pscale_ref, wo_ref, fn_ref,
                    xo_ref, h_ref, halo_ref, *, tile_rows):
    tm = xp_ref.shape[1]
    i = pl.program_id(1)

    @pl.when(i == 0)
    def _():
        halo_ref[...] = jnp.zeros_like(halo_ref)

    xp = xp_ref[0]
    xx = jnp.concatenate([halo_ref[...], xp], axis=0)
    halo_ref[...] = xp[tm - POOL_HALO:, :]

    w2 = xx[1:, :] + xx[:-1, :]
    w4 = w2[2:, :] + w2[:-2, :]
    w8 = w4[4:, :] + w4[:-4, :]
    w16 = w8[8:, :] + w8[:-8, :]
    sums = (w2[POOL_HALO - 1:, :], w4[POOL_HALO - 3:, :], w8[POOL_HALO - 7:, :], w16[POOL_HALO - 15:, :])
    t = i * tm + lax.broadcasted_iota(jnp.int32, (tm, POOL_WIDTH), 0)
    lane = lax.broadcasted_iota(jnp.int32, (tm, POOL_WIDTH), 1)
    pooled = jnp.zeros((tm, POOL_WIDTH), F32)
    for gi, w in enumerate(POOL_WINDOWS):
        cnt = jnp.minimum(t + 1, w).astype(F32)
        pooled = jnp.where((lane >> 6) == gi, sums[gi] / cnt, pooled)
    pooled = pooled - xp
    oc = _dot(pooled.astype(BF16), wpool_ref[...]) * pscale_ref[...]

    y = _dot(oa_ref[0], wo_ref[0:HG_WIDTH, :])
    y = y + _dot(ob_ref[0], wo_ref[HG_WIDTH:HG_WIDTH + MLA_WIDTH, :])
    y = y + _dot(oc.astype(BF16), wo_ref[HG_WIDTH + MLA_WIDTH:, :])
    xn = x_ref[0] + y
    xo_ref[0] = xn
    h = _rms(xn, fn_ref[...])
    if tile_rows:
        _store_tile_rows(h_ref, h)
    else:
        h_ref[0] = h.astype(h_ref.dtype)


def _mix_out(x, oa, ob, xp, wpool_bd, pscale, wo, fnorm, tile_rows):
    B, S, D = x.shape
    tm = min(TOKEN_TILE, S)
    nt = S // tm
    tok = lambda w: pl.BlockSpec((1, tm, w), lambda b, i: (b, i, 0))
    if tile_rows:
        assert D == SUBLANES * LANES
        h_spec = pl.BlockSpec((tm * SUBLANES, LANES), lambda b, i: (b * nt + i, 0))
        h_shape = jax.ShapeDtypeStruct((B * S * SUBLANES, LANES), F32)
    else:
        h_spec, h_shape = tok(D), jax.ShapeDtypeStruct((B, S, D), BF16)
    return pl.pallas_call(
        functools.partial(_mix_out_kernel, tile_rows=tile_rows),
        grid=(B, nt),
        in_specs=[tok(D), tok(HG_WIDTH), tok(MLA_WIDTH), tok(POOL_WIDTH),
                  _const_spec(wpool_bd.shape), _const_spec((1, POOL_WIDTH)), _const_spec(wo.shape),
                  _const_spec((1, D))],
        out_specs=(tok(D), h_spec),
        out_shape=(jax.ShapeDtypeStruct((B, S, D), F32), h_shape),
        scratch_shapes=[pltpu.VMEM((POOL_HALO, POOL_WIDTH), F32)],
        compiler_params=pltpu.CompilerParams(dimension_semantics=("arbitrary", "arbitrary")),
        name="mix_out",
    )(x, oa, ob, xp, wpool_bd, pscale, wo, fnorm)


def _dense_ffn_kernel(x_ref, h_ref, wg_ref, wu_ref, wd_ref, o_ref):
    h = h_ref[...]
    acc = x_ref[...]
    dff = wg_ref.shape[1]
    for c0 in range(0, dff, FF_CHUNK):
        g = _dot(h, wg_ref[:, c0:c0 + FF_CHUNK])
        u = _dot(h, wu_ref[:, c0:c0 + FF_CHUNK])
        acc = acc + _dot((jax.nn.silu(g) * u).astype(BF16), wd_ref[c0:c0 + FF_CHUNK, :])
    o_ref[...] = acc


def _dense_ffn(x2d, h2d, wg, wu, wd):
    T, D = x2d.shape
    tm = min(TOKEN_TILE, T)
    tok = pl.BlockSpec((tm, D), lambda i: (i, 0))
    return pl.pallas_call(
        _dense_ffn_kernel,
        grid=(T // tm,),
        in_specs=[tok, tok, _const_spec(wg.shape), _const_spec(wu.shape), _const_spec(wd.shape)],
        out_specs=tok,
        out_shape=jax.ShapeDtypeStruct((T, D), F32),
        compiler_params=pltpu.CompilerParams(dimension_semantics=("arbitrary",)),
        name="dense_ffn",
    )(x2d, h2d, wg, wu, wd)


def _router_kernel(h_ref, rt_ref, meta_ref, gate_ref, cnt_ref, run_ref):
    tm = h_ref.shape[0] // SUBLANES
    E = N_EXPERTS

    @pl.when(pl.program_id(0) == 0)
    def _():
        run_ref[...] = jnp.zeros_like(run_ref)

    logits = lax.dot_general(rt_ref[...], _load_tile_rows(h_ref, tm), (((1,), (1,)), ((), ())),
                             precision=lax.Precision.HIGHEST, preferred_element_type=F32)
    eid = lax.broadcasted_iota(jnp.int32, (E, tm), 0)
    m1 = jnp.max(logits, axis=0, keepdims=True)
    i1 = jnp.min(jnp.where(logits == m1, eid, E), axis=0, keepdims=True)
    rest = jnp.where(eid == i1, -jnp.inf, logits)
    m2 = jnp.max(rest, axis=0, keepdims=True)
    i2 = jnp.min(jnp.where(rest == m2, eid, E), axis=0, keepdims=True)
    e2 = jnp.exp(m2 - m1)
    g1 = 1.0 / (1.0 + e2)
    g2 = e2 / (1.0 + e2)

    sel = ((eid == i1) | (eid == i2))
    r = lax.broadcasted_iota(jnp.int32, (tm, tm), 0)
    c = lax.broadcasted_iota(jnp.int32, (tm, tm), 1)
    before = (r < c).astype(BF16)
    excl = _dot(sel.astype(BF16), before) + run_ref[:, 0:1]
    rank1 = jnp.sum(jnp.where(eid == i1, excl, 0.0), axis=0, keepdims=True).astype(jnp.int32)
    rank2 = jnp.sum(jnp.where(eid == i2, excl, 0.0), axis=0, keepdims=True).astype(jnp.int32)
    run_ref[...] = run_ref[...] + jnp.sum(sel.astype(F32), axis=1, keepdims=True)
    cnt_ref[...] = run_ref[...].astype(jnp.int32)

    zi = jnp.zeros((1, tm), jnp.int32)
    meta_ref[0] = jnp.concatenate([i1, i2, rank1, rank2, zi, zi, zi, zi], axis=0)
    zf = jnp.zeros((1, tm), F32)
    gate_ref[0] = jnp.concatenate([g1, g2, zf, zf, zf, zf, zf, zf], axis=0)


def _router(h_rows, router_t):
    T = h_rows.shape[0] // SUBLANES
    tm = min(ROUTE_TILE, T)
    nt = T // tm
    return pl.pallas_call(
        _router_kernel,
        grid=(nt,),
        in_specs=[pl.BlockSpec((tm * SUBLANES, LANES), lambda i: (i, 0)), _const_spec(router_t.shape)],
        out_specs=(pl.BlockSpec((1, SUBLANES, tm), lambda i: (i, 0, 0)),
                   pl.BlockSpec((1, SUBLANES, tm), lambda i: (i, 0, 0)),
                   _const_spec((N_EXPERTS, LANES))),
        out_shape=(jax.ShapeDtypeStruct((nt, SUBLANES, tm), jnp.int32),
                   jax.ShapeDtypeStruct((nt, SUBLANES, tm), F32),
                   jax.ShapeDtypeStruct((N_EXPERTS, LANES), jnp.int32)),
        scratch_shapes=[pltpu.VMEM((N_EXPERTS, LANES), F32)],
        compiler_params=pltpu.CompilerParams(dimension_semantics=("arbitrary",)),
        name="moe_router",
    )(h_rows, router_t)


def _dest_kernel(start_ref, meta_ref, dst_ref):
    meta = meta_ref[0]
    rows = []
    for k in range(2):
        e = meta[k:k + 1, :]
        base = jnp.zeros_like(e)
        for ex in range(N_EXPERTS):
            base = jnp.where(e == ex, start_ref[ex], base)
        rows.append(base + meta[2 + k:3 + k, :])
    dst_ref[0] = jnp.concatenate(rows + [jnp.zeros_like(rows[0])] * (SUBLANES - 2), axis=0)


def _dest_rows(starts, meta):
    nt, _, tm = meta.shape
    spec = pl.BlockSpec((1, SUBLANES, tm), lambda i, s: (i, 0, 0))
    return pl.pallas_call(
        _dest_kernel,
        grid_spec=pltpu.PrefetchScalarGridSpec(num_scalar_prefetch=1, grid=(nt,), in_specs=[spec], out_specs=spec),
        out_shape=jax.ShapeDtypeStruct(meta.shape, jnp.int32),
        compiler_params=pltpu.CompilerParams(dimension_semantics=("arbitrary",)),
        name="moe_dest",
    )(starts, meta)


def _scatter_kernel(zrow_ref, dst_ref, h_ref, xs_hbm, zero_ref, sem, zsem):
    tm = dst_ref.shape[2]

    @pl.when(pl.program_id(0) == 0)
    def _():
        zero_ref[...] = jnp.zeros_like(zero_ref)
        tile_rows = zero_ref.shape[0]

        def fill(row):
            start = pl.multiple_of(row * SUBLANES, SUBLANES)
            return pltpu.make_async_copy(zero_ref, xs_hbm.at[pl.ds(start, tile_rows), :], zsem)

        fills = [fill(zrow_ref[e]) for e in range(N_EXPERTS)]
        for f in fills:
            f.start()
        for f in fills:
            f.wait()
        n_tiles = xs_hbm.shape[0] // tile_rows
        for j in range(N_EXPERTS):
            @pl.when(n_tiles - 1 - j >= zrow_ref[N_EXPERTS])
            def _():
                f = fill((n_tiles - 1 - j) * (tile_rows // SUBLANES))
                f.start()
                f.wait()

    def row_copy(src_row, dst_row):
        src = pl.multiple_of(src_row * SUBLANES, SUBLANES)
        dst = pl.multiple_of(dst_row * SUBLANES, SUBLANES)
        return pltpu.make_async_copy(h_ref.at[pl.ds(src, SUBLANES), :], xs_hbm.at[pl.ds(dst, SUBLANES), :], sem)

    def issue(t, _):
        for k in range(2):
            row_copy(t, dst_ref[0, k, t]).start(priority=k)
        return 0

    lax.fori_loop(0, tm, issue, 0, unroll=DMA_UNROLL)

    def drain(t, _):
        row_copy(0, 0).wait()
        row_copy(0, 0).wait()
        return 0

    lax.fori_loop(0, tm, drain, 0, unroll=DMA_UNROLL)


def _scatter_rows(zero_rows, dst, h_rows, n_rows):
    nt, _, tm = dst.shape
    grid_spec = pltpu.PrefetchScalarGridSpec(
        num_scalar_prefetch=1,
        grid=(nt,),
        in_specs=[pl.BlockSpec((1, SUBLANES, tm), lambda i, z: (i, 0, 0), memory_space=pltpu.SMEM),
                  pl.BlockSpec((tm * SUBLANES, LANES), lambda i, z: (i, 0))],
        out_specs=pl.BlockSpec(memory_space=pl.ANY),
        scratch_shapes=[pltpu.VMEM((MOE_TILE * SUBLANES, LANES), F32), pltpu.SemaphoreType.DMA(()),
                        pltpu.SemaphoreType.DMA(())],
    )
    return pl.pallas_call(
        _scatter_kernel,
        grid_spec=grid_spec,
        out_shape=jax.ShapeDtypeStruct((n_rows * SUBLANES, LANES), F32),
        compiler_params=pltpu.CompilerParams(dimension_semantics=("arbitrary",), has_side_effects=True),
        name="moe_scatter",
    )(zero_rows, dst, h_rows)


def _expert_kernel(te_ref, nu_ref, x_ref, wg_ref, wu_ref, wd_ref, y_ref, *, fc):
    m = pl.program_id(0)
    tr = x_ref.shape[0] // SUBLANES
    dff = wg_ref.shape[2]

    @pl.when(m < nu_ref[0])
    def _():
        xb = _load_tile_rows(x_ref, tr).astype(BF16)
        acc = jnp.zeros((tr, wd_ref.shape[2]), F32)
        for c0 in range(0, dff, fc):
            g = _dot(xb, wg_ref[0, :, c0:c0 + fc])
            u = _dot(xb, wu_ref[0, :, c0:c0 + fc])
            acc = acc + _dot((jax.nn.silu(g) * u).astype(BF16), wd_ref[0, c0:c0 + fc, :])
        _store_tile_rows(y_ref, acc)

    @pl.when(m >= nu_ref[0])
    def _():
        y_ref[...] = jnp.zeros_like(y_ref)


def _expert_ffn(tile_expert, n_used, xs, wg, wu, wd):
    E, D, F = wg.shape
    R = xs.shape[0] // SUBLANES
    tr = MOE_TILE
    n_tiles = R // tr
    fc = min(MOE_FF_CHUNK, F)

    def row_map(m, te, nu):
        return (jnp.minimum(m, nu[0] - 1), 0)

    def w_map(m, te, nu):
        return (te[jnp.minimum(m, nu[0] - 1)], 0, 0)

    resident = dict(pipeline_mode=pl.Buffered(1))
    grid_spec = pltpu.PrefetchScalarGridSpec(
        num_scalar_prefetch=2,
        grid=(n_tiles,),
        in_specs=[
            pl.BlockSpec((tr * SUBLANES, LANES), row_map),
            pl.BlockSpec((1, D, F), w_map, **resident),
            pl.BlockSpec((1, D, F), w_map, **resident),
            pl.BlockSpec((1, F, D), w_map, **resident),
        ],
        out_specs=pl.BlockSpec((tr * SUBLANES, LANES), lambda m, te, nu: (m, 0)),
    )
    return pl.pallas_call(
        functools.partial(_expert_kernel, fc=fc),
        grid_spec=grid_spec,
        out_shape=jax.ShapeDtypeStruct(xs.shape, F32),
        compiler_params=pltpu.CompilerParams(dimension_semantics=("arbitrary",)),
        name="moe_experts",
    )(tile_expert, n_used, xs, wg, wu, wd)


def _combine_kernel(dcur_ref, dnxt_ref, x_ref, gate_ref, fn_ref, ys_hbm, o_ref, buf_ref, sems, *, final_norm):
    tm = x_ref.shape[0]
    i = pl.program_id(0)
    slot = i % 2

    def row_copy(src_row, s, k, t):
        src = pl.multiple_of(src_row * SUBLANES, SUBLANES)
        dst = pl.multiple_of(t * SUBLANES, SUBLANES)
        return pltpu.make_async_copy(ys_hbm.at[pl.ds(src, SUBLANES), :],
                                     buf_ref.at[s, k, pl.ds(dst, SUBLANES), :], sems.at[s])

    def issue_tile(d_ref, s):
        def issue(t, _):
            for k in range(2):
                row_copy(d_ref[0, k, t], s, k, t).start(priority=k)
            return 0

        lax.fori_loop(0, tm, issue, 0, unroll=DMA_UNROLL)

    @pl.when(i == 0)
    def _():
        issue_tile(dcur_ref, 0)

    @pl.when(i + 1 < pl.num_programs(0))
    def _():
        issue_tile(dnxt_ref, 1 - slot)

    def drain(t, _):
        row_copy(0, slot, 0, 0).wait()
        row_copy(0, slot, 1, 0).wait()
        return 0

    lax.fori_loop(0, tm, drain, 0, unroll=DMA_UNROLL)
    g = gate_ref[...]
    y0 = _load_tile_rows(buf_ref.at[slot, 0], tm)
    y1 = _load_tile_rows(buf_ref.at[slot, 1], tm)
    xn = x_ref[...] + g[:, 0:1] * y0 + g[:, 1:2] * y1
    o_ref[...] = _rms(xn, fn_ref[...]) if final_norm else xn


def _combine(dst_c, x2d, gates_col, fnorm, ys, final_norm):
    T, D = x2d.shape
    nt, _, tm = dst_c.shape
    return pl.pallas_call(
        functools.partial(_combine_kernel, final_norm=final_norm),
        grid=(nt,),
        in_specs=[pl.BlockSpec((1, SUBLANES, tm), lambda i: (i, 0, 0), memory_space=pltpu.SMEM),
                  pl.BlockSpec((1, SUBLANES, tm), lambda i: (jnp.minimum(i + 1, nt - 1), 0, 0),
                               memory_space=pltpu.SMEM),
                  pl.BlockSpec((tm, D), lambda i: (i, 0)),
                  pl.BlockSpec((tm, SUBLANES), lambda i: (i, 0)),
                  pl.BlockSpec((1, D), lambda i: (0, 0)),
                  pl.BlockSpec(memory_space=pl.ANY)],
        out_specs=pl.BlockSpec((tm, D), lambda i: (i, 0)),
        scratch_shapes=[pltpu.VMEM((2, 2, tm * SUBLANES, LANES), F32), pltpu.SemaphoreType.DMA((2,))],
        out_shape=jax.ShapeDtypeStruct((T, D), F32),
        compiler_params=pltpu.CompilerParams(dimension_semantics=("arbitrary",)),
        name="moe_combine",
    )(dst_c, dst_c, x2d, gates_col, fnorm, ys)


def _swap_halves_cols(w):
    half = w.shape[-1] // 2
    return jnp.concatenate([w[..., half:], w[..., :half]], axis=-1)


def _pad_cols(w, width):
    return jnp.pad(w, [(0, 0)] * (w.ndim - 1) + [(0, width - w.shape[-1])])


def _prep_w_in(w):
    kpe0 = _C_CKV + MLA_KV_RANK - 0
    kpe = w[:, kpe0:kpe0 + MLA_ROPE]
    return jnp.concatenate([
        w[:, :kpe0],
        _pad_cols(kpe, LANES), _pad_cols(_swap_halves_cols(kpe), LANES),
        w[:, kpe0 + MLA_ROPE:],
    ], axis=1).astype(BF16)


def _prep_w_uq(w):
    w = w.reshape(MLA_Q_RANK, MLA_HEADS, MLA_NOPE + MLA_ROPE)
    nope, pe = w[..., :MLA_NOPE], w[..., MLA_NOPE:]
    out = jnp.concatenate([nope, _pad_cols(pe, LANES), _pad_cols(_swap_halves_cols(pe), LANES)], axis=-1)
    return out.reshape(MLA_Q_RANK, MLA_HEADS * _Q_HEAD_COLS).astype(BF16)


def _prep_w_ukv(w):
    w = w.reshape(MLA_KV_RANK, MLA_HEADS, MLA_NOPE + MLA_V)
    wukt = jnp.transpose(w[..., :MLA_NOPE], (1, 2, 0)).reshape(MLA_HEADS * MLA_NOPE, MLA_KV_RANK)
    wuv = w[..., MLA_NOPE:].reshape(MLA_KV_RANK, MLA_HEADS * MLA_V)
    return wukt.astype(BF16), wuv.astype(BF16)


def _prep_w_kpe_t(w_ext):
    return w_ext[:, _C_KPE:_C_XP].T


def _rope_tables(seq):
    pos = jnp.arange(seq, dtype=F32)
    inv_freq = 1.0 / (ROPE_THETA ** (jnp.arange(0, MLA_ROPE, 2, dtype=F32) / MLA_ROPE))
    ang = pos[:, None] * inv_freq[None, :]
    cos, sin = jnp.cos(ang), jnp.sin(ang)
    cpad = _pad_cols(jnp.concatenate([cos, cos], axis=-1), LANES)
    spad = _pad_cols(jnp.concatenate([-sin, sin], axis=-1), LANES)
    return cpad, spad


def _block_diag_pool(w):
    G, c, _ = w.shape
    eye = jnp.eye(G, dtype=w.dtype)
    return (eye[:, None, :, None] * w[:, :, None, :]).reshape(G * c, G * c).astype(BF16)


def _moe_layout(counts, n_tiles, tile):
    tiles_per = (counts + tile - 1) // tile
    ends = jnp.cumsum(tiles_per)
    starts = (ends - tiles_per) * tile
    tile_expert = jnp.sum((jnp.arange(n_tiles)[:, None] >= ends[None, :]).astype(jnp.int32), axis=1)
    tile_expert = jnp.minimum(tile_expert, N_EXPERTS - 1)
    zero_rows = jnp.minimum(starts + counts, (n_tiles - 1) * tile)
    zero_info = jnp.concatenate([zero_rows, ends[-1:]])
    return (starts.astype(jnp.int32), tile_expert.astype(jnp.int32), ends[-1:].astype(jnp.int32),
            zero_info.astype(jnp.int32))


def kernel(x, attn_norm, w_in, hgrn_lower_bounds, hgrn_out_norm, mla_q_norm, mla_w_uq, mla_kv_norm,
           mla_w_ukv, pool_w, pool_scale, w_o, ffn_norm, dense_w_gate, dense_w_up, dense_w_down,
           moe_router, moe_w_gate, moe_w_up, moe_w_down, final_norm):
    B, S, D = x.shape
    T = B * S
    depth = w_in.shape[0]
    cpad, spad = _rope_tables(S)
    p_lb = jax.nn.softmax(hgrn_lower_bounds.astype(F32), axis=0)
    lbs = jnp.cumsum(p_lb, axis=0) - p_lb[0:1]

    assert depth % 2 == 0, "the final RMSNorm is fused into the last (MoE) layer's combine kernel"
    for l in range(depth):
        wukt, wuv = _prep_w_ukv(mla_w_ukv[l])
        w_ext = _prep_w_in(w_in[l])
        hg, xp, q, kt, v = _in_proj(
            x, attn_norm[l][None], w_ext, mla_q_norm[l][None], _prep_w_uq(mla_w_uq[l]),
            mla_kv_norm[l][None], wukt, wuv, _prep_w_kpe_t(w_ext), cpad, spad)
        o_a = _hgrn(hg, lbs[l][None], hgrn_out_norm[l][None])
        o_b = _attention(q, kt, v)
        moe_layer = (l % 2 == 1)
        x, h = _mix_out(x, o_a, o_b, xp, _block_diag_pool(pool_w[l]), pool_scale[l][None],
                        w_o[l].astype(BF16), ffn_norm[l][None], moe_layer)
        j = l // 2
        if not moe_layer:
            x = _dense_ffn(x.reshape(T, D), h.reshape(T, D), dense_w_gate[j].astype(BF16),
                           dense_w_up[j].astype(BF16), dense_w_down[j].astype(BF16)).reshape(B, S, D)
        else:
            meta, gates, counts = _router(h, moe_router[j].T)
            n_tiles = (2 * T) // MOE_TILE + N_EXPERTS
            starts, tile_expert, n_used, zero_rows = _moe_layout(counts[:, 0], n_tiles, MOE_TILE)
            dst = _dest_rows(starts, meta)
            xs = _scatter_rows(zero_rows, dst, h, n_tiles * MOE_TILE)
            ys = _expert_ffn(tile_expert, n_used, xs, moe_w_gate[j].astype(BF16), moe_w_up[j].astype(BF16),
                             moe_w_down[j].astype(BF16))
            ct = min(COMBINE_TILE, T)
            dst_c = dst.transpose(1, 0, 2).reshape(SUBLANES, T // ct, ct).transpose(1, 0, 2)
            gates_col = gates.transpose(0, 2, 1).reshape(T, SUBLANES)
            last = (l == depth - 1)
            y = _combine(dst_c, x.reshape(T, D), gates_col, final_norm[None], ys, last)
            x = y.reshape(B, S, D)
    return x
```

```python
import functools
import math

import jax
import jax.numpy as jnp
import numpy as np
from jax import lax
from jax.experimental import pallas as pl
from jax.experimental.pallas import tpu as pltpu

F32 = jnp.float32
BF16 = jnp.bfloat16

HG_HEADS = 4
HG_KEY_DIM = 128
HG_VAL_DIM = 64
HG_KEY_WIDTH = HG_HEADS * HG_KEY_DIM
HG_WIDTH = HG_HEADS * HG_VAL_DIM
MIN_FORGET = 1e-20
MLA_HEADS = 4
MLA_Q_RANK = 256
MLA_KV_RANK = 128
MLA_NOPE = 128
MLA_ROPE = 64
MLA_V = 128
MLA_WIDTH = MLA_HEADS * MLA_V
ROPE_THETA = 10000.0
MASK_VALUE = -1e30
POOL_GROUPS = 4
POOL_WINDOWS = (2, 4, 8, 16)
POOL_WIDTH = 256
POOL_GROUP_DIM = POOL_WIDTH // POOL_GROUPS
N_EXPERTS = 8
EPS = 1e-6

LANES = 128
SUBLANES = 8
QK_PAD = 256
V_EXT = 256

TOKEN_TILE = 1024
IN_PROJ_CHAINS = 1
HGRN_CHUNK = 256
ATTN_BK = 512
ATTN_HEADS_PER_STEP = 1
FF_CHUNK = 512
MOE_FF_CHUNK = 512
MOE_TILE = 1024
ROUTE_TILE = 512
COMBINE_TILE = 256
POOL_HALO = 16
DMA_UNROLL = 8

_C_HG = 0
_C_CQ = 2 * HG_KEY_WIDTH + 2 * HG_WIDTH
_C_CKV = _C_CQ + MLA_Q_RANK
_C_KPE = _C_CKV + MLA_KV_RANK
_C_KPES = _C_KPE + LANES
_C_XP = _C_KPES + LANES
_C_END = _C_XP + POOL_WIDTH
_Q_HEAD_COLS = 3 * LANES


def _rms(x, g):
    return x * lax.rsqrt(jnp.mean(x * x, axis=-1, keepdims=True) + EPS) * g


def _dot(a, b):
    return jnp.dot(a, b, preferred_element_type=F32)


def _dot_nt(a, b):
    return lax.dot_general(a, b, (((1,), (1,)), ((), ())), preferred_element_type=F32)


def _dot_tn(a, b):
    return lax.dot_general(a, b, (((0,), (0,)), ((), ())), preferred_element_type=F32)


def _const_spec(shape):
    nd = len(shape)
    return pl.BlockSpec(shape, lambda *_: (0,) * nd)


def _load_tile_rows(ref, n):
    return jnp.concatenate([ref[pl.ds(s, n, stride=SUBLANES), :] for s in range(SUBLANES)], axis=1)


def _store_tile_rows(ref, val):
    n = val.shape[0]
    for s in range(SUBLANES):
        ref[pl.ds(s, n, stride=SUBLANES), :] = val[:, s * LANES:(s + 1) * LANES]


def _in_proj_kernel(x_ref, g_ref, w_ref, qn_ref, wuq_ref, kvn_ref, wukt_ref, wuv_ref, wkpet_ref,
                    cpad_ref, spad_ref, cpadt_ref, spadt_ref,
                    hg_ref, xp_ref, q_ref, kt_ref, v_ref):
    scale = (MLA_NOPE + MLA_ROPE) ** -0.5 * math.log2(math.e)
    tm = x_ref.shape[1]
    rows = tm // IN_PROJ_CHAINS
    ones_col = (lax.broadcasted_iota(jnp.int32, (rows, LANES), 1) == 0).astype(BF16)

    for r0 in range(0, tm, rows):
        rs = slice(r0, r0 + rows)
        h = _rms(x_ref[0, rs, :], g_ref[...]).astype(BF16)
        hg_ref[0, rs, :] = _dot(h, w_ref[:, _C_HG:_C_CQ])
        xp_ref[0, rs, :] = _dot(h, w_ref[:, _C_XP:_C_END])
        cpad = cpad_ref[rs, :]
        spad = spad_ref[rs, :]

        cq = _dot(h, w_ref[:, _C_CQ:_C_CKV])
        cqn = _rms(cq, qn_ref[...]).astype(BF16)
        for hd in range(MLA_HEADS):
            qh = _dot(cqn, wuq_ref[:, hd * _Q_HEAD_COLS:(hd + 1) * _Q_HEAD_COLS])
            q_ref[0, hd, rs, 0:LANES] = (qh[:, 0:LANES] * scale).astype(BF16)
            pe = qh[:, LANES:2 * LANES] * cpad + qh[:, 2 * LANES:3 * LANES] * spad
            q_ref[0, hd, rs, LANES:QK_PAD] = (pe * scale).astype(BF16)

        ckv = _dot(h, w_ref[:, _C_CKV:_C_KPE])
        ckvn = _rms(ckv, kvn_ref[...]).astype(BF16)
        kpet = (_dot_nt(wkpet_ref[0:LANES, :], h) * cpadt_ref[:, rs]
                + _dot_nt(wkpet_ref[LANES:2 * LANES, :], h) * spadt_ref[:, rs]).astype(BF16)
        v_all = _dot(ckvn, wuv_ref[...]).astype(BF16)
        for hd in range(MLA_HEADS):
            kt_ref[0, hd, 0:LANES, rs] = _dot_nt(wukt_ref[hd * LANES:(hd + 1) * LANES, :], ckvn).astype(BF16)
            kt_ref[0, hd, LANES:QK_PAD, rs] = kpet
            v_ref[0, hd, rs, 0:MLA_V] = v_all[:, hd * MLA_V:(hd + 1) * MLA_V]
            v_ref[0, hd, rs, MLA_V:V_EXT] = ones_col


def _in_proj(x, g, w_ext, qn, wuq_ext, kvn, wukt, wuv, wkpet, cpad, spad):
    B, S, D = x.shape
    tm = min(TOKEN_TILE, S)
    grid = (B, S // tm)
    n_hg = _C_CQ
    out_shape = (
        jax.ShapeDtypeStruct((B, S, n_hg), F32),
        jax.ShapeDtypeStruct((B, S, POOL_WIDTH), F32),
        jax.ShapeDtypeStruct((B, MLA_HEADS, S, QK_PAD), BF16),
        jax.ShapeDtypeStruct((B, MLA_HEADS, QK_PAD, S), BF16),
        jax.ShapeDtypeStruct((B, MLA_HEADS, S, V_EXT), BF16),
    )
    return pl.pallas_call(
        _in_proj_kernel,
        grid=grid,
        in_specs=[
            pl.BlockSpec((1, tm, D), lambda b, i: (b, i, 0)),
            _const_spec((1, D)),
            _const_spec(w_ext.shape),
            _const_spec((1, MLA_Q_RANK)),
            _const_spec(wuq_ext.shape),
            _const_spec((1, MLA_KV_RANK)),
            _const_spec(wukt.shape),
            _const_spec(wuv.shape),
            _const_spec(wkpet.shape),
            pl.BlockSpec((tm, LANES), lambda b, i: (i, 0)),
            pl.BlockSpec((tm, LANES), lambda b, i: (i, 0)),
            pl.BlockSpec((LANES, tm), lambda b, i: (0, i)),
            pl.BlockSpec((LANES, tm), lambda b, i: (0, i)),
        ],
        out_specs=(
            pl.BlockSpec((1, tm, n_hg), lambda b, i: (b, i, 0)),
            pl.BlockSpec((1, tm, POOL_WIDTH), lambda b, i: (b, i, 0)),
            pl.BlockSpec((1, MLA_HEADS, tm, QK_PAD), lambda b, i: (b, 0, i, 0)),
            pl.BlockSpec((1, MLA_HEADS, QK_PAD, tm), lambda b, i: (b, 0, 0, i)),
            pl.BlockSpec((1, MLA_HEADS, tm, V_EXT), lambda b, i: (b, 0, i, 0)),
        ),
        out_shape=out_shape,
        compiler_params=pltpu.CompilerParams(dimension_semantics=("arbitrary", "arbitrary")),
        name="in_proj",
    )(x, g, w_ext, qn, wuq_ext, kvn, wukt, wuv, wkpet, cpad, spad, cpad.T, spad.T)


def _split3(x):
    hi = x.astype(BF16)
    r1 = x - hi.astype(F32)
    mid = r1.astype(BF16)
    lo = (r1 - mid.astype(F32)).astype(BF16)
    return hi, mid, lo


def _hgrn_kernel(q_ref, f_ref, i_ref, g_ref, lb_ref, on_ref, tril_ref, grp_ref, o_ref, st_ref):
    C = q_ref.shape[1]
    KW = HG_KEY_WIDTH

    @pl.when(pl.program_id(1) == 0)
    def _():
        st_ref[...] = jnp.zeros_like(st_ref)

    lb = lb_ref[...]
    z = f_ref[0]
    forget = lb + (1.0 - lb) * jax.nn.sigmoid(z)
    lg = jnp.log2(jnp.maximum(forget, MIN_FORGET))
    kk = (1.0 - lb) * jax.nn.sigmoid(-z)
    qq = jax.nn.silu(q_ref[0])
    vv = i_ref[0]

    tril = tril_ref[...]
    b = sum(_dot(tril, part) for part in _split3(lg))

    sides, masks = [], []
    row = lax.broadcasted_iota(jnp.int32, (C, C), 0)
    col = lax.broadcasted_iota(jnp.int32, (C, C), 1)
    sub = lax.broadcasted_iota(jnp.int32, (C, KW), 0)
    half = C // 2
    while half >= 4:
        blk = 2 * half
        b3 = b.reshape(C // blk, blk, KW)
        ref_row = jnp.broadcast_to(b3[:, half - 1:half, :], b3.shape).reshape(C, KW)
        e = jnp.exp2(-jnp.abs(b - ref_row))
        x = (jnp.where((sub & half) != 0, qq, kk) * e).astype(BF16)
        sides.append((x, x))
        shift = int(math.log2(blk))
        masks.append(((row >> shift) == (col >> shift)) & ((row & half) != 0) & ((col & half) == 0))
        half //= 2
    b3 = b.reshape(C // 8, 8, KW)
    mid_lo = 0.5 * (b3[:, 0:1, :] + b3[:, 3:4, :])
    mid_hi = 0.5 * (b3[:, 4:5, :] + b3[:, 7:8, :])
    sub8 = lax.broadcasted_iota(jnp.int32, b3.shape, 1)
    mid = jnp.where(sub8 < 4, mid_lo, mid_hi).reshape(C, KW)
    sides.append(((qq * jnp.exp2(b - mid)).astype(BF16), (kk * jnp.exp2(mid - b)).astype(BF16)))
    masks.append(((row >> 2) == (col >> 2)) & (col <= row))

    lane_v = lax.broadcasted_iota(jnp.int32, (C, HG_WIDTH), 1)
    o = _dot_nt((qq * jnp.exp2(b)).astype(BF16), st_ref[...].astype(BF16))
    for hd in range(HG_HEADS):
        ks = slice(hd * HG_KEY_DIM, (hd + 1) * HG_KEY_DIM)
        a = jnp.zeros((C, C), F32)
        for (qt, kt), m in zip(sides, masks):
            a = jnp.where(m, _dot_nt(qt[:, ks], kt[:, ks]), a)
        v_h = jnp.where((lane_v >> 6) == hd, vv, 0.0).astype(BF16)
        o = o + _dot(a.astype(BF16), v_h)

    b_last = b[C - 1:C, :]
    khat = (kk * jnp.exp2(b_last - b)).astype(BF16)
    st_row = lax.broadcasted_iota(jnp.int32, (HG_WIDTH, KW), 0)
    st_col = lax.broadcasted_iota(jnp.int32, (HG_WIDTH, KW), 1)
    new_st = st_ref[...] * jnp.exp2(b_last) + _dot_tn(vv.astype(BF16), khat)
    st_ref[...] = jnp.where((st_row >> 6) == (st_col >> 7), new_st, 0.0)

    grp = grp_ref[...]
    ssq = sum(_dot(part, grp) for part in _split3(o * o))
    on = o * lax.rsqrt(ssq * (1.0 / HG_VAL_DIM) + EPS) * on_ref[...]
    o_ref[0] = (on * jax.nn.silu(g_ref[0])).astype(o_ref.dtype)


def _hgrn(hg, lb, out_norm):
    B, S, _ = hg.shape
    C = min(HGRN_CHUNK, S)
    kb = HG_KEY_WIDTH // HG_KEY_WIDTH
    tril = jnp.asarray(np.tril(np.ones((C, C), np.float32)), BF16)
    lane_group = np.arange(HG_WIDTH) // HG_VAL_DIM
    grp = jnp.asarray((lane_group[:, None] == lane_group[None, :]).astype(np.float32), BF16)
    return pl.pallas_call(
        _hgrn_kernel,
        grid=(B, S // C),
        in_specs=[
            pl.BlockSpec((1, C, HG_KEY_WIDTH), lambda b, c: (b, c, 0)),
            pl.BlockSpec((1, C, HG_KEY_WIDTH), lambda b, c: (b, c, kb)),
            pl.BlockSpec((1, C, HG_WIDTH), lambda b, c: (b, c, 2 * HG_KEY_WIDTH // HG_WIDTH)),
            pl.BlockSpec((1, C, HG_WIDTH), lambda b, c: (b, c, 2 * HG_KEY_WIDTH // HG_WIDTH + 1)),
            _const_spec((1, HG_KEY_WIDTH)),
            _const_spec((1, HG_WIDTH)),
            _const_spec(tril.shape),
            _const_spec(grp.shape),
        ],
        out_specs=pl.BlockSpec((1, C, HG_WIDTH), lambda b, c: (b, c, 0)),
        out_shape=jax.ShapeDtypeStruct((B, S, HG_WIDTH), BF16),
        scratch_shapes=[pltpu.VMEM((HG_WIDTH, HG_KEY_WIDTH), F32)],
        compiler_params=pltpu.CompilerParams(dimension_semantics=("arbitrary", "arbitrary")),
        name="hgrn2",
    )(hg, hg, hg, hg, lb, out_norm, tril, grp)


def _attn_kernel(q_ref, kt_ref, v_ref, o_ref, s0_ref, s1_ref, acc_ref, *, bk):
    bq = q_ref.shape[2]
    nh = q_ref.shape[1]
    i = pl.program_id(2)
    qs = [q_ref[0, hd] for hd in range(nh)]

    def causal(s, key0):
        qry = lax.broadcasted_iota(jnp.int32, (bq, bk), 0)
        key = key0 + lax.broadcasted_iota(jnp.int32, (bq, bk), 1)
        return jnp.where(key <= qry, s, MASK_VALUE)

    def scores(blk, s_ref):
        start = pl.multiple_of(blk * bk, bk)
        for hd in range(nh):
            s_ref[hd] = _dot(qs[hd], kt_ref[0, hd, :, pl.ds(start, bk)])

    def softmax_pv(blk, s_ref, ms, key0=None):
        start = pl.multiple_of(blk * bk, bk)
        out = []
        for hd in range(nh):
            s = s_ref[hd]
            if key0 is not None:
                s = causal(s, key0)
            m_new = jnp.maximum(ms[hd], jnp.max(s, axis=1, keepdims=True))
            p = jnp.exp2(s - m_new).astype(BF16)
            acc_ref[hd] = jnp.exp2(ms[hd] - m_new) * acc_ref[hd] + _dot(p, v_ref[0, hd, pl.ds(start, bk), :])
            out.append(m_new)
        return tuple(out)

    acc_ref[...] = jnp.zeros_like(acc_ref)
    scores(0, s0_ref)

    def pair(t, ms):
        scores(2 * t + 1, s1_ref)
        ms = softmax_pv(2 * t, s0_ref, ms)
        scores(2 * t + 2, s0_ref)
        return softmax_pv(2 * t + 1, s1_ref, ms)

    ms = tuple(jnp.full((bq, 1), -jnp.inf, F32) for _ in range(nh))
    ms = lax.fori_loop(0, i // 2, lambda t, c: pair(2 * t + 1, pair(2 * t, c)), ms)
    ms = lax.cond(i % 2 == 1, lambda c: pair(i - 1, c), lambda c: c, ms)
    start1 = pl.multiple_of((2 * i + 1) * bk, bk)
    lower = [_dot(qs[hd][bk:, :], kt_ref[0, hd, :, pl.ds(start1, bk)]) for hd in range(nh)]
    ms = softmax_pv(2 * i, s0_ref, ms, 0)
    qry = lax.broadcasted_iota(jnp.int32, (bk, bk), 0)
    key = lax.broadcasted_iota(jnp.int32, (bk, bk), 1)
    for hd in range(nh):
        s = jnp.where(key <= qry, lower[hd], MASK_VALUE)
        m_old = ms[hd][bk:, :]
        m_new = jnp.maximum(m_old, jnp.max(s, axis=1, keepdims=True))
        p = jnp.exp2(s - m_new).astype(BF16)
        acc_ref[hd, bk:, :] = (jnp.exp2(m_old - m_new) * acc_ref[hd, bk:, :]
                               + _dot(p, v_ref[0, hd, pl.ds(start1, bk), :]))
    for hd in range(nh):
        acc = acc_ref[hd]
        o_ref[0, :, hd * MLA_V:(hd + 1) * MLA_V] = (acc[:, 0:MLA_V] / acc[:, MLA_V:MLA_V + 1]).astype(o_ref.dtype)


def _attention(q, kt, v):
    B, H, S, _ = q.shape
    bk = min(ATTN_BK, S // 2)
    bq = 2 * bk
    nh = ATTN_HEADS_PER_STEP
    resident = dict(pipeline_mode=pl.Buffered(1)) if nh > 1 else {}
    return pl.pallas_call(
        functools.partial(_attn_kernel, bk=bk),
        grid=(B, H // nh, S // bq),
        in_specs=[
            pl.BlockSpec((1, nh, bq, QK_PAD), lambda b, h, i: (b, h, i, 0)),
            pl.BlockSpec((1, nh, QK_PAD, S), lambda b, h, i: (b, h, 0, 0), **resident),
            pl.BlockSpec((1, nh, S, V_EXT), lambda b, h, i: (b, h, 0, 0), **resident),
        ],
        out_specs=pl.BlockSpec((1, bq, nh * MLA_V), lambda b, h, i: (b, i, h)),
        out_shape=jax.ShapeDtypeStruct((B, S, H * MLA_V), BF16),
        scratch_shapes=[pltpu.VMEM((nh, bq, bk), F32), pltpu.VMEM((nh, bq, bk), F32),
                        pltpu.VMEM((nh, bq, V_EXT), F32)],
        compiler_params=pltpu.CompilerParams(dimension_semantics=("arbitrary", "arbitrary", "arbitrary")),
        name="mla_attention",
    )(q, kt, v)


def _mix_out_kernel(x_ref, oa_ref, ob_ref, xp_ref, wpool_ref, pscale_ref, wo_ref, fn_ref,
                    xo_ref, h_ref, halo_ref, *, tile_rows):
    tm = xp_ref.shape[1]
    i = pl.program_id(1)

    @pl.when(i == 0)
    def _():
        halo_ref[...] = jnp.zeros_like(halo_ref)

    xp = xp_ref[0]
    xx = jnp.concatenate([halo_ref[...], xp], axis=0)
    halo_ref[...] = xp[tm - POOL_HALO:, :]

    w2 = xx[1:, :] + xx[:-1, :]
    w4 = w2[2:, :] + w2[:-2, :]
    w8 = w4[4:, :] + w4[:-4, :]
    w16 = w8[8:, :] + w8[:-8, :]
    sums = (w2[POOL_HALO - 1:, :], w4[POOL_HALO - 3:, :], w8[POOL_HALO - 7:, :], w16[POOL_HALO - 15:, :])
    t = i * tm + lax.broadcasted_iota(jnp.int32, (tm, POOL_WIDTH), 0)
    lane = lax.broadcasted_iota(jnp.int32, (tm, POOL_WIDTH), 1)
    pooled = jnp.zeros((tm, POOL_WIDTH), F32)
    for gi, w in enumerate(POOL_WINDOWS):
        cnt = jnp.minimum(t + 1, w).astype(F32)
        pooled = jnp.where((lane >> 6) == gi, sums[gi] / cnt, pooled)
    pooled = pooled - xp
    oc = _dot(pooled.astype(BF16), wpool_ref[...]) * pscale_ref[...]

    y = _dot(oa_ref[0], wo_ref[0:HG_WIDTH, :])
    y = y + _dot(ob_ref[0], wo_ref[HG_WIDTH:HG_WIDTH + MLA_WIDTH, :])
    y = y + _dot(oc.astype(BF16), wo_ref[HG_WIDTH + MLA_WIDTH:, :])
    xn = x_ref[0] + y
    xo_ref[0] = xn
    h = _rms(xn, fn_ref[...])
    if tile_rows:
        _store_tile_rows(h_ref, h)
    else:
        h_ref[0] = h.astype(h_ref.dtype)


def _mix_out(x, oa, ob, xp, wpool_bd, pscale, wo, fnorm, tile_rows):
    B, S, D = x.shape
    tm = min(TOKEN_TILE, S)
    nt = S // tm
    tok = lambda w: pl.BlockSpec((1, tm, w), lambda b, i: (b, i, 0))
    if tile_rows:
        assert D == SUBLANES * LANES
        h_spec = pl.BlockSpec((tm * SUBLANES, LANES), lambda b, i: (b * nt + i, 0))
        h_shape = jax.ShapeDtypeStruct((B * S * SUBLANES, LANES), F32)
    else:
        h_spec, h_shape = tok(D), jax.ShapeDtypeStruct((B, S, D), BF16)
    return pl.pallas_call(
        functools.partial(_mix_out_kernel, tile_rows=tile_rows),
        grid=(B, nt),
        in_specs=[tok(D), tok(HG_WIDTH), tok(MLA_WIDTH), tok(POOL_WIDTH),
                  _const_spec(wpool_bd.shape), _const_spec((1, POOL_WIDTH)), _const_spec(wo.shape),
                  _const_spec((1, D))],
        out_specs=(tok(D), h_spec),
        out_shape=(jax.ShapeDtypeStruct((B, S, D), F32), h_shape),
        scratch_shapes=[pltpu.VMEM((POOL_HALO, POOL_WIDTH), F32)],
        compiler_params=pltpu.CompilerParams(dimension_semantics=("arbitrary", "arbitrary")),
        name="mix_out",
    )(x, oa, ob, xp, wpool_bd, pscale, wo, fnorm)


def _dense_ffn_kernel(x_ref, h_ref, wg_ref, wu_ref, wd_ref, o_ref):
    h = h_ref[...]
    acc = x_ref[...]
    dff = wg_ref.shape[1]
    for c0 in range(0, dff, FF_CHUNK):
        g = _dot(h, wg_ref[:, c0:c0 + FF_CHUNK])
        u = _dot(h, wu_ref[:, c0:c0 + FF_CHUNK])
        acc = acc + _dot((jax.nn.silu(g) * u).astype(BF16), wd_ref[c0:c0 + FF_CHUNK, :])
    o_ref[...] = acc


def _dense_ffn(x2d, h2d, wg, wu, wd):
    T, D = x2d.shape
    tm = min(TOKEN_TILE, T)
    tok = pl.BlockSpec((tm, D), lambda i: (i, 0))
    return pl.pallas_call(
        _dense_ffn_kernel,
        grid=(T // tm,),
        in_specs=[tok, tok, _const_spec(wg.shape), _const_spec(wu.shape), _const_spec(wd.shape)],
        out_specs=tok,
        out_shape=jax.ShapeDtypeStruct((T, D), F32),
        compiler_params=pltpu.CompilerParams(dimension_semantics=("arbitrary",)),
        name="dense_ffn",
    )(x2d, h2d, wg, wu, wd)


def _router_kernel(h_ref, rt_ref, meta_ref, gate_ref, cnt_ref, run_ref):
    tm = h_ref.shape[0] // SUBLANES
    E = N_EXPERTS

    @pl.when(pl.program_id(0) == 0)
    def _():
        run_ref[...] = jnp.zeros_like(run_ref)

    logits = lax.dot_general(rt_ref[...], _load_tile_rows(h_ref, tm), (((1,), (1,)), ((), ())),
                             precision=lax.Precision.HIGHEST, preferred_element_type=F32)
    eid = lax.broadcasted_iota(jnp.int32, (E, tm), 0)
    m1 = jnp.max(logits, axis=0, keepdims=True)
    i1 = jnp.min(jnp.where(logits == m1, eid, E), axis=0, keepdims=True)
    rest = jnp.where(eid == i1, -jnp.inf, logits)
    m2 = jnp.max(rest, axis=0, keepdims=True)
    i2 = jnp.min(jnp.where(rest == m2, eid, E), axis=0, keepdims=True)
    e2 = jnp.exp(m2 - m1)
    g1 = 1.0 / (1.0 + e2)
    g2 = e2 / (1.0 + e2)

    sel = ((eid == i1) | (eid == i2))
    r = lax.broadcasted_iota(jnp.int32, (tm, tm), 0)
    c = lax.broadcasted_iota(jnp.int32, (tm, tm), 1)
    before = (r < c).astype(BF16)
    excl = _dot(sel.astype(BF16), before) + run_ref[:, 0:1]
    rank1 = jnp.sum(jnp.where(eid == i1, excl, 0.0), axis=0, keepdims=True).astype(jnp.int32)
    rank2 = jnp.sum(jnp.where(eid == i2, excl, 0.0), axis=0, keepdims=True).astype(jnp.int32)
    run_ref[...] = run_ref[...] + jnp.sum(sel.astype(F32), axis=1, keepdims=True)
    cnt_ref[...] = run_ref[...].astype(jnp.int32)

    zi = jnp.zeros((1, tm), jnp.int32)
    meta_ref[0] = jnp.concatenate([i1, i2, rank1, rank2, zi, zi, zi, zi], axis=0)
    zf = jnp.zeros((1, tm), F32)
    gate_ref[0] = jnp.concatenate([g1, g2, zf, zf, zf, zf, zf, zf], axis=0)


def _router(h_rows, router_t):
    T = h_rows.shape[0] // SUBLANES
    tm = min(ROUTE_TILE, T)
    nt = T // tm
    return pl.pallas_call(
        _router_kernel,
        grid=(nt,),
        in_specs=[pl.BlockSpec((tm * SUBLANES, LANES), lambda i: (i, 0)), _const_spec(router_t.shape)],
        out_specs=(pl.BlockSpec((1, SUBLANES, tm), lambda i: (i, 0, 0)),
                   pl.BlockSpec((1, SUBLANES, tm), lambda i: (i, 0, 0)),
                   _const_spec((N_EXPERTS, LANES))),
        out_shape=(jax.ShapeDtypeStruct((nt, SUBLANES, tm), jnp.int32),
                   jax.ShapeDtypeStruct((nt, SUBLANES, tm), F32),
                   jax.ShapeDtypeStruct((N_EXPERTS, LANES), jnp.int32)),
        scratch_shapes=[pltpu.VMEM((N_EXPERTS, LANES), F32)],
        compiler_params=pltpu.CompilerParams(dimension_semantics=("arbitrary",)),
        name="moe_router",
    )(h_rows, router_t)


def _dest_kernel(start_ref, meta_ref, dst_ref):
    meta = meta_ref[0]
    rows = []
    for k in range(2):
        e = meta[k:k + 1, :]
        base = jnp.zeros_like(e)
        for ex in range(N_EXPERTS):
            base = jnp.where(e == ex, start_ref[ex], base)
        rows.append(base + meta[2 + k:3 + k, :])
    dst_ref[0] = jnp.concatenate(rows + [jnp.zeros_like(rows[0])] * (SUBLANES - 2), axis=0)


def _dest_rows(starts, meta):
    nt, _, tm = meta.shape
    spec = pl.BlockSpec((1, SUBLANES, tm), lambda i, s: (i, 0, 0))
    return pl.pallas_call(
        _dest_kernel,
        grid_spec=pltpu.PrefetchScalarGridSpec(num_scalar_prefetch=1, grid=(nt,), in_specs=[spec], out_specs=spec),
        out_shape=jax.ShapeDtypeStruct(meta.shape, jnp.int32),
        compiler_params=pltpu.CompilerParams(dimension_semantics=("arbitrary",)),
        name="moe_dest",
    )(starts, meta)


def _scatter_kernel(zrow_ref, dst_ref, h_ref, xs_hbm, zero_ref, sem, zsem):
    tm = dst_ref.shape[2]

    @pl.when(pl.program_id(0) == 0)
    def _():
        zero_ref[...] = jnp.zeros_like(zero_ref)
        tile_rows = zero_ref.shape[0]

        def fill(row):
            start = pl.multiple_of(row * SUBLANES, SUBLANES)
            return pltpu.make_async_copy(zero_ref, xs_hbm.at[pl.ds(start, tile_rows), :], zsem)

        fills = [fill(zrow_ref[e]) for e in range(N_EXPERTS)]
        for f in fills:
            f.start()
        for f in fills:
            f.wait()
        n_tiles = xs_hbm.shape[0] // tile_rows
        for j in range(N_EXPERTS):
            @pl.when(n_tiles - 1 - j >= zrow_ref[N_EXPERTS])
            def _():
                f = fill((n_tiles - 1 - j) * (tile_rows // SUBLANES))
                f.start()
                f.wait()

    def row_copy(src_row, dst_row):
        src = pl.multiple_of(src_row * SUBLANES, SUBLANES)
        dst = pl.multiple_of(dst_row * SUBLANES, SUBLANES)
        return pltpu.make_async_copy(h_ref.at[pl.ds(src, SUBLANES), :], xs_hbm.at[pl.ds(dst, SUBLANES), :], sem)

    def issue(t, _):
        for k in range(2):
            row_copy(t, dst_ref[0, k, t]).start(priority=k)
        return 0

    lax.fori_loop(0, tm, issue, 0, unroll=DMA_UNROLL)

    def drain(t, _):
        row_copy(0, 0).wait()
        row_copy(0, 0).wait()
        return 0

    lax.fori_loop(0, tm, drain, 0, unroll=DMA_UNROLL)


def _scatter_rows(zero_rows, dst, h_rows, n_rows):
    nt, _, tm = dst.shape
    grid_spec = pltpu.PrefetchScalarGridSpec(
        num_scalar_prefetch=1,
        grid=(nt,),
        in_specs=[pl.BlockSpec((1, SUBLANES, tm), lambda i, z: (i, 0, 0), memory_space=pltpu.SMEM),
                  pl.BlockSpec((tm * SUBLANES, LANES), lambda i, z: (i, 0))],
        out_specs=pl.BlockSpec(memory_space=pl.ANY),
        scratch_shapes=[pltpu.VMEM((MOE_TILE * SUBLANES, LANES), F32), pltpu.SemaphoreType.DMA(()),
                        pltpu.SemaphoreType.DMA(())],
    )
    return pl.pallas_call(
        _scatter_kernel,
        grid_spec=grid_spec,
        out_shape=jax.ShapeDtypeStruct((n_rows * SUBLANES, LANES), F32),
        compiler_params=pltpu.CompilerParams(dimension_semantics=("arbitrary",), has_side_effects=True),
        name="moe_scatter",
    )(zero_rows, dst, h_rows)


def _expert_kernel(te_ref, nu_ref, x_ref, wg_ref, wu_ref, wd_ref, y_ref, *, fc):
    m = pl.program_id(0)
    tr = x_ref.shape[0] // SUBLANES
    dff = wg_ref.shape[2]

    @pl.when(m < nu_ref[0])
    def _():
        xb = _load_tile_rows(x_ref, tr).astype(BF16)
        acc = jnp.zeros((tr, wd_ref.shape[2]), F32)
        for c0 in range(0, dff, fc):
            g = _dot(xb, wg_ref[0, :, c0:c0 + fc])
            u = _dot(xb, wu_ref[0, :, c0:c0 + fc])
            acc = acc + _dot((jax.nn.silu(g) * u).astype(BF16), wd_ref[0, c0:c0 + fc, :])
        _store_tile_rows(y_ref, acc)

    @pl.when(m >= nu_ref[0])
    def _():
        y_ref[...] = jnp.zeros_like(y_ref)


def _expert_ffn(tile_expert, n_used, xs, wg, wu, wd):
    E, D, F = wg.shape
    R = xs.shape[0] // SUBLANES
    tr = MOE_TILE
    n_tiles = R // tr
    fc = min(MOE_FF_CHUNK, F)

    def row_map(m, te, nu):
        return (jnp.minimum(m, nu[0] - 1), 0)

    def w_map(m, te, nu):
        return (te[jnp.minimum(m, nu[0] - 1)], 0, 0)

    resident = dict(pipeline_mode=pl.Buffered(1))
    grid_spec = pltpu.PrefetchScalarGridSpec(
        num_scalar_prefetch=2,
        grid=(n_tiles,),
        in_specs=[
            pl.BlockSpec((tr * SUBLANES, LANES), row_map),
            pl.BlockSpec((1, D, F), w_map, **resident),
            pl.BlockSpec((1, D, F), w_map, **resident),
            pl.BlockSpec((1, F, D), w_map, **resident),
        ],
        out_specs=pl.BlockSpec((tr * SUBLANES, LANES), lambda m, te, nu: (m, 0)),
    )
    return pl.pallas_call(
        functools.partial(_expert_kernel, fc=fc),
        grid_spec=grid_spec,
        out_shape=jax.ShapeDtypeStruct(xs.shape, F32),
        compiler_params=pltpu.CompilerParams(dimension_semantics=("arbitrary",)),
        name="moe_experts",
    )(tile_expert, n_used, xs, wg, wu, wd)


def _combine_kernel(dcur_ref, dnxt_ref, x_ref, gate_ref, fn_ref, ys_hbm, o_ref, buf_ref, sems, *, final_norm):
    tm = x_ref.shape[0]
    i = pl.program_id(0)
    slot = i % 2

    def row_copy(src_row, s, k, t):
        src = pl.multiple_of(src_row * SUBLANES, SUBLANES)
        dst = pl.multiple_of(t * SUBLANES, SUBLANES)
        return pltpu.make_async_copy(ys_hbm.at[pl.ds(src, SUBLANES), :],
                                     buf_ref.at[s, k, pl.ds(dst, SUBLANES), :], sems.at[s])

    def issue_tile(d_ref, s):
        def issue(t, _):
            for k in range(2):
                row_copy(d_ref[0, k, t], s, k, t).start(priority=k)
            return 0

        lax.fori_loop(0, tm, issue, 0, unroll=DMA_UNROLL)

    @pl.when(i == 0)
    def _():
        issue_tile(dcur_ref, 0)

    @pl.when(i + 1 < pl.num_programs(0))
    def _():
        issue_tile(dnxt_ref, 1 - slot)

    def drain(t, _):
        row_copy(0, slot, 0, 0).wait()
        row_copy(0, slot, 1, 0).wait()
        return 0

    lax.fori_loop(0, tm, drain, 0, unroll=DMA_UNROLL)
    g = gate_ref[...]
    y0 = _load_tile_rows(buf_ref.at[slot, 0], tm)
    y1 = _load_tile_rows(buf_ref.at[slot, 1], tm)
    xn = x_ref[...] + g[:, 0:1] * y0 + g[:, 1:2] * y1
    o_ref[...] = _rms(xn, fn_ref[...]) if final_norm else xn


def _combine(dst_c, x2d, gates_col, fnorm, ys, final_norm):
    T, D = x2d.shape
    nt, _, tm = dst_c.shape
    return pl.pallas_call(
        functools.partial(_combine_kernel, final_norm=final_norm),
        grid=(nt,),
        in_specs=[pl.BlockSpec((1, SUBLANES, tm), lambda i: (i, 0, 0), memory_space=pltpu.SMEM),
                  pl.BlockSpec((1, SUBLANES, tm), lambda i: (jnp.minimum(i + 1, nt - 1), 0, 0),
                               memory_space=pltpu.SMEM),
                  pl.BlockSpec((tm, D), lambda i: (i, 0)),
                  pl.BlockSpec((tm, SUBLANES), lambda i: (i, 0)),
                  pl.BlockSpec((1, D), lambda i: (0, 0)),
                  pl.BlockSpec(memory_space=pl.ANY)],
        out_specs=pl.BlockSpec((tm, D), lambda i: (i, 0)),
        scratch_shapes=[pltpu.VMEM((2, 2, tm * SUBLANES, LANES), F32), pltpu.SemaphoreType.DMA((2,))],
        out_shape=jax.ShapeDtypeStruct((T, D), F32),
        compiler_params=pltpu.CompilerParams(dimension_semantics=("arbitrary",)),
        name="moe_combine",
    )(dst_c, dst_c, x2d, gates_col, fnorm, ys)


def _swap_halves_cols(w):
    half = w.shape[-1] // 2
    return jnp.concatenate([w[..., half:], w[..., :half]], axis=-1)


def _pad_cols(w, width):
    return jnp.pad(w, [(0, 0)] * (w.ndim - 1) + [(0, width - w.shape[-1])])


def _prep_w_in(w):
    kpe0 = _C_CKV + MLA_KV_RANK - 0
    kpe = w[:, kpe0:kpe0 + MLA_ROPE]
    return jnp.concatenate([
        w[:, :kpe0],
        _pad_cols(kpe, LANES), _pad_cols(_swap_halves_cols(kpe), LANES),
        w[:, kpe0 + MLA_ROPE:],
    ], axis=1).astype(BF16)


def _prep_w_uq(w):
    w = w.reshape(MLA_Q_RANK, MLA_HEADS, MLA_NOPE + MLA_ROPE)
    nope, pe = w[..., :MLA_NOPE], w[..., MLA_NOPE:]
    out = jnp.concatenate([nope, _pad_cols(pe, LANES), _pad_cols(_swap_halves_cols(pe), LANES)], axis=-1)
    return out.reshape(MLA_Q_RANK, MLA_HEADS * _Q_HEAD_COLS).astype(BF16)


def _prep_w_ukv(w):
    w = w.reshape(MLA_KV_RANK, MLA_HEADS, MLA_NOPE + MLA_V)
    wukt = jnp.transpose(w[..., :MLA_NOPE], (1, 2, 0)).reshape(MLA_HEADS * MLA_NOPE, MLA_KV_RANK)
    wuv = w[..., MLA_NOPE:].reshape(MLA_KV_RANK, MLA_HEADS * MLA_V)
    return wukt.astype(BF16), wuv.astype(BF16)


def _prep_w_kpe_t(w_ext):
    return w_ext[:, _C_KPE:_C_XP].T


def _rope_tables(seq):
    pos = jnp.arange(seq, dtype=F32)
    inv_freq = 1.0 / (ROPE_THETA ** (jnp.arange(0, MLA_ROPE, 2, dtype=F32) / MLA_ROPE))
    ang = pos[:, None] * inv_freq[None, :]
    cos, sin = jnp.cos(ang), jnp.sin(ang)
    cpad = _pad_cols(jnp.concatenate([cos, cos], axis=-1), LANES)
    spad = _pad_cols(jnp.concatenate([-sin, sin], axis=-1), LANES)
    return cpad, spad


def _block_diag_pool(w):
    G, c, _ = w.shape
    eye = jnp.eye(G, dtype=w.dtype)
    return (eye[:, None, :, None] * w[:, :, None, :]).reshape(G * c, G * c).astype(BF16)


def _moe_layout(counts, n_tiles, tile):
    tiles_per = (counts + tile - 1) // tile
    ends = jnp.cumsum(tiles_per)
    starts = (ends - tiles_per) * tile
    tile_expert = jnp.sum((jnp.arange(n_tiles)[:, None] >= ends[None, :]).astype(jnp.int32), axis=1)
    tile_expert = jnp.minimum(tile_expert, N_EXPERTS - 1)
    zero_rows = jnp.minimum(starts + counts, (n_tiles - 1) * tile)
    zero_info = jnp.concatenate([zero_rows, ends[-1:]])
    return (starts.astype(jnp.int32), tile_expert.astype(jnp.int32), ends[-1:].astype(jnp.int32),
            zero_info.astype(jnp.int32))


def kernel(x, attn_norm, w_in, hgrn_lower_bounds, hgrn_out_norm, mla_q_norm, mla_w_uq, mla_kv_norm,
           mla_w_ukv, pool_w, pool_scale, w_o, ffn_norm, dense_w_gate, dense_w_up, dense_w_down,
           moe_router, moe_w_gate, moe_w_up, moe_w_down, final_norm):
    B, S, D = x.shape
    T = B * S
    depth = w_in.shape[0]
    cpad, spad = _rope_tables(S)
    p_lb = jax.nn.softmax(hgrn_lower_bounds.astype(F32), axis=0)
    lbs = jnp.cumsum(p_lb, axis=0) - p_lb[0:1]

    assert depth % 2 == 0, "the final RMSNorm is fused into the last (MoE) layer's combine kernel"
    for l in range(depth):
        wukt, wuv = _prep_w_ukv(mla_w_ukv[l])
        w_ext = _prep_w_in(w_in[l])
        hg, xp, q, kt, v = _in_proj(
            x, attn_norm[l][None], w_ext, mla_q_norm[l][None], _prep_w_uq(mla_w_uq[l]),
            mla_kv_norm[l][None], wukt, wuv, _prep_w_kpe_t(w_ext), cpad, spad)
        o_a = _hgrn(hg, lbs[l][None], hgrn_out_norm[l][None])
        o_b = _attention(q, kt, v)
        moe_layer = (l % 2 == 1)
        x, h = _mix_out(x, o_a, o_b, xp, _block_diag_pool(pool_w[l]), pool_scale[l][None],
                        w_o[l].astype(BF16), ffn_norm[l][None], moe_layer)
        j = l // 2
        if not moe_layer:
            x = _dense_ffn(x.reshape(T, D), h.reshape(T, D), dense_w_gate[j].astype(BF16),
                           dense_w_up[j].astype(BF16), dense_w_down[j].astype(BF16)).reshape(B, S, D)
        else:
            meta, gates, counts = _router(h, moe_router[j].T)
            n_tiles = (2 * T) // MOE_TILE + N_EXPERTS
            starts, tile_expert, n_used, zero_rows = _moe_layout(counts[:, 0], n_tiles, MOE_TILE)
            dst = _dest_rows(starts, meta)
            xs = _scatter_rows(zero_rows, dst, h, n_tiles * MOE_TILE)
            ys = _expert_ffn(tile_expert, n_used, xs, moe_w_gate[j].astype(BF16), moe_w_up[j].astype(BF16),
                             moe_w_down[j].astype(BF16))
            ct = min(COMBINE_TILE, T)
            dst_c = dst.transpose(1, 0, 2).reshape(SUBLANES, T // ct, ct).transpose(1, 0, 2)
            gates_col = gates.transpose(0, 2, 1).reshape(T, SUBLANES)
            last = (l == depth - 1)
            y = _combine(dst_c, x.reshape(T, D), gates_col, final_norm[None], ys, last)
            x = y.reshape(B, S, D)
    return x
```

```python
import functools
import math

import jax
import jax.numpy as jnp
import numpy as np
from jax import lax
from jax.experimental import pallas as pl
from jax.experimental.pallas import tpu as pltpu

F32 = jnp.float32
BF16 = jnp.bfloat16

HG_HEADS = 4
HG_KEY_DIM = 128
HG_VAL_DIM = 64
HG_KEY_WIDTH = HG_HEADS * HG_KEY_DIM
HG_WIDTH = HG_HEADS * HG_VAL_DIM
MIN_FORGET = 1e-20
MLA_HEADS = 4
MLA_Q_RANK = 256
MLA_KV_RANK = 128
MLA_NOPE = 128
MLA_ROPE = 64
MLA_V = 128
MLA_WIDTH = MLA_HEADS * MLA_V
ROPE_THETA = 10000.0
MASK_VALUE = -1e30
POOL_GROUPS = 4
POOL_WINDOWS = (2, 4, 8, 16)
POOL_WIDTH = 256
POOL_GROUP_DIM = POOL_WIDTH // POOL_GROUPS
N_EXPERTS = 8
EPS = 1e-6

LANES = 128
SUBLANES = 8
QK_PAD = 256
V_EXT = 256

TOKEN_TILE = 1024
IN_PROJ_CHAINS = 1
HGRN_CHUNK = 256
ATTN_BK = 512
ATTN_HEADS_PER_STEP = 1
FF_CHUNK = 512
MOE_FF_CHUNK = 512
MOE_TILE = 1024
ROUTE_TILE = 512
COMBINE_TILE = 256
POOL_HALO = 16
DMA_UNROLL = 8

_C_HG = 0
_C_CQ = 2 * HG_KEY_WIDTH + 2 * HG_WIDTH
_C_CKV = _C_CQ + MLA_Q_RANK
_C_KPE = _C_CKV + MLA_KV_RANK
_C_KPES = _C_KPE + LANES
_C_XP = _C_KPES + LANES
_C_END = _C_XP + POOL_WIDTH
_Q_HEAD_COLS = 3 * LANES


def _rms(x, g):
    return x * lax.rsqrt(jnp.mean(x * x, axis=-1, keepdims=True) + EPS) * g


def _dot(a, b):
    return jnp.dot(a, b, preferred_element_type=F32)


def _dot_nt(a, b):
    return lax.dot_general(a, b, (((1,), (1,)), ((), ())), preferred_element_type=F32)


def _dot_tn(a, b):
    return lax.dot_general(a, b, (((0,), (0,)), ((), ())), preferred_element_type=F32)


def _const_spec(shape):
    nd = len(shape)
    return pl.BlockSpec(shape, lambda *_: (0,) * nd)


def _load_tile_rows(ref, n):
    return jnp.concatenate([ref[pl.ds(s, n, stride=SUBLANES), :] for s in range(SUBLANES)], axis=1)


def _store_tile_rows(ref, val):
    n = val.shape[0]
    for s in range(SUBLANES):
        ref[pl.ds(s, n, stride=SUBLANES), :] = val[:, s * LANES:(s + 1) * LANES]


def _in_proj_kernel(x_ref, g_ref, w_ref, qn_ref, wuq_ref, kvn_ref, wukt_ref, wuv_ref, wkpet_ref,
                    cpad_ref, spad_ref, cpadt_ref, spadt_ref,
                    hg_ref, xp_ref, q_ref, kt_ref, v_ref):
    scale = (MLA_NOPE + MLA_ROPE) ** -0.5 * math.log2(math.e)
    tm = x_ref.shape[1]
    rows = tm // IN_PROJ_CHAINS
    ones_col = (lax.broadcasted_iota(jnp.int32, (rows, LANES), 1) == 0).astype(BF16)

    for r0 in range(0, tm, rows):
        rs = slice(r0, r0 + rows)
        h = _rms(x_ref[0, rs, :], g_ref[...]).astype(BF16)
        hg_ref[0, rs, :] = _dot(h, w_ref[:, _C_HG:_C_CQ])
        xp_ref[0, rs, :] = _dot(h, w_ref[:, _C_XP:_C_END])
        cpad = cpad_ref[rs, :]
        spad = spad_ref[rs, :]

        cq = _dot(h, w_ref[:, _C_CQ:_C_CKV])
        cqn = _rms(cq, qn_ref[...]).astype(BF16)
        for hd in range(MLA_HEADS):
            qh = _dot(cqn, wuq_ref[:, hd * _Q_HEAD_COLS:(hd + 1) * _Q_HEAD_COLS])
            q_ref[0, hd, rs, 0:LANES] = (qh[:, 0:LANES] * scale).astype(BF16)
            pe = qh[:, LANES:2 * LANES] * cpad + qh[:, 2 * LANES:3 * LANES] * spad
            q_ref[0, hd, rs, LANES:QK_PAD] = (pe * scale).astype(BF16)

        ckv = _dot(h, w_ref[:, _C_CKV:_C_KPE])
        ckvn = _rms(ckv, kvn_ref[...]).astype(BF16)
        kpet = (_dot_nt(wkpet_ref[0:LANES, :], h) * cpadt_ref[:, rs]
                + _dot_nt(wkpet_ref[LANES:2 * LANES, :], h) * spadt_ref[:, rs]).astype(BF16)
        v_all = _dot(ckvn, wuv_ref[...]).astype(BF16)
        for hd in range(MLA_HEADS):
            kt_ref[0, hd, 0:LANES, rs] = _dot_nt(wukt_ref[hd * LANES:(hd + 1) * LANES, :], ckvn).astype(BF16)
            kt_ref[0, hd, LANES:QK_PAD, rs] = kpet
            v_ref[0, hd, rs, 0:MLA_V] = v_all[:, hd * MLA_V:(hd + 1) * MLA_V]
            v_ref[0, hd, rs, MLA_V:V_EXT] = ones_col


def _in_proj(x, g, w_ext, qn, wuq_ext, kvn, wukt, wuv, wkpet, cpad, spad):
    B, S, D = x.shape
    tm = min(TOKEN_TILE, S)
    grid = (B, S // tm)
    n_hg = _C_CQ
    out_shape = (
        jax.ShapeDtypeStruct((B, S, n_hg), F32),
        jax.ShapeDtypeStruct((B, S, POOL_WIDTH), F32),
        jax.ShapeDtypeStruct((B, MLA_HEADS, S, QK_PAD), BF16),
        jax.ShapeDtypeStruct((B, MLA_HEADS, QK_PAD, S), BF16),
        jax.ShapeDtypeStruct((B, MLA_HEADS, S, V_EXT), BF16),
    )
    return pl.pallas_call(
        _in_proj_kernel,
        grid=grid,
        in_specs=[
            pl.BlockSpec((1, tm, D), lambda b, i: (b, i, 0)),
            _const_spec((1, D)),
            _const_spec(w_ext.shape),
            _const_spec((1, MLA_Q_RANK)),
            _const_spec(wuq_ext.shape),
            _const_spec((1, MLA_KV_RANK)),
            _const_spec(wukt.shape),
            _const_spec(wuv.shape),
            _const_spec(wkpet.shape),
            pl.BlockSpec((tm, LANES), lambda b, i: (i, 0)),
            pl.BlockSpec((tm, LANES), lambda b, i: (i, 0)),
            pl.BlockSpec((LANES, tm), lambda b, i: (0, i)),
            pl.BlockSpec((LANES, tm), lambda b, i: (0, i)),
        ],
        out_specs=(
            pl.BlockSpec((1, tm, n_hg), lambda b, i: (b, i, 0)),
            pl.BlockSpec((1, tm, POOL_WIDTH), lambda b, i: (b, i, 0)),
            pl.BlockSpec((1, MLA_HEADS, tm, QK_PAD), lambda b, i: (b, 0, i, 0)),
            pl.BlockSpec((1, MLA_HEADS, QK_PAD, tm), lambda b, i: (b, 0, 0, i)),
            pl.BlockSpec((1, MLA_HEADS, tm, V_EXT), lambda b, i: (b, 0, i, 0)),
        ),
        out_shape=out_shape,
        compiler_params=pltpu.CompilerParams(dimension_semantics=("arbitrary", "arbitrary")),
        name="in_proj",
    )(x, g, w_ext, qn, wuq_ext, kvn, wukt, wuv, wkpet, cpad, spad, cpad.T, spad.T)


def _split3(x):
    hi = x.astype(BF16)
    r1 = x - hi.astype(F32)
    mid = r1.astype(BF16)
    lo = (r1 - mid.astype(F32)).astype(BF16)
    return hi, mid, lo


def _hgrn_kernel(q_ref, f_ref, i_ref, g_ref, lb_ref, on_ref, tril_ref, grp_ref, o_ref, st_ref):
    C = q_ref.shape[1]
    KW = HG_KEY_WIDTH

    @pl.when(pl.program_id(1) == 0)
    def _():
        st_ref[...] = jnp.zeros_like(st_ref)

    lb = lb_ref[...]
    z = f_ref[0]
    forget = lb + (1.0 - lb) * jax.nn.sigmoid(z)
    lg = jnp.log2(jnp.maximum(forget, MIN_FORGET))
    kk = (1.0 - lb) * jax.nn.sigmoid(-z)
    qq = jax.nn.silu(q_ref[0])
    vv = i_ref[0]

    tril = tril_ref[...]
    b = sum(_dot(tril, part) for part in _split3(lg))

    sides, masks = [], []
    row = lax.broadcasted_iota(jnp.int32, (C, C), 0)
    col = lax.broadcasted_iota(jnp.int32, (C, C), 1)
    sub = lax.broadcasted_iota(jnp.int32, (C, KW), 0)
    half = C // 2
    while half >= 4:
        blk = 2 * half
        b3 = b.reshape(C // blk, blk, KW)
        ref_row = jnp.broadcast_to(b3[:, half - 1:half, :], b3.shape).reshape(C, KW)
        e = jnp.exp2(-jnp.abs(b - ref_row))
        x = (jnp.where((sub & half) != 0, qq, kk) * e).astype(BF16)
        sides.append((x, x))
        shift = int(math.log2(blk))
        masks.append(((row >> shift) == (col >> shift)) & ((row & half) != 0) & ((col & half) == 0))
        half //= 2
    b3 = b.reshape(C // 8, 8, KW)
    mid_lo = 0.5 * (b3[:, 0:1, :] + b3[:, 3:4, :])
    mid_hi = 0.5 * (b3[:, 4:5, :] + b3[:, 7:8, :])
    sub8 = lax.broadcasted_iota(jnp.int32, b3.shape, 1)
    mid = jnp.where(sub8 < 4, mid_lo, mid_hi).reshape(C, KW)
    sides.append(((qq * jnp.exp2(b - mid)).astype(BF16), (kk * jnp.exp2(mid - b)).astype(BF16)))
    masks.append(((row >> 2) == (col >> 2)) & (col <= row))

    lane_v = lax.broadcasted_iota(jnp.int32, (C, HG_WIDTH), 1)
    o = _dot_nt((qq * jnp.exp2(b)).astype(BF16), st_ref[...].astype(BF16))
    for hd in range(HG_HEADS):
        ks = slice(hd * HG_KEY_DIM, (hd + 1) * HG_KEY_DIM)
        a = jnp.zeros((C, C), F32)
        for (qt, kt), m in zip(sides, masks):
            a = jnp.where(m, _dot_nt(qt[:, ks], kt[:, ks]), a)
        v_h = jnp.where((lane_v >> 6) == hd, vv, 0.0).astype(BF16)
        o = o + _dot(a.astype(BF16), v_h)

    b_last = b[C - 1:C, :]
    khat = (kk * jnp.exp2(b_last - b)).astype(BF16)
    st_row = lax.broadcasted_iota(jnp.int32, (HG_WIDTH, KW), 0)
    st_col = lax.broadcasted_iota(jnp.int32, (HG_WIDTH, KW), 1)
    new_st = st_ref[...] * jnp.exp2(b_last) + _dot_tn(vv.astype(BF16), khat)
    st_ref[...] = jnp.where((st_row >> 6) == (st_col >> 7), new_st, 0.0)

    grp = grp_ref[...]
    ssq = sum(_dot(part, grp) for part in _split3(o * o))
    on = o * lax.rsqrt(ssq * (1.0 / HG_VAL_DIM) + EPS) * on_ref[...]
    o_ref[0] = (on * jax.nn.silu(g_ref[0])).astype(o_ref.dtype)


def _hgrn(hg, lb, out_norm):
    B, S, _ = hg.shape
    C = min(HGRN_CHUNK, S)
    kb = HG_KEY_WIDTH // HG_KEY_WIDTH
    tril = jnp.asarray(np.tril(np.ones((C, C), np.float32)), BF16)
    lane_group = np.arange(HG_WIDTH) // HG_VAL_DIM
    grp = jnp.asarray((lane_group[:, None] == lane_group[None, :]).astype(np.float32), BF16)
    return pl.pallas_call(
        _hgrn_kernel,
        grid=(B, S // C),
        in_specs=[
            pl.BlockSpec((1, C, HG_KEY_WIDTH), lambda b, c: (b, c, 0)),
            pl.BlockSpec((1, C, HG_KEY_WIDTH), lambda b, c: (b, c, kb)),
            pl.BlockSpec((1, C, HG_WIDTH), lambda b, c: (b, c, 2 * HG_KEY_WIDTH // HG_WIDTH)),
            pl.BlockSpec((1, C, HG_WIDTH), lambda b, c: (b, c, 2 * HG_KEY_WIDTH // HG_WIDTH + 1)),
            _const_spec((1, HG_KEY_WIDTH)),
            _const_spec((1, HG_WIDTH)),
            _const_spec(tril.shape),
            _const_spec(grp.shape),
        ],
        out_specs=pl.BlockSpec((1, C, HG_WIDTH), lambda b, c: (b, c, 0)),
        out_shape=jax.ShapeDtypeStruct((B, S, HG_WIDTH), BF16),
        scratch_shapes=[pltpu.VMEM((HG_WIDTH, HG_KEY_WIDTH), F32)],
        compiler_params=pltpu.CompilerParams(dimension_semantics=("arbitrary", "arbitrary")),
        name="hgrn2",
    )(hg, hg, hg, hg, lb, out_norm, tril, grp)


def _attn_kernel(q_ref, kt_ref, v_ref, o_ref, s0_ref, s1_ref, acc_ref, *, bk):
    bq = q_ref.shape[2]
    nh = q_ref.shape[1]
    i = pl.program_id(2)
    qs = [q_ref[0, hd] for hd in range(nh)]

    def causal(s, key0):
        qry = lax.broadcasted_iota(jnp.int32, (bq, bk), 0)
        key = key0 + lax.broadcasted_iota(jnp.int32, (bq, bk), 1)
        return jnp.where(key <= qry, s, MASK_VALUE)

    def scores(blk, s_ref):
        start = pl.multiple_of(blk * bk, bk)
        for hd in range(nh):
            s_ref[hd] = _dot(qs[hd], kt_ref[0, hd, :, pl.ds(start, bk)])

    def softmax_pv(blk, s_ref, ms, key0=None):
        start = pl.multiple_of(blk * bk, bk)
        out = []
        for hd in range(nh):
            s = s_ref[hd]
            if key0 is not None:
                s = causal(s, key0)
            m_new = jnp.maximum(ms[hd], jnp.max(s, axis=1, keepdims=True))
            p = jnp.exp2(s - m_new).astype(BF16)
            acc_ref[hd] = jnp.exp2(ms[hd] - m_new) * acc_ref[hd] + _dot(p, v_ref[0, hd, pl.ds(start, bk), :])
            out.append(m_new)
        return tuple(out)

    acc_ref[...] = jnp.zeros_like(acc_ref)
    scores(0, s0_ref)

    def pair(t, ms):
        scores(2 * t + 1, s1_ref)
        ms = softmax_pv(2 * t, s0_ref, ms)
        scores(2 * t + 2, s0_ref)
        return softmax_pv(2 * t + 1, s1_ref, ms)

    def pairs(t0, n, c):
        for k in range(n):
            c = pair(t0 + k, c)
        return c

    ms = tuple(jnp.full((bq, 1), -jnp.inf, F32) for _ in range(nh))
    ms = lax.fori_loop(0, i // 4, lambda t, c: pairs(4 * t, 4, c), ms)
    done = (i // 4) * 4
    ms = lax.cond((i & 2) != 0, lambda c: pairs(done, 2, c), lambda c: c, ms)
    done = done + (i & 2)
    ms = lax.cond((i & 1) != 0, lambda c: pairs(done, 1, c), lambda c: c, ms)
    start1 = pl.multiple_of((2 * i + 1) * bk, bk)
    lower = [_dot(qs[hd][bk:, :], kt_ref[0, hd, :, pl.ds(start1, bk)]) for hd in range(nh)]
    ms = softmax_pv(2 * i, s0_ref, ms, 0)
    qry = lax.broadcasted_iota(jnp.int32, (bk, bk), 0)
    key = lax.broadcasted_iota(jnp.int32, (bk, bk), 1)
    for hd in range(nh):
        s = jnp.where(key <= qry, lower[hd], MASK_VALUE)
        m_old = ms[hd][bk:, :]
        m_new = jnp.maximum(m_old, jnp.max(s, axis=1, keepdims=True))
        p = jnp.exp2(s - m_new).astype(BF16)
        acc_ref[hd, bk:, :] = (jnp.exp2(m_old - m_new) * acc_ref[hd, bk:, :]
                               + _dot(p, v_ref[0, hd, pl.ds(start1, bk), :]))
    for hd in range(nh):
        acc = acc_ref[hd]
        o_ref[0, :, hd * MLA_V:(hd + 1) * MLA_V] = (acc[:, 0:MLA_V] / acc[:, MLA_V:MLA_V + 1]).astype(o_ref.dtype)


def _attention(q, kt, v):
    B, H, S, _ = q.shape
    bk = min(ATTN_BK, S // 2)
    bq = 2 * bk
    nh = ATTN_HEADS_PER_STEP
    resident = dict(pipeline_mode=pl.Buffered(1)) if nh > 1 else {}
    return pl.pallas_call(
        functools.partial(_attn_kernel, bk=bk),
        grid=(B, H // nh, S // bq),
        in_specs=[
            pl.BlockSpec((1, nh, bq, QK_PAD), lambda b, h, i: (b, h, i, 0)),
            pl.BlockSpec((1, nh, QK_PAD, S), lambda b, h, i: (b, h, 0, 0), **resident),
            pl.BlockSpec((1, nh, S, V_EXT), lambda b, h, i: (b, h, 0, 0), **resident),
        ],
        out_specs=pl.BlockSpec((1, bq, nh * MLA_V), lambda b, h, i: (b, i, h)),
        out_shape=jax.ShapeDtypeStruct((B, S, H * MLA_V), BF16),
        scratch_shapes=[pltpu.VMEM((nh, bq, bk), F32), pltpu.VMEM((nh, bq, bk), F32),
                        pltpu.VMEM((nh, bq, V_EXT), F32)],
        compiler_params=pltpu.CompilerParams(dimension_semantics=("arbitrary", "arbitrary", "arbitrary")),
        name="mla_attention",
    )(q, kt, v)


def _mix_out_kernel(x_ref, oa_ref, ob_ref, xp_ref, wpool_ref, pscale_ref, wo_ref, fn_ref,
                    xo_ref, h_ref, halo_ref, *, tile_rows):
    tm = xp_ref.shape[1]
    i = pl.program_id(1)

    @pl.when(i == 0)
    def _():
        halo_ref[...] = jnp.zeros_like(halo_ref)

    xp = xp_ref[0]
    xx = jnp.concatenate([halo_ref[...], xp], axis=0)
    halo_ref[...] = xp[tm - POOL_HALO:, :]

    w2 = xx[1:, :] + xx[:-1, :]
    w4 = w2[2:, :] + w2[:-2, :]
    w8 = w4[4:, :] + w4[:-4, :]
    w16 = w8[8:, :] + w8[:-8, :]
    sums = (w2[POOL_HALO - 1:, :], w4[POOL_HALO - 3:, :], w8[POOL_HALO - 7:, :], w16[POOL_HALO - 15:, :])
    t = i * tm + lax.broadcasted_iota(jnp.int32, (tm, POOL_WIDTH), 0)
    lane = lax.broadcasted_iota(jnp.int32, (tm, POOL_WIDTH), 1)
    pooled = jnp.zeros((tm, POOL_WIDTH), F32)
    for gi, w in enumerate(POOL_WINDOWS):
        cnt = jnp.minimum(t + 1, w).astype(F32)
        pooled = jnp.where((lane >> 6) == gi, sums[gi] / cnt, pooled)
    pooled = pooled - xp
    oc = _dot(pooled.astype(BF16), wpool_ref[...]) * pscale_ref[...]

    y = _dot(oa_ref[0], wo_ref[0:HG_WIDTH, :])
    y = y + _dot(ob_ref[0], wo_ref[HG_WIDTH:HG_WIDTH + MLA_WIDTH, :])
    y = y + _dot(oc.astype(BF16), wo_ref[HG_WIDTH + MLA_WIDTH:, :])
    xn = x_ref[0] + y
    xo_ref[0] = xn
    h = _rms(xn, fn_ref[...])
    if tile_rows:
        _store_tile_rows(h_ref, h)
    else:
        h_ref[0] = h.astype(h_ref.dtype)


def _mix_out(x, oa, ob, xp, wpool_bd, pscale, wo, fnorm, tile_rows):
    B, S, D = x.shape
    tm = min(TOKEN_TILE, S)
    nt = S // tm
    tok = lambda w: pl.BlockSpec((1, tm, w), lambda b, i: (b, i, 0))
    if tile_rows:
        assert D == SUBLANES * LANES
        h_spec = pl.BlockSpec((tm * SUBLANES, LANES), lambda b, i: (b * nt + i, 0))
        h_shape = jax.ShapeDtypeStruct((B * S * SUBLANES, LANES), F32)
    else:
        h_spec, h_shape = tok(D), jax.ShapeDtypeStruct((B, S, D), BF16)
    return pl.pallas_call(
        functools.partial(_mix_out_kernel, tile_rows=tile_rows),
        grid=(B, nt),
        in_specs=[tok(D), tok(HG_WIDTH), tok(MLA_WIDTH), tok(POOL_WIDTH),
                  _const_spec(wpool_bd.shape), _const_spec((1, POOL_WIDTH)), _const_spec(wo.shape),
                  _const_spec((1, D))],
        out_specs=(tok(D), h_spec),
        out_shape=(jax.ShapeDtypeStruct((B, S, D), F32), h_shape),
        scratch_shapes=[pltpu.VMEM((POOL_HALO, POOL_WIDTH), F32)],
        compiler_params=pltpu.CompilerParams(dimension_semantics=("arbitrary", "arbitrary")),
        name="mix_out",
    )(x, oa, ob, xp, wpool_bd, pscale, wo, fnorm)


def _dense_ffn_kernel(x_ref, h_ref, wg_ref, wu_ref, wd_ref, o_ref):
    h = h_ref[...]
    acc = x_ref[...]
    dff = wg_ref.shape[1]
    for c0 in range(0, dff, FF_CHUNK):
        g = _dot(h, wg_ref[:, c0:c0 + FF_CHUNK])
        u = _dot(h, wu_ref[:, c0:c0 + FF_CHUNK])
        acc = acc + _dot((jax.nn.silu(g) * u).astype(BF16), wd_ref[c0:c0 + FF_CHUNK, :])
    o_ref[...] = acc


def _dense_ffn(x2d, h2d, wg, wu, wd):
    T, D = x2d.shape
    tm = min(TOKEN_TILE, T)
    tok = pl.BlockSpec((tm, D), lambda i: (i, 0))
    return pl.pallas_call(
        _dense_ffn_kernel,
        grid=(T // tm,),
        in_specs=[tok, tok, _const_spec(wg.shape), _const_spec(wu.shape), _const_spec(wd.shape)],
        out_specs=tok,
        out_shape=jax.ShapeDtypeStruct((T, D), F32),
        compiler_params=pltpu.CompilerParams(dimension_semantics=("arbitrary",)),
        name="dense_ffn",
    )(x2d, h2d, wg, wu, wd)


def _router_kernel(h_ref, rt_ref, meta_ref, gate_ref, cnt_ref, run_ref):
    tm = h_ref.shape[0] // SUBLANES
    E = N_EXPERTS

    @pl.when(pl.program_id(0) == 0)
    def _():
        run_ref[...] = jnp.zeros_like(run_ref)

    logits = lax.dot_general(rt_ref[...], _load_tile_rows(h_ref, tm), (((1,), (1,)), ((), ())),
                             precision=lax.Precision.HIGHEST, preferred_element_type=F32)
    eid = lax.broadcasted_iota(jnp.int32, (E, tm), 0)
    m1 = jnp.max(logits, axis=0, keepdims=True)
    i1 = jnp.min(jnp.where(logits == m1, eid, E), axis=0, keepdims=True)
    rest = jnp.where(eid == i1, -jnp.inf, logits)
    m2 = jnp.max(rest, axis=0, keepdims=True)
    i2 = jnp.min(jnp.where(rest == m2, eid, E), axis=0, keepdims=True)
    e2 = jnp.exp(m2 - m1)
    g1 = 1.0 / (1.0 + e2)
    g2 = e2 / (1.0 + e2)

    sel = ((eid == i1) | (eid == i2))
    r = lax.broadcasted_iota(jnp.int32, (tm, tm), 0)
    c = lax.broadcasted_iota(jnp.int32, (tm, tm), 1)
    before = (r < c).astype(BF16)
    excl = _dot(sel.astype(BF16), before) + run_ref[:, 0:1]
    rank1 = jnp.sum(jnp.where(eid == i1, excl, 0.0), axis=0, keepdims=True).astype(jnp.int32)
    rank2 = jnp.sum(jnp.where(eid == i2, excl, 0.0), axis=0, keepdims=True).astype(jnp.int32)
    run_ref[...] = run_ref[...] + jnp.sum(sel.astype(F32), axis=1, keepdims=True)
    cnt_ref[...] = run_ref[...].astype(jnp.int32)

    zi = jnp.zeros((1, tm), jnp.int32)
    meta_ref[0] = jnp.concatenate([i1, i2, rank1, rank2, zi, zi, zi, zi], axis=0)
    zf = jnp.zeros((1, tm), F32)
    gate_ref[0] = jnp.concatenate([g1, g2, zf, zf, zf, zf, zf, zf], axis=0)


def _router(h_rows, router_t):
    T = h_rows.shape[0] // SUBLANES
    tm = min(ROUTE_TILE, T)
    nt = T // tm
    return pl.pallas_call(
        _router_kernel,
        grid=(nt,),
        in_specs=[pl.BlockSpec((tm * SUBLANES, LANES), lambda i: (i, 0)), _const_spec(router_t.shape)],
        out_specs=(pl.BlockSpec((1, SUBLANES, tm), lambda i: (i, 0, 0)),
                   pl.BlockSpec((1, SUBLANES, tm), lambda i: (i, 0, 0)),
                   _const_spec((N_EXPERTS, LANES))),
        out_shape=(jax.ShapeDtypeStruct((nt, SUBLANES, tm), jnp.int32),
                   jax.ShapeDtypeStruct((nt, SUBLANES, tm), F32),
                   jax.ShapeDtypeStruct((N_EXPERTS, LANES), jnp.int32)),
        scratch_shapes=[pltpu.VMEM((N_EXPERTS, LANES), F32)],
        compiler_params=pltpu.CompilerParams(dimension_semantics=("arbitrary",)),
        name="moe_router",
    )(h_rows, router_t)


def _dest_kernel(start_ref, meta_ref, dst_ref):
    meta = meta_ref[0]
    rows = []
    for k in range(2):
        e = meta[k:k + 1, :]
        base = jnp.zeros_like(e)
        for ex in range(N_EXPERTS):
            base = jnp.where(e == ex, start_ref[ex], base)
        rows.append(base + meta[2 + k:3 + k, :])
    dst_ref[0] = jnp.concatenate(rows + [jnp.zeros_like(rows[0])] * (SUBLANES - 2), axis=0)


def _dest_rows(starts, meta):
    nt, _, tm = meta.shape
    spec = pl.BlockSpec((1, SUBLANES, tm), lambda i, s: (i, 0, 0))
    return pl.pallas_call(
        _dest_kernel,
        grid_spec=pltpu.PrefetchScalarGridSpec(num_scalar_prefetch=1, grid=(nt,), in_specs=[spec], out_specs=spec),
        out_shape=jax.ShapeDtypeStruct(meta.shape, jnp.int32),
        compiler_params=pltpu.CompilerParams(dimension_semantics=("arbitrary",)),
        name="moe_dest",
    )(starts, meta)


def _scatter_kernel(zrow_ref, dst_ref, h_ref, xs_hbm, zero_ref, sem, zsem):
    tm = dst_ref.shape[2]

    @pl.when(pl.program_id(0) == 0)
    def _():
        zero_ref[...] = jnp.zeros_like(zero_ref)
        tile_rows = zero_ref.shape[0]

        def fill(row):
            start = pl.multiple_of(row * SUBLANES, SUBLANES)
            return pltpu.make_async_copy(zero_ref, xs_hbm.at[pl.ds(start, tile_rows), :], zsem)

        fills = [fill(zrow_ref[e]) for e in range(N_EXPERTS)]
        for f in fills:
            f.start()
        for f in fills:
            f.wait()
        n_tiles = xs_hbm.shape[0] // tile_rows
        for j in range(N_EXPERTS):
            @pl.when(n_tiles - 1 - j >= zrow_ref[N_EXPERTS])
            def _():
                f = fill((n_tiles - 1 - j) * (tile_rows // SUBLANES))
                f.start()
                f.wait()

    def row_copy(src_row, dst_row):
        src = pl.multiple_of(src_row * SUBLANES, SUBLANES)
        dst = pl.multiple_of(dst_row * SUBLANES, SUBLANES)
        return pltpu.make_async_copy(h_ref.at[pl.ds(src, SUBLANES), :], xs_hbm.at[pl.ds(dst, SUBLANES), :], sem)

    def issue(t, _):
        for k in range(2):
            row_copy(t, dst_ref[0, k, t]).start(priority=k)
        return 0

    lax.fori_loop(0, tm, issue, 0, unroll=DMA_UNROLL)

    def drain(t, _):
        row_copy(0, 0).wait()
        row_copy(0, 0).wait()
        return 0

    lax.fori_loop(0, tm, drain, 0, unroll=DMA_UNROLL)


def _scatter_rows(zero_rows, dst, h_rows, n_rows):
    nt, _, tm = dst.shape
    grid_spec = pltpu.PrefetchScalarGridSpec(
        num_scalar_prefetch=1,
        grid=(nt,),
        in_specs=[pl.BlockSpec((1, SUBLANES, tm), lambda i, z: (i, 0, 0), memory_space=pltpu.SMEM),
                  pl.BlockSpec((tm * SUBLANES, LANES), lambda i, z: (i, 0))],
        out_specs=pl.BlockSpec(memory_space=pl.ANY),
        scratch_shapes=[pltpu.VMEM((MOE_TILE * SUBLANES, LANES), F32), pltpu.SemaphoreType.DMA(()),
                        pltpu.SemaphoreType.DMA(())],
    )
    return pl.pallas_call(
        _scatter_kernel,
        grid_spec=grid_spec,
        out_shape=jax.ShapeDtypeStruct((n_rows * SUBLANES, LANES), F32),
        compiler_params=pltpu.CompilerParams(dimension_semantics=("arbitrary",), has_side_effects=True),
        name="moe_scatter",
    )(zero_rows, dst, h_rows)


def _expert_kernel(te_ref, nu_ref, x_ref, wg_ref, wu_ref, wd_ref, y_ref, *, fc):
    m = pl.program_id(0)
    tr = x_ref.shape[0] // SUBLANES
    dff = wg_ref.shape[2]

    @pl.when(m < nu_ref[0])
    def _():
        xb = _load_tile_rows(x_ref, tr).astype(BF16)
        acc = jnp.zeros((tr, wd_ref.shape[2]), F32)
        for c0 in range(0, dff, fc):
            g = _dot(xb, wg_ref[0, :, c0:c0 + fc])
            u = _dot(xb, wu_ref[0, :, c0:c0 + fc])
            acc = acc + _dot((jax.nn.silu(g) * u).astype(BF16), wd_ref[0, c0:c0 + fc, :])
        _store_tile_rows(y_ref, acc)

    @pl.when(m >= nu_ref[0])
    def _():
        y_ref[...] = jnp.zeros_like(y_ref)


def _expert_ffn(tile_expert, n_used, xs, wg, wu, wd):
    E, D, F = wg.shape
    R = xs.shape[0] // SUBLANES
    tr = MOE_TILE
    n_tiles = R // tr
    fc = min(MOE_FF_CHUNK, F)

    def row_map(m, te, nu):
        return (jnp.minimum(m, nu[0] - 1), 0)

    def w_map(m, te, nu):
        return (te[jnp.minimum(m, nu[0] - 1)], 0, 0)

    resident = dict(pipeline_mode=pl.Buffered(1))
    grid_spec = pltpu.PrefetchScalarGridSpec(
        num_scalar_prefetch=2,
        grid=(n_tiles,),
        in_specs=[
            pl.BlockSpec((tr * SUBLANES, LANES), row_map),
            pl.BlockSpec((1, D, F), w_map, **resident),
            pl.BlockSpec((1, D, F), w_map, **resident),
            pl.BlockSpec((1, F, D), w_map, **resident),
        ],
        out_specs=pl.BlockSpec((tr * SUBLANES, LANES), lambda m, te, nu: (m, 0)),
    )
    return pl.pallas_call(
        functools.partial(_expert_kernel, fc=fc),
        grid_spec=grid_spec,
        out_shape=jax.ShapeDtypeStruct(xs.shape, F32),
        compiler_params=pltpu.CompilerParams(dimension_semantics=("arbitrary",)),
        name="moe_experts",
    )(tile_expert, n_used, xs, wg, wu, wd)


def _combine_kernel(dcur_ref, dnxt_ref, x_ref, gate_ref, fn_ref, ys_hbm, o_ref, buf_ref, sems, *, final_norm):
    tm = x_ref.shape[0]
    i = pl.program_id(0)
    slot = i % 2

    def row_copy(src_row, s, k, t):
        src = pl.multiple_of(src_row * SUBLANES, SUBLANES)
        dst = pl.multiple_of(t * SUBLANES, SUBLANES)
        return pltpu.make_async_copy(ys_hbm.at[pl.ds(src, SUBLANES), :],
                                     buf_ref.at[s, k, pl.ds(dst, SUBLANES), :], sems.at[s])

    def issue_tile(d_ref, s):
        def issue(t, _):
            for k in range(2):
                row_copy(d_ref[0, k, t], s, k, t).start(priority=k)
            return 0

        lax.fori_loop(0, tm, issue, 0, unroll=DMA_UNROLL)

    @pl.when(i == 0)
    def _():
        issue_tile(dcur_ref, 0)

    @pl.when(i + 1 < pl.num_programs(0))
    def _():
        issue_tile(dnxt_ref, 1 - slot)

    def drain(t, _):
        row_copy(0, slot, 0, 0).wait()
        row_copy(0, slot, 1, 0).wait()
        return 0

    lax.fori_loop(0, tm, drain, 0, unroll=DMA_UNROLL)
    g = gate_ref[...]
    y0 = _load_tile_rows(buf_ref.at[slot, 0], tm)
    y1 = _load_tile_rows(buf_ref.at[slot, 1], tm)
    xn = x_ref[...] + g[:, 0:1] * y0 + g[:, 1:2] * y1
    o_ref[...] = _rms(xn, fn_ref[...]) if final_norm else xn


def _combine(dst_c, x2d, gates_col, fnorm, ys, final_norm):
    T, D = x2d.shape
    nt, _, tm = dst_c.shape
    return pl.pallas_call(
        functools.partial(_combine_kernel, final_norm=final_norm),
        grid=(nt,),
        in_specs=[pl.BlockSpec((1, SUBLANES, tm), lambda i: (i, 0, 0), memory_space=pltpu.SMEM),
                  pl.BlockSpec((1, SUBLANES, tm), lambda i: (jnp.minimum(i + 1, nt - 1), 0, 0),
                               memory_space=pltpu.SMEM),
                  pl.BlockSpec((tm, D), lambda i: (i, 0)),
                  pl.BlockSpec((tm, SUBLANES), lambda i: (i, 0)),
                  pl.BlockSpec((1, D), lambda i: (0, 0)),
                  pl.BlockSpec(memory_space=pl.ANY)],
        out_specs=pl.BlockSpec((tm, D), lambda i: (i, 0)),
        scratch_shapes=[pltpu.VMEM((2, 2, tm * SUBLANES, LANES), F32), pltpu.SemaphoreType.DMA((2,))],
        out_shape=jax.ShapeDtypeStruct((T, D), F32),
        compiler_params=pltpu.CompilerParams(dimension_semantics=("arbitrary",)),
        name="moe_combine",
    )(dst_c, dst_c, x2d, gates_col, fnorm, ys)


def _swap_halves_cols(w):
    half = w.shape[-1] // 2
    return jnp.concatenate([w[..., half:], w[..., :half]], axis=-1)


def _pad_cols(w, width):
    return jnp.pad(w, [(0, 0)] * (w.ndim - 1) + [(0, width - w.shape[-1])])


def _prep_w_in(w):
    kpe0 = _C_CKV + MLA_KV_RANK - 0
    kpe = w[:, kpe0:kpe0 + MLA_ROPE]
    return jnp.concatenate([
        w[:, :kpe0],
        _pad_cols(kpe, LANES), _pad_cols(_swap_halves_cols(kpe), LANES),
        w[:, kpe0 + MLA_ROPE:],
    ], axis=1).astype(BF16)


def _prep_w_uq(w):
    w = w.reshape(MLA_Q_RANK, MLA_HEADS, MLA_NOPE + MLA_ROPE)
    nope, pe = w[..., :MLA_NOPE], w[..., MLA_NOPE:]
    out = jnp.concatenate([nope, _pad_cols(pe, LANES), _pad_cols(_swap_halves_cols(pe), LANES)], axis=-1)
    return out.reshape(MLA_Q_RANK, MLA_HEADS * _Q_HEAD_COLS).astype(BF16)


def _prep_w_ukv(w):
    w = w.reshape(MLA_KV_RANK, MLA_HEADS, MLA_NOPE + MLA_V)
    wukt = jnp.transpose(w[..., :MLA_NOPE], (1, 2, 0)).reshape(MLA_HEADS * MLA_NOPE, MLA_KV_RANK)
    wuv = w[..., MLA_NOPE:].reshape(MLA_KV_RANK, MLA_HEADS * MLA_V)
    return wukt.astype(BF16), wuv.astype(BF16)


def _prep_w_kpe_t(w_ext):
    return w_ext[:, _C_KPE:_C_XP].T


def _rope_tables(seq):
    pos = jnp.arange(seq, dtype=F32)
    inv_freq = 1.0 / (ROPE_THETA ** (jnp.arange(0, MLA_ROPE, 2, dtype=F32) / MLA_ROPE))
    ang = pos[:, None] * inv_freq[None, :]
    cos, sin = jnp.cos(ang), jnp.sin(ang)
    cpad = _pad_cols(jnp.concatenate([cos, cos], axis=-1), LANES)
    spad = _pad_cols(jnp.concatenate([-sin, sin], axis=-1), LANES)
    return cpad, spad


def _block_diag_pool(w):
    G, c, _ = w.shape
    eye = jnp.eye(G, dtype=w.dtype)
    return (eye[:, None, :, None] * w[:, :, None, :]).reshape(G * c, G * c).astype(BF16)


def _moe_layout(counts, n_tiles, tile):
    tiles_per = (counts + tile - 1) // tile
    ends = jnp.cumsum(tiles_per)
    starts = (ends - tiles_per) * tile
    tile_expert = jnp.sum((jnp.arange(n_tiles)[:, None] >= ends[None, :]).astype(jnp.int32), axis=1)
    tile_expert = jnp.minimum(tile_expert, N_EXPERTS - 1)
    zero_rows = jnp.minimum(starts + counts, (n_tiles - 1) * tile)
    zero_info = jnp.concatenate([zero_rows, ends[-1:]])
    return (starts.astype(jnp.int32), tile_expert.astype(jnp.int32), ends[-1:].astype(jnp.int32),
            zero_info.astype(jnp.int32))


def kernel(x, attn_norm, w_in, hgrn_lower_bounds, hgrn_out_norm, mla_q_norm, mla_w_uq, mla_kv_norm,
           mla_w_ukv, pool_w, pool_scale, w_o, ffn_norm, dense_w_gate, dense_w_up, dense_w_down,
           moe_router, moe_w_gate, moe_w_up, moe_w_down, final_norm):
    B, S, D = x.shape
    T = B * S
    depth = w_in.shape[0]
    cpad, spad = _rope_tables(S)
    p_lb = jax.nn.softmax(hgrn_lower_bounds.astype(F32), axis=0)
    lbs = jnp.cumsum(p_lb, axis=0) - p_lb[0:1]

    assert depth % 2 == 0, "the final RMSNorm is fused into the last (MoE) layer's combine kernel"
    for l in range(depth):
        wukt, wuv = _prep_w_ukv(mla_w_ukv[l])
        w_ext = _prep_w_in(w_in[l])
        hg, xp, q, kt, v = _in_proj(
            x, attn_norm[l][None], w_ext, mla_q_norm[l][None], _prep_w_uq(mla_w_uq[l]),
            mla_kv_norm[l][None], wukt, wuv, _prep_w_kpe_t(w_ext), cpad, spad)
        o_a = _hgrn(hg, lbs[l][None], hgrn_out_norm[l][None])
        o_b = _attention(q, kt, v)
        moe_layer = (l % 2 == 1)
        x, h = _mix_out(x, o_a, o_b, xp, _block_diag_pool(pool_w[l]), pool_scale[l][None],
                        w_o[l].astype(BF16), ffn_norm[l][None], moe_layer)
        j = l // 2
        if not moe_layer:
            x = _dense_ffn(x.reshape(T, D), h.reshape(T, D), dense_w_gate[j].astype(BF16),
                           dense_w_up[j].astype(BF16), dense_w_down[j].astype(BF16)).reshape(B, S, D)
        else:
            meta, gates, counts = _router(h, moe_router[j].T)
            n_tiles = (2 * T) // MOE_TILE + N_EXPERTS
            starts, tile_expert, n_used, zero_rows = _moe_layout(counts[:, 0], n_tiles, MOE_TILE)
            dst = _dest_rows(starts, meta)
            xs = _scatter_rows(zero_rows, dst, h, n_tiles * MOE_TILE)
            ys = _expert_ffn(tile_expert, n_used, xs, moe_w_gate[j].astype(BF16), moe_w_up[j].astype(BF16),
                             moe_w_down[j].astype(BF16))
            ct = min(COMBINE_TILE, T)
            dst_c = dst.transpose(1, 0, 2).reshape(SUBLANES, T // ct, ct).transpose(1, 0, 2)
            gates_col = gates.transpose(0, 2, 1).reshape(T, SUBLANES)
            last = (l == depth - 1)
            y = _combine(dst_c, x.reshape(T, D), gates_col, final_norm[None], ys, last)
            x = y.reshape(B, S, D)
    return x
```

```python
import functools
import math

import jax
import jax.numpy as jnp
import numpy as np
from jax import lax
from jax.experimental import pallas as pl
from jax.experimental.pallas import tpu as pltpu

F32 = jnp.float32
BF16 = jnp.bfloat16

HG_HEADS = 4
HG_KEY_DIM = 128
HG_VAL_DIM = 64
HG_KEY_WIDTH = HG_HEADS * HG_KEY_DIM
HG_WIDTH = HG_HEADS * HG_VAL_DIM
MIN_FORGET = 1e-20
MLA_HEADS = 4
MLA_Q_RANK = 256
MLA_KV_RANK = 128
MLA_NOPE = 128
MLA_ROPE = 64
MLA_V = 128
MLA_WIDTH = MLA_HEADS * MLA_V
ROPE_THETA = 10000.0
MASK_VALUE = -1e30
POOL_GROUPS = 4
POOL_WINDOWS = (2, 4, 8, 16)
POOL_WIDTH = 256
POOL_GROUP_DIM = POOL_WIDTH // POOL_GROUPS
N_EXPERTS = 8
EPS = 1e-6

LANES = 128
SUBLANES = 8
QK_PAD = 256
V_EXT = 256

TOKEN_TILE = 1024
IN_PROJ_CHAINS = 1
HGRN_CHUNK = 256
ATTN_BK = 512
ATTN_HEADS_PER_STEP = 1
FF_CHUNK = 512
MOE_FF_CHUNK = 512
MOE_TILE = 1024
ROUTE_TILE = 512
COMBINE_TILE = 256
POOL_HALO = 16
DMA_UNROLL = 8

_C_HG = 0
_C_CQ = 2 * HG_KEY_WIDTH + 2 * HG_WIDTH
_C_CKV = _C_CQ + MLA_Q_RANK
_C_KPE = _C_CKV + MLA_KV_RANK
_C_KPES = _C_KPE + LANES
_C_XP = _C_KPES + LANES
_C_END = _C_XP + POOL_WIDTH
_Q_HEAD_COLS = 3 * LANES


def _rms(x, g):
    return x * lax.rsqrt(jnp.mean(x * x, axis=-1, keepdims=True) + EPS) * g


def _dot(a, b):
    return jnp.dot(a, b, preferred_element_type=F32)


def _dot_nt(a, b):
    return lax.dot_general(a, b, (((1,), (1,)), ((), ())), preferred_element_type=F32)


def _dot_tn(a, b):
    return lax.dot_general(a, b, (((0,), (0,)), ((), ())), preferred_element_type=F32)


def _const_spec(shape):
    nd = len(shape)
    return pl.BlockSpec(shape, lambda *_: (0,) * nd)


def _load_tile_rows(ref, n):
    return jnp.concatenate([ref[pl.ds(s, n, stride=SUBLANES), :] for s in range(SUBLANES)], axis=1)


def _store_tile_rows(ref, val):
    n = val.shape[0]
    for s in range(SUBLANES):
        ref[pl.ds(s, n, stride=SUBLANES), :] = val[:, s * LANES:(s + 1) * LANES]


def _in_proj_kernel(x_ref, g_ref, w_ref, qn_ref, wuq_ref, kvn_ref, wukt_ref, wuv_ref, wkpet_ref,
                    cpad_ref, spad_ref, cpadt_ref, spadt_ref,
                    hg_ref, xp_ref, q_ref, kt_ref, v_ref):
    scale = (MLA_NOPE + MLA_ROPE) ** -0.5 * math.log2(math.e)
    tm = x_ref.shape[1]
    rows = tm // IN_PROJ_CHAINS
    ones_col = (lax.broadcasted_iota(jnp.int32, (rows, LANES), 1) == 0).astype(BF16)

    for r0 in range(0, tm, rows):
        rs = slice(r0, r0 + rows)
        h = _rms(x_ref[0, rs, :], g_ref[...]).astype(BF16)
        hg_ref[0, rs, :] = _dot(h, w_ref[:, _C_HG:_C_CQ])
        xp_ref[0, rs, :] = _dot(h, w_ref[:, _C_XP:_C_END])
        cpad = cpad_ref[rs, :]
        spad = spad_ref[rs, :]

        cq = _dot(h, w_ref[:, _C_CQ:_C_CKV])
        cqn = _rms(cq, qn_ref[...]).astype(BF16)
        for hd in range(MLA_HEADS):
            qh = _dot(cqn, wuq_ref[:, hd * _Q_HEAD_COLS:(hd + 1) * _Q_HEAD_COLS])
            q_ref[0, hd, rs, 0:LANES] = (qh[:, 0:LANES] * scale).astype(BF16)
            pe = qh[:, LANES:2 * LANES] * cpad + qh[:, 2 * LANES:3 * LANES] * spad
            q_ref[0, hd, rs, LANES:QK_PAD] = (pe * scale).astype(BF16)

        ckv = _dot(h, w_ref[:, _C_CKV:_C_KPE])
        ckvn = _rms(ckv, kvn_ref[...]).astype(BF16)
        kpet = (_dot_nt(wkpet_ref[0:LANES, :], h) * cpadt_ref[:, rs]
                + _dot_nt(wkpet_ref[LANES:2 * LANES, :], h) * spadt_ref[:, rs]).astype(BF16)
        v_all = _dot(ckvn, wuv_ref[...]).astype(BF16)
        for hd in range(MLA_HEADS):
            kt_ref[0, hd, 0:LANES, rs] = _dot_nt(wukt_ref[hd * LANES:(hd + 1) * LANES, :], ckvn).astype(BF16)
            kt_ref[0, hd, LANES:QK_PAD, rs] = kpet
            v_ref[0, hd, rs, 0:MLA_V] = v_all[:, hd * MLA_V:(hd + 1) * MLA_V]
            v_ref[0, hd, rs, MLA_V:V_EXT] = ones_col


def _in_proj(x, g, w_ext, qn, wuq_ext, kvn, wukt, wuv, wkpet, cpad, spad):
    B, S, D = x.shape
    tm = min(TOKEN_TILE, S)
    grid = (B, S // tm)
    n_hg = _C_CQ
    out_shape = (
        jax.ShapeDtypeStruct((B, S, n_hg), F32),
        jax.ShapeDtypeStruct((B, S, POOL_WIDTH), F32),
        jax.ShapeDtypeStruct((B, MLA_HEADS, S, QK_PAD), BF16),
        jax.ShapeDtypeStruct((B, MLA_HEADS, QK_PAD, S), BF16),
        jax.ShapeDtypeStruct((B, MLA_HEADS, S, V_EXT), BF16),
    )
    return pl.pallas_call(
        _in_proj_kernel,
        grid=grid,
        in_specs=[
            pl.BlockSpec((1, tm, D), lambda b, i: (b, i, 0)),
            _const_spec((1, D)),
            _const_spec(w_ext.shape),
            _const_spec((1, MLA_Q_RANK)),
            _const_spec(wuq_ext.shape),
            _const_spec((1, MLA_KV_RANK)),
            _const_spec(wukt.shape),
            _const_spec(wuv.shape),
            _const_spec(wkpet.shape),
            pl.BlockSpec((tm, LANES), lambda b, i: (i, 0)),
            pl.BlockSpec((tm, LANES), lambda b, i: (i, 0)),
            pl.BlockSpec((LANES, tm), lambda b, i: (0, i)),
            pl.BlockSpec((LANES, tm), lambda b, i: (0, i)),
        ],
        out_specs=(
            pl.BlockSpec((1, tm, n_hg), lambda b, i: (b, i, 0)),
            pl.BlockSpec((1, tm, POOL_WIDTH), lambda b, i: (b, i, 0)),
            pl.BlockSpec((1, MLA_HEADS, tm, QK_PAD), lambda b, i: (b, 0, i, 0)),
            pl.BlockSpec((1, MLA_HEADS, QK_PAD, tm), lambda b, i: (b, 0, 0, i)),
            pl.BlockSpec((1, MLA_HEADS, tm, V_EXT), lambda b, i: (b, 0, i, 0)),
        ),
        out_shape=out_shape,
        compiler_params=pltpu.CompilerParams(dimension_semantics=("arbitrary", "arbitrary")),
        name="in_proj",
    )(x, g, w_ext, qn, wuq_ext, kvn, wukt, wuv, wkpet, cpad, spad, cpad.T, spad.T)


def _split2(x):
    hi = x.astype(BF16)
    return hi, (x - hi.astype(F32)).astype(BF16)


def _hgrn_kernel(q_ref, f_ref, i_ref, g_ref, lb_ref, on_ref, tril_ref, grp_ref, o_ref, st_ref):
    C = q_ref.shape[1]
    KW = HG_KEY_WIDTH

    @pl.when(pl.program_id(1) == 0)
    def _():
        st_ref[...] = jnp.zeros_like(st_ref)

    lb = lb_ref[...]
    z = f_ref[0]
    sig = jax.nn.sigmoid(z)
    forget = lb + (1.0 - lb) * sig
    lg = jnp.log2(jnp.maximum(forget, MIN_FORGET))
    kk = (1.0 - lb) * (1.0 - sig)
    qq = jax.nn.silu(q_ref[0])
    vv = i_ref[0]

    tril = tril_ref[...]
    b = sum(_dot(tril, part) for part in _split2(lg))

    sides, masks = [], []
    row = lax.broadcasted_iota(jnp.int32, (C, C), 0)
    col = lax.broadcasted_iota(jnp.int32, (C, C), 1)
    sub = lax.broadcasted_iota(jnp.int32, (C, KW), 0)
    half = C // 2
    while half >= 4:
        blk = 2 * half
        b3 = b.reshape(C // blk, blk, KW)
        ref_row = jnp.broadcast_to(b3[:, half - 1:half, :], b3.shape).reshape(C, KW)
        e = jnp.exp2(-jnp.abs(b - ref_row))
        x = (jnp.where((sub & half) != 0, qq, kk) * e).astype(BF16)
        sides.append((x, x))
        shift = int(math.log2(blk))
        masks.append(((row >> shift) == (col >> shift)) & ((row & half) != 0) & ((col & half) == 0))
        half //= 2
    b3 = b.reshape(C // 8, 8, KW)
    mid_lo = 0.5 * (b3[:, 0:1, :] + b3[:, 3:4, :])
    mid_hi = 0.5 * (b3[:, 4:5, :] + b3[:, 7:8, :])
    sub8 = lax.broadcasted_iota(jnp.int32, b3.shape, 1)
    mid = jnp.where(sub8 < 4, mid_lo, mid_hi).reshape(C, KW)
    sides.append(((qq * jnp.exp2(b - mid)).astype(BF16), (kk * jnp.exp2(mid - b)).astype(BF16)))
    masks.append(((row >> 2) == (col >> 2)) & (col <= row))

    lane_v = lax.broadcasted_iota(jnp.int32, (C, HG_WIDTH), 1)
    o = _dot_nt((qq * jnp.exp2(b)).astype(BF16), st_ref[...].astype(BF16))
    for hd in range(HG_HEADS):
        ks = slice(hd * HG_KEY_DIM, (hd + 1) * HG_KEY_DIM)
        a = jnp.zeros((C, C), F32)
        for (qt, kt), m in zip(sides, masks):
            a = jnp.where(m, _dot_nt(qt[:, ks], kt[:, ks]), a)
        v_h = jnp.where((lane_v >> 6) == hd, vv, 0.0).astype(BF16)
        o = o + _dot(a.astype(BF16), v_h)

    b_last = b[C - 1:C, :]
    khat = (kk * jnp.exp2(b_last - b)).astype(BF16)
    st_row = lax.broadcasted_iota(jnp.int32, (HG_WIDTH, KW), 0)
    st_col = lax.broadcasted_iota(jnp.int32, (HG_WIDTH, KW), 1)
    new_st = st_ref[...] * jnp.exp2(b_last) + _dot_tn(vv.astype(BF16), khat)
    st_ref[...] = jnp.where((st_row >> 6) == (st_col >> 7), new_st, 0.0)

    grp = grp_ref[...]
    ssq = sum(_dot(part, grp) for part in _split2(o * o))
    on = o * lax.rsqrt(ssq * (1.0 / HG_VAL_DIM) + EPS) * on_ref[...]
    o_ref[0] = (on * jax.nn.silu(g_ref[0])).astype(o_ref.dtype)


def _hgrn(hg, lb, out_norm):
    B, S, _ = hg.shape
    C = min(HGRN_CHUNK, S)
    kb = HG_KEY_WIDTH // HG_KEY_WIDTH
    tril = jnp.asarray(np.tril(np.ones((C, C), np.float32)), BF16)
    lane_group = np.arange(HG_WIDTH) // HG_VAL_DIM
    grp = jnp.asarray((lane_group[:, None] == lane_group[None, :]).astype(np.float32), BF16)
    return pl.pallas_call(
        _hgrn_kernel,
        grid=(B, S // C),
        in_specs=[
            pl.BlockSpec((1, C, HG_KEY_WIDTH), lambda b, c: (b, c, 0)),
            pl.BlockSpec((1, C, HG_KEY_WIDTH), lambda b, c: (b, c, kb)),
            pl.BlockSpec((1, C, HG_WIDTH), lambda b, c: (b, c, 2 * HG_KEY_WIDTH // HG_WIDTH)),
            pl.BlockSpec((1, C, HG_WIDTH), lambda b, c: (b, c, 2 * HG_KEY_WIDTH // HG_WIDTH + 1)),
            _const_spec((1, HG_KEY_WIDTH)),
            _const_spec((1, HG_WIDTH)),
            _const_spec(tril.shape),
            _const_spec(grp.shape),
        ],
        out_specs=pl.BlockSpec((1, C, HG_WIDTH), lambda b, c: (b, c, 0)),
        out_shape=jax.ShapeDtypeStruct((B, S, HG_WIDTH), BF16),
        scratch_shapes=[pltpu.VMEM((HG_WIDTH, HG_KEY_WIDTH), F32)],
        compiler_params=pltpu.CompilerParams(dimension_semantics=("arbitrary", "arbitrary")),
        name="hgrn2",
    )(hg, hg, hg, hg, lb, out_norm, tril, grp)


def _attn_kernel(q_ref, kt_ref, v_ref, o_ref, s0_ref, s1_ref, acc_ref, *, bk):
    bq = q_ref.shape[2]
    nh = q_ref.shape[1]
    i = pl.program_id(2)
    qs = [q_ref[0, hd] for hd in range(nh)]

    def causal(s, key0):
        qry = lax.broadcasted_iota(jnp.int32, (bq, bk), 0)
        key = key0 + lax.broadcasted_iota(jnp.int32, (bq, bk), 1)
        return jnp.where(key <= qry, s, MASK_VALUE)

    def scores(blk, s_ref):
        start = pl.multiple_of(blk * bk, bk)
        for hd in range(nh):
            s_ref[hd] = _dot(qs[hd], kt_ref[0, hd, :, pl.ds(start, bk)])

    def softmax_pv(blk, s_ref, ms, key0=None):
        start = pl.multiple_of(blk * bk, bk)
        out = []
        for hd in range(nh):
            s = s_ref[hd]
            if key0 is not None:
                s = causal(s, key0)
            m_new = jnp.maximum(ms[hd], jnp.max(s, axis=1, keepdims=True))
            p = jnp.exp2(s - m_new).astype(BF16)
            acc_ref[hd] = jnp.exp2(ms[hd] - m_new) * acc_ref[hd] + _dot(p, v_ref[0, hd, pl.ds(start, bk), :])
            out.append(m_new)
        return tuple(out)

    acc_ref[...] = jnp.zeros_like(acc_ref)
    scores(0, s0_ref)

    def pair(t, ms):
        scores(2 * t + 1, s1_ref)
        ms = softmax_pv(2 * t, s0_ref, ms)
        scores(2 * t + 2, s0_ref)
        return softmax_pv(2 * t + 1, s1_ref, ms)

    def pairs(t0, n, c):
        for k in range(n):
            c = pair(t0 + k, c)
        return c

    ms = tuple(jnp.full((bq, 1), -jnp.inf, F32) for _ in range(nh))
    ms = lax.fori_loop(0, i // 4, lambda t, c: pairs(4 * t, 4, c), ms)
    done = (i // 4) * 4
    ms = lax.cond((i & 2) != 0, lambda c: pairs(done, 2, c), lambda c: c, ms)
    done = done + (i & 2)
    ms = lax.cond((i & 1) != 0, lambda c: pairs(done, 1, c), lambda c: c, ms)
    start1 = pl.multiple_of((2 * i + 1) * bk, bk)
    lower = [_dot(qs[hd][bk:, :], kt_ref[0, hd, :, pl.ds(start1, bk)]) for hd in range(nh)]
    ms = softmax_pv(2 * i, s0_ref, ms, 0)
    qry = lax.broadcasted_iota(jnp.int32, (bk, bk), 0)
    key = lax.broadcasted_iota(jnp.int32, (bk, bk), 1)
    for hd in range(nh):
        s = jnp.where(key <= qry, lower[hd], MASK_VALUE)
        m_old = ms[hd][bk:, :]
        m_new = jnp.maximum(m_old, jnp.max(s, axis=1, keepdims=True))
        p = jnp.exp2(s - m_new).astype(BF16)
        acc_ref[hd, bk:, :] = (jnp.exp2(m_old - m_new) * acc_ref[hd, bk:, :]
                               + _dot(p, v_ref[0, hd, pl.ds(start1, bk), :]))
    for hd in range(nh):
        acc = acc_ref[hd]
        o_ref[0, :, hd * MLA_V:(hd + 1) * MLA_V] = (acc[:, 0:MLA_V] / acc[:, MLA_V:MLA_V + 1]).astype(o_ref.dtype)


def _attention(q, kt, v):
    B, H, S, _ = q.shape
    bk = min(ATTN_BK, S // 2)
    bq = 2 * bk
    nh = ATTN_HEADS_PER_STEP
    resident = dict(pipeline_mode=pl.Buffered(1)) if nh > 1 else {}
    return pl.pallas_call(
        functools.partial(_attn_kernel, bk=bk),
        grid=(B, H // nh, S // bq),
        in_specs=[
            pl.BlockSpec((1, nh, bq, QK_PAD), lambda b, h, i: (b, h, i, 0)),
            pl.BlockSpec((1, nh, QK_PAD, S), lambda b, h, i: (b, h, 0, 0), **resident),
            pl.BlockSpec((1, nh, S, V_EXT), lambda b, h, i: (b, h, 0, 0), **resident),
        ],
        out_specs=pl.BlockSpec((1, bq, nh * MLA_V), lambda b, h, i: (b, i, h)),
        out_shape=jax.ShapeDtypeStruct((B, S, H * MLA_V), BF16),
        scratch_shapes=[pltpu.VMEM((nh, bq, bk), F32), pltpu.VMEM((nh, bq, bk), F32),
                        pltpu.VMEM((nh, bq, V_EXT), F32)],
        compiler_params=pltpu.CompilerParams(dimension_semantics=("arbitrary", "arbitrary", "arbitrary")),
        name="mla_attention",
    )(q, kt, v)


def _mix_out_kernel(x_ref, oa_ref, ob_ref, xp_ref, wpool_ref, pscale_ref, wo_ref, fn_ref,
                    xo_ref, h_ref, halo_ref, *, tile_rows):
    tm = xp_ref.shape[1]
    i = pl.program_id(1)

    @pl.when(i == 0)
    def _():
        halo_ref[...] = jnp.zeros_like(halo_ref)

    xp = xp_ref[0]
    xx = jnp.concatenate([halo_ref[...], xp], axis=0)
    halo_ref[...] = xp[tm - POOL_HALO:, :]

    w2 = xx[1:, :] + xx[:-1, :]
    w4 = w2[2:, :] + w2[:-2, :]
    w8 = w4[4:, :] + w4[:-4, :]
    w16 = w8[8:, :] + w8[:-8, :]
    sums = (w2[POOL_HALO - 1:, :], w4[POOL_HALO - 3:, :], w8[POOL_HALO - 7:, :], w16[POOL_HALO - 15:, :])
    t = i * tm + lax.broadcasted_iota(jnp.int32, (tm, POOL_WIDTH), 0)
    lane = lax.broadcasted_iota(jnp.int32, (tm, POOL_WIDTH), 1)
    pooled = jnp.zeros((tm, POOL_WIDTH), F32)
    for gi, w in enumerate(POOL_WINDOWS):
        cnt = jnp.minimum(t + 1, w).astype(F32)
        pooled = jnp.where((lane >> 6) == gi, sums[gi] / cnt, pooled)
    pooled = pooled - xp
    oc = _dot(pooled.astype(BF16), wpool_ref[...]) * pscale_ref[...]

    y = _dot(oa_ref[0], wo_ref[0:HG_WIDTH, :])
    y = y + _dot(ob_ref[0], wo_ref[HG_WIDTH:HG_WIDTH + MLA_WIDTH, :])
    y = y + _dot(oc.astype(BF16), wo_ref[HG_WIDTH + MLA_WIDTH:, :])
    xn = x_ref[0] + y
    xo_ref[0] = xn
    h = _rms(xn, fn_ref[...])
    if tile_rows:
        _store_tile_rows(h_ref, h)
    else:
        h_ref[0] = h.astype(h_ref.dtype)


def _mix_out(x, oa, ob, xp, wpool_bd, pscale, wo, fnorm, tile_rows):
    B, S, D = x.shape
    tm = min(TOKEN_TILE, S)
    nt = S // tm
    tok = lambda w: pl.BlockSpec((1, tm, w), lambda b, i: (b, i, 0))
    if tile_rows:
        assert D == SUBLANES * LANES
        h_spec = pl.BlockSpec((tm * SUBLANES, LANES), lambda b, i: (b * nt + i, 0))
        h_shape = jax.ShapeDtypeStruct((B * S * SUBLANES, LANES), F32)
    else:
        h_spec, h_shape = tok(D), jax.ShapeDtypeStruct((B, S, D), BF16)
    return pl.pallas_call(
        functools.partial(_mix_out_kernel, tile_rows=tile_rows),
        grid=(B, nt),
        in_specs=[tok(D), tok(HG_WIDTH), tok(MLA_WIDTH), tok(POOL_WIDTH),
                  _const_spec(wpool_bd.shape), _const_spec((1, POOL_WIDTH)), _const_spec(wo.shape),
                  _const_spec((1, D))],
        out_specs=(tok(D), h_spec),
        out_shape=(jax.ShapeDtypeStruct((B, S, D), F32), h_shape),
        scratch_shapes=[pltpu.VMEM((POOL_HALO, POOL_WIDTH), F32)],
        compiler_params=pltpu.CompilerParams(dimension_semantics=("arbitrary", "arbitrary")),
        name="mix_out",
    )(x, oa, ob, xp, wpool_bd, pscale, wo, fnorm)


def _dense_ffn_kernel(x_ref, h_ref, wg_ref, wu_ref, wd_ref, o_ref):
    h = h_ref[...]
    acc = x_ref[...]
    dff = wg_ref.shape[1]
    for c0 in range(0, dff, FF_CHUNK):
        g = _dot(h, wg_ref[:, c0:c0 + FF_CHUNK])
        u = _dot(h, wu_ref[:, c0:c0 + FF_CHUNK])
        acc = acc + _dot((jax.nn.silu(g) * u).astype(BF16), wd_ref[c0:c0 + FF_CHUNK, :])
    o_ref[...] = acc


def _dense_ffn(x2d, h2d, wg, wu, wd):
    T, D = x2d.shape
    tm = min(TOKEN_TILE, T)
    tok = pl.BlockSpec((tm, D), lambda i: (i, 0))
    return pl.pallas_call(
        _dense_ffn_kernel,
        grid=(T // tm,),
        in_specs=[tok, tok, _const_spec(wg.shape), _const_spec(wu.shape), _const_spec(wd.shape)],
        out_specs=tok,
        out_shape=jax.ShapeDtypeStruct((T, D), F32),
        compiler_params=pltpu.CompilerParams(dimension_semantics=("arbitrary",)),
        name="dense_ffn",
    )(x2d, h2d, wg, wu, wd)


def _router_kernel(h_ref, rt_ref, meta_ref, gate_ref, cnt_ref, run_ref):
    tm = h_ref.shape[0] // SUBLANES
    E = N_EXPERTS

    @pl.when(pl.program_id(0) == 0)
    def _():
        run_ref[...] = jnp.zeros_like(run_ref)

    logits = lax.dot_general(rt_ref[...], _load_tile_rows(h_ref, tm), (((1,), (1,)), ((), ())),
                             precision=lax.Precision.HIGHEST, preferred_element_type=F32)
    eid = lax.broadcasted_iota(jnp.int32, (E, tm), 0)
    m1 = jnp.max(logits, axis=0, keepdims=True)
    i1 = jnp.min(jnp.where(logits == m1, eid, E), axis=0, keepdims=True)
    rest = jnp.where(eid == i1, -jnp.inf, logits)
    m2 = jnp.max(rest, axis=0, keepdims=True)
    i2 = jnp.min(jnp.where(rest == m2, eid, E), axis=0, keepdims=True)
    e2 = jnp.exp(m2 - m1)
    g1 = 1.0 / (1.0 + e2)
    g2 = e2 / (1.0 + e2)

    sel = ((eid == i1) | (eid == i2))
    r = lax.broadcasted_iota(jnp.int32, (tm, tm), 0)
    c = lax.broadcasted_iota(jnp.int32, (tm, tm), 1)
    before = (r < c).astype(BF16)
    excl = _dot(sel.astype(BF16), before) + run_ref[:, 0:1]
    rank1 = jnp.sum(jnp.where(eid == i1, excl, 0.0), axis=0, keepdims=True).astype(jnp.int32)
    rank2 = jnp.sum(jnp.where(eid == i2, excl, 0.0), axis=0, keepdims=True).astype(jnp.int32)
    run_ref[...] = run_ref[...] + jnp.sum(sel.astype(F32), axis=1, keepdims=True)
    cnt_ref[...] = run_ref[...].astype(jnp.int32)

    zi = jnp.zeros((1, tm), jnp.int32)
    meta_ref[0] = jnp.concatenate([i1, i2, rank1, rank2, zi, zi, zi, zi], axis=0)
    zf = jnp.zeros((1, tm), F32)
    gate_ref[0] = jnp.concatenate([g1, g2, zf, zf, zf, zf, zf, zf], axis=0)


def _router(h_rows, router_t):
    T = h_rows.shape[0] // SUBLANES
    tm = min(ROUTE_TILE, T)
    nt = T // tm
    return pl.pallas_call(
        _router_kernel,
        grid=(nt,),
        in_specs=[pl.BlockSpec((tm * SUBLANES, LANES), lambda i: (i, 0)), _const_spec(router_t.shape)],
        out_specs=(pl.BlockSpec((1, SUBLANES, tm), lambda i: (i, 0, 0)),
                   pl.BlockSpec((1, SUBLANES, tm), lambda i: (i, 0, 0)),
                   _const_spec((N_EXPERTS, LANES))),
        out_shape=(jax.ShapeDtypeStruct((nt, SUBLANES, tm), jnp.int32),
                   jax.ShapeDtypeStruct((nt, SUBLANES, tm), F32),
                   jax.ShapeDtypeStruct((N_EXPERTS, LANES), jnp.int32)),
        scratch_shapes=[pltpu.VMEM((N_EXPERTS, LANES), F32)],
        compiler_params=pltpu.CompilerParams(dimension_semantics=("arbitrary",)),
        name="moe_router",
    )(h_rows, router_t)


def _dest_kernel(start_ref, meta_ref, dst_ref):
    meta = meta_ref[0]
    rows = []
    for k in range(2):
        e = meta[k:k + 1, :]
        base = jnp.zeros_like(e)
        for ex in range(N_EXPERTS):
            base = jnp.where(e == ex, start_ref[ex], base)
        rows.append(base + meta[2 + k:3 + k, :])
    dst_ref[0] = jnp.concatenate(rows + [jnp.zeros_like(rows[0])] * (SUBLANES - 2), axis=0)


def _dest_rows(starts, meta):
    nt, _, tm = meta.shape
    spec = pl.BlockSpec((1, SUBLANES, tm), lambda i, s: (i, 0, 0))
    return pl.pallas_call(
        _dest_kernel,
        grid_spec=pltpu.PrefetchScalarGridSpec(num_scalar_prefetch=1, grid=(nt,), in_specs=[spec], out_specs=spec),
        out_shape=jax.ShapeDtypeStruct(meta.shape, jnp.int32),
        compiler_params=pltpu.CompilerParams(dimension_semantics=("arbitrary",)),
        name="moe_dest",
    )(starts, meta)


def _scatter_kernel(zrow_ref, dst_ref, h_ref, xs_hbm, zero_ref, sem, zsem):
    tm = dst_ref.shape[2]

    @pl.when(pl.program_id(0) == 0)
    def _():
        zero_ref[...] = jnp.zeros_like(zero_ref)
        tile_rows = zero_ref.shape[0]

        def fill(row):
            start = pl.multiple_of(row * SUBLANES, SUBLANES)
            return pltpu.make_async_copy(zero_ref, xs_hbm.at[pl.ds(start, tile_rows), :], zsem)

        fills = [fill(zrow_ref[e]) for e in range(N_EXPERTS)]
        for f in fills:
            f.start()
        for f in fills:
            f.wait()
        n_tiles = xs_hbm.shape[0] // tile_rows
        for j in range(N_EXPERTS):
            @pl.when(n_tiles - 1 - j >= zrow_ref[N_EXPERTS])
            def _():
                f = fill((n_tiles - 1 - j) * (tile_rows // SUBLANES))
                f.start()
                f.wait()

    def row_copy(src_row, dst_row):
        src = pl.multiple_of(src_row * SUBLANES, SUBLANES)
        dst = pl.multiple_of(dst_row * SUBLANES, SUBLANES)
        return pltpu.make_async_copy(h_ref.at[pl.ds(src, SUBLANES), :], xs_hbm.at[pl.ds(dst, SUBLANES), :], sem)

    def issue(t, _):
        for k in range(2):
            row_copy(t, dst_ref[0, k, t]).start(priority=k)
        return 0

    lax.fori_loop(0, tm, issue, 0, unroll=DMA_UNROLL)

    def drain(t, _):
        row_copy(0, 0).wait()
        row_copy(0, 0).wait()
        return 0

    lax.fori_loop(0, tm, drain, 0, unroll=DMA_UNROLL)


def _scatter_rows(zero_rows, dst, h_rows, n_rows):
    nt, _, tm = dst.shape
    grid_spec = pltpu.PrefetchScalarGridSpec(
        num_scalar_prefetch=1,
        grid=(nt,),
        in_specs=[pl.BlockSpec((1, SUBLANES, tm), lambda i, z: (i, 0, 0), memory_space=pltpu.SMEM),
                  pl.BlockSpec((tm * SUBLANES, LANES), lambda i, z: (i, 0))],
        out_specs=pl.BlockSpec(memory_space=pl.ANY),
        scratch_shapes=[pltpu.VMEM((MOE_TILE * SUBLANES, LANES), F32), pltpu.SemaphoreType.DMA(()),
                        pltpu.SemaphoreType.DMA(())],
    )
    return pl.pallas_call(
        _scatter_kernel,
        grid_spec=grid_spec,
        out_shape=jax.ShapeDtypeStruct((n_rows * SUBLANES, LANES), F32),
        compiler_params=pltpu.CompilerParams(dimension_semantics=("arbitrary",), has_side_effects=True),
        name="moe_scatter",
    )(zero_rows, dst, h_rows)


def _expert_kernel(te_ref, nu_ref, x_ref, wg_ref, wu_ref, wd_ref, y_ref, *, fc):
    m = pl.program_id(0)
    tr = x_ref.shape[0] // SUBLANES
    dff = wg_ref.shape[2]

    @pl.when(m < nu_ref[0])
    def _():
        xb = _load_tile_rows(x_ref, tr).astype(BF16)
        acc = jnp.zeros((tr, wd_ref.shape[2]), F32)
        for c0 in range(0, dff, fc):
            g = _dot(xb, wg_ref[0, :, c0:c0 + fc])
            u = _dot(xb, wu_ref[0, :, c0:c0 + fc])
            acc = acc + _dot((jax.nn.silu(g) * u).astype(BF16), wd_ref[0, c0:c0 + fc, :])
        _store_tile_rows(y_ref, acc)

    @pl.when(m >= nu_ref[0])
    def _():
        y_ref[...] = jnp.zeros_like(y_ref)


def _expert_ffn(tile_expert, n_used, xs, wg, wu, wd):
    E, D, F = wg.shape
    R = xs.shape[0] // SUBLANES
    tr = MOE_TILE
    n_tiles = R // tr
    fc = min(MOE_FF_CHUNK, F)

    def row_map(m, te, nu):
        return (jnp.minimum(m, nu[0] - 1), 0)

    def w_map(m, te, nu):
        return (te[jnp.minimum(m, nu[0] - 1)], 0, 0)

    resident = dict(pipeline_mode=pl.Buffered(1))
    grid_spec = pltpu.PrefetchScalarGridSpec(
        num_scalar_prefetch=2,
        grid=(n_tiles,),
        in_specs=[
            pl.BlockSpec((tr * SUBLANES, LANES), row_map),
            pl.BlockSpec((1, D, F), w_map, **resident),
            pl.BlockSpec((1, D, F), w_map, **resident),
            pl.BlockSpec((1, F, D), w_map, **resident),
        ],
        out_specs=pl.BlockSpec((tr * SUBLANES, LANES), lambda m, te, nu: (m, 0)),
    )
    return pl.pallas_call(
        functools.partial(_expert_kernel, fc=fc),
        grid_spec=grid_spec,
        out_shape=jax.ShapeDtypeStruct(xs.shape, F32),
        compiler_params=pltpu.CompilerParams(dimension_semantics=("arbitrary",)),
        name="moe_experts",
    )(tile_expert, n_used, xs, wg, wu, wd)


def _combine_kernel(dcur_ref, dnxt_ref, x_ref, gate_ref, fn_ref, ys_hbm, o_ref, buf_ref, sems, *, final_norm):
    tm = x_ref.shape[0]
    i = pl.program_id(0)
    slot = i % 2

    def row_copy(src_row, s, k, t):
        src = pl.multiple_of(src_row * SUBLANES, SUBLANES)
        dst = pl.multiple_of(t * SUBLANES, SUBLANES)
        return pltpu.make_async_copy(ys_hbm.at[pl.ds(src, SUBLANES), :],
                                     buf_ref.at[s, k, pl.ds(dst, SUBLANES), :], sems.at[s])

    def issue_tile(d_ref, s):
        def issue(t, _):
            for k in range(2):
                row_copy(d_ref[0, k, t], s, k, t).start(priority=k)
            return 0

        lax.fori_loop(0, tm, issue, 0, unroll=DMA_UNROLL)

    @pl.when(i == 0)
    def _():
        issue_tile(dcur_ref, 0)

    @pl.when(i + 1 < pl.num_programs(0))
    def _():
        issue_tile(dnxt_ref, 1 - slot)

    def drain(t, _):
        row_copy(0, slot, 0, 0).wait()
        row_copy(0, slot, 1, 0).wait()
        return 0

    lax.fori_loop(0, tm, drain, 0, unroll=DMA_UNROLL)
    g = gate_ref[...]
    y0 = _load_tile_rows(buf_ref.at[slot, 0], tm)
    y1 = _load_tile_rows(buf_ref.at[slot, 1], tm)
    xn = x_ref[...] + g[:, 0:1] * y0 + g[:, 1:2] * y1
    o_ref[...] = _rms(xn, fn_ref[...]) if final_norm else xn


def _combine(dst_c, x2d, gates_col, fnorm, ys, final_norm):
    T, D = x2d.shape
    nt, _, tm = dst_c.shape
    return pl.pallas_call(
        functools.partial(_combine_kernel, final_norm=final_norm),
        grid=(nt,),
        in_specs=[pl.BlockSpec((1, SUBLANES, tm), lambda i: (i, 0, 0), memory_space=pltpu.SMEM),
                  pl.BlockSpec((1, SUBLANES, tm), lambda i: (jnp.minimum(i + 1, nt - 1), 0, 0),
                               memory_space=pltpu.SMEM),
                  pl.BlockSpec((tm, D), lambda i: (i, 0)),
                  pl.BlockSpec((tm, SUBLANES), lambda i: (i, 0)),
                  pl.BlockSpec((1, D), lambda i: (0, 0)),
                  pl.BlockSpec(memory_space=pl.ANY)],
        out_specs=pl.BlockSpec((tm, D), lambda i: (i, 0)),
        scratch_shapes=[pltpu.VMEM((2, 2, tm * SUBLANES, LANES), F32), pltpu.SemaphoreType.DMA((2,))],
        out_shape=jax.ShapeDtypeStruct((T, D), F32),
        compiler_params=pltpu.CompilerParams(dimension_semantics=("arbitrary",)),
        name="moe_combine",
    )(dst_c, dst_c, x2d, gates_col, fnorm, ys)


def _swap_halves_cols(w):
    half = w.shape[-1] // 2
    return jnp.concatenate([w[..., half:], w[..., :half]], axis=-1)


def _pad_cols(w, width):
    return jnp.pad(w, [(0, 0)] * (w.ndim - 1) + [(0, width - w.shape[-1])])


def _prep_w_in(w):
    kpe0 = _C_CKV + MLA_KV_RANK - 0
    kpe = w[:, kpe0:kpe0 + MLA_ROPE]
    return jnp.concatenate([
        w[:, :kpe0],
        _pad_cols(kpe, LANES), _pad_cols(_swap_halves_cols(kpe), LANES),
        w[:, kpe0 + MLA_ROPE:],
    ], axis=1).astype(BF16)


def _prep_w_uq(w):
    w = w.reshape(MLA_Q_RANK, MLA_HEADS, MLA_NOPE + MLA_ROPE)
    nope, pe = w[..., :MLA_NOPE], w[..., MLA_NOPE:]
    out = jnp.concatenate([nope, _pad_cols(pe, LANES), _pad_cols(_swap_halves_cols(pe), LANES)], axis=-1)
    return out.reshape(MLA_Q_RANK, MLA_HEADS * _Q_HEAD_COLS).astype(BF16)


def _prep_w_ukv(w):
    w = w.reshape(MLA_KV_RANK, MLA_HEADS, MLA_NOPE + MLA_V)
    wukt = jnp.transpose(w[..., :MLA_NOPE], (1, 2, 0)).reshape(MLA_HEADS * MLA_NOPE, MLA_KV_RANK)
    wuv = w[..., MLA_NOPE:].reshape(MLA_KV_RANK, MLA_HEADS * MLA_V)
    return wukt.astype(BF16), wuv.astype(BF16)


def _prep_w_kpe_t(w_ext):
    return w_ext[:, _C_KPE:_C_XP].T


def _rope_tables(seq):
    pos = jnp.arange(seq, dtype=F32)
    inv_freq = 1.0 / (ROPE_THETA ** (jnp.arange(0, MLA_ROPE, 2, dtype=F32) / MLA_ROPE))
    ang = pos[:, None] * inv_freq[None, :]
    cos, sin = jnp.cos(ang), jnp.sin(ang)
    cpad = _pad_cols(jnp.concatenate([cos, cos], axis=-1), LANES)
    spad = _pad_cols(jnp.concatenate([-sin, sin], axis=-1), LANES)
    return cpad, spad


def _block_diag_pool(w):
    G, c, _ = w.shape
    eye = jnp.eye(G, dtype=w.dtype)
    return (eye[:, None, :, None] * w[:, :, None, :]).reshape(G * c, G * c).astype(BF16)


def _moe_layout(counts, n_tiles, tile):
    tiles_per = (counts + tile - 1) // tile
    ends = jnp.cumsum(tiles_per)
    starts = (ends - tiles_per) * tile
    tile_expert = jnp.sum((jnp.arange(n_tiles)[:, None] >= ends[None, :]).astype(jnp.int32), axis=1)
    tile_expert = jnp.minimum(tile_expert, N_EXPERTS - 1)
    zero_rows = jnp.minimum(starts + counts, (n_tiles - 1) * tile)
    zero_info = jnp.concatenate([zero_rows, ends[-1:]])
    return (starts.astype(jnp.int32), tile_expert.astype(jnp.int32), ends[-1:].astype(jnp.int32),
            zero_info.astype(jnp.int32))


def kernel(x, attn_norm, w_in, hgrn_lower_bounds, hgrn_out_norm, mla_q_norm, mla_w_uq, mla_kv_norm,
           mla_w_ukv, pool_w, pool_scale, w_o, ffn_norm, dense_w_gate, dense_w_up, dense_w_down,
           moe_router, moe_w_gate, moe_w_up, moe_w_down, final_norm):
    B, S, D = x.shape
    T = B * S
    depth = w_in.shape[0]
    cpad, spad = _rope_tables(S)
    p_lb = jax.nn.softmax(hgrn_lower_bounds.astype(F32), axis=0)
    lbs = jnp.cumsum(p_lb, axis=0) - p_lb[0:1]

    assert depth % 2 == 0, "the final RMSNorm is fused into the last (MoE) layer's combine kernel"
    for l in range(depth):
        wukt, wuv = _prep_w_ukv(mla_w_ukv[l])
        w_ext = _prep_w_in(w_in[l])
        hg, xp, q, kt, v = _in_proj(
            x, attn_norm[l][None], w_ext, mla_q_norm[l][None], _prep_w_uq(mla_w_uq[l]),
            mla_kv_norm[l][None], wukt, wuv, _prep_w_kpe_t(w_ext), cpad, spad)
        o_a = _hgrn(hg, lbs[l][None], hgrn_out_norm[l][None])
        o_b = _attention(q, kt, v)
        moe_layer = (l % 2 == 1)
        x, h = _mix_out(x, o_a, o_b, xp, _block_diag_pool(pool_w[l]), pool_scale[l][None],
                        w_o[l].astype(BF16), ffn_norm[l][None], moe_layer)
        j = l // 2
        if not moe_layer:
            x = _dense_ffn(x.reshape(T, D), h.reshape(T, D), dense_w_gate[j].astype(BF16),
                           dense_w_up[j].astype(BF16), dense_w_down[j].astype(BF16)).reshape(B, S, D)
        else:
            meta, gates, counts = _router(h, moe_router[j].T)
            n_tiles = (2 * T) // MOE_TILE + N_EXPERTS
            starts, tile_expert, n_used, zero_rows = _moe_layout(counts[:, 0], n_tiles, MOE_TILE)
            dst = _dest_rows(starts, meta)
            xs = _scatter_rows(zero_rows, dst, h, n_tiles * MOE_TILE)
            ys = _expert_ffn(tile_expert, n_used, xs, moe_w_gate[j].astype(BF16), moe_w_up[j].astype(BF16),
                             moe_w_down[j].astype(BF16))
            ct = min(COMBINE_TILE, T)
            dst_c = dst.transpose(1, 0, 2).reshape(SUBLANES, T // ct, ct).transpose(1, 0, 2)
            gates_col = gates.transpose(0, 2, 1).reshape(T, SUBLANES)
            last = (l == depth - 1)
            y = _combine(dst_c, x.reshape(T, D), gates_col, final_norm[None], ys, last)
            x = y.reshape(B, S, D)
    return x
```

```python
import functools
import math

import jax
import jax.numpy as jnp
import numpy as np
from jax import lax
from jax.experimental import pallas as pl
from jax.experimental.pallas import tpu as pltpu

F32 = jnp.float32
BF16 = jnp.bfloat16

HG_HEADS = 4
HG_KEY_DIM = 128
HG_VAL_DIM = 64
HG_KEY_WIDTH = HG_HEADS * HG_KEY_DIM
HG_WIDTH = HG_HEADS * HG_VAL_DIM
MIN_FORGET = 1e-20
MLA_HEADS = 4
MLA_Q_RANK = 256
MLA_KV_RANK = 128
MLA_NOPE = 128
MLA_ROPE = 64
MLA_V = 128
MLA_WIDTH = MLA_HEADS * MLA_V
ROPE_THETA = 10000.0
MASK_VALUE = -1e30
POOL_GROUPS = 4
POOL_WINDOWS = (2, 4, 8, 16)
POOL_WIDTH = 256
POOL_GROUP_DIM = POOL_WIDTH // POOL_GROUPS
N_EXPERTS = 8
EPS = 1e-6

LANES = 128
SUBLANES = 8
QK_PAD = 256
V_EXT = 256

TOKEN_TILE = 1024
IN_PROJ_CHAINS = 1
HGRN_CHUNK = 256
ATTN_BK = 512
ATTN_HEADS_PER_STEP = 1
ATTN_PAIRS_PER_TRIP = 8
FF_CHUNK = 512
MOE_FF_CHUNK = 512
MOE_TILE = 1024
ROUTE_TILE = 512
COMBINE_TILE = 256
POOL_HALO = 16
DMA_UNROLL = 8

_C_HG = 0
_C_CQ = 2 * HG_KEY_WIDTH + 2 * HG_WIDTH
_C_CKV = _C_CQ + MLA_Q_RANK
_C_KPE = _C_CKV + MLA_KV_RANK
_C_KPES = _C_KPE + LANES
_C_XP = _C_KPES + LANES
_C_END = _C_XP + POOL_WIDTH
_Q_HEAD_COLS = 3 * LANES


def _rms(x, g):
    return x * lax.rsqrt(jnp.mean(x * x, axis=-1, keepdims=True) + EPS) * g


def _dot(a, b):
    return jnp.dot(a, b, preferred_element_type=F32)


def _dot_nt(a, b):
    return lax.dot_general(a, b, (((1,), (1,)), ((), ())), preferred_element_type=F32)


def _dot_tn(a, b):
    return lax.dot_general(a, b, (((0,), (0,)), ((), ())), preferred_element_type=F32)


def _const_spec(shape):
    nd = len(shape)
    return pl.BlockSpec(shape, lambda *_: (0,) * nd)


def _load_tile_rows(ref, n):
    return jnp.concatenate([ref[pl.ds(s, n, stride=SUBLANES), :] for s in range(SUBLANES)], axis=1)


def _store_tile_rows(ref, val):
    n = val.shape[0]
    for s in range(SUBLANES):
        ref[pl.ds(s, n, stride=SUBLANES), :] = val[:, s * LANES:(s + 1) * LANES]


def _in_proj_kernel(x_ref, g_ref, w_ref, qn_ref, wuq_ref, kvn_ref, wukt_ref, wuv_ref, wkpet_ref,
                    cpad_ref, spad_ref, cpadt_ref, spadt_ref,
                    hg_ref, xp_ref, q_ref, kt_ref, v_ref):
    scale = (MLA_NOPE + MLA_ROPE) ** -0.5 * math.log2(math.e)
    tm = x_ref.shape[1]
    rows = tm // IN_PROJ_CHAINS
    ones_col = (lax.broadcasted_iota(jnp.int32, (rows, LANES), 1) == 0).astype(BF16)

    for r0 in range(0, tm, rows):
        rs = slice(r0, r0 + rows)
        h = _rms(x_ref[0, rs, :], g_ref[...]).astype(BF16)
        hg_ref[0, rs, :] = _dot(h, w_ref[:, _C_HG:_C_CQ])
        xp_ref[0, rs, :] = _dot(h, w_ref[:, _C_XP:_C_END])
        cpad = cpad_ref[rs, :]
        spad = spad_ref[rs, :]

        cq = _dot(h, w_ref[:, _C_CQ:_C_CKV])
        cqn = _rms(cq, qn_ref[...]).astype(BF16)
        for hd in range(MLA_HEADS):
            qh = _dot(cqn, wuq_ref[:, hd * _Q_HEAD_COLS:(hd + 1) * _Q_HEAD_COLS])
            q_ref[0, hd, rs, 0:LANES] = (qh[:, 0:LANES] * scale).astype(BF16)
            pe = qh[:, LANES:2 * LANES] * cpad + qh[:, 2 * LANES:3 * LANES] * spad
            q_ref[0, hd, rs, LANES:QK_PAD] = (pe * scale).astype(BF16)

        ckv = _dot(h, w_ref[:, _C_CKV:_C_KPE])
        ckvn = _rms(ckv, kvn_ref[...]).astype(BF16)
        kpet = (_dot_nt(wkpet_ref[0:LANES, :], h) * cpadt_ref[:, rs]
                + _dot_nt(wkpet_ref[LANES:2 * LANES, :], h) * spadt_ref[:, rs]).astype(BF16)
        v_all = _dot(ckvn, wuv_ref[...]).astype(BF16)
        for hd in range(MLA_HEADS):
            kt_ref[0, hd, 0:LANES, rs] = _dot_nt(wukt_ref[hd * LANES:(hd + 1) * LANES, :], ckvn).astype(BF16)
            kt_ref[0, hd, LANES:QK_PAD, rs] = kpet
            v_ref[0, hd, rs, 0:MLA_V] = v_all[:, hd * MLA_V:(hd + 1) * MLA_V]
            v_ref[0, hd, rs, MLA_V:V_EXT] = ones_col


def _in_proj(x, g, w_ext, qn, wuq_ext, kvn, wukt, wuv, wkpet, cpad, spad):
    B, S, D = x.shape
    tm = min(TOKEN_TILE, S)
    grid = (B, S // tm)
    n_hg = _C_CQ
    out_shape = (
        jax.ShapeDtypeStruct((B, S, n_hg), F32),
        jax.ShapeDtypeStruct((B, S, POOL_WIDTH), F32),
        jax.ShapeDtypeStruct((B, MLA_HEADS, S, QK_PAD), BF16),
        jax.ShapeDtypeStruct((B, MLA_HEADS, QK_PAD, S), BF16),
        jax.ShapeDtypeStruct((B, MLA_HEADS, S, V_EXT), BF16),
    )
    return pl.pallas_call(
        _in_proj_kernel,
        grid=grid,
        in_specs=[
            pl.BlockSpec((1, tm, D), lambda b, i: (b, i, 0)),
            _const_spec((1, D)),
            _const_spec(w_ext.shape),
            _const_spec((1, MLA_Q_RANK)),
            _const_spec(wuq_ext.shape),
            _const_spec((1, MLA_KV_RANK)),
            _const_spec(wukt.shape),
            _const_spec(wuv.shape),
            _const_spec(wkpet.shape),
            pl.BlockSpec((tm, LANES), lambda b, i: (i, 0)),
            pl.BlockSpec((tm, LANES), lambda b, i: (i, 0)),
            pl.BlockSpec((LANES, tm), lambda b, i: (0, i)),
            pl.BlockSpec((LANES, tm), lambda b, i: (0, i)),
        ],
        out_specs=(
            pl.BlockSpec((1, tm, n_hg), lambda b, i: (b, i, 0)),
            pl.BlockSpec((1, tm, POOL_WIDTH), lambda b, i: (b, i, 0)),
            pl.BlockSpec((1, MLA_HEADS, tm, QK_PAD), lambda b, i: (b, 0, i, 0)),
            pl.BlockSpec((1, MLA_HEADS, QK_PAD, tm), lambda b, i: (b, 0, 0, i)),
            pl.BlockSpec((1, MLA_HEADS, tm, V_EXT), lambda b, i: (b, 0, i, 0)),
        ),
        out_shape=out_shape,
        compiler_params=pltpu.CompilerParams(dimension_semantics=("arbitrary", "arbitrary")),
        name="in_proj",
    )(x, g, w_ext, qn, wuq_ext, kvn, wukt, wuv, wkpet, cpad, spad, cpad.T, spad.T)


def _split2(x):
    hi = x.astype(BF16)
    return hi, (x - hi.astype(F32)).astype(BF16)


def _hgrn_kernel(q_ref, f_ref, i_ref, g_ref, lb_ref, on_ref, tril_ref, grp_ref, o_ref, st_ref):
    C = q_ref.shape[1]
    KW = HG_KEY_WIDTH

    @pl.when(pl.program_id(1) == 0)
    def _():
        st_ref[...] = jnp.zeros_like(st_ref)

    lb = lb_ref[...]
    z = f_ref[0]
    sig = jax.nn.sigmoid(z)
    forget = lb + (1.0 - lb) * sig
    lg = jnp.log2(jnp.maximum(forget, MIN_FORGET))
    kk = (1.0 - lb) * (1.0 - sig)
    qq = jax.nn.silu(q_ref[0])
    vv = i_ref[0]

    tril = tril_ref[...]
    b = sum(_dot(tril, part) for part in _split2(lg))

    sides, masks = [], []
    row = lax.broadcasted_iota(jnp.int32, (C, C), 0)
    col = lax.broadcasted_iota(jnp.int32, (C, C), 1)
    sub = lax.broadcasted_iota(jnp.int32, (C, KW), 0)
    half = C // 2
    while half >= 4:
        blk = 2 * half
        b3 = b.reshape(C // blk, blk, KW)
        ref_row = jnp.broadcast_to(b3[:, half - 1:half, :], b3.shape).reshape(C, KW)
        e = jnp.exp2(-jnp.abs(b - ref_row))
        x = (jnp.where((sub & half) != 0, qq, kk) * e).astype(BF16)
        sides.append((x, x))
        shift = int(math.log2(blk))
        masks.append(((row >> shift) == (col >> shift)) & ((row & half) != 0) & ((col & half) == 0))
        half //= 2
    b3 = b.reshape(C // 8, 8, KW)
    mid_lo = 0.5 * (b3[:, 0:1, :] + b3[:, 3:4, :])
    mid_hi = 0.5 * (b3[:, 4:5, :] + b3[:, 7:8, :])
    sub8 = lax.broadcasted_iota(jnp.int32, b3.shape, 1)
    mid = jnp.where(sub8 < 4, mid_lo, mid_hi).reshape(C, KW)
    sides.append(((qq * jnp.exp2(b - mid)).astype(BF16), (kk * jnp.exp2(mid - b)).astype(BF16)))
    masks.append(((row >> 2) == (col >> 2)) & (col <= row))

    lane_v = lax.broadcasted_iota(jnp.int32, (C, HG_WIDTH), 1)
    o = _dot_nt((qq * jnp.exp2(b)).astype(BF16), st_ref[...].astype(BF16))
    for hd in range(HG_HEADS):
        ks = slice(hd * HG_KEY_DIM, (hd + 1) * HG_KEY_DIM)
        a = jnp.zeros((C, C), F32)
        for (qt, kt), m in zip(sides, masks):
            a = jnp.where(m, _dot_nt(qt[:, ks], kt[:, ks]), a)
        v_h = jnp.where((lane_v >> 6) == hd, vv, 0.0).astype(BF16)
        o = o + _dot(a.astype(BF16), v_h)

    b_last = b[C - 1:C, :]
    khat = (kk * jnp.exp2(b_last - b)).astype(BF16)
    st_row = lax.broadcasted_iota(jnp.int32, (HG_WIDTH, KW), 0)
    st_col = lax.broadcasted_iota(jnp.int32, (HG_WIDTH, KW), 1)
    new_st = st_ref[...] * jnp.exp2(b_last) + _dot_tn(vv.astype(BF16), khat)
    st_ref[...] = jnp.where((st_row >> 6) == (st_col >> 7), new_st, 0.0)

    grp = grp_ref[...]
    ssq = sum(_dot(part, grp) for part in _split2(o * o))
    on = o * lax.rsqrt(ssq * (1.0 / HG_VAL_DIM) + EPS) * on_ref[...]
    o_ref[0] = (on * jax.nn.silu(g_ref[0])).astype(o_ref.dtype)


def _hgrn(hg, lb, out_norm):
    B, S, _ = hg.shape
    C = min(HGRN_CHUNK, S)
    kb = HG_KEY_WIDTH // HG_KEY_WIDTH
    tril = jnp.asarray(np.tril(np.ones((C, C), np.float32)), BF16)
    lane_group = np.arange(HG_WIDTH) // HG_VAL_DIM
    grp = jnp.asarray((lane_group[:, None] == lane_group[None, :]).astype(np.float32), BF16)
    return pl.pallas_call(
        _hgrn_kernel,
        grid=(B, S // C),
        in_specs=[
            pl.BlockSpec((1, C, HG_KEY_WIDTH), lambda b, c: (b, c, 0)),
            pl.BlockSpec((1, C, HG_KEY_WIDTH), lambda b, c: (b, c, kb)),
            pl.BlockSpec((1, C, HG_WIDTH), lambda b, c: (b, c, 2 * HG_KEY_WIDTH // HG_WIDTH)),
            pl.BlockSpec((1, C, HG_WIDTH), lambda b, c: (b, c, 2 * HG_KEY_WIDTH // HG_WIDTH + 1)),
            _const_spec((1, HG_KEY_WIDTH)),
            _const_spec((1, HG_WIDTH)),
            _const_spec(tril.shape),
            _const_spec(grp.shape),
        ],
        out_specs=pl.BlockSpec((1, C, HG_WIDTH), lambda b, c: (b, c, 0)),
        out_shape=jax.ShapeDtypeStruct((B, S, HG_WIDTH), BF16),
        scratch_shapes=[pltpu.VMEM((HG_WIDTH, HG_KEY_WIDTH), F32)],
        compiler_params=pltpu.CompilerParams(dimension_semantics=("arbitrary", "arbitrary")),
        name="hgrn2",
    )(hg, hg, hg, hg, lb, out_norm, tril, grp)


def _attn_kernel(q_ref, kt_ref, v_ref, o_ref, s0_ref, s1_ref, acc_ref, *, bk):
    bq = q_ref.shape[2]
    nh = q_ref.shape[1]
    i = pl.program_id(2)
    qs = [q_ref[0, hd] for hd in range(nh)]

    def causal(s, key0):
        qry = lax.broadcasted_iota(jnp.int32, (bq, bk), 0)
        key = key0 + lax.broadcasted_iota(jnp.int32, (bq, bk), 1)
        return jnp.where(key <= qry, s, MASK_VALUE)

    def scores(blk, s_ref):
        start = pl.multiple_of(blk * bk, bk)
        for hd in range(nh):
            s_ref[hd] = _dot(qs[hd], kt_ref[0, hd, :, pl.ds(start, bk)])

    def softmax_pv(blk, s_ref, ms, key0=None):
        start = pl.multiple_of(blk * bk, bk)
        out = []
        for hd in range(nh):
            s = s_ref[hd]
            if key0 is not None:
                s = causal(s, key0)
            m_new = jnp.maximum(ms[hd], jnp.max(s, axis=1, keepdims=True))
            p = jnp.exp2(s - m_new).astype(BF16)
            acc_ref[hd] = jnp.exp2(ms[hd] - m_new) * acc_ref[hd] + _dot(p, v_ref[0, hd, pl.ds(start, bk), :])
            out.append(m_new)
        return tuple(out)

    acc_ref[...] = jnp.zeros_like(acc_ref)
    scores(0, s0_ref)

    def pair(t, ms):
        scores(2 * t + 1, s1_ref)
        ms = softmax_pv(2 * t, s0_ref, ms)
        scores(2 * t + 2, s0_ref)
        return softmax_pv(2 * t + 1, s1_ref, ms)

    def pairs(t0, n, c):
        for k in range(n):
            c = pair(t0 + k, c)
        return c

    ms = tuple(jnp.full((bq, 1), -jnp.inf, F32) for _ in range(nh))
    group = ATTN_PAIRS_PER_TRIP
    ms = lax.fori_loop(0, i // group, lambda t, c: pairs(group * t, group, c), ms)
    done = (i // group) * group
    while group > 1:
        group //= 2
        ms = lax.cond((i & group) != 0, functools.partial(pairs, done, group), lambda c: c, ms)
        done = done + (i & group)
    start1 = pl.multiple_of((2 * i + 1) * bk, bk)
    lower = [_dot(qs[hd][bk:, :], kt_ref[0, hd, :, pl.ds(start1, bk)]) for hd in range(nh)]
    ms = softmax_pv(2 * i, s0_ref, ms, 0)
    qry = lax.broadcasted_iota(jnp.int32, (bk, bk), 0)
    key = lax.broadcasted_iota(jnp.int32, (bk, bk), 1)
    for hd in range(nh):
        s = jnp.where(key <= qry, lower[hd], MASK_VALUE)
        m_old = ms[hd][bk:, :]
        m_new = jnp.maximum(m_old, jnp.max(s, axis=1, keepdims=True))
        p = jnp.exp2(s - m_new).astype(BF16)
        acc_ref[hd, bk:, :] = (jnp.exp2(m_old - m_new) * acc_ref[hd, bk:, :]
                               + _dot(p, v_ref[0, hd, pl.ds(start1, bk), :]))
    for hd in range(nh):
        acc = acc_ref[hd]
        o_ref[0, :, hd * MLA_V:(hd + 1) * MLA_V] = (acc[:, 0:MLA_V] / acc[:, MLA_V:MLA_V + 1]).astype(o_ref.dtype)


def _attention(q, kt, v):
    B, H, S, _ = q.shape
    bk = min(ATTN_BK, S // 2)
    bq = 2 * bk
    nh = ATTN_HEADS_PER_STEP
    resident = dict(pipeline_mode=pl.Buffered(1)) if nh > 1 else {}
    return pl.pallas_call(
        functools.partial(_attn_kernel, bk=bk),
        grid=(B, H // nh, S // bq),
        in_specs=[
            pl.BlockSpec((1, nh, bq, QK_PAD), lambda b, h, i: (b, h, i, 0)),
            pl.BlockSpec((1, nh, QK_PAD, S), lambda b, h, i: (b, h, 0, 0), **resident),
            pl.BlockSpec((1, nh, S, V_EXT), lambda b, h, i: (b, h, 0, 0), **resident),
        ],
        out_specs=pl.BlockSpec((1, bq, nh * MLA_V), lambda b, h, i: (b, i, h)),
        out_shape=jax.ShapeDtypeStruct((B, S, H * MLA_V), BF16),
        scratch_shapes=[pltpu.VMEM((nh, bq, bk), F32), pltpu.VMEM((nh, bq, bk), F32),
                        pltpu.VMEM((nh, bq, V_EXT), F32)],
        compiler_params=pltpu.CompilerParams(dimension_semantics=("arbitrary", "arbitrary", "arbitrary")),
        name="mla_attention",
    )(q, kt, v)


def _mix_out_kernel(x_ref, oa_ref, ob_ref, xp_ref, wpool_ref, pscale_ref, wo_ref, fn_ref,
                    xo_ref, h_ref, halo_ref, *, tile_rows):
    tm = xp_ref.shape[1]
    i = pl.program_id(1)

    @pl.when(i == 0)
    def _():
        halo_ref[...] = jnp.zeros_like(halo_ref)

    xp = xp_ref[0]
    xx = jnp.concatenate([halo_ref[...], xp], axis=0)
    halo_ref[...] = xp[tm - POOL_HALO:, :]

    w2 = xx[1:, :] + xx[:-1, :]
    w4 = w2[2:, :] + w2[:-2, :]
    w8 = w4[4:, :] + w4[:-4, :]
    w16 = w8[8:, :] + w8[:-8, :]
    sums = (w2[POOL_HALO - 1:, :], w4[POOL_HALO - 3:, :], w8[POOL_HALO - 7:, :], w16[POOL_HALO - 15:, :])
    t = i * tm + lax.broadcasted_iota(jnp.int32, (tm, POOL_WIDTH), 0)
    lane = lax.broadcasted_iota(jnp.int32, (tm, POOL_WIDTH), 1)
    pooled = jnp.zeros((tm, POOL_WIDTH), F32)
    for gi, w in enumerate(POOL_WINDOWS):
        cnt = jnp.minimum(t + 1, w).astype(F32)
        pooled = jnp.where((lane >> 6) == gi, sums[gi] / cnt, pooled)
    pooled = pooled - xp
    oc = _dot(pooled.astype(BF16), wpool_ref[...]) * pscale_ref[...]

    y = _dot(oa_ref[0], wo_ref[0:HG_WIDTH, :])
    y = y + _dot(ob_ref[0], wo_ref[HG_WIDTH:HG_WIDTH + MLA_WIDTH, :])
    y = y + _dot(oc.astype(BF16), wo_ref[HG_WIDTH + MLA_WIDTH:, :])
    xn = x_ref[0] + y
    xo_ref[0] = xn
    h = _rms(xn, fn_ref[...])
    if tile_rows:
        _store_tile_rows(h_ref, h)
    else:
        h_ref[0] = h.astype(h_ref.dtype)


def _mix_out(x, oa, ob, xp, wpool_bd, pscale, wo, fnorm, tile_rows):
    B, S, D = x.shape
    tm = min(TOKEN_TILE, S)
    nt = S // tm
    tok = lambda w: pl.BlockSpec((1, tm, w), lambda b, i: (b, i, 0))
    if tile_rows:
        assert D == SUBLANES * LANES
        h_spec = pl.BlockSpec((tm * SUBLANES, LANES), lambda b, i: (b * nt + i, 0))
        h_shape = jax.ShapeDtypeStruct((B * S * SUBLANES, LANES), F32)
    else:
        h_spec, h_shape = tok(D), jax.ShapeDtypeStruct((B, S, D), BF16)
    return pl.pallas_call(
        functools.partial(_mix_out_kernel, tile_rows=tile_rows),
        grid=(B, nt),
        in_specs=[tok(D), tok(HG_WIDTH), tok(MLA_WIDTH), tok(POOL_WIDTH),
                  _const_spec(wpool_bd.shape), _const_spec((1, POOL_WIDTH)), _const_spec(wo.shape),
                  _const_spec((1, D))],
        out_specs=(tok(D), h_spec),
        out_shape=(jax.ShapeDtypeStruct((B, S, D), F32), h_shape),
        scratch_shapes=[pltpu.VMEM((POOL_HALO, POOL_WIDTH), F32)],
        compiler_params=pltpu.CompilerParams(dimension_semantics=("arbitrary", "arbitrary")),
        name="mix_out",
    )(x, oa, ob, xp, wpool_bd, pscale, wo, fnorm)


def _dense_ffn_kernel(x_ref, h_ref, wg_ref, wu_ref, wd_ref, o_ref):
    h = h_ref[...]
    acc = x_ref[...]
    dff = wg_ref.shape[1]
    for c0 in range(0, dff, FF_CHUNK):
        g = _dot(h, wg_ref[:, c0:c0 + FF_CHUNK])
        u = _dot(h, wu_ref[:, c0:c0 + FF_CHUNK])
        acc = acc + _dot((jax.nn.silu(g) * u).astype(BF16), wd_ref[c0:c0 + FF_CHUNK, :])
    o_ref[...] = acc


def _dense_ffn(x2d, h2d, wg, wu, wd):
    T, D = x2d.shape
    tm = min(TOKEN_TILE, T)
    tok = pl.BlockSpec((tm, D), lambda i: (i, 0))
    return pl.pallas_call(
        _dense_ffn_kernel,
        grid=(T // tm,),
        in_specs=[tok, tok, _const_spec(wg.shape), _const_spec(wu.shape), _const_spec(wd.shape)],
        out_specs=tok,
        out_shape=jax.ShapeDtypeStruct((T, D), F32),
        compiler_params=pltpu.CompilerParams(dimension_semantics=("arbitrary",)),
        name="dense_ffn",
    )(x2d, h2d, wg, wu, wd)


def _router_kernel(h_ref, rt_ref, meta_ref, gate_ref, cnt_ref, run_ref):
    tm = h_ref.shape[0] // SUBLANES
    E = N_EXPERTS

    @pl.when(pl.program_id(0) == 0)
    def _():
        run_ref[...] = jnp.zeros_like(run_ref)

    logits = lax.dot_general(rt_ref[...], _load_tile_rows(h_ref, tm), (((1,), (1,)), ((), ())),
                             precision=lax.Precision.HIGHEST, preferred_element_type=F32)
    eid = lax.broadcasted_iota(jnp.int32, (E, tm), 0)
    m1 = jnp.max(logits, axis=0, keepdims=True)
    i1 = jnp.min(jnp.where(logits == m1, eid, E), axis=0, keepdims=True)
    rest = jnp.where(eid == i1, -jnp.inf, logits)
    m2 = jnp.max(rest, axis=0, keepdims=True)
    i2 = jnp.min(jnp.where(rest == m2, eid, E), axis=0, keepdims=True)
    e2 = jnp.exp(m2 - m1)
    g1 = 1.0 / (1.0 + e2)
    g2 = e2 / (1.0 + e2)

    sel = ((eid == i1) | (eid == i2))
    r = lax.broadcasted_iota(jnp.int32, (tm, tm), 0)
    c = lax.broadcasted_iota(jnp.int32, (tm, tm), 1)
    before = (r < c).astype(BF16)
    excl = _dot(sel.astype(BF16), before) + run_ref[:, 0:1]
    rank1 = jnp.sum(jnp.where(eid == i1, excl, 0.0), axis=0, keepdims=True).astype(jnp.int32)
    rank2 = jnp.sum(jnp.where(eid == i2, excl, 0.0), axis=0, keepdims=True).astype(jnp.int32)
    run_ref[...] = run_ref[...] + jnp.sum(sel.astype(F32), axis=1, keepdims=True)
    cnt_ref[...] = run_ref[...].astype(jnp.int32)

    zi = jnp.zeros((1, tm), jnp.int32)
    meta_ref[0] = jnp.concatenate([i1, i2, rank1, rank2, zi, zi, zi, zi], axis=0)
    zf = jnp.zeros((1, tm), F32)
    gate_ref[0] = jnp.concatenate([g1, g2, zf, zf, zf, zf, zf, zf], axis=0)


def _router(h_rows, router_t):
    T = h_rows.shape[0] // SUBLANES
    tm = min(ROUTE_TILE, T)
    nt = T // tm
    return pl.pallas_call(
        _router_kernel,
        grid=(nt,),
        in_specs=[pl.BlockSpec((tm * SUBLANES, LANES), lambda i: (i, 0)), _const_spec(router_t.shape)],
        out_specs=(pl.BlockSpec((1, SUBLANES, tm), lambda i: (i, 0, 0)),
                   pl.BlockSpec((1, SUBLANES, tm), lambda i: (i, 0, 0)),
                   _const_spec((N_EXPERTS, LANES))),
        out_shape=(jax.ShapeDtypeStruct((nt, SUBLANES, tm), jnp.int32),
                   jax.ShapeDtypeStruct((nt, SUBLANES, tm), F32),
                   jax.ShapeDtypeStruct((N_EXPERTS, LANES), jnp.int32)),
        scratch_shapes=[pltpu.VMEM((N_EXPERTS, LANES), F32)],
        compiler_params=pltpu.CompilerParams(dimension_semantics=("arbitrary",)),
        name="moe_router",
    )(h_rows, router_t)


def _dest_kernel(start_ref, meta_ref, dst_ref):
    meta = meta_ref[0]
    rows = []
    for k in range(2):
        e = meta[k:k + 1, :]
        base = jnp.zeros_like(e)
        for ex in range(N_EXPERTS):
            base = jnp.where(e == ex, start_ref[ex], base)
        rows.append(base + meta[2 + k:3 + k, :])
    dst_ref[0] = jnp.concatenate(rows + [jnp.zeros_like(rows[0])] * (SUBLANES - 2), axis=0)


def _dest_rows(starts, meta):
    nt, _, tm = meta.shape
    spec = pl.BlockSpec((1, SUBLANES, tm), lambda i, s: (i, 0, 0))
    return pl.pallas_call(
        _dest_kernel,
        grid_spec=pltpu.PrefetchScalarGridSpec(num_scalar_prefetch=1, grid=(nt,), in_specs=[spec], out_specs=spec),
        out_shape=jax.ShapeDtypeStruct(meta.shape, jnp.int32),
        compiler_params=pltpu.CompilerParams(dimension_semantics=("arbitrary",)),
        name="moe_dest",
    )(starts, meta)


def _scatter_kernel(zrow_ref, dst_ref, h_ref, xs_hbm, zero_ref, sem, zsem):
    tm = dst_ref.shape[2]

    @pl.when(pl.program_id(0) == 0)
    def _():
        zero_ref[...] = jnp.zeros_like(zero_ref)
        tile_rows = zero_ref.shape[0]

        def fill(row):
            start = pl.multiple_of(row * SUBLANES, SUBLANES)
            return pltpu.make_async_copy(zero_ref, xs_hbm.at[pl.ds(start, tile_rows), :], zsem)

        fills = [fill(zrow_ref[e]) for e in range(N_EXPERTS)]
        for f in fills:
            f.start()
        for f in fills:
            f.wait()
        n_tiles = xs_hbm.shape[0] // tile_rows
        for j in range(N_EXPERTS):
            @pl.when(n_tiles - 1 - j >= zrow_ref[N_EXPERTS])
            def _():
                f = fill((n_tiles - 1 - j) * (tile_rows // SUBLANES))
                f.start()
                f.wait()

    def row_copy(src_row, dst_row):
        src = pl.multiple_of(src_row * SUBLANES, SUBLANES)
        dst = pl.multiple_of(dst_row * SUBLANES, SUBLANES)
        return pltpu.make_async_copy(h_ref.at[pl.ds(src, SUBLANES), :], xs_hbm.at[pl.ds(dst, SUBLANES), :], sem)

    def issue(t, _):
        for k in range(2):
            row_copy(t, dst_ref[0, k, t]).start(priority=k)
        return 0

    lax.fori_loop(0, tm, issue, 0, unroll=DMA_UNROLL)

    def drain(t, _):
        row_copy(0, 0).wait()
        row_copy(0, 0).wait()
        return 0

    lax.fori_loop(0, tm, drain, 0, unroll=DMA_UNROLL)


def _scatter_rows(zero_rows, dst, h_rows, n_rows):
    nt, _, tm = dst.shape
    grid_spec = pltpu.PrefetchScalarGridSpec(
        num_scalar_prefetch=1,
        grid=(nt,),
        in_specs=[pl.BlockSpec((1, SUBLANES, tm), lambda i, z: (i, 0, 0), memory_space=pltpu.SMEM),
                  pl.BlockSpec((tm * SUBLANES, LANES), lambda i, z: (i, 0))],
        out_specs=pl.BlockSpec(memory_space=pl.ANY),
        scratch_shapes=[pltpu.VMEM((MOE_TILE * SUBLANES, LANES), F32), pltpu.SemaphoreType.DMA(()),
                        pltpu.SemaphoreType.DMA(())],
    )
    return pl.pallas_call(
        _scatter_kernel,
        grid_spec=grid_spec,
        out_shape=jax.ShapeDtypeStruct((n_rows * SUBLANES, LANES), F32),
        compiler_params=pltpu.CompilerParams(dimension_semantics=("arbitrary",), has_side_effects=True),
        name="moe_scatter",
    )(zero_rows, dst, h_rows)


def _expert_kernel(te_ref, nu_ref, x_ref, wg_ref, wu_ref, wd_ref, y_ref, *, fc):
    m = pl.program_id(0)
    tr = x_ref.shape[0] // SUBLANES
    dff = wg_ref.shape[2]

    @pl.when(m < nu_ref[0])
    def _():
        xb = _load_tile_rows(x_ref, tr).astype(BF16)
        acc = jnp.zeros((tr, wd_ref.shape[2]), F32)
        for c0 in range(0, dff, fc):
            g = _dot(xb, wg_ref[0, :, c0:c0 + fc])
            u = _dot(xb, wu_ref[0, :, c0:c0 + fc])
            acc = acc + _dot((jax.nn.silu(g) * u).astype(BF16), wd_ref[0, c0:c0 + fc, :])
        _store_tile_rows(y_ref, acc)

    @pl.when(m >= nu_ref[0])
    def _():
        y_ref[...] = jnp.zeros_like(y_ref)


def _expert_ffn(tile_expert, n_used, xs, wg, wu, wd):
    E, D, F = wg.shape
    R = xs.shape[0] // SUBLANES
    tr = MOE_TILE
    n_tiles = R // tr
    fc = min(MOE_FF_CHUNK, F)

    def row_map(m, te, nu):
        return (jnp.minimum(m, nu[0] - 1), 0)

    def w_map(m, te, nu):
        return (te[jnp.minimum(m, nu[0] - 1)], 0, 0)

    resident = dict(pipeline_mode=pl.Buffered(1))
    grid_spec = pltpu.PrefetchScalarGridSpec(
        num_scalar_prefetch=2,
        grid=(n_tiles,),
        in_specs=[
            pl.BlockSpec((tr * SUBLANES, LANES), row_map),
            pl.BlockSpec((1, D, F), w_map, **resident),
            pl.BlockSpec((1, D, F), w_map, **resident),
            pl.BlockSpec((1, F, D), w_map, **resident),
        ],
        out_specs=pl.BlockSpec((tr * SUBLANES, LANES), lambda m, te, nu: (m, 0)),
    )
    return pl.pallas_call(
        functools.partial(_expert_kernel, fc=fc),
        grid_spec=grid_spec,
        out_shape=jax.ShapeDtypeStruct(xs.shape, F32),
        compiler_params=pltpu.CompilerParams(dimension_semantics=("arbitrary",)),
        name="moe_experts",
    )(tile_expert, n_used, xs, wg, wu, wd)


def _combine_kernel(dcur_ref, dnxt_ref, x_ref, gate_ref, fn_ref, ys_hbm, o_ref, buf_ref, sems, *, final_norm):
    tm = x_ref.shape[0]
    i = pl.program_id(0)
    slot = i % 2

    def row_copy(src_row, s, k, t):
        src = pl.multiple_of(src_row * SUBLANES, SUBLANES)
        dst = pl.multiple_of(t * SUBLANES, SUBLANES)
        return pltpu.make_async_copy(ys_hbm.at[pl.ds(src, SUBLANES), :],
                                     buf_ref.at[s, k, pl.ds(dst, SUBLANES), :], sems.at[s])

    def issue_tile(d_ref, s):
        def issue(t, _):
            for k in range(2):
                row_copy(d_ref[0, k, t], s, k, t).start(priority=k)
            return 0

        lax.fori_loop(0, tm, issue, 0, unroll=DMA_UNROLL)

    @pl.when(i == 0)
    def _():
        issue_tile(dcur_ref, 0)

    @pl.when(i + 1 < pl.num_programs(0))
    def _():
        issue_tile(dnxt_ref, 1 - slot)

    def drain(t, _):
        row_copy(0, slot, 0, 0).wait()
        row_copy(0, slot, 1, 0).wait()
        return 0

    lax.fori_loop(0, tm, drain, 0, unroll=DMA_UNROLL)
    g = gate_ref[...]
    y0 = _load_tile_rows(buf_ref.at[slot, 0], tm)
    y1 = _load_tile_rows(buf_ref.at[slot, 1], tm)
    xn = x_ref[...] + g[:, 0:1] * y0 + g[:, 1:2] * y1
    o_ref[...] = _rms(xn, fn_ref[...]) if final_norm else xn


def _combine(dst_c, x2d, gates_col, fnorm, ys, final_norm):
    T, D = x2d.shape
    nt, _, tm = dst_c.shape
    return pl.pallas_call(
        functools.partial(_combine_kernel, final_norm=final_norm),
        grid=(nt,),
        in_specs=[pl.BlockSpec((1, SUBLANES, tm), lambda i: (i, 0, 0), memory_space=pltpu.SMEM),
                  pl.BlockSpec((1, SUBLANES, tm), lambda i: (jnp.minimum(i + 1, nt - 1), 0, 0),
                               memory_space=pltpu.SMEM),
                  pl.BlockSpec((tm, D), lambda i: (i, 0)),
                  pl.BlockSpec((tm, SUBLANES), lambda i: (i, 0)),
                  pl.BlockSpec((1, D), lambda i: (0, 0)),
                  pl.BlockSpec(memory_space=pl.ANY)],
        out_specs=pl.BlockSpec((tm, D), lambda i: (i, 0)),
        scratch_shapes=[pltpu.VMEM((2, 2, tm * SUBLANES, LANES), F32), pltpu.SemaphoreType.DMA((2,))],
        out_shape=jax.ShapeDtypeStruct((T, D), F32),
        compiler_params=pltpu.CompilerParams(dimension_semantics=("arbitrary",)),
        name="moe_combine",
    )(dst_c, dst_c, x2d, gates_col, fnorm, ys)


def _swap_halves_cols(w):
    half = w.shape[-1] // 2
    return jnp.concatenate([w[..., half:], w[..., :half]], axis=-1)


def _pad_cols(w, width):
    return jnp.pad(w, [(0, 0)] * (w.ndim - 1) + [(0, width - w.shape[-1])])


def _prep_w_in(w):
    kpe0 = _C_CKV + MLA_KV_RANK - 0
    kpe = w[:, kpe0:kpe0 + MLA_ROPE]
    return jnp.concatenate([
        w[:, :kpe0],
        _pad_cols(kpe, LANES), _pad_cols(_swap_halves_cols(kpe), LANES),
        w[:, kpe0 + MLA_ROPE:],
    ], axis=1).astype(BF16)


def _prep_w_uq(w):
    w = w.reshape(MLA_Q_RANK, MLA_HEADS, MLA_NOPE + MLA_ROPE)
    nope, pe = w[..., :MLA_NOPE], w[..., MLA_NOPE:]
    out = jnp.concatenate([nope, _pad_cols(pe, LANES), _pad_cols(_swap_halves_cols(pe), LANES)], axis=-1)
    return out.reshape(MLA_Q_RANK, MLA_HEADS * _Q_HEAD_COLS).astype(BF16)


def _prep_w_ukv(w):
    w = w.reshape(MLA_KV_RANK, MLA_HEADS, MLA_NOPE + MLA_V)
    wukt = jnp.transpose(w[..., :MLA_NOPE], (1, 2, 0)).reshape(MLA_HEADS * MLA_NOPE, MLA_KV_RANK)
    wuv = w[..., MLA_NOPE:].reshape(MLA_KV_RANK, MLA_HEADS * MLA_V)
    return wukt.astype(BF16), wuv.astype(BF16)


def _prep_w_kpe_t(w_ext):
    return w_ext[:, _C_KPE:_C_XP].T


def _rope_tables(seq):
    pos = jnp.arange(seq, dtype=F32)
    inv_freq = 1.0 / (ROPE_THETA ** (jnp.arange(0, MLA_ROPE, 2, dtype=F32) / MLA_ROPE))
    ang = pos[:, None] * inv_freq[None, :]
    cos, sin = jnp.cos(ang), jnp.sin(ang)
    cpad = _pad_cols(jnp.concatenate([cos, cos], axis=-1), LANES)
    spad = _pad_cols(jnp.concatenate([-sin, sin], axis=-1), LANES)
    return cpad, spad


def _block_diag_pool(w):
    G, c, _ = w.shape
    eye = jnp.eye(G, dtype=w.dtype)
    return (eye[:, None, :, None] * w[:, :, None, :]).reshape(G * c, G * c).astype(BF16)


def _moe_layout(counts, n_tiles, tile):
    tiles_per = (counts + tile - 1) // tile
    ends = jnp.cumsum(tiles_per)
    starts = (ends - tiles_per) * tile
    tile_expert = jnp.sum((jnp.arange(n_tiles)[:, None] >= ends[None, :]).astype(jnp.int32), axis=1)
    tile_expert = jnp.minimum(tile_expert, N_EXPERTS - 1)
    zero_rows = jnp.minimum(starts + counts, (n_tiles - 1) * tile)
    zero_info = jnp.concatenate([zero_rows, ends[-1:]])
    return (starts.astype(jnp.int32), tile_expert.astype(jnp.int32), ends[-1:].astype(jnp.int32),
            zero_info.astype(jnp.int32))


def kernel(x, attn_norm, w_in, hgrn_lower_bounds, hgrn_out_norm, mla_q_norm, mla_w_uq, mla_kv_norm,
           mla_w_ukv, pool_w, pool_scale, w_o, ffn_norm, dense_w_gate, dense_w_up, dense_w_down,
           moe_router, moe_w_gate, moe_w_up, moe_w_down, final_norm):
    B, S, D = x.shape
    T = B * S
    depth = w_in.shape[0]
    cpad, spad = _rope_tables(S)
    p_lb = jax.nn.softmax(hgrn_lower_bounds.astype(F32), axis=0)
    lbs = jnp.cumsum(p_lb, axis=0) - p_lb[0:1]

    assert depth % 2 == 0, "the final RMSNorm is fused into the last (MoE) layer's combine kernel"
    for l in range(depth):
        wukt, wuv = _prep_w_ukv(mla_w_ukv[l])
        w_ext = _prep_w_in(w_in[l])
        hg, xp, q, kt, v = _in_proj(
            x, attn_norm[l][None], w_ext, mla_q_norm[l][None], _prep_w_uq(mla_w_uq[l]),
            mla_kv_norm[l][None], wukt, wuv, _prep_w_kpe_t(w_ext), cpad, spad)
        o_a = _hgrn(hg, lbs[l][None], hgrn_out_norm[l][None])
        o_b = _attention(q, kt, v)
        moe_layer = (l % 2 == 1)
        x, h = _mix_out(x, o_a, o_b, xp, _block_diag_pool(pool_w[l]), pool_scale[l][None],
                        w_o[l].astype(BF16), ffn_norm[l][None], moe_layer)
        j = l // 2
        if not moe_layer:
            x = _dense_ffn(x.reshape(T, D), h.reshape(T, D), dense_w_gate[j].astype(BF16),
                           dense_w_up[j].astype(BF16), dense_w_down[j].astype(BF16)).reshape(B, S, D)
        else:
            meta, gates, counts = _router(h, moe_router[j].T)
            n_tiles = (2 * T) // MOE_TILE + N_EXPERTS
            starts, tile_expert, n_used, zero_rows = _moe_layout(counts[:, 0], n_tiles, MOE_TILE)
            dst = _dest_rows(starts, meta)
            xs = _scatter_rows(zero_rows, dst, h, n_tiles * MOE_TILE)
            ys = _expert_ffn(tile_expert, n_used, xs, moe_w_gate[j].astype(BF16), moe_w_up[j].astype(BF16),
                             moe_w_down[j].astype(BF16))
            ct = min(COMBINE_TILE, T)
            dst_c = dst.transpose(1, 0, 2).reshape(SUBLANES, T // ct, ct).transpose(1, 0, 2)
            gates_col = gates.transpose(0, 2, 1).reshape(T, SUBLANES)
            last = (l == depth - 1)
            y = _combine(dst_c, x.reshape(T, D), gates_col, final_norm[None], ys, last)
            x = y.reshape(B, S, D)
    return x
```

```python
import functools
import math

import jax
import jax.numpy as jnp
import numpy as np
from jax import lax
from jax.experimental import pallas as pl
from jax.experimental.pallas import tpu as pltpu

F32 = jnp.float32
BF16 = jnp.bfloat16

HG_HEADS = 4
HG_KEY_DIM = 128
HG_VAL_DIM = 64
HG_KEY_WIDTH = HG_HEADS * HG_KEY_DIM
HG_WIDTH = HG_HEADS * HG_VAL_DIM
MIN_FORGET = 1e-20
MLA_HEADS = 4
MLA_Q_RANK = 256
MLA_KV_RANK = 128
MLA_NOPE = 128
MLA_ROPE = 64
MLA_V = 128
MLA_WIDTH = MLA_HEADS * MLA_V
ROPE_THETA = 10000.0
MASK_VALUE = -1e30
POOL_GROUPS = 4
POOL_WINDOWS = (2, 4, 8, 16)
POOL_WIDTH = 256
POOL_GROUP_DIM = POOL_WIDTH // POOL_GROUPS
N_EXPERTS = 8
EPS = 1e-6

LANES = 128
SUBLANES = 8
QK_PAD = 256
V_EXT = 256

TOKEN_TILE = 1024
IN_PROJ_CHAINS = 1
HGRN_CHUNK = 256
ATTN_BK = 512
ATTN_HEADS_PER_STEP = 1
ATTN_PAIRS_PER_TRIP = 8
FF_CHUNK = 512
MOE_FF_CHUNK = 256
MOE_TILE = 1024
ROUTE_TILE = 512
COMBINE_TILE = 256
POOL_HALO = 16
DMA_UNROLL = 8

_C_HG = 0
_C_CQ = 2 * HG_KEY_WIDTH + 2 * HG_WIDTH
_C_CKV = _C_CQ + MLA_Q_RANK
_C_KPE = _C_CKV + MLA_KV_RANK
_C_KPES = _C_KPE + LANES
_C_XP = _C_KPES + LANES
_C_END = _C_XP + POOL_WIDTH
_Q_HEAD_COLS = 3 * LANES


def _rms(x, g):
    return x * lax.rsqrt(jnp.mean(x * x, axis=-1, keepdims=True) + EPS) * g


def _dot(a, b):
    return jnp.dot(a, b, preferred_element_type=F32)


def _dot_nt(a, b):
    return lax.dot_general(a, b, (((1,), (1,)), ((), ())), preferred_element_type=F32)


def _dot_tn(a, b):
    return lax.dot_general(a, b, (((0,), (0,)), ((), ())), preferred_element_type=F32)


def _const_spec(shape):
    nd = len(shape)
    return pl.BlockSpec(shape, lambda *_: (0,) * nd)


def _load_tile_rows(ref, n):
    return jnp.concatenate([ref[pl.ds(s, n, stride=SUBLANES), :] for s in range(SUBLANES)], axis=1)


def _store_tile_rows(ref, val):
    n = val.shape[0]
    for s in range(SUBLANES):
        ref[pl.ds(s, n, stride=SUBLANES), :] = val[:, s * LANES:(s + 1) * LANES]


def _in_proj_kernel(x_ref, g_ref, w_ref, qn_ref, wuq_ref, kvn_ref, wukt_ref, wuv_ref, wkpet_ref,
                    cpad_ref, spad_ref, cpadt_ref, spadt_ref,
                    hg_ref, xp_ref, q_ref, kt_ref, v_ref):
    scale = (MLA_NOPE + MLA_ROPE) ** -0.5 * math.log2(math.e)
    tm = x_ref.shape[1]
    rows = tm // IN_PROJ_CHAINS
    ones_col = (lax.broadcasted_iota(jnp.int32, (rows, LANES), 1) == 0).astype(BF16)

    for r0 in range(0, tm, rows):
        rs = slice(r0, r0 + rows)
        h = _rms(x_ref[0, rs, :], g_ref[...]).astype(BF16)
        hg_ref[0, rs, :] = _dot(h, w_ref[:, _C_HG:_C_CQ])
        xp_ref[0, rs, :] = _dot(h, w_ref[:, _C_XP:_C_END])
        cpad = cpad_ref[rs, :]
        spad = spad_ref[rs, :]

        cq = _dot(h, w_ref[:, _C_CQ:_C_CKV])
        cqn = _rms(cq, qn_ref[...]).astype(BF16)
        for hd in range(MLA_HEADS):
            qh = _dot(cqn, wuq_ref[:, hd * _Q_HEAD_COLS:(hd + 1) * _Q_HEAD_COLS])
            q_ref[0, hd, rs, 0:LANES] = (qh[:, 0:LANES] * scale).astype(BF16)
            pe = qh[:, LANES:2 * LANES] * cpad + qh[:, 2 * LANES:3 * LANES] * spad
            q_ref[0, hd, rs, LANES:QK_PAD] = (pe * scale).astype(BF16)

        ckv = _dot(h, w_ref[:, _C_CKV:_C_KPE])
        ckvn = _rms(ckv, kvn_ref[...]).astype(BF16)
        kpet = (_dot_nt(wkpet_ref[0:LANES, :], h) * cpadt_ref[:, rs]
                + _dot_nt(wkpet_ref[LANES:2 * LANES, :], h) * spadt_ref[:, rs]).astype(BF16)
        v_all = _dot(ckvn, wuv_ref[...]).astype(BF16)
        for hd in range(MLA_HEADS):
            kt_ref[0, hd, 0:LANES, rs] = _dot_nt(wukt_ref[hd * LANES:(hd + 1) * LANES, :], ckvn).astype(BF16)
            kt_ref[0, hd, LANES:QK_PAD, rs] = kpet
            v_ref[0, hd, rs, 0:MLA_V] = v_all[:, hd * MLA_V:(hd + 1) * MLA_V]
            v_ref[0, hd, rs, MLA_V:V_EXT] = ones_col


def _in_proj(x, g, w_ext, qn, wuq_ext, kvn, wukt, wuv, wkpet, cpad, spad):
    B, S, D = x.shape
    tm = min(TOKEN_TILE, S)
    grid = (B, S // tm)
    n_hg = _C_CQ
    out_shape = (
        jax.ShapeDtypeStruct((B, S, n_hg), F32),
        jax.ShapeDtypeStruct((B, S, POOL_WIDTH), F32),
        jax.ShapeDtypeStruct((B, MLA_HEADS, S, QK_PAD), BF16),
        jax.ShapeDtypeStruct((B, MLA_HEADS, QK_PAD, S), BF16),
        jax.ShapeDtypeStruct((B, MLA_HEADS, S, V_EXT), BF16),
    )
    return pl.pallas_call(
        _in_proj_kernel,
        grid=grid,
        in_specs=[
            pl.BlockSpec((1, tm, D), lambda b, i: (b, i, 0)),
            _const_spec((1, D)),
            _const_spec(w_ext.shape),
            _const_spec((1, MLA_Q_RANK)),
            _const_spec(wuq_ext.shape),
            _const_spec((1, MLA_KV_RANK)),
            _const_spec(wukt.shape),
            _const_spec(wuv.shape),
            _const_spec(wkpet.shape),
            pl.BlockSpec((tm, LANES), lambda b, i: (i, 0)),
            pl.BlockSpec((tm, LANES), lambda b, i: (i, 0)),
            pl.BlockSpec((LANES, tm), lambda b, i: (0, i)),
            pl.BlockSpec((LANES, tm), lambda b, i: (0, i)),
        ],
        out_specs=(
            pl.BlockSpec((1, tm, n_hg), lambda b, i: (b, i, 0)),
            pl.BlockSpec((1, tm, POOL_WIDTH), lambda b, i: (b, i, 0)),
            pl.BlockSpec((1, MLA_HEADS, tm, QK_PAD), lambda b, i: (b, 0, i, 0)),
            pl.BlockSpec((1, MLA_HEADS, QK_PAD, tm), lambda b, i: (b, 0, 0, i)),
            pl.BlockSpec((1, MLA_HEADS, tm, V_EXT), lambda b, i: (b, 0, i, 0)),
        ),
        out_shape=out_shape,
        compiler_params=pltpu.CompilerParams(dimension_semantics=("arbitrary", "arbitrary")),
        name="in_proj",
    )(x, g, w_ext, qn, wuq_ext, kvn, wukt, wuv, wkpet, cpad, spad, cpad.T, spad.T)


def _split2(x):
    hi = x.astype(BF16)
    return hi, (x - hi.astype(F32)).astype(BF16)


def _hgrn_kernel(q_ref, f_ref, i_ref, g_ref, lb_ref, on_ref, tril_ref, grp_ref, o_ref, st_ref):
    C = q_ref.shape[1]
    KW = HG_KEY_WIDTH

    @pl.when(pl.program_id(1) == 0)
    def _():
        st_ref[...] = jnp.zeros_like(st_ref)

    lb = lb_ref[...]
    z = f_ref[0]
    sig = jax.nn.sigmoid(z)
    forget = lb + (1.0 - lb) * sig
    lg = jnp.log2(jnp.maximum(forget, MIN_FORGET))
    kk = (1.0 - lb) * (1.0 - sig)
    qq = jax.nn.silu(q_ref[0])
    vv = i_ref[0]

    tril = tril_ref[...]
    b = sum(_dot(tril, part) for part in _split2(lg))

    sides, masks = [], []
    row = lax.broadcasted_iota(jnp.int32, (C, C), 0)
    col = lax.broadcasted_iota(jnp.int32, (C, C), 1)
    sub = lax.broadcasted_iota(jnp.int32, (C, KW), 0)
    half = C // 2
    while half >= 4:
        blk = 2 * half
        b3 = b.reshape(C // blk, blk, KW)
        ref_row = jnp.broadcast_to(b3[:, half - 1:half, :], b3.shape).reshape(C, KW)
        e = jnp.exp2(-jnp.abs(b - ref_row))
        x = (jnp.where((sub & half) != 0, qq, kk) * e).astype(BF16)
        sides.append((x, x))
        shift = int(math.log2(blk))
        masks.append(((row >> shift) == (col >> shift)) & ((row & half) != 0) & ((col & half) == 0))
        half //= 2
    b3 = b.reshape(C // 8, 8, KW)
    mid_lo = 0.5 * (b3[:, 0:1, :] + b3[:, 3:4, :])
    mid_hi = 0.5 * (b3[:, 4:5, :] + b3[:, 7:8, :])
    sub8 = lax.broadcasted_iota(jnp.int32, b3.shape, 1)
    mid = jnp.where(sub8 < 4, mid_lo, mid_hi).reshape(C, KW)
    sides.append(((qq * jnp.exp2(b - mid)).astype(BF16), (kk * jnp.exp2(mid - b)).astype(BF16)))
    masks.append(((row >> 2) == (col >> 2)) & (col <= row))

    lane_v = lax.broadcasted_iota(jnp.int32, (C, HG_WIDTH), 1)
    o = _dot_nt((qq * jnp.exp2(b)).astype(BF16), st_ref[...].astype(BF16))
    for hd in range(HG_HEADS):
        ks = slice(hd * HG_KEY_DIM, (hd + 1) * HG_KEY_DIM)
        a = jnp.zeros((C, C), F32)
        for (qt, kt), m in zip(sides, masks):
            a = jnp.where(m, _dot_nt(qt[:, ks], kt[:, ks]), a)
        v_h = jnp.where((lane_v >> 6) == hd, vv, 0.0).astype(BF16)
        o = o + _dot(a.astype(BF16), v_h)

    b_last = b[C - 1:C, :]
    khat = (kk * jnp.exp2(b_last - b)).astype(BF16)
    st_row = lax.broadcasted_iota(jnp.int32, (HG_WIDTH, KW), 0)
    st_col = lax.broadcasted_iota(jnp.int32, (HG_WIDTH, KW), 1)
    new_st = st_ref[...] * jnp.exp2(b_last) + _dot_tn(vv.astype(BF16), khat)
    st_ref[...] = jnp.where((st_row >> 6) == (st_col >> 7), new_st, 0.0)

    grp = grp_ref[...]
    ssq = sum(_dot(part, grp) for part in _split2(o * o))
    on = o * lax.rsqrt(ssq * (1.0 / HG_VAL_DIM) + EPS) * on_ref[...]
    o_ref[0] = (on * jax.nn.silu(g_ref[0])).astype(o_ref.dtype)


def _hgrn(hg, lb, out_norm):
    B, S, _ = hg.shape
    C = min(HGRN_CHUNK, S)
    kb = HG_KEY_WIDTH // HG_KEY_WIDTH
    tril = jnp.asarray(np.tril(np.ones((C, C), np.float32)), BF16)
    lane_group = np.arange(HG_WIDTH) // HG_VAL_DIM
    grp = jnp.asarray((lane_group[:, None] == lane_group[None, :]).astype(np.float32), BF16)
    return pl.pallas_call(
        _hgrn_kernel,
        grid=(B, S // C),
        in_specs=[
            pl.BlockSpec((1, C, HG_KEY_WIDTH), lambda b, c: (b, c, 0)),
            pl.BlockSpec((1, C, HG_KEY_WIDTH), lambda b, c: (b, c, kb)),
            pl.BlockSpec((1, C, HG_WIDTH), lambda b, c: (b, c, 2 * HG_KEY_WIDTH // HG_WIDTH)),
            pl.BlockSpec((1, C, HG_WIDTH), lambda b, c: (b, c, 2 * HG_KEY_WIDTH // HG_WIDTH + 1)),
            _const_spec((1, HG_KEY_WIDTH)),
            _const_spec((1, HG_WIDTH)),
            _const_spec(tril.shape),
            _const_spec(grp.shape),
        ],
        out_specs=pl.BlockSpec((1, C, HG_WIDTH), lambda b, c: (b, c, 0)),
        out_shape=jax.ShapeDtypeStruct((B, S, HG_WIDTH), BF16),
        scratch_shapes=[pltpu.VMEM((HG_WIDTH, HG_KEY_WIDTH), F32)],
        compiler_params=pltpu.CompilerParams(dimension_semantics=("arbitrary", "arbitrary")),
        name="hgrn2",
    )(hg, hg, hg, hg, lb, out_norm, tril, grp)


def _attn_kernel(q_ref, kt_ref, v_ref, o_ref, s0_ref, s1_ref, acc_ref, *, bk):
    bq = q_ref.shape[2]
    nh = q_ref.shape[1]
    i = pl.program_id(2)
    qs = [q_ref[0, hd] for hd in range(nh)]

    def causal(s, key0):
        qry = lax.broadcasted_iota(jnp.int32, (bq, bk), 0)
        key = key0 + lax.broadcasted_iota(jnp.int32, (bq, bk), 1)
        return jnp.where(key <= qry, s, MASK_VALUE)

    def scores(blk, s_ref):
        start = pl.multiple_of(blk * bk, bk)
        for hd in range(nh):
            s_ref[hd] = _dot(qs[hd], kt_ref[0, hd, :, pl.ds(start, bk)])

    def softmax_pv(blk, s_ref, ms, key0=None):
        start = pl.multiple_of(blk * bk, bk)
        out = []
        for hd in range(nh):
            s = s_ref[hd]
            if key0 is not None:
                s = causal(s, key0)
            m_new = jnp.maximum(ms[hd], jnp.max(s, axis=1, keepdims=True))
            p = jnp.exp2(s - m_new).astype(BF16)
            acc_ref[hd] = jnp.exp2(ms[hd] - m_new) * acc_ref[hd] + _dot(p, v_ref[0, hd, pl.ds(start, bk), :])
            out.append(m_new)
        return tuple(out)

    acc_ref[...] = jnp.zeros_like(acc_ref)
    scores(0, s0_ref)

    def pair(t, ms):
        scores(2 * t + 1, s1_ref)
        ms = softmax_pv(2 * t, s0_ref, ms)
        scores(2 * t + 2, s0_ref)
        return softmax_pv(2 * t + 1, s1_ref, ms)

    def pairs(t0, n, c):
        for k in range(n):
            c = pair(t0 + k, c)
        return c

    ms = tuple(jnp.full((bq, 1), -jnp.inf, F32) for _ in range(nh))
    group = ATTN_PAIRS_PER_TRIP
    ms = lax.fori_loop(0, i // group, lambda t, c: pairs(group * t, group, c), ms)
    done = (i // group) * group
    while group > 1:
        group //= 2
        ms = lax.cond((i & group) != 0, functools.partial(pairs, done, group), lambda c: c, ms)
        done = done + (i & group)
    start1 = pl.multiple_of((2 * i + 1) * bk, bk)
    lower = [_dot(qs[hd][bk:, :], kt_ref[0, hd, :, pl.ds(start1, bk)]) for hd in range(nh)]
    ms = softmax_pv(2 * i, s0_ref, ms, 0)
    qry = lax.broadcasted_iota(jnp.int32, (bk, bk), 0)
    key = lax.broadcasted_iota(jnp.int32, (bk, bk), 1)
    for hd in range(nh):
        s = jnp.where(key <= qry, lower[hd], MASK_VALUE)
        m_old = ms[hd][bk:, :]
        m_new = jnp.maximum(m_old, jnp.max(s, axis=1, keepdims=True))
        p = jnp.exp2(s - m_new).astype(BF16)
        acc_ref[hd, bk:, :] = (jnp.exp2(m_old - m_new) * acc_ref[hd, bk:, :]
                               + _dot(p, v_ref[0, hd, pl.ds(start1, bk), :]))
    for hd in range(nh):
        acc = acc_ref[hd]
        o_ref[0, :, hd * MLA_V:(hd + 1) * MLA_V] = (acc[:, 0:MLA_V] / acc[:, MLA_V:MLA_V + 1]).astype(o_ref.dtype)


def _attention(q, kt, v):
    B, H, S, _ = q.shape
    bk = min(ATTN_BK, S // 2)
    bq = 2 * bk
    nh = ATTN_HEADS_PER_STEP
    resident = dict(pipeline_mode=pl.Buffered(1)) if nh > 1 else {}
    return pl.pallas_call(
        functools.partial(_attn_kernel, bk=bk),
        grid=(B, H // nh, S // bq),
        in_specs=[
            pl.BlockSpec((1, nh, bq, QK_PAD), lambda b, h, i: (b, h, i, 0)),
            pl.BlockSpec((1, nh, QK_PAD, S), lambda b, h, i: (b, h, 0, 0), **resident),
            pl.BlockSpec((1, nh, S, V_EXT), lambda b, h, i: (b, h, 0, 0), **resident),
        ],
        out_specs=pl.BlockSpec((1, bq, nh * MLA_V), lambda b, h, i: (b, i, h)),
        out_shape=jax.ShapeDtypeStruct((B, S, H * MLA_V), BF16),
        scratch_shapes=[pltpu.VMEM((nh, bq, bk), F32), pltpu.VMEM((nh, bq, bk), F32),
                        pltpu.VMEM((nh, bq, V_EXT), F32)],
        compiler_params=pltpu.CompilerParams(dimension_semantics=("arbitrary", "arbitrary", "arbitrary")),
        name="mla_attention",
    )(q, kt, v)


def _mix_out_kernel(x_ref, oa_ref, ob_ref, xp_ref, wpool_ref, pscale_ref, wo_ref, fn_ref,
                    xo_ref, h_ref, halo_ref, *, tile_rows):
    tm = xp_ref.shape[1]
    i = pl.program_id(1)

    @pl.when(i == 0)
    def _():
        halo_ref[...] = jnp.zeros_like(halo_ref)

    xp = xp_ref[0]
    xx = jnp.concatenate([halo_ref[...], xp], axis=0)
    halo_ref[...] = xp[tm - POOL_HALO:, :]

    w2 = xx[1:, :] + xx[:-1, :]
    w4 = w2[2:, :] + w2[:-2, :]
    w8 = w4[4:, :] + w4[:-4, :]
    w16 = w8[8:, :] + w8[:-8, :]
    sums = (w2[POOL_HALO - 1:, :], w4[POOL_HALO - 3:, :], w8[POOL_HALO - 7:, :], w16[POOL_HALO - 15:, :])
    t = i * tm + lax.broadcasted_iota(jnp.int32, (tm, POOL_WIDTH), 0)
    lane = lax.broadcasted_iota(jnp.int32, (tm, POOL_WIDTH), 1)
    pooled = jnp.zeros((tm, POOL_WIDTH), F32)
    for gi, w in enumerate(POOL_WINDOWS):
        cnt = jnp.minimum(t + 1, w).astype(F32)
        pooled = jnp.where((lane >> 6) == gi, sums[gi] / cnt, pooled)
    pooled = pooled - xp
    oc = _dot(pooled.astype(BF16), wpool_ref[...]) * pscale_ref[...]

    y = _dot(oa_ref[0], wo_ref[0:HG_WIDTH, :])
    y = y + _dot(ob_ref[0], wo_ref[HG_WIDTH:HG_WIDTH + MLA_WIDTH, :])
    y = y + _dot(oc.astype(BF16), wo_ref[HG_WIDTH + MLA_WIDTH:, :])
    xn = x_ref[0] + y
    xo_ref[0] = xn
    h = _rms(xn, fn_ref[...])
    if tile_rows:
        _store_tile_rows(h_ref, h)
    else:
        h_ref[0] = h.astype(h_ref.dtype)


def _mix_out(x, oa, ob, xp, wpool_bd, pscale, wo, fnorm, tile_rows):
    B, S, D = x.shape
    tm = min(TOKEN_TILE, S)
    nt = S // tm
    tok = lambda w: pl.BlockSpec((1, tm, w), lambda b, i: (b, i, 0))
    if tile_rows:
        assert D == SUBLANES * LANES
        h_spec = pl.BlockSpec((tm * SUBLANES, LANES), lambda b, i: (b * nt + i, 0))
        h_shape = jax.ShapeDtypeStruct((B * S * SUBLANES, LANES), F32)
    else:
        h_spec, h_shape = tok(D), jax.ShapeDtypeStruct((B, S, D), BF16)
    return pl.pallas_call(
        functools.partial(_mix_out_kernel, tile_rows=tile_rows),
        grid=(B, nt),
        in_specs=[tok(D), tok(HG_WIDTH), tok(MLA_WIDTH), tok(POOL_WIDTH),
                  _const_spec(wpool_bd.shape), _const_spec((1, POOL_WIDTH)), _const_spec(wo.shape),
                  _const_spec((1, D))],
        out_specs=(tok(D), h_spec),
        out_shape=(jax.ShapeDtypeStruct((B, S, D), F32), h_shape),
        scratch_shapes=[pltpu.VMEM((POOL_HALO, POOL_WIDTH), F32)],
        compiler_params=pltpu.CompilerParams(dimension_semantics=("arbitrary", "arbitrary")),
        name="mix_out",
    )(x, oa, ob, xp, wpool_bd, pscale, wo, fnorm)


def _dense_ffn_kernel(x_ref, h_ref, wg_ref, wu_ref, wd_ref, o_ref):
    h = h_ref[...]
    acc = x_ref[...]
    dff = wg_ref.shape[1]
    for c0 in range(0, dff, FF_CHUNK):
        g = _dot(h, wg_ref[:, c0:c0 + FF_CHUNK])
        u = _dot(h, wu_ref[:, c0:c0 + FF_CHUNK])
        acc = acc + _dot((jax.nn.silu(g) * u).astype(BF16), wd_ref[c0:c0 + FF_CHUNK, :])
    o_ref[...] = acc


def _dense_ffn(x2d, h2d, wg, wu, wd):
    T, D = x2d.shape
    tm = min(TOKEN_TILE, T)
    tok = pl.BlockSpec((tm, D), lambda i: (i, 0))
    return pl.pallas_call(
        _dense_ffn_kernel,
        grid=(T // tm,),
        in_specs=[tok, tok, _const_spec(wg.shape), _const_spec(wu.shape), _const_spec(wd.shape)],
        out_specs=tok,
        out_shape=jax.ShapeDtypeStruct((T, D), F32),
        compiler_params=pltpu.CompilerParams(dimension_semantics=("arbitrary",)),
        name="dense_ffn",
    )(x2d, h2d, wg, wu, wd)


def _router_kernel(h_ref, rt_ref, meta_ref, gate_ref, cnt_ref, run_ref):
    tm = h_ref.shape[0] // SUBLANES
    E = N_EXPERTS

    @pl.when(pl.program_id(0) == 0)
    def _():
        run_ref[...] = jnp.zeros_like(run_ref)

    logits = lax.dot_general(rt_ref[...], _load_tile_rows(h_ref, tm), (((1,), (1,)), ((), ())),
                             precision=lax.Precision.HIGHEST, preferred_element_type=F32)
    eid = lax.broadcasted_iota(jnp.int32, (E, tm), 0)
    m1 = jnp.max(logits, axis=0, keepdims=True)
    i1 = jnp.min(jnp.where(logits == m1, eid, E), axis=0, keepdims=True)
    rest = jnp.where(eid == i1, -jnp.inf, logits)
    m2 = jnp.max(rest, axis=0, keepdims=True)
    i2 = jnp.min(jnp.where(rest == m2, eid, E), axis=0, keepdims=True)
    e2 = jnp.exp(m2 - m1)
    g1 = 1.0 / (1.0 + e2)
    g2 = e2 / (1.0 + e2)

    sel = ((eid == i1) | (eid == i2))
    r = lax.broadcasted_iota(jnp.int32, (tm, tm), 0)
    c = lax.broadcasted_iota(jnp.int32, (tm, tm), 1)
    before = (r < c).astype(BF16)
    excl = _dot(sel.astype(BF16), before) + run_ref[:, 0:1]
    rank1 = jnp.sum(jnp.where(eid == i1, excl, 0.0), axis=0, keepdims=True).astype(jnp.int32)
    rank2 = jnp.sum(jnp.where(eid == i2, excl, 0.0), axis=0, keepdims=True).astype(jnp.int32)
    run_ref[...] = run_ref[...] + jnp.sum(sel.astype(F32), axis=1, keepdims=True)
    cnt_ref[...] = run_ref[...].astype(jnp.int32)

    zi = jnp.zeros((1, tm), jnp.int32)
    meta_ref[0] = jnp.concatenate([i1, i2, rank1, rank2, zi, zi, zi, zi], axis=0)
    zf = jnp.zeros((1, tm), F32)
    gate_ref[0] = jnp.concatenate([g1, g2, zf, zf, zf, zf, zf, zf], axis=0)


def _router(h_rows, router_t):
    T = h_rows.shape[0] // SUBLANES
    tm = min(ROUTE_TILE, T)
    nt = T // tm
    return pl.pallas_call(
        _router_kernel,
        grid=(nt,),
        in_specs=[pl.BlockSpec((tm * SUBLANES, LANES), lambda i: (i, 0)), _const_spec(router_t.shape)],
        out_specs=(pl.BlockSpec((1, SUBLANES, tm), lambda i: (i, 0, 0)),
                   pl.BlockSpec((1, SUBLANES, tm), lambda i: (i, 0, 0)),
                   _const_spec((N_EXPERTS, LANES))),
        out_shape=(jax.ShapeDtypeStruct((nt, SUBLANES, tm), jnp.int32),
                   jax.ShapeDtypeStruct((nt, SUBLANES, tm), F32),
                   jax.ShapeDtypeStruct((N_EXPERTS, LANES), jnp.int32)),
        scratch_shapes=[pltpu.VMEM((N_EXPERTS, LANES), F32)],
        compiler_params=pltpu.CompilerParams(dimension_semantics=("arbitrary",)),
        name="moe_router",
    )(h_rows, router_t)


def _dest_kernel(start_ref, meta_ref, dst_ref):
    meta = meta_ref[0]
    rows = []
    for k in range(2):
        e = meta[k:k + 1, :]
        base = jnp.zeros_like(e)
        for ex in range(N_EXPERTS):
            base = jnp.where(e == ex, start_ref[ex], base)
        rows.append(base + meta[2 + k:3 + k, :])
    dst_ref[0] = jnp.concatenate(rows + [jnp.zeros_like(rows[0])] * (SUBLANES - 2), axis=0)


def _dest_rows(starts, meta):
    nt, _, tm = meta.shape
    spec = pl.BlockSpec((1, SUBLANES, tm), lambda i, s: (i, 0, 0))
    return pl.pallas_call(
        _dest_kernel,
        grid_spec=pltpu.PrefetchScalarGridSpec(num_scalar_prefetch=1, grid=(nt,), in_specs=[spec], out_specs=spec),
        out_shape=jax.ShapeDtypeStruct(meta.shape, jnp.int32),
        compiler_params=pltpu.CompilerParams(dimension_semantics=("arbitrary",)),
        name="moe_dest",
    )(starts, meta)


def _scatter_kernel(zrow_ref, dst_ref, h_ref, xs_hbm, zero_ref, sem, zsem):
    tm = dst_ref.shape[2]

    @pl.when(pl.program_id(0) == 0)
    def _():
        zero_ref[...] = jnp.zeros_like(zero_ref)
        tile_rows = zero_ref.shape[0]

        def fill(row):
            start = pl.multiple_of(row * SUBLANES, SUBLANES)
            return pltpu.make_async_copy(zero_ref, xs_hbm.at[pl.ds(start, tile_rows), :], zsem)

        fills = [fill(zrow_ref[e]) for e in range(N_EXPERTS)]
        for f in fills:
            f.start()
        for f in fills:
            f.wait()
        n_tiles = xs_hbm.shape[0] // tile_rows
        for j in range(N_EXPERTS):
            @pl.when(n_tiles - 1 - j >= zrow_ref[N_EXPERTS])
            def _():
                f = fill((n_tiles - 1 - j) * (tile_rows // SUBLANES))
                f.start()
                f.wait()

    def row_copy(src_row, dst_row):
        src = pl.multiple_of(src_row * SUBLANES, SUBLANES)
        dst = pl.multiple_of(dst_row * SUBLANES, SUBLANES)
        return pltpu.make_async_copy(h_ref.at[pl.ds(src, SUBLANES), :], xs_hbm.at[pl.ds(dst, SUBLANES), :], sem)

    def issue(t, _):
        for k in range(2):
            row_copy(t, dst_ref[0, k, t]).start(priority=k)
        return 0

    lax.fori_loop(0, tm, issue, 0, unroll=DMA_UNROLL)

    def drain(t, _):
        row_copy(0, 0).wait()
        row_copy(0, 0).wait()
        return 0

    lax.fori_loop(0, tm, drain, 0, unroll=DMA_UNROLL)


def _scatter_rows(zero_rows, dst, h_rows, n_rows):
    nt, _, tm = dst.shape
    grid_spec = pltpu.PrefetchScalarGridSpec(
        num_scalar_prefetch=1,
        grid=(nt,),
        in_specs=[pl.BlockSpec((1, SUBLANES, tm), lambda i, z: (i, 0, 0), memory_space=pltpu.SMEM),
                  pl.BlockSpec((tm * SUBLANES, LANES), lambda i, z: (i, 0))],
        out_specs=pl.BlockSpec(memory_space=pl.ANY),
        scratch_shapes=[pltpu.VMEM((MOE_TILE * SUBLANES, LANES), F32), pltpu.SemaphoreType.DMA(()),
                        pltpu.SemaphoreType.DMA(())],
    )
    return pl.pallas_call(
        _scatter_kernel,
        grid_spec=grid_spec,
        out_shape=jax.ShapeDtypeStruct((n_rows * SUBLANES, LANES), F32),
        compiler_params=pltpu.CompilerParams(dimension_semantics=("arbitrary",), has_side_effects=True),
        name="moe_scatter",
    )(zero_rows, dst, h_rows)


def _expert_kernel(te_ref, nu_ref, x_ref, wg_ref, wu_ref, wd_ref, y_ref, *, fc):
    m = pl.program_id(0)
    tr = x_ref.shape[0] // SUBLANES
    dff = wg_ref.shape[2]

    @pl.when(m < nu_ref[0])
    def _():
        xb = _load_tile_rows(x_ref, tr).astype(BF16)
        acc = jnp.zeros((tr, wd_ref.shape[2]), F32)
        for c0 in range(0, dff, fc):
            g = _dot(xb, wg_ref[0, :, c0:c0 + fc])
            u = _dot(xb, wu_ref[0, :, c0:c0 + fc])
            acc = acc + _dot((jax.nn.silu(g) * u).astype(BF16), wd_ref[0, c0:c0 + fc, :])
        _store_tile_rows(y_ref, acc)

    @pl.when(m >= nu_ref[0])
    def _():
        y_ref[...] = jnp.zeros_like(y_ref)


def _expert_ffn(tile_expert, n_used, xs, wg, wu, wd):
    E, D, F = wg.shape
    R = xs.shape[0] // SUBLANES
    tr = MOE_TILE
    n_tiles = R // tr
    fc = min(MOE_FF_CHUNK, F)

    def row_map(m, te, nu):
        return (jnp.minimum(m, nu[0] - 1), 0)

    def w_map(m, te, nu):
        return (te[jnp.minimum(m, nu[0] - 1)], 0, 0)

    resident = dict(pipeline_mode=pl.Buffered(1))
    grid_spec = pltpu.PrefetchScalarGridSpec(
        num_scalar_prefetch=2,
        grid=(n_tiles,),
        in_specs=[
            pl.BlockSpec((tr * SUBLANES, LANES), row_map),
            pl.BlockSpec((1, D, F), w_map, **resident),
            pl.BlockSpec((1, D, F), w_map, **resident),
            pl.BlockSpec((1, F, D), w_map, **resident),
        ],
        out_specs=pl.BlockSpec((tr * SUBLANES, LANES), lambda m, te, nu: (m, 0)),
    )
    return pl.pallas_call(
        functools.partial(_expert_kernel, fc=fc),
        grid_spec=grid_spec,
        out_shape=jax.ShapeDtypeStruct(xs.shape, F32),
        compiler_params=pltpu.CompilerParams(dimension_semantics=("arbitrary",)),
        name="moe_experts",
    )(tile_expert, n_used, xs, wg, wu, wd)


def _combine_kernel(dcur_ref, dnxt_ref, x_ref, gate_ref, fn_ref, ys_hbm, o_ref, buf_ref, sems, *, final_norm):
    tm = x_ref.shape[0]
    i = pl.program_id(0)
    slot = i % 2

    def row_copy(src_row, s, k, t):
        src = pl.multiple_of(src_row * SUBLANES, SUBLANES)
        dst = pl.multiple_of(t * SUBLANES, SUBLANES)
        return pltpu.make_async_copy(ys_hbm.at[pl.ds(src, SUBLANES), :],
                                     buf_ref.at[s, k, pl.ds(dst, SUBLANES), :], sems.at[s])

    def issue_tile(d_ref, s):
        def issue(t, _):
            for k in range(2):
                row_copy(d_ref[0, k, t], s, k, t).start(priority=k)
            return 0

        lax.fori_loop(0, tm, issue, 0, unroll=DMA_UNROLL)

    @pl.when(i == 0)
    def _():
        issue_tile(dcur_ref, 0)

    @pl.when(i + 1 < pl.num_programs(0))
    def _():
        issue_tile(dnxt_ref, 1 - slot)

    def drain(t, _):
        row_copy(0, slot, 0, 0).wait()
        row_copy(0, slot, 1, 0).wait()
        return 0

    lax.fori_loop(0, tm, drain, 0, unroll=DMA_UNROLL)
    g = gate_ref[...]
    y0 = _load_tile_rows(buf_ref.at[slot, 0], tm)
    y1 = _load_tile_rows(buf_ref.at[slot, 1], tm)
    xn = x_ref[...] + g[:, 0:1] * y0 + g[:, 1:2] * y1
    o_ref[...] = _rms(xn, fn_ref[...]) if final_norm else xn


def _combine(dst_c, x2d, gates_col, fnorm, ys, final_norm):
    T, D = x2d.shape
    nt, _, tm = dst_c.shape
    return pl.pallas_call(
        functools.partial(_combine_kernel, final_norm=final_norm),
        grid=(nt,),
        in_specs=[pl.BlockSpec((1, SUBLANES, tm), lambda i: (i, 0, 0), memory_space=pltpu.SMEM),
                  pl.BlockSpec((1, SUBLANES, tm), lambda i: (jnp.minimum(i + 1, nt - 1), 0, 0),
                               memory_space=pltpu.SMEM),
                  pl.BlockSpec((tm, D), lambda i: (i, 0)),
                  pl.BlockSpec((tm, SUBLANES), lambda i: (i, 0)),
                  pl.BlockSpec((1, D), lambda i: (0, 0)),
                  pl.BlockSpec(memory_space=pl.ANY)],
        out_specs=pl.BlockSpec((tm, D), lambda i: (i, 0)),
        scratch_shapes=[pltpu.VMEM((2, 2, tm * SUBLANES, LANES), F32), pltpu.SemaphoreType.DMA((2,))],
        out_shape=jax.ShapeDtypeStruct((T, D), F32),
        compiler_params=pltpu.CompilerParams(dimension_semantics=("arbitrary",)),
        name="moe_combine",
    )(dst_c, dst_c, x2d, gates_col, fnorm, ys)


def _swap_halves_cols(w):
    half = w.shape[-1] // 2
    return jnp.concatenate([w[..., half:], w[..., :half]], axis=-1)


def _pad_cols(w, width):
    return jnp.pad(w, [(0, 0)] * (w.ndim - 1) + [(0, width - w.shape[-1])])


def _prep_w_in(w):
    kpe0 = _C_CKV + MLA_KV_RANK - 0
    kpe = w[:, kpe0:kpe0 + MLA_ROPE]
    return jnp.concatenate([
        w[:, :kpe0],
        _pad_cols(kpe, LANES), _pad_cols(_swap_halves_cols(kpe), LANES),
        w[:, kpe0 + MLA_ROPE:],
    ], axis=1).astype(BF16)


def _prep_w_uq(w):
    w = w.reshape(MLA_Q_RANK, MLA_HEADS, MLA_NOPE + MLA_ROPE)
    nope, pe = w[..., :MLA_NOPE], w[..., MLA_NOPE:]
    out = jnp.concatenate([nope, _pad_cols(pe, LANES), _pad_cols(_swap_halves_cols(pe), LANES)], axis=-1)
    return out.reshape(MLA_Q_RANK, MLA_HEADS * _Q_HEAD_COLS).astype(BF16)


def _prep_w_ukv(w):
    w = w.reshape(MLA_KV_RANK, MLA_HEADS, MLA_NOPE + MLA_V)
    wukt = jnp.transpose(w[..., :MLA_NOPE], (1, 2, 0)).reshape(MLA_HEADS * MLA_NOPE, MLA_KV_RANK)
    wuv = w[..., MLA_NOPE:].reshape(MLA_KV_RANK, MLA_HEADS * MLA_V)
    return wukt.astype(BF16), wuv.astype(BF16)


def _prep_w_kpe_t(w_ext):
    return w_ext[:, _C_KPE:_C_XP].T


def _rope_tables(seq):
    pos = jnp.arange(seq, dtype=F32)
    inv_freq = 1.0 / (ROPE_THETA ** (jnp.arange(0, MLA_ROPE, 2, dtype=F32) / MLA_ROPE))
    ang = pos[:, None] * inv_freq[None, :]
    cos, sin = jnp.cos(ang), jnp.sin(ang)
    cpad = _pad_cols(jnp.concatenate([cos, cos], axis=-1), LANES)
    spad = _pad_cols(jnp.concatenate([-sin, sin], axis=-1), LANES)
    return cpad, spad


def _block_diag_pool(w):
    G, c, _ = w.shape
    eye = jnp.eye(G, dtype=w.dtype)
    return (eye[:, None, :, None] * w[:, :, None, :]).reshape(G * c, G * c).astype(BF16)


def _moe_layout(counts, n_tiles, tile):
    tiles_per = (counts + tile - 1) // tile
    ends = jnp.cumsum(tiles_per)
    starts = (ends - tiles_per) * tile
    tile_expert = jnp.sum((jnp.arange(n_tiles)[:, None] >= ends[None, :]).astype(jnp.int32), axis=1)
    tile_expert = jnp.minimum(tile_expert, N_EXPERTS - 1)
    zero_rows = jnp.minimum(starts + counts, (n_tiles - 1) * tile)
    zero_info = jnp.concatenate([zero_rows, ends[-1:]])
    return (starts.astype(jnp.int32), tile_expert.astype(jnp.int32), ends[-1:].astype(jnp.int32),
            zero_info.astype(jnp.int32))


def kernel(x, attn_norm, w_in, hgrn_lower_bounds, hgrn_out_norm, mla_q_norm, mla_w_uq, mla_kv_norm,
           mla_w_ukv, pool_w, pool_scale, w_o, ffn_norm, dense_w_gate, dense_w_up, dense_w_down,
           moe_router, moe_w_gate, moe_w_up, moe_w_down, final_norm):
    B, S, D = x.shape
    T = B * S
    depth = w_in.shape[0]
    cpad, spad = _rope_tables(S)
    p_lb = jax.nn.softmax(hgrn_lower_bounds.astype(F32), axis=0)
    lbs = jnp.cumsum(p_lb, axis=0) - p_lb[0:1]

    assert depth % 2 == 0, "the final RMSNorm is fused into the last (MoE) layer's combine kernel"
    for l in range(depth):
        wukt, wuv = _prep_w_ukv(mla_w_ukv[l])
        w_ext = _prep_w_in(w_in[l])
        hg, xp, q, kt, v = _in_proj(
            x, attn_norm[l][None], w_ext, mla_q_norm[l][None], _prep_w_uq(mla_w_uq[l]),
            mla_kv_norm[l][None], wukt, wuv, _prep_w_kpe_t(w_ext), cpad, spad)
        o_a = _hgrn(hg, lbs[l][None], hgrn_out_norm[l][None])
        o_b = _attention(q, kt, v)
        moe_layer = (l % 2 == 1)
        x, h = _mix_out(x, o_a, o_b, xp, _block_diag_pool(pool_w[l]), pool_scale[l][None],
                        w_o[l].astype(BF16), ffn_norm[l][None], moe_layer)
        j = l // 2
        if not moe_layer:
            x = _dense_ffn(x.reshape(T, D), h.reshape(T, D), dense_w_gate[j].astype(BF16),
                           dense_w_up[j].astype(BF16), dense_w_down[j].astype(BF16)).reshape(B, S, D)
        else:
            meta, gates, counts = _router(h, moe_router[j].T)
            n_tiles = (2 * T) // MOE_TILE + N_EXPERTS
            starts, tile_expert, n_used, zero_rows = _moe_layout(counts[:, 0], n_tiles, MOE_TILE)
            dst = _dest_rows(starts, meta)
            xs = _scatter_rows(zero_rows, dst, h, n_tiles * MOE_TILE)
            ys = _expert_ffn(tile_expert, n_used, xs, moe_w_gate[j].astype(BF16), moe_w_up[j].astype(BF16),
                             moe_w_down[j].astype(BF16))
            ct = min(COMBINE_TILE, T)
            dst_c = dst.transpose(1, 0, 2).reshape(SUBLANES, T // ct, ct).transpose(1, 0, 2)
            gates_col = gates.transpose(0, 2, 1).reshape(T, SUBLANES)
            last = (l == depth - 1)
            y = _combine(dst_c, x.reshape(T, D), gates_col, final_norm[None], ys, last)
            x = y.reshape(B, S, D)
    return x
```

```python
import functools
import math

import jax
import jax.numpy as jnp
import numpy as np
from jax import lax
from jax.experimental import pallas as pl
from jax.experimental.pallas import tpu as pltpu

F32 = jnp.float32
BF16 = jnp.bfloat16

HG_HEADS = 4
HG_KEY_DIM = 128
HG_VAL_DIM = 64
HG_KEY_WIDTH = HG_HEADS * HG_KEY_DIM
HG_WIDTH = HG_HEADS * HG_VAL_DIM
MIN_FORGET = 1e-20
MLA_HEADS = 4
MLA_Q_RANK = 256
MLA_KV_RANK = 128
MLA_NOPE = 128
MLA_ROPE = 64
MLA_V = 128
MLA_WIDTH = MLA_HEADS * MLA_V
ROPE_THETA = 10000.0
MASK_VALUE = -1e30
POOL_GROUPS = 4
POOL_WINDOWS = (2, 4, 8, 16)
POOL_WIDTH = 256
POOL_GROUP_DIM = POOL_WIDTH // POOL_GROUPS
N_EXPERTS = 8
EPS = 1e-6

LANES = 128
SUBLANES = 8
QK_PAD = 256
V_EXT = 256

TOKEN_TILE = 1024
IN_PROJ_CHAINS = 1
HGRN_CHUNK = 256
ATTN_BK = 512
ATTN_HEADS_PER_STEP = 1
ATTN_PAIRS_PER_TRIP = 8
FF_CHUNK = 256
MOE_FF_CHUNK = 256
MOE_TILE = 1024
ROUTE_TILE = 512
COMBINE_TILE = 256
POOL_HALO = 16
DMA_UNROLL = 8

_C_HG = 0
_C_CQ = 2 * HG_KEY_WIDTH + 2 * HG_WIDTH
_C_CKV = _C_CQ + MLA_Q_RANK
_C_KPE = _C_CKV + MLA_KV_RANK
_C_KPES = _C_KPE + LANES
_C_XP = _C_KPES + LANES
_C_END = _C_XP + POOL_WIDTH
_Q_HEAD_COLS = 3 * LANES


def _rms(x, g):
    return x * lax.rsqrt(jnp.mean(x * x, axis=-1, keepdims=True) + EPS) * g


def _dot(a, b):
    return jnp.dot(a, b, preferred_element_type=F32)


def _dot_nt(a, b):
    return lax.dot_general(a, b, (((1,), (1,)), ((), ())), preferred_element_type=F32)


def _dot_tn(a, b):
    return lax.dot_general(a, b, (((0,), (0,)), ((), ())), preferred_element_type=F32)


def _const_spec(shape):
    nd = len(shape)
    return pl.BlockSpec(shape, lambda *_: (0,) * nd)


def _load_tile_rows(ref, n):
    return jnp.concatenate([ref[pl.ds(s, n, stride=SUBLANES), :] for s in range(SUBLANES)], axis=1)


def _store_tile_rows(ref, val):
    n = val.shape[0]
    for s in range(SUBLANES):
        ref[pl.ds(s, n, stride=SUBLANES), :] = val[:, s * LANES:(s + 1) * LANES]


def _in_proj_kernel(x_ref, g_ref, w_ref, qn_ref, wuq_ref, kvn_ref, wukt_ref, wuv_ref, wkpet_ref,
                    cpad_ref, spad_ref, cpadt_ref, spadt_ref,
                    hg_ref, xp_ref, q_ref, kt_ref, v_ref):
    scale = (MLA_NOPE + MLA_ROPE) ** -0.5 * math.log2(math.e)
    tm = x_ref.shape[1]
    rows = tm // IN_PROJ_CHAINS
    ones_col = (lax.broadcasted_iota(jnp.int32, (rows, LANES), 1) == 0).astype(BF16)

    for r0 in range(0, tm, rows):
        rs = slice(r0, r0 + rows)
        h = _rms(x_ref[0, rs, :], g_ref[...]).astype(BF16)
        hg_ref[0, rs, :] = _dot(h, w_ref[:, _C_HG:_C_CQ])
        xp_ref[0, rs, :] = _dot(h, w_ref[:, _C_XP:_C_END])
        cpad = cpad_ref[rs, :]
        spad = spad_ref[rs, :]

        cq = _dot(h, w_ref[:, _C_CQ:_C_CKV])
        cqn = _rms(cq, qn_ref[...]).astype(BF16)
        for hd in range(MLA_HEADS):
            qh = _dot(cqn, wuq_ref[:, hd * _Q_HEAD_COLS:(hd + 1) * _Q_HEAD_COLS])
            q_ref[0, hd, rs, 0:LANES] = (qh[:, 0:LANES] * scale).astype(BF16)
            pe = qh[:, LANES:2 * LANES] * cpad + qh[:, 2 * LANES:3 * LANES] * spad
            q_ref[0, hd, rs, LANES:QK_PAD] = (pe * scale).astype(BF16)

        ckv = _dot(h, w_ref[:, _C_CKV:_C_KPE])
        ckvn = _rms(ckv, kvn_ref[...]).astype(BF16)
        kpet = (_dot_nt(wkpet_ref[0:LANES, :], h) * cpadt_ref[:, rs]
                + _dot_nt(wkpet_ref[LANES:2 * LANES, :], h) * spadt_ref[:, rs]).astype(BF16)
        v_all = _dot(ckvn, wuv_ref[...]).astype(BF16)
        for hd in range(MLA_HEADS):
            kt_ref[0, hd, 0:LANES, rs] = _dot_nt(wukt_ref[hd * LANES:(hd + 1) * LANES, :], ckvn).astype(BF16)
            kt_ref[0, hd, LANES:QK_PAD, rs] = kpet
            v_ref[0, hd, rs, 0:MLA_V] = v_all[:, hd * MLA_V:(hd + 1) * MLA_V]
            v_ref[0, hd, rs, MLA_V:V_EXT] = ones_col


def _in_proj(x, g, w_ext, qn, wuq_ext, kvn, wukt, wuv, wkpet, cpad, spad):
    B, S, D = x.shape
    tm = min(TOKEN_TILE, S)
    grid = (B, S // tm)
    n_hg = _C_CQ
    out_shape = (
        jax.ShapeDtypeStruct((B, S, n_hg), F32),
        jax.ShapeDtypeStruct((B, S, POOL_WIDTH), F32),
        jax.ShapeDtypeStruct((B, MLA_HEADS, S, QK_PAD), BF16),
        jax.ShapeDtypeStruct((B, MLA_HEADS, QK_PAD, S), BF16),
        jax.ShapeDtypeStruct((B, MLA_HEADS, S, V_EXT), BF16),
    )
    return pl.pallas_call(
        _in_proj_kernel,
        grid=grid,
        in_specs=[
            pl.BlockSpec((1, tm, D), lambda b, i: (b, i, 0)),
            _const_spec((1, D)),
            _const_spec(w_ext.shape),
            _const_spec((1, MLA_Q_RANK)),
            _const_spec(wuq_ext.shape),
            _const_spec((1, MLA_KV_RANK)),
            _const_spec(wukt.shape),
            _const_spec(wuv.shape),
            _const_spec(wkpet.shape),
            pl.BlockSpec((tm, LANES), lambda b, i: (i, 0)),
            pl.BlockSpec((tm, LANES), lambda b, i: (i, 0)),
            pl.BlockSpec((LANES, tm), lambda b, i: (0, i)),
            pl.BlockSpec((LANES, tm), lambda b, i: (0, i)),
        ],
        out_specs=(
            pl.BlockSpec((1, tm, n_hg), lambda b, i: (b, i, 0)),
            pl.BlockSpec((1, tm, POOL_WIDTH), lambda b, i: (b, i, 0)),
            pl.BlockSpec((1, MLA_HEADS, tm, QK_PAD), lambda b, i: (b, 0, i, 0)),
            pl.BlockSpec((1, MLA_HEADS, QK_PAD, tm), lambda b, i: (b, 0, 0, i)),
            pl.BlockSpec((1, MLA_HEADS, tm, V_EXT), lambda b, i: (b, 0, i, 0)),
        ),
        out_shape=out_shape,
        compiler_params=pltpu.CompilerParams(dimension_semantics=("arbitrary", "arbitrary")),
        name="in_proj",
    )(x, g, w_ext, qn, wuq_ext, kvn, wukt, wuv, wkpet, cpad, spad, cpad.T, spad.T)


def _split2(x):
    hi = x.astype(BF16)
    return hi, (x - hi.astype(F32)).astype(BF16)


def _hgrn_kernel(q_ref, f_ref, i_ref, g_ref, lb_ref, on_ref, tril_ref, grp_ref, o_ref, st_ref):
    C = q_ref.shape[1]
    KW = HG_KEY_WIDTH

    @pl.when(pl.program_id(1) == 0)
    def _():
        st_ref[...] = jnp.zeros_like(st_ref)

    lb = lb_ref[...]
    z = f_ref[0]
    sig = jax.nn.sigmoid(z)
    forget = lb + (1.0 - lb) * sig
    lg = jnp.log2(jnp.maximum(forget, MIN_FORGET))
    kk = (1.0 - lb) * (1.0 - sig)
    qq = jax.nn.silu(q_ref[0])
    vv = i_ref[0]

    tril = tril_ref[...]
    b = sum(_dot(tril, part) for part in _split2(lg))

    sides, masks = [], []
    row = lax.broadcasted_iota(jnp.int32, (C, C), 0)
    col = lax.broadcasted_iota(jnp.int32, (C, C), 1)
    sub = lax.broadcasted_iota(jnp.int32, (C, KW), 0)
    half = C // 2
    while half >= 4:
        blk = 2 * half
        b3 = b.reshape(C // blk, blk, KW)
        ref_row = jnp.broadcast_to(b3[:, half - 1:half, :], b3.shape).reshape(C, KW)
        e = jnp.exp2(-jnp.abs(b - ref_row))
        x = (jnp.where((sub & half) != 0, qq, kk) * e).astype(BF16)
        sides.append((x, x))
        shift = int(math.log2(blk))
        masks.append(((row >> shift) == (col >> shift)) & ((row & half) != 0) & ((col & half) == 0))
        half //= 2
    b3 = b.reshape(C // 8, 8, KW)
    mid_lo = 0.5 * (b3[:, 0:1, :] + b3[:, 3:4, :])
    mid_hi = 0.5 * (b3[:, 4:5, :] + b3[:, 7:8, :])
    sub8 = lax.broadcasted_iota(jnp.int32, b3.shape, 1)
    mid = jnp.where(sub8 < 4, mid_lo, mid_hi).reshape(C, KW)
    sides.append(((qq * jnp.exp2(b - mid)).astype(BF16), (kk * jnp.exp2(mid - b)).astype(BF16)))
    masks.append(((row >> 2) == (col >> 2)) & (col <= row))

    lane_v = lax.broadcasted_iota(jnp.int32, (C, HG_WIDTH), 1)
    o = _dot_nt((qq * jnp.exp2(b)).astype(BF16), st_ref[...].astype(BF16))
    for hd in range(HG_HEADS):
        ks = slice(hd * HG_KEY_DIM, (hd + 1) * HG_KEY_DIM)
        a = jnp.zeros((C, C), F32)
        for (qt, kt), m in zip(sides, masks):
            a = jnp.where(m, _dot_nt(qt[:, ks], kt[:, ks]), a)
        v_h = jnp.where((lane_v >> 6) == hd, vv, 0.0).astype(BF16)
        o = o + _dot(a.astype(BF16), v_h)

    b_last = b[C - 1:C, :]
    khat = (kk * jnp.exp2(b_last - b)).astype(BF16)
    st_row = lax.broadcasted_iota(jnp.int32, (HG_WIDTH, KW), 0)
    st_col = lax.broadcasted_iota(jnp.int32, (HG_WIDTH, KW), 1)
    new_st = st_ref[...] * jnp.exp2(b_last) + _dot_tn(vv.astype(BF16), khat)
    st_ref[...] = jnp.where((st_row >> 6) == (st_col >> 7), new_st, 0.0)

    grp = grp_ref[...]
    ssq = sum(_dot(part, grp) for part in _split2(o * o))
    on = o * lax.rsqrt(ssq * (1.0 / HG_VAL_DIM) + EPS) * on_ref[...]
    o_ref[0] = (on * jax.nn.silu(g_ref[0])).astype(o_ref.dtype)


def _hgrn(hg, lb, out_norm):
    B, S, _ = hg.shape
    C = min(HGRN_CHUNK, S)
    kb = HG_KEY_WIDTH // HG_KEY_WIDTH
    tril = jnp.asarray(np.tril(np.ones((C, C), np.float32)), BF16)
    lane_group = np.arange(HG_WIDTH) // HG_VAL_DIM
    grp = jnp.asarray((lane_group[:, None] == lane_group[None, :]).astype(np.float32), BF16)
    return pl.pallas_call(
        _hgrn_kernel,
        grid=(B, S // C),
        in_specs=[
            pl.BlockSpec((1, C, HG_KEY_WIDTH), lambda b, c: (b, c, 0)),
            pl.BlockSpec((1, C, HG_KEY_WIDTH), lambda b, c: (b, c, kb)),
            pl.BlockSpec((1, C, HG_WIDTH), lambda b, c: (b, c, 2 * HG_KEY_WIDTH // HG_WIDTH)),
            pl.BlockSpec((1, C, HG_WIDTH), lambda b, c: (b, c, 2 * HG_KEY_WIDTH // HG_WIDTH + 1)),
            _const_spec((1, HG_KEY_WIDTH)),
            _const_spec((1, HG_WIDTH)),
            _const_spec(tril.shape),
            _const_spec(grp.shape),
        ],
        out_specs=pl.BlockSpec((1, C, HG_WIDTH), lambda b, c: (b, c, 0)),
        out_shape=jax.ShapeDtypeStruct((B, S, HG_WIDTH), BF16),
        scratch_shapes=[pltpu.VMEM((HG_WIDTH, HG_KEY_WIDTH), F32)],
        compiler_params=pltpu.CompilerParams(dimension_semantics=("arbitrary", "arbitrary")),
        name="hgrn2",
    )(hg, hg, hg, hg, lb, out_norm, tril, grp)


def _attn_kernel(q_ref, kt_ref, v_ref, o_ref, s0_ref, s1_ref, acc_ref, *, bk):
    bq = q_ref.shape[2]
    nh = q_ref.shape[1]
    i = pl.program_id(2)
    qs = [q_ref[0, hd] for hd in range(nh)]

    def causal(s, key0):
        qry = lax.broadcasted_iota(jnp.int32, (bq, bk), 0)
        key = key0 + lax.broadcasted_iota(jnp.int32, (bq, bk), 1)
        return jnp.where(key <= qry, s, MASK_VALUE)

    def scores(blk, s_ref):
        start = pl.multiple_of(blk * bk, bk)
        for hd in range(nh):
            s_ref[hd] = _dot(qs[hd], kt_ref[0, hd, :, pl.ds(start, bk)])

    def softmax_pv(blk, s_ref, ms, key0=None):
        start = pl.multiple_of(blk * bk, bk)
        out = []
        for hd in range(nh):
            s = s_ref[hd]
            if key0 is not None:
                s = causal(s, key0)
            m_new = jnp.maximum(ms[hd], jnp.max(s, axis=1, keepdims=True))
            p = jnp.exp2(s - m_new).astype(BF16)
            acc_ref[hd] = jnp.exp2(ms[hd] - m_new) * acc_ref[hd] + _dot(p, v_ref[0, hd, pl.ds(start, bk), :])
            out.append(m_new)
        return tuple(out)

    acc_ref[...] = jnp.zeros_like(acc_ref)
    scores(0, s0_ref)

    def pair(t, ms):
        scores(2 * t + 1, s1_ref)
        ms = softmax_pv(2 * t, s0_ref, ms)
        scores(2 * t + 2, s0_ref)
        return softmax_pv(2 * t + 1, s1_ref, ms)

    def pairs(t0, n, c):
        for k in range(n):
            c = pair(t0 + k, c)
        return c

    ms = tuple(jnp.full((bq, 1), -jnp.inf, F32) for _ in range(nh))
    group = ATTN_PAIRS_PER_TRIP
    ms = lax.fori_loop(0, i // group, lambda t, c: pairs(group * t, group, c), ms)
    done = (i // group) * group
    while group > 1:
        group //= 2
        ms = lax.cond((i & group) != 0, functools.partial(pairs, done, group), lambda c: c, ms)
        done = done + (i & group)
    start1 = pl.multiple_of((2 * i + 1) * bk, bk)
    lower = [_dot(qs[hd][bk:, :], kt_ref[0, hd, :, pl.ds(start1, bk)]) for hd in range(nh)]
    ms = softmax_pv(2 * i, s0_ref, ms, 0)
    qry = lax.broadcasted_iota(jnp.int32, (bk, bk), 0)
    key = lax.broadcasted_iota(jnp.int32, (bk, bk), 1)
    for hd in range(nh):
        s = jnp.where(key <= qry, lower[hd], MASK_VALUE)
        m_old = ms[hd][bk:, :]
        m_new = jnp.maximum(m_old, jnp.max(s, axis=1, keepdims=True))
        p = jnp.exp2(s - m_new).astype(BF16)
        acc_ref[hd, bk:, :] = (jnp.exp2(m_old - m_new) * acc_ref[hd, bk:, :]
                               + _dot(p, v_ref[0, hd, pl.ds(start1, bk), :]))
    for hd in range(nh):
        acc = acc_ref[hd]
        o_ref[0, :, hd * MLA_V:(hd + 1) * MLA_V] = (acc[:, 0:MLA_V] / acc[:, MLA_V:MLA_V + 1]).astype(o_ref.dtype)


def _attention(q, kt, v):
    B, H, S, _ = q.shape
    bk = min(ATTN_BK, S // 2)
    bq = 2 * bk
    nh = ATTN_HEADS_PER_STEP
    resident = dict(pipeline_mode=pl.Buffered(1)) if nh > 1 else {}
    return pl.pallas_call(
        functools.partial(_attn_kernel, bk=bk),
        grid=(B, H // nh, S // bq),
        in_specs=[
            pl.BlockSpec((1, nh, bq, QK_PAD), lambda b, h, i: (b, h, i, 0)),
            pl.BlockSpec((1, nh, QK_PAD, S), lambda b, h, i: (b, h, 0, 0), **resident),
            pl.BlockSpec((1, nh, S, V_EXT), lambda b, h, i: (b, h, 0, 0), **resident),
        ],
        out_specs=pl.BlockSpec((1, bq, nh * MLA_V), lambda b, h, i: (b, i, h)),
        out_shape=jax.ShapeDtypeStruct((B, S, H * MLA_V), BF16),
        scratch_shapes=[pltpu.VMEM((nh, bq, bk), F32), pltpu.VMEM((nh, bq, bk), F32),
                        pltpu.VMEM((nh, bq, V_EXT), F32)],
        compiler_params=pltpu.CompilerParams(dimension_semantics=("arbitrary", "arbitrary", "arbitrary")),
        name="mla_attention",
    )(q, kt, v)


def _mix_out_kernel(x_ref, oa_ref, ob_ref, xp_ref, wpool_ref, pscale_ref, wo_ref, fn_ref,
                    xo_ref, h_ref, halo_ref, *, tile_rows):
    tm = xp_ref.shape[1]
    i = pl.program_id(1)

    @pl.when(i == 0)
    def _():
        halo_ref[...] = jnp.zeros_like(halo_ref)

    xp = xp_ref[0]
    xx = jnp.concatenate([halo_ref[...], xp], axis=0)
    halo_ref[...] = xp[tm - POOL_HALO:, :]

    w2 = xx[1:, :] + xx[:-1, :]
    w4 = w2[2:, :] + w2[:-2, :]
    w8 = w4[4:, :] + w4[:-4, :]
    w16 = w8[8:, :] + w8[:-8, :]
    sums = (w2[POOL_HALO - 1:, :], w4[POOL_HALO - 3:, :], w8[POOL_HALO - 7:, :], w16[POOL_HALO - 15:, :])
    t = i * tm + lax.broadcasted_iota(jnp.int32, (tm, POOL_WIDTH), 0)
    lane = lax.broadcasted_iota(jnp.int32, (tm, POOL_WIDTH), 1)
    pooled = jnp.zeros((tm, POOL_WIDTH), F32)
    for gi, w in enumerate(POOL_WINDOWS):
        cnt = jnp.minimum(t + 1, w).astype(F32)
        pooled = jnp.where((lane >> 6) == gi, sums[gi] / cnt, pooled)
    pooled = pooled - xp
    oc = _dot(pooled.astype(BF16), wpool_ref[...]) * pscale_ref[...]

    y = _dot(oa_ref[0], wo_ref[0:HG_WIDTH, :])
    y = y + _dot(ob_ref[0], wo_ref[HG_WIDTH:HG_WIDTH + MLA_WIDTH, :])
    y = y + _dot(oc.astype(BF16), wo_ref[HG_WIDTH + MLA_WIDTH:, :])
    xn = x_ref[0] + y
    xo_ref[0] = xn
    h = _rms(xn, fn_ref[...])
    if tile_rows:
        _store_tile_rows(h_ref, h)
    else:
        h_ref[0] = h.astype(h_ref.dtype)


def _mix_out(x, oa, ob, xp, wpool_bd, pscale, wo, fnorm, tile_rows):
    B, S, D = x.shape
    tm = min(TOKEN_TILE, S)
    nt = S // tm
    tok = lambda w: pl.BlockSpec((1, tm, w), lambda b, i: (b, i, 0))
    if tile_rows:
        assert D == SUBLANES * LANES
        h_spec = pl.BlockSpec((tm * SUBLANES, LANES), lambda b, i: (b * nt + i, 0))
        h_shape = jax.ShapeDtypeStruct((B * S * SUBLANES, LANES), F32)
    else:
        h_spec, h_shape = tok(D), jax.ShapeDtypeStruct((B, S, D), BF16)
    return pl.pallas_call(
        functools.partial(_mix_out_kernel, tile_rows=tile_rows),
        grid=(B, nt),
        in_specs=[tok(D), tok(HG_WIDTH), tok(MLA_WIDTH), tok(POOL_WIDTH),
                  _const_spec(wpool_bd.shape), _const_spec((1, POOL_WIDTH)), _const_spec(wo.shape),
                  _const_spec((1, D))],
        out_specs=(tok(D), h_spec),
        out_shape=(jax.ShapeDtypeStruct((B, S, D), F32), h_shape),
        scratch_shapes=[pltpu.VMEM((POOL_HALO, POOL_WIDTH), F32)],
        compiler_params=pltpu.CompilerParams(dimension_semantics=("arbitrary", "arbitrary")),
        name="mix_out",
    )(x, oa, ob, xp, wpool_bd, pscale, wo, fnorm)


def _dense_ffn_kernel(x_ref, h_ref, wg_ref, wu_ref, wd_ref, o_ref):
    h = h_ref[...]
    acc = x_ref[...]
    dff = wg_ref.shape[1]
    for c0 in range(0, dff, FF_CHUNK):
        g = _dot(h, wg_ref[:, c0:c0 + FF_CHUNK])
        u = _dot(h, wu_ref[:, c0:c0 + FF_CHUNK])
        acc = acc + _dot((jax.nn.silu(g) * u).astype(BF16), wd_ref[c0:c0 + FF_CHUNK, :])
    o_ref[...] = acc


def _dense_ffn(x2d, h2d, wg, wu, wd):
    T, D = x2d.shape
    tm = min(TOKEN_TILE, T)
    tok = pl.BlockSpec((tm, D), lambda i: (i, 0))
    return pl.pallas_call(
        _dense_ffn_kernel,
        grid=(T // tm,),
        in_specs=[tok, tok, _const_spec(wg.shape), _const_spec(wu.shape), _const_spec(wd.shape)],
        out_specs=tok,
        out_shape=jax.ShapeDtypeStruct((T, D), F32),
        compiler_params=pltpu.CompilerParams(dimension_semantics=("arbitrary",)),
        name="dense_ffn",
    )(x2d, h2d, wg, wu, wd)


def _router_kernel(h_ref, rt_ref, meta_ref, gate_ref, cnt_ref, run_ref):
    tm = h_ref.shape[0] // SUBLANES
    E = N_EXPERTS

    @pl.when(pl.program_id(0) == 0)
    def _():
        run_ref[...] = jnp.zeros_like(run_ref)

    logits = lax.dot_general(rt_ref[...], _load_tile_rows(h_ref, tm), (((1,), (1,)), ((), ())),
                             precision=lax.Precision.HIGHEST, preferred_element_type=F32)
    eid = lax.broadcasted_iota(jnp.int32, (E, tm), 0)
    m1 = jnp.max(logits, axis=0, keepdims=True)
    i1 = jnp.min(jnp.where(logits == m1, eid, E), axis=0, keepdims=True)
    rest = jnp.where(eid == i1, -jnp.inf, logits)
    m2 = jnp.max(rest, axis=0, keepdims=True)
    i2 = jnp.min(jnp.where(rest == m2, eid, E), axis=0, keepdims=True)
    e2 = jnp.exp(m2 - m1)
    g1 = 1.0 / (1.0 + e2)
    g2 = e2 / (1.0 + e2)

    sel = ((eid == i1) | (eid == i2))
    r = lax.broadcasted_iota(jnp.int32, (tm, tm), 0)
    c = lax.broadcasted_iota(jnp.int32, (tm, tm), 1)
    before = (r < c).astype(BF16)
    excl = _dot(sel.astype(BF16), before) + run_ref[:, 0:1]
    rank1 = jnp.sum(jnp.where(eid == i1, excl, 0.0), axis=0, keepdims=True).astype(jnp.int32)
    rank2 = jnp.sum(jnp.where(eid == i2, excl, 0.0), axis=0, keepdims=True).astype(jnp.int32)
    run_ref[...] = run_ref[...] + jnp.sum(sel.astype(F32), axis=1, keepdims=True)
    cnt_ref[...] = run_ref[...].astype(jnp.int32)

    zi = jnp.zeros((1, tm), jnp.int32)
    meta_ref[0] = jnp.concatenate([i1, i2, rank1, rank2, zi, zi, zi, zi], axis=0)
    zf = jnp.zeros((1, tm), F32)
    gate_ref[0] = jnp.concatenate([g1, g2, zf, zf, zf, zf, zf, zf], axis=0)


def _router(h_rows, router_t):
    T = h_rows.shape[0] // SUBLANES
    tm = min(ROUTE_TILE, T)
    nt = T // tm
    return pl.pallas_call(
        _router_kernel,
        grid=(nt,),
        in_specs=[pl.BlockSpec((tm * SUBLANES, LANES), lambda i: (i, 0)), _const_spec(router_t.shape)],
        out_specs=(pl.BlockSpec((1, SUBLANES, tm), lambda i: (i, 0, 0)),
                   pl.BlockSpec((1, SUBLANES, tm), lambda i: (i, 0, 0)),
                   _const_spec((N_EXPERTS, LANES))),
        out_shape=(jax.ShapeDtypeStruct((nt, SUBLANES, tm), jnp.int32),
                   jax.ShapeDtypeStruct((nt, SUBLANES, tm), F32),
                   jax.ShapeDtypeStruct((N_EXPERTS, LANES), jnp.int32)),
        scratch_shapes=[pltpu.VMEM((N_EXPERTS, LANES), F32)],
        compiler_params=pltpu.CompilerParams(dimension_semantics=("arbitrary",)),
        name="moe_router",
    )(h_rows, router_t)


def _dest_kernel(start_ref, meta_ref, dst_ref):
    meta = meta_ref[0]
    rows = []
    for k in range(2):
        e = meta[k:k + 1, :]
        base = jnp.zeros_like(e)
        for ex in range(N_EXPERTS):
            base = jnp.where(e == ex, start_ref[ex], base)
        rows.append(base + meta[2 + k:3 + k, :])
    dst_ref[0] = jnp.concatenate(rows + [jnp.zeros_like(rows[0])] * (SUBLANES - 2), axis=0)


def _dest_rows(starts, meta):
    nt, _, tm = meta.shape
    spec = pl.BlockSpec((1, SUBLANES, tm), lambda i, s: (i, 0, 0))
    return pl.pallas_call(
        _dest_kernel,
        grid_spec=pltpu.PrefetchScalarGridSpec(num_scalar_prefetch=1, grid=(nt,), in_specs=[spec], out_specs=spec),
        out_shape=jax.ShapeDtypeStruct(meta.shape, jnp.int32),
        compiler_params=pltpu.CompilerParams(dimension_semantics=("arbitrary",)),
        name="moe_dest",
    )(starts, meta)


def _scatter_kernel(zrow_ref, dst_ref, h_ref, xs_hbm, zero_ref, sem, zsem):
    tm = dst_ref.shape[2]

    @pl.when(pl.program_id(0) == 0)
    def _():
        zero_ref[...] = jnp.zeros_like(zero_ref)
        tile_rows = zero_ref.shape[0]

        def fill(row):
            start = pl.multiple_of(row * SUBLANES, SUBLANES)
            return pltpu.make_async_copy(zero_ref, xs_hbm.at[pl.ds(start, tile_rows), :], zsem)

        fills = [fill(zrow_ref[e]) for e in range(N_EXPERTS)]
        for f in fills:
            f.start()
        for f in fills:
            f.wait()
        n_tiles = xs_hbm.shape[0] // tile_rows
        for j in range(N_EXPERTS):
            @pl.when(n_tiles - 1 - j >= zrow_ref[N_EXPERTS])
            def _():
                f = fill((n_tiles - 1 - j) * (tile_rows // SUBLANES))
                f.start()
                f.wait()

    def row_copy(src_row, dst_row):
        src = pl.multiple_of(src_row * SUBLANES, SUBLANES)
        dst = pl.multiple_of(dst_row * SUBLANES, SUBLANES)
        return pltpu.make_async_copy(h_ref.at[pl.ds(src, SUBLANES), :], xs_hbm.at[pl.ds(dst, SUBLANES), :], sem)

    def issue(t, _):
        for k in range(2):
            row_copy(t, dst_ref[0, k, t]).start(priority=k)
        return 0

    lax.fori_loop(0, tm, issue, 0, unroll=DMA_UNROLL)

    def drain(t, _):
        row_copy(0, 0).wait()
        row_copy(0, 0).wait()
        return 0

    lax.fori_loop(0, tm, drain, 0, unroll=DMA_UNROLL)


def _scatter_rows(zero_rows, dst, h_rows, n_rows):
    nt, _, tm = dst.shape
    grid_spec = pltpu.PrefetchScalarGridSpec(
        num_scalar_prefetch=1,
        grid=(nt,),
        in_specs=[pl.BlockSpec((1, SUBLANES, tm), lambda i, z: (i, 0, 0), memory_space=pltpu.SMEM),
                  pl.BlockSpec((tm * SUBLANES, LANES), lambda i, z: (i, 0))],
        out_specs=pl.BlockSpec(memory_space=pl.ANY),
        scratch_shapes=[pltpu.VMEM((MOE_TILE * SUBLANES, LANES), F32), pltpu.SemaphoreType.DMA(()),
                        pltpu.SemaphoreType.DMA(())],
    )
    return pl.pallas_call(
        _scatter_kernel,
        grid_spec=grid_spec,
        out_shape=jax.ShapeDtypeStruct((n_rows * SUBLANES, LANES), F32),
        compiler_params=pltpu.CompilerParams(dimension_semantics=("arbitrary",), has_side_effects=True),
        name="moe_scatter",
    )(zero_rows, dst, h_rows)


def _expert_kernel(te_ref, nu_ref, x_ref, wg_ref, wu_ref, wd_ref, y_ref, *, fc):
    m = pl.program_id(0)
    tr = x_ref.shape[0] // SUBLANES
    dff = wg_ref.shape[2]

    @pl.when(m < nu_ref[0])
    def _():
        xb = _load_tile_rows(x_ref, tr).astype(BF16)
        acc = jnp.zeros((tr, wd_ref.shape[2]), F32)
        for c0 in range(0, dff, fc):
            g = _dot(xb, wg_ref[0, :, c0:c0 + fc])
            u = _dot(xb, wu_ref[0, :, c0:c0 + fc])
            acc = acc + _dot((jax.nn.silu(g) * u).astype(BF16), wd_ref[0, c0:c0 + fc, :])
        _store_tile_rows(y_ref, acc)

    @pl.when(m >= nu_ref[0])
    def _():
        y_ref[...] = jnp.zeros_like(y_ref)


def _expert_ffn(tile_expert, n_used, xs, wg, wu, wd):
    E, D, F = wg.shape
    R = xs.shape[0] // SUBLANES
    tr = MOE_TILE
    n_tiles = R // tr
    fc = min(MOE_FF_CHUNK, F)

    def row_map(m, te, nu):
        return (jnp.minimum(m, nu[0] - 1), 0)

    def w_map(m, te, nu):
        return (te[jnp.minimum(m, nu[0] - 1)], 0, 0)

    resident = dict(pipeline_mode=pl.Buffered(1))
    grid_spec = pltpu.PrefetchScalarGridSpec(
        num_scalar_prefetch=2,
        grid=(n_tiles,),
        in_specs=[
            pl.BlockSpec((tr * SUBLANES, LANES), row_map),
            pl.BlockSpec((1, D, F), w_map, **resident),
            pl.BlockSpec((1, D, F), w_map, **resident),
            pl.BlockSpec((1, F, D), w_map, **resident),
        ],
        out_specs=pl.BlockSpec((tr * SUBLANES, LANES), lambda m, te, nu: (m, 0)),
    )
    return pl.pallas_call(
        functools.partial(_expert_kernel, fc=fc),
        grid_spec=grid_spec,
        out_shape=jax.ShapeDtypeStruct(xs.shape, F32),
        compiler_params=pltpu.CompilerParams(dimension_semantics=("arbitrary",)),
        name="moe_experts",
    )(tile_expert, n_used, xs, wg, wu, wd)


def _combine_kernel(dcur_ref, dnxt_ref, x_ref, gate_ref, fn_ref, ys_hbm, o_ref, buf_ref, sems, *, final_norm):
    tm = x_ref.shape[0]
    i = pl.program_id(0)
    slot = i % 2

    def row_copy(src_row, s, k, t):
        src = pl.multiple_of(src_row * SUBLANES, SUBLANES)
        dst = pl.multiple_of(t * SUBLANES, SUBLANES)
        return pltpu.make_async_copy(ys_hbm.at[pl.ds(src, SUBLANES), :],
                                     buf_ref.at[s, k, pl.ds(dst, SUBLANES), :], sems.at[s])

    def issue_tile(d_ref, s):
        def issue(t, _):
            for k in range(2):
                row_copy(d_ref[0, k, t], s, k, t).start(priority=k)
            return 0

        lax.fori_loop(0, tm, issue, 0, unroll=DMA_UNROLL)

    @pl.when(i == 0)
    def _():
        issue_tile(dcur_ref, 0)

    @pl.when(i + 1 < pl.num_programs(0))
    def _():
        issue_tile(dnxt_ref, 1 - slot)

    def drain(t, _):
        row_copy(0, slot, 0, 0).wait()
        row_copy(0, slot, 1, 0).wait()
        return 0

    lax.fori_loop(0, tm, drain, 0, unroll=DMA_UNROLL)
    g = gate_ref[...]
    y0 = _load_tile_rows(buf_ref.at[slot, 0], tm)
    y1 = _load_tile_rows(buf_ref.at[slot, 1], tm)
    xn = x_ref[...] + g[:, 0:1] * y0 + g[:, 1:2] * y1
    o_ref[...] = _rms(xn, fn_ref[...]) if final_norm else xn


def _combine(dst_c, x2d, gates_col, fnorm, ys, final_norm):
    T, D = x2d.shape
    nt, _, tm = dst_c.shape
    return pl.pallas_call(
        functools.partial(_combine_kernel, final_norm=final_norm),
        grid=(nt,),
        in_specs=[pl.BlockSpec((1, SUBLANES, tm), lambda i: (i, 0, 0), memory_space=pltpu.SMEM),
                  pl.BlockSpec((1, SUBLANES, tm), lambda i: (jnp.minimum(i + 1, nt - 1), 0, 0),
                               memory_space=pltpu.SMEM),
                  pl.BlockSpec((tm, D), lambda i: (i, 0)),
                  pl.BlockSpec((tm, SUBLANES), lambda i: (i, 0)),
                  pl.BlockSpec((1, D), lambda i: (0, 0)),
                  pl.BlockSpec(memory_space=pl.ANY)],
        out_specs=pl.BlockSpec((tm, D), lambda i: (i, 0)),
        scratch_shapes=[pltpu.VMEM((2, 2, tm * SUBLANES, LANES), F32), pltpu.SemaphoreType.DMA((2,))],
        out_shape=jax.ShapeDtypeStruct((T, D), F32),
        compiler_params=pltpu.CompilerParams(dimension_semantics=("arbitrary",)),
        name="moe_combine",
    )(dst_c, dst_c, x2d, gates_col, fnorm, ys)


def _swap_halves_cols(w):
    half = w.shape[-1] // 2
    return jnp.concatenate([w[..., half:], w[..., :half]], axis=-1)


def _pad_cols(w, width):
    return jnp.pad(w, [(0, 0)] * (w.ndim - 1) + [(0, width - w.shape[-1])])


def _prep_w_in(w):
    kpe0 = _C_CKV + MLA_KV_RANK - 0
    kpe = w[:, kpe0:kpe0 + MLA_ROPE]
    return jnp.concatenate([
        w[:, :kpe0],
        _pad_cols(kpe, LANES), _pad_cols(_swap_halves_cols(kpe), LANES),
        w[:, kpe0 + MLA_ROPE:],
    ], axis=1).astype(BF16)


def _prep_w_uq(w):
    w = w.reshape(MLA_Q_RANK, MLA_HEADS, MLA_NOPE + MLA_ROPE)
    nope, pe = w[..., :MLA_NOPE], w[..., MLA_NOPE:]
    out = jnp.concatenate([nope, _pad_cols(pe, LANES), _pad_cols(_swap_halves_cols(pe), LANES)], axis=-1)
    return out.reshape(MLA_Q_RANK, MLA_HEADS * _Q_HEAD_COLS).astype(BF16)


def _prep_w_ukv(w):
    w = w.reshape(MLA_KV_RANK, MLA_HEADS, MLA_NOPE + MLA_V)
    wukt = jnp.transpose(w[..., :MLA_NOPE], (1, 2, 0)).reshape(MLA_HEADS * MLA_NOPE, MLA_KV_RANK)
    wuv = w[..., MLA_NOPE:].reshape(MLA_KV_RANK, MLA_HEADS * MLA_V)
    return wukt.astype(BF16), wuv.astype(BF16)


def _prep_w_kpe_t(w_ext):
    return w_ext[:, _C_KPE:_C_XP].T


def _rope_tables(seq):
    pos = jnp.arange(seq, dtype=F32)
    inv_freq = 1.0 / (ROPE_THETA ** (jnp.arange(0, MLA_ROPE, 2, dtype=F32) / MLA_ROPE))
    ang = pos[:, None] * inv_freq[None, :]
    cos, sin = jnp.cos(ang), jnp.sin(ang)
    cpad = _pad_cols(jnp.concatenate([cos, cos], axis=-1), LANES)
    spad = _pad_cols(jnp.concatenate([-sin, sin], axis=-1), LANES)
    return cpad, spad


def _block_diag_pool(w):
    G, c, _ = w.shape
    eye = jnp.eye(G, dtype=w.dtype)
    return (eye[:, None, :, None] * w[:, :, None, :]).reshape(G * c, G * c).astype(BF16)


def _moe_layout(counts, n_tiles, tile):
    tiles_per = (counts + tile - 1) // tile
    ends = jnp.cumsum(tiles_per)
    starts = (ends - tiles_per) * tile
    tile_expert = jnp.sum((jnp.arange(n_tiles)[:, None] >= ends[None, :]).astype(jnp.int32), axis=1)
    tile_expert = jnp.minimum(tile_expert, N_EXPERTS - 1)
    zero_rows = jnp.minimum(starts + counts, (n_tiles - 1) * tile)
    zero_info = jnp.concatenate([zero_rows, ends[-1:]])
    return (starts.astype(jnp.int32), tile_expert.astype(jnp.int32), ends[-1:].astype(jnp.int32),
            zero_info.astype(jnp.int32))


def kernel(x, attn_norm, w_in, hgrn_lower_bounds, hgrn_out_norm, mla_q_norm, mla_w_uq, mla_kv_norm,
           mla_w_ukv, pool_w, pool_scale, w_o, ffn_norm, dense_w_gate, dense_w_up, dense_w_down,
           moe_router, moe_w_gate, moe_w_up, moe_w_down, final_norm):
    B, S, D = x.shape
    T = B * S
    depth = w_in.shape[0]
    cpad, spad = _rope_tables(S)
    p_lb = jax.nn.softmax(hgrn_lower_bounds.astype(F32), axis=0)
    lbs = jnp.cumsum(p_lb, axis=0) - p_lb[0:1]

    assert depth % 2 == 0, "the final RMSNorm is fused into the last (MoE) layer's combine kernel"
    for l in range(depth):
        wukt, wuv = _prep_w_ukv(mla_w_ukv[l])
        w_ext = _prep_w_in(w_in[l])
        hg, xp, q, kt, v = _in_proj(
            x, attn_norm[l][None], w_ext, mla_q_norm[l][None], _prep_w_uq(mla_w_uq[l]),
            mla_kv_norm[l][None], wukt, wuv, _prep_w_kpe_t(w_ext), cpad, spad)
        o_a = _hgrn(hg, lbs[l][None], hgrn_out_norm[l][None])
        o_b = _attention(q, kt, v)
        moe_layer = (l % 2 == 1)
        x, h = _mix_out(x, o_a, o_b, xp, _block_diag_pool(pool_w[l]), pool_scale[l][None],
                        w_o[l].astype(BF16), ffn_norm[l][None], moe_layer)
        j = l // 2
        if not moe_layer:
            x = _dense_ffn(x.reshape(T, D), h.reshape(T, D), dense_w_gate[j].astype(BF16),
                           dense_w_up[j].astype(BF16), dense_w_down[j].astype(BF16)).reshape(B, S, D)
        else:
            meta, gates, counts = _router(h, moe_router[j].T)
            n_tiles = (2 * T) // MOE_TILE + N_EXPERTS
            starts, tile_expert, n_used, zero_rows = _moe_layout(counts[:, 0], n_tiles, MOE_TILE)
            dst = _dest_rows(starts, meta)
            xs = _scatter_rows(zero_rows, dst, h, n_tiles * MOE_TILE)
            ys = _expert_ffn(tile_expert, n_used, xs, moe_w_gate[j].astype(BF16), moe_w_up[j].astype(BF16),
                             moe_w_down[j].astype(BF16))
            ct = min(COMBINE_TILE, T)
            dst_c = dst.transpose(1, 0, 2).reshape(SUBLANES, T // ct, ct).transpose(1, 0, 2)
            gates_col = gates.transpose(0, 2, 1).reshape(T, SUBLANES)
            last = (l == depth - 1)
            y = _combine(dst_c, x.reshape(T, D), gates_col, final_norm[None], ys, last)
            x = y.reshape(B, S, D)
    return x
```

```python
import functools
import math

import jax
import jax.numpy as jnp
import numpy as np
from jax import lax
from jax.experimental import pallas as pl
from jax.experimental.pallas import tpu as pltpu

F32 = jnp.float32
BF16 = jnp.bfloat16

HG_HEADS = 4
HG_KEY_DIM = 128
HG_VAL_DIM = 64
HG_KEY_WIDTH = HG_HEADS * HG_KEY_DIM
HG_WIDTH = HG_HEADS * HG_VAL_DIM
MIN_FORGET = 1e-20
MLA_HEADS = 4
MLA_Q_RANK = 256
MLA_KV_RANK = 128
MLA_NOPE = 128
MLA_ROPE = 64
MLA_V = 128
MLA_WIDTH = MLA_HEADS * MLA_V
ROPE_THETA = 10000.0
MASK_VALUE = -1e30
POOL_GROUPS = 4
POOL_WINDOWS = (2, 4, 8, 16)
POOL_WIDTH = 256
POOL_GROUP_DIM = POOL_WIDTH // POOL_GROUPS
N_EXPERTS = 8
EPS = 1e-6

LANES = 128
SUBLANES = 8
QK_PAD = 256
V_EXT = 256

TOKEN_TILE = 1024
IN_PROJ_CHAINS = 1
HGRN_CHUNK = 256
ATTN_BK = 512
ATTN_HEADS_PER_STEP = 1
ATTN_PAIRS_PER_TRIP = 8
FF_CHUNK = 256
MOE_FF_CHUNK = 256
MOE_TILE = 1024
ROUTE_TILE = 512
COMBINE_TILE = 512
POOL_HALO = 16
DMA_UNROLL = 8

_C_HG = 0
_C_CQ = 2 * HG_KEY_WIDTH + 2 * HG_WIDTH
_C_CKV = _C_CQ + MLA_Q_RANK
_C_KPE = _C_CKV + MLA_KV_RANK
_C_KPES = _C_KPE + LANES
_C_XP = _C_KPES + LANES
_C_END = _C_XP + POOL_WIDTH
_Q_HEAD_COLS = 3 * LANES


def _rms(x, g):
    return x * lax.rsqrt(jnp.mean(x * x, axis=-1, keepdims=True) + EPS) * g


def _dot(a, b):
    return jnp.dot(a, b, preferred_element_type=F32)


def _dot_nt(a, b):
    return lax.dot_general(a, b, (((1,), (1,)), ((), ())), preferred_element_type=F32)


def _dot_tn(a, b):
    return lax.dot_general(a, b, (((0,), (0,)), ((), ())), preferred_element_type=F32)


def _const_spec(shape):
    nd = len(shape)
    return pl.BlockSpec(shape, lambda *_: (0,) * nd)


def _load_tile_rows(ref, n):
    return jnp.concatenate([ref[pl.ds(s, n, stride=SUBLANES), :] for s in range(SUBLANES)], axis=1)


def _store_tile_rows(ref, val):
    n = val.shape[0]
    for s in range(SUBLANES):
        ref[pl.ds(s, n, stride=SUBLANES), :] = val[:, s * LANES:(s + 1) * LANES]


def _in_proj_kernel(x_ref, g_ref, w_ref, qn_ref, wuq_ref, kvn_ref, wukt_ref, wuv_ref, wkpet_ref,
                    cpad_ref, spad_ref, cpadt_ref, spadt_ref,
                    hg_ref, xp_ref, q_ref, kt_ref, v_ref):
    scale = (MLA_NOPE + MLA_ROPE) ** -0.5 * math.log2(math.e)
    tm = x_ref.shape[1]
    rows = tm // IN_PROJ_CHAINS
    ones_col = (lax.broadcasted_iota(jnp.int32, (rows, LANES), 1) == 0).astype(BF16)

    for r0 in range(0, tm, rows):
        rs = slice(r0, r0 + rows)
        h = _rms(x_ref[0, rs, :], g_ref[...]).astype(BF16)
        hg_ref[0, rs, :] = _dot(h, w_ref[:, _C_HG:_C_CQ])
        xp_ref[0, rs, :] = _dot(h, w_ref[:, _C_XP:_C_END])
        cpad = cpad_ref[rs, :]
        spad = spad_ref[rs, :]

        cq = _dot(h, w_ref[:, _C_CQ:_C_CKV])
        cqn = _rms(cq, qn_ref[...]).astype(BF16)
        for hd in range(MLA_HEADS):
            qh = _dot(cqn, wuq_ref[:, hd * _Q_HEAD_COLS:(hd + 1) * _Q_HEAD_COLS])
            q_ref[0, hd, rs, 0:LANES] = (qh[:, 0:LANES] * scale).astype(BF16)
            pe = qh[:, LANES:2 * LANES] * cpad + qh[:, 2 * LANES:3 * LANES] * spad
            q_ref[0, hd, rs, LANES:QK_PAD] = (pe * scale).astype(BF16)

        ckv = _dot(h, w_ref[:, _C_CKV:_C_KPE])
        ckvn = _rms(ckv, kvn_ref[...]).astype(BF16)
        kpet = (_dot_nt(wkpet_ref[0:LANES, :], h) * cpadt_ref[:, rs]
                + _dot_nt(wkpet_ref[LANES:2 * LANES, :], h) * spadt_ref[:, rs]).astype(BF16)
        v_all = _dot(ckvn, wuv_ref[...]).astype(BF16)
        for hd in range(MLA_HEADS):
            kt_ref[0, hd, 0:LANES, rs] = _dot_nt(wukt_ref[hd * LANES:(hd + 1) * LANES, :], ckvn).astype(BF16)
            kt_ref[0, hd, LANES:QK_PAD, rs] = kpet
            v_ref[0, hd, rs, 0:MLA_V] = v_all[:, hd * MLA_V:(hd + 1) * MLA_V]
            v_ref[0, hd, rs, MLA_V:V_EXT] = ones_col


def _in_proj(x, g, w_ext, qn, wuq_ext, kvn, wukt, wuv, wkpet, cpad, spad):
    B, S, D = x.shape
    tm = min(TOKEN_TILE, S)
    grid = (B, S // tm)
    n_hg = _C_CQ
    out_shape = (
        jax.ShapeDtypeStruct((B, S, n_hg), F32),
        jax.ShapeDtypeStruct((B, S, POOL_WIDTH), F32),
        jax.ShapeDtypeStruct((B, MLA_HEADS, S, QK_PAD), BF16),
        jax.ShapeDtypeStruct((B, MLA_HEADS, QK_PAD, S), BF16),
        jax.ShapeDtypeStruct((B, MLA_HEADS, S, V_EXT), BF16),
    )
    return pl.pallas_call(
        _in_proj_kernel,
        grid=grid,
        in_specs=[
            pl.BlockSpec((1, tm, D), lambda b, i: (b, i, 0)),
            _const_spec((1, D)),
            _const_spec(w_ext.shape),
            _const_spec((1, MLA_Q_RANK)),
            _const_spec(wuq_ext.shape),
            _const_spec((1, MLA_KV_RANK)),
            _const_spec(wukt.shape),
            _const_spec(wuv.shape),
            _const_spec(wkpet.shape),
            pl.BlockSpec((tm, LANES), lambda b, i: (i, 0)),
            pl.BlockSpec((tm, LANES), lambda b, i: (i, 0)),
            pl.BlockSpec((LANES, tm), lambda b, i: (0, i)),
            pl.BlockSpec((LANES, tm), lambda b, i: (0, i)),
        ],
        out_specs=(
            pl.BlockSpec((1, tm, n_hg), lambda b, i: (b, i, 0)),
            pl.BlockSpec((1, tm, POOL_WIDTH), lambda b, i: (b, i, 0)),
            pl.BlockSpec((1, MLA_HEADS, tm, QK_PAD), lambda b, i: (b, 0, i, 0)),
            pl.BlockSpec((1, MLA_HEADS, QK_PAD, tm), lambda b, i: (b, 0, 0, i)),
            pl.BlockSpec((1, MLA_HEADS, tm, V_EXT), lambda b, i: (b, 0, i, 0)),
        ),
        out_shape=out_shape,
        compiler_params=pltpu.CompilerParams(dimension_semantics=("arbitrary", "arbitrary")),
        name="in_proj",
    )(x, g, w_ext, qn, wuq_ext, kvn, wukt, wuv, wkpet, cpad, spad, cpad.T, spad.T)


def _split2(x):
    hi = x.astype(BF16)
    return hi, (x - hi.astype(F32)).astype(BF16)


def _hgrn_kernel(q_ref, f_ref, i_ref, g_ref, lb_ref, on_ref, tril_ref, grp_ref, o_ref, st_ref):
    C = q_ref.shape[1]
    KW = HG_KEY_WIDTH

    @pl.when(pl.program_id(1) == 0)
    def _():
        st_ref[...] = jnp.zeros_like(st_ref)

    lb = lb_ref[...]
    z = f_ref[0]
    sig = jax.nn.sigmoid(z)
    forget = lb + (1.0 - lb) * sig
    lg = jnp.log2(jnp.maximum(forget, MIN_FORGET))
    kk = (1.0 - lb) * (1.0 - sig)
    qq = jax.nn.silu(q_ref[0])
    vv = i_ref[0]

    tril = tril_ref[...]
    b = sum(_dot(tril, part) for part in _split2(lg))

    sides, masks = [], []
    row = lax.broadcasted_iota(jnp.int32, (C, C), 0)
    col = lax.broadcasted_iota(jnp.int32, (C, C), 1)
    sub = lax.broadcasted_iota(jnp.int32, (C, KW), 0)
    half = C // 2
    while half >= 4:
        blk = 2 * half
        b3 = b.reshape(C // blk, blk, KW)
        ref_row = jnp.broadcast_to(b3[:, half - 1:half, :], b3.shape).reshape(C, KW)
        e = jnp.exp2(-jnp.abs(b - ref_row))
        x = (jnp.where((sub & half) != 0, qq, kk) * e).astype(BF16)
        sides.append((x, x))
        shift = int(math.log2(blk))
        masks.append(((row >> shift) == (col >> shift)) & ((row & half) != 0) & ((col & half) == 0))
        half //= 2
    b3 = b.reshape(C // 8, 8, KW)
    mid_lo = 0.5 * (b3[:, 0:1, :] + b3[:, 3:4, :])
    mid_hi = 0.5 * (b3[:, 4:5, :] + b3[:, 7:8, :])
    sub8 = lax.broadcasted_iota(jnp.int32, b3.shape, 1)
    mid = jnp.where(sub8 < 4, mid_lo, mid_hi).reshape(C, KW)
    sides.append(((qq * jnp.exp2(b - mid)).astype(BF16), (kk * jnp.exp2(mid - b)).astype(BF16)))
    masks.append(((row >> 2) == (col >> 2)) & (col <= row))

    lane_v = lax.broadcasted_iota(jnp.int32, (C, HG_WIDTH), 1)
    o = _dot_nt((qq * jnp.exp2(b)).astype(BF16), st_ref[...].astype(BF16))
    for hd in range(HG_HEADS):
        ks = slice(hd * HG_KEY_DIM, (hd + 1) * HG_KEY_DIM)
        a = jnp.zeros((C, C), F32)
        for (qt, kt), m in zip(sides, masks):
            a = jnp.where(m, _dot_nt(qt[:, ks], kt[:, ks]), a)
        v_h = jnp.where((lane_v >> 6) == hd, vv, 0.0).astype(BF16)
        o = o + _dot(a.astype(BF16), v_h)

    b_last = b[C - 1:C, :]
    khat = (kk * jnp.exp2(b_last - b)).astype(BF16)
    st_row = lax.broadcasted_iota(jnp.int32, (HG_WIDTH, KW), 0)
    st_col = lax.broadcasted_iota(jnp.int32, (HG_WIDTH, KW), 1)
    new_st = st_ref[...] * jnp.exp2(b_last) + _dot_tn(vv.astype(BF16), khat)
    st_ref[...] = jnp.where((st_row >> 6) == (st_col >> 7), new_st, 0.0)

    grp = grp_ref[...]
    ssq = sum(_dot(part, grp) for part in _split2(o * o))
    on = o * lax.rsqrt(ssq * (1.0 / HG_VAL_DIM) + EPS) * on_ref[...]
    o_ref[0] = (on * jax.nn.silu(g_ref[0])).astype(o_ref.dtype)


def _hgrn(hg, lb, out_norm):
    B, S, _ = hg.shape
    C = min(HGRN_CHUNK, S)
    kb = HG_KEY_WIDTH // HG_KEY_WIDTH
    tril = jnp.asarray(np.tril(np.ones((C, C), np.float32)), BF16)
    lane_group = np.arange(HG_WIDTH) // HG_VAL_DIM
    grp = jnp.asarray((lane_group[:, None] == lane_group[None, :]).astype(np.float32), BF16)
    return pl.pallas_call(
        _hgrn_kernel,
        grid=(B, S // C),
        in_specs=[
            pl.BlockSpec((1, C, HG_KEY_WIDTH), lambda b, c: (b, c, 0)),
            pl.BlockSpec((1, C, HG_KEY_WIDTH), lambda b, c: (b, c, kb)),
            pl.BlockSpec((1, C, HG_WIDTH), lambda b, c: (b, c, 2 * HG_KEY_WIDTH // HG_WIDTH)),
            pl.BlockSpec((1, C, HG_WIDTH), lambda b, c: (b, c, 2 * HG_KEY_WIDTH // HG_WIDTH + 1)),
            _const_spec((1, HG_KEY_WIDTH)),
            _const_spec((1, HG_WIDTH)),
            _const_spec(tril.shape),
            _const_spec(grp.shape),
        ],
        out_specs=pl.BlockSpec((1, C, HG_WIDTH), lambda b, c: (b, c, 0)),
        out_shape=jax.ShapeDtypeStruct((B, S, HG_WIDTH), BF16),
        scratch_shapes=[pltpu.VMEM((HG_WIDTH, HG_KEY_WIDTH), F32)],
        compiler_params=pltpu.CompilerParams(dimension_semantics=("arbitrary", "arbitrary")),
        name="hgrn2",
    )(hg, hg, hg, hg, lb, out_norm, tril, grp)


def _attn_kernel(q_ref, kt_ref, v_ref, o_ref, s0_ref, s1_ref, acc_ref, *, bk):
    bq = q_ref.shape[2]
    nh = q_ref.shape[1]
    i = pl.program_id(2)
    qs = [q_ref[0, hd] for hd in range(nh)]

    def causal(s, key0):
        qry = lax.broadcasted_iota(jnp.int32, (bq, bk), 0)
        key = key0 + lax.broadcasted_iota(jnp.int32, (bq, bk), 1)
        return jnp.where(key <= qry, s, MASK_VALUE)

    def scores(blk, s_ref):
        start = pl.multiple_of(blk * bk, bk)
        for hd in range(nh):
            s_ref[hd] = _dot(qs[hd], kt_ref[0, hd, :, pl.ds(start, bk)])

    def softmax_pv(blk, s_ref, ms, key0=None):
        start = pl.multiple_of(blk * bk, bk)
        out = []
        for hd in range(nh):
            s = s_ref[hd]
            if key0 is not None:
                s = causal(s, key0)
            m_new = jnp.maximum(ms[hd], jnp.max(s, axis=1, keepdims=True))
            p = jnp.exp2(s - m_new).astype(BF16)
            acc_ref[hd] = jnp.exp2(ms[hd] - m_new) * acc_ref[hd] + _dot(p, v_ref[0, hd, pl.ds(start, bk), :])
            out.append(m_new)
        return tuple(out)

    acc_ref[...] = jnp.zeros_like(acc_ref)
    scores(0, s0_ref)

    def pair(t, ms):
        scores(2 * t + 1, s1_ref)
        ms = softmax_pv(2 * t, s0_ref, ms)
        scores(2 * t + 2, s0_ref)
        return softmax_pv(2 * t + 1, s1_ref, ms)

    def pairs(t0, n, c):
        for k in range(n):
            c = pair(t0 + k, c)
        return c

    ms = tuple(jnp.full((bq, 1), -jnp.inf, F32) for _ in range(nh))
    group = ATTN_PAIRS_PER_TRIP
    ms = lax.fori_loop(0, i // group, lambda t, c: pairs(group * t, group, c), ms)
    done = (i // group) * group
    while group > 1:
        group //= 2
        ms = lax.cond((i & group) != 0, functools.partial(pairs, done, group), lambda c: c, ms)
        done = done + (i & group)
    start1 = pl.multiple_of((2 * i + 1) * bk, bk)
    lower = [_dot(qs[hd][bk:, :], kt_ref[0, hd, :, pl.ds(start1, bk)]) for hd in range(nh)]
    ms = softmax_pv(2 * i, s0_ref, ms, 0)
    qry = lax.broadcasted_iota(jnp.int32, (bk, bk), 0)
    key = lax.broadcasted_iota(jnp.int32, (bk, bk), 1)
    for hd in range(nh):
        s = jnp.where(key <= qry, lower[hd], MASK_VALUE)
        m_old = ms[hd][bk:, :]
        m_new = jnp.maximum(m_old, jnp.max(s, axis=1, keepdims=True))
        p = jnp.exp2(s - m_new).astype(BF16)
        acc_ref[hd, bk:, :] = (jnp.exp2(m_old - m_new) * acc_ref[hd, bk:, :]
                               + _dot(p, v_ref[0, hd, pl.ds(start1, bk), :]))
    for hd in range(nh):
        acc = acc_ref[hd]
        o_ref[0, :, hd * MLA_V:(hd + 1) * MLA_V] = (acc[:, 0:MLA_V] / acc[:, MLA_V:MLA_V + 1]).astype(o_ref.dtype)


def _attention(q, kt, v):
    B, H, S, _ = q.shape
    bk = min(ATTN_BK, S // 2)
    bq = 2 * bk
    nh = ATTN_HEADS_PER_STEP
    resident = dict(pipeline_mode=pl.Buffered(1)) if nh > 1 else {}
    return pl.pallas_call(
        functools.partial(_attn_kernel, bk=bk),
        grid=(B, H // nh, S // bq),
        in_specs=[
            pl.BlockSpec((1, nh, bq, QK_PAD), lambda b, h, i: (b, h, i, 0)),
            pl.BlockSpec((1, nh, QK_PAD, S), lambda b, h, i: (b, h, 0, 0), **resident),
            pl.BlockSpec((1, nh, S, V_EXT), lambda b, h, i: (b, h, 0, 0), **resident),
        ],
        out_specs=pl.BlockSpec((1, bq, nh * MLA_V), lambda b, h, i: (b, i, h)),
        out_shape=jax.ShapeDtypeStruct((B, S, H * MLA_V), BF16),
        scratch_shapes=[pltpu.VMEM((nh, bq, bk), F32), pltpu.VMEM((nh, bq, bk), F32),
                        pltpu.VMEM((nh, bq, V_EXT), F32)],
        compiler_params=pltpu.CompilerParams(dimension_semantics=("arbitrary", "arbitrary", "arbitrary")),
        name="mla_attention",
    )(q, kt, v)


def _mix_out_kernel(x_ref, oa_ref, ob_ref, xp_ref, wpool_ref, pscale_ref, wo_ref, fn_ref,
                    xo_ref, h_ref, halo_ref, *, tile_rows):
    tm = xp_ref.shape[1]
    i = pl.program_id(1)

    @pl.when(i == 0)
    def _():
        halo_ref[...] = jnp.zeros_like(halo_ref)

    xp = xp_ref[0]
    xx = jnp.concatenate([halo_ref[...], xp], axis=0)
    halo_ref[...] = xp[tm - POOL_HALO:, :]

    w2 = xx[1:, :] + xx[:-1, :]
    w4 = w2[2:, :] + w2[:-2, :]
    w8 = w4[4:, :] + w4[:-4, :]
    w16 = w8[8:, :] + w8[:-8, :]
    sums = (w2[POOL_HALO - 1:, :], w4[POOL_HALO - 3:, :], w8[POOL_HALO - 7:, :], w16[POOL_HALO - 15:, :])
    t = i * tm + lax.broadcasted_iota(jnp.int32, (tm, POOL_WIDTH), 0)
    lane = lax.broadcasted_iota(jnp.int32, (tm, POOL_WIDTH), 1)
    pooled = jnp.zeros((tm, POOL_WIDTH), F32)
    for gi, w in enumerate(POOL_WINDOWS):
        cnt = jnp.minimum(t + 1, w).astype(F32)
        pooled = jnp.where((lane >> 6) == gi, sums[gi] / cnt, pooled)
    pooled = pooled - xp
    oc = _dot(pooled.astype(BF16), wpool_ref[...]) * pscale_ref[...]

    y = _dot(oa_ref[0], wo_ref[0:HG_WIDTH, :])
    y = y + _dot(ob_ref[0], wo_ref[HG_WIDTH:HG_WIDTH + MLA_WIDTH, :])
    y = y + _dot(oc.astype(BF16), wo_ref[HG_WIDTH + MLA_WIDTH:, :])
    xn = x_ref[0] + y
    xo_ref[0] = xn
    h = _rms(xn, fn_ref[...])
    if tile_rows:
        _store_tile_rows(h_ref, h)
    else:
        h_ref[0] = h.astype(h_ref.dtype)


def _mix_out(x, oa, ob, xp, wpool_bd, pscale, wo, fnorm, tile_rows):
    B, S, D = x.shape
    tm = min(TOKEN_TILE, S)
    nt = S // tm
    tok = lambda w: pl.BlockSpec((1, tm, w), lambda b, i: (b, i, 0))
    if tile_rows:
        assert D == SUBLANES * LANES
        h_spec = pl.BlockSpec((tm * SUBLANES, LANES), lambda b, i: (b * nt + i, 0))
        h_shape = jax.ShapeDtypeStruct((B * S * SUBLANES, LANES), F32)
    else:
        h_spec, h_shape = tok(D), jax.ShapeDtypeStruct((B, S, D), BF16)
    return pl.pallas_call(
        functools.partial(_mix_out_kernel, tile_rows=tile_rows),
        grid=(B, nt),
        in_specs=[tok(D), tok(HG_WIDTH), tok(MLA_WIDTH), tok(POOL_WIDTH),
                  _const_spec(wpool_bd.shape), _const_spec((1, POOL_WIDTH)), _const_spec(wo.shape),
                  _const_spec((1, D))],
        out_specs=(tok(D), h_spec),
        out_shape=(jax.ShapeDtypeStruct((B, S, D), F32), h_shape),
        scratch_shapes=[pltpu.VMEM((POOL_HALO, POOL_WIDTH), F32)],
        compiler_params=pltpu.CompilerParams(dimension_semantics=("arbitrary", "arbitrary")),
        name="mix_out",
    )(x, oa, ob, xp, wpool_bd, pscale, wo, fnorm)


def _dense_ffn_kernel(x_ref, h_ref, wg_ref, wu_ref, wd_ref, o_ref):
    h = h_ref[...]
    acc = x_ref[...]
    dff = wg_ref.shape[1]
    for c0 in range(0, dff, FF_CHUNK):
        g = _dot(h, wg_ref[:, c0:c0 + FF_CHUNK])
        u = _dot(h, wu_ref[:, c0:c0 + FF_CHUNK])
        acc = acc + _dot((jax.nn.silu(g) * u).astype(BF16), wd_ref[c0:c0 + FF_CHUNK, :])
    o_ref[...] = acc


def _dense_ffn(x2d, h2d, wg, wu, wd):
    T, D = x2d.shape
    tm = min(TOKEN_TILE, T)
    tok = pl.BlockSpec((tm, D), lambda i: (i, 0))
    return pl.pallas_call(
        _dense_ffn_kernel,
        grid=(T // tm,),
        in_specs=[tok, tok, _const_spec(wg.shape), _const_spec(wu.shape), _const_spec(wd.shape)],
        out_specs=tok,
        out_shape=jax.ShapeDtypeStruct((T, D), F32),
        compiler_params=pltpu.CompilerParams(dimension_semantics=("arbitrary",)),
        name="dense_ffn",
    )(x2d, h2d, wg, wu, wd)


def _router_kernel(h_ref, rt_ref, meta_ref, gate_ref, cnt_ref, run_ref):
    tm = h_ref.shape[0] // SUBLANES
    E = N_EXPERTS

    @pl.when(pl.program_id(0) == 0)
    def _():
        run_ref[...] = jnp.zeros_like(run_ref)

    logits = lax.dot_general(rt_ref[...], _load_tile_rows(h_ref, tm), (((1,), (1,)), ((), ())),
                             precision=lax.Precision.HIGHEST, preferred_element_type=F32)
    eid = lax.broadcasted_iota(jnp.int32, (E, tm), 0)
    m1 = jnp.max(logits, axis=0, keepdims=True)
    i1 = jnp.min(jnp.where(logits == m1, eid, E), axis=0, keepdims=True)
    rest = jnp.where(eid == i1, -jnp.inf, logits)
    m2 = jnp.max(rest, axis=0, keepdims=True)
    i2 = jnp.min(jnp.where(rest == m2, eid, E), axis=0, keepdims=True)
    e2 = jnp.exp(m2 - m1)
    g1 = 1.0 / (1.0 + e2)
    g2 = e2 / (1.0 + e2)

    sel = ((eid == i1) | (eid == i2))
    r = lax.broadcasted_iota(jnp.int32, (tm, tm), 0)
    c = lax.broadcasted_iota(jnp.int32, (tm, tm), 1)
    before = (r < c).astype(BF16)
    excl = _dot(sel.astype(BF16), before) + run_ref[:, 0:1]
    rank1 = jnp.sum(jnp.where(eid == i1, excl, 0.0), axis=0, keepdims=True).astype(jnp.int32)
    rank2 = jnp.sum(jnp.where(eid == i2, excl, 0.0), axis=0, keepdims=True).astype(jnp.int32)
    run_ref[...] = run_ref[...] + jnp.sum(sel.astype(F32), axis=1, keepdims=True)
    cnt_ref[...] = run_ref[...].astype(jnp.int32)

    zi = jnp.zeros((1, tm), jnp.int32)
    meta_ref[0] = jnp.concatenate([i1, i2, rank1, rank2, zi, zi, zi, zi], axis=0)
    zf = jnp.zeros((1, tm), F32)
    gate_ref[0] = jnp.concatenate([g1, g2, zf, zf, zf, zf, zf, zf], axis=0)


def _router(h_rows, router_t):
    T = h_rows.shape[0] // SUBLANES
    tm = min(ROUTE_TILE, T)
    nt = T // tm
    return pl.pallas_call(
        _router_kernel,
        grid=(nt,),
        in_specs=[pl.BlockSpec((tm * SUBLANES, LANES), lambda i: (i, 0)), _const_spec(router_t.shape)],
        out_specs=(pl.BlockSpec((1, SUBLANES, tm), lambda i: (i, 0, 0)),
                   pl.BlockSpec((1, SUBLANES, tm), lambda i: (i, 0, 0)),
                   _const_spec((N_EXPERTS, LANES))),
        out_shape=(jax.ShapeDtypeStruct((nt, SUBLANES, tm), jnp.int32),
                   jax.ShapeDtypeStruct((nt, SUBLANES, tm), F32),
                   jax.ShapeDtypeStruct((N_EXPERTS, LANES), jnp.int32)),
        scratch_shapes=[pltpu.VMEM((N_EXPERTS, LANES), F32)],
        compiler_params=pltpu.CompilerParams(dimension_semantics=("arbitrary",)),
        name="moe_router",
    )(h_rows, router_t)


def _dest_kernel(start_ref, meta_ref, dst_ref):
    meta = meta_ref[0]
    rows = []
    for k in range(2):
        e = meta[k:k + 1, :]
        base = jnp.zeros_like(e)
        for ex in range(N_EXPERTS):
            base = jnp.where(e == ex, start_ref[ex], base)
        rows.append(base + meta[2 + k:3 + k, :])
    dst_ref[0] = jnp.concatenate(rows + [jnp.zeros_like(rows[0])] * (SUBLANES - 2), axis=0)


def _dest_rows(starts, meta):
    nt, _, tm = meta.shape
    spec = pl.BlockSpec((1, SUBLANES, tm), lambda i, s: (i, 0, 0))
    return pl.pallas_call(
        _dest_kernel,
        grid_spec=pltpu.PrefetchScalarGridSpec(num_scalar_prefetch=1, grid=(nt,), in_specs=[spec], out_specs=spec),
        out_shape=jax.ShapeDtypeStruct(meta.shape, jnp.int32),
        compiler_params=pltpu.CompilerParams(dimension_semantics=("arbitrary",)),
        name="moe_dest",
    )(starts, meta)


def _scatter_kernel(zrow_ref, dst_ref, h_ref, xs_hbm, zero_ref, sem, zsem):
    tm = dst_ref.shape[2]

    @pl.when(pl.program_id(0) == 0)
    def _():
        zero_ref[...] = jnp.zeros_like(zero_ref)
        tile_rows = zero_ref.shape[0]

        def fill(row):
            start = pl.multiple_of(row * SUBLANES, SUBLANES)
            return pltpu.make_async_copy(zero_ref, xs_hbm.at[pl.ds(start, tile_rows), :], zsem)

        fills = [fill(zrow_ref[e]) for e in range(N_EXPERTS)]
        for f in fills:
            f.start()
        for f in fills:
            f.wait()
        n_tiles = xs_hbm.shape[0] // tile_rows
        for j in range(N_EXPERTS):
            @pl.when(n_tiles - 1 - j >= zrow_ref[N_EXPERTS])
            def _():
                f = fill((n_tiles - 1 - j) * (tile_rows // SUBLANES))
                f.start()
                f.wait()

    def row_copy(src_row, dst_row):
        src = pl.multiple_of(src_row * SUBLANES, SUBLANES)
        dst = pl.multiple_of(dst_row * SUBLANES, SUBLANES)
        return pltpu.make_async_copy(h_ref.at[pl.ds(src, SUBLANES), :], xs_hbm.at[pl.ds(dst, SUBLANES), :], sem)

    def issue(t, _):
        for k in range(2):
            row_copy(t, dst_ref[0, k, t]).start(priority=k)
        return 0

    lax.fori_loop(0, tm, issue, 0, unroll=DMA_UNROLL)

    def drain(t, _):
        row_copy(0, 0).wait()
        row_copy(0, 0).wait()
        return 0

    lax.fori_loop(0, tm, drain, 0, unroll=DMA_UNROLL)


def _scatter_rows(zero_rows, dst, h_rows, n_rows):
    nt, _, tm = dst.shape
    grid_spec = pltpu.PrefetchScalarGridSpec(
        num_scalar_prefetch=1,
        grid=(nt,),
        in_specs=[pl.BlockSpec((1, SUBLANES, tm), lambda i, z: (i, 0, 0), memory_space=pltpu.SMEM),
                  pl.BlockSpec((tm * SUBLANES, LANES), lambda i, z: (i, 0))],
        out_specs=pl.BlockSpec(memory_space=pl.ANY),
        scratch_shapes=[pltpu.VMEM((MOE_TILE * SUBLANES, LANES), F32), pltpu.SemaphoreType.DMA(()),
                        pltpu.SemaphoreType.DMA(())],
    )
    return pl.pallas_call(
        _scatter_kernel,
        grid_spec=grid_spec,
        out_shape=jax.ShapeDtypeStruct((n_rows * SUBLANES, LANES), F32),
        compiler_params=pltpu.CompilerParams(dimension_semantics=("arbitrary",), has_side_effects=True),
        name="moe_scatter",
    )(zero_rows, dst, h_rows)


def _expert_kernel(te_ref, nu_ref, x_ref, wg_ref, wu_ref, wd_ref, y_ref, *, fc):
    m = pl.program_id(0)
    tr = x_ref.shape[0] // SUBLANES
    dff = wg_ref.shape[2]

    @pl.when(m < nu_ref[0])
    def _():
        xb = _load_tile_rows(x_ref, tr).astype(BF16)
        acc = jnp.zeros((tr, wd_ref.shape[2]), F32)
        for c0 in range(0, dff, fc):
            g = _dot(xb, wg_ref[0, :, c0:c0 + fc])
            u = _dot(xb, wu_ref[0, :, c0:c0 + fc])
            acc = acc + _dot((jax.nn.silu(g) * u).astype(BF16), wd_ref[0, c0:c0 + fc, :])
        _store_tile_rows(y_ref, acc)

    @pl.when(m >= nu_ref[0])
    def _():
        y_ref[...] = jnp.zeros_like(y_ref)


def _expert_ffn(tile_expert, n_used, xs, wg, wu, wd):
    E, D, F = wg.shape
    R = xs.shape[0] // SUBLANES
    tr = MOE_TILE
    n_tiles = R // tr
    fc = min(MOE_FF_CHUNK, F)

    def row_map(m, te, nu):
        return (jnp.minimum(m, nu[0] - 1), 0)

    def w_map(m, te, nu):
        return (te[jnp.minimum(m, nu[0] - 1)], 0, 0)

    resident = dict(pipeline_mode=pl.Buffered(1))
    grid_spec = pltpu.PrefetchScalarGridSpec(
        num_scalar_prefetch=2,
        grid=(n_tiles,),
        in_specs=[
            pl.BlockSpec((tr * SUBLANES, LANES), row_map),
            pl.BlockSpec((1, D, F), w_map, **resident),
            pl.BlockSpec((1, D, F), w_map, **resident),
            pl.BlockSpec((1, F, D), w_map, **resident),
        ],
        out_specs=pl.BlockSpec((tr * SUBLANES, LANES), lambda m, te, nu: (m, 0)),
    )
    return pl.pallas_call(
        functools.partial(_expert_kernel, fc=fc),
        grid_spec=grid_spec,
        out_shape=jax.ShapeDtypeStruct(xs.shape, F32),
        compiler_params=pltpu.CompilerParams(dimension_semantics=("arbitrary",)),
        name="moe_experts",
    )(tile_expert, n_used, xs, wg, wu, wd)


def _combine_kernel(dcur_ref, dnxt_ref, x_ref, gate_ref, fn_ref, ys_hbm, o_ref, buf_ref, sems, *, final_norm):
    tm = x_ref.shape[0]
    i = pl.program_id(0)
    slot = i % 2

    def row_copy(src_row, s, k, t):
        src = pl.multiple_of(src_row * SUBLANES, SUBLANES)
        dst = pl.multiple_of(t * SUBLANES, SUBLANES)
        return pltpu.make_async_copy(ys_hbm.at[pl.ds(src, SUBLANES), :],
                                     buf_ref.at[s, k, pl.ds(dst, SUBLANES), :], sems.at[s])

    def issue_tile(d_ref, s):
        def issue(t, _):
            for k in range(2):
                row_copy(d_ref[0, k, t], s, k, t).start(priority=k)
            return 0

        lax.fori_loop(0, tm, issue, 0, unroll=DMA_UNROLL)

    @pl.when(i == 0)
    def _():
        issue_tile(dcur_ref, 0)

    @pl.when(i + 1 < pl.num_programs(0))
    def _():
        issue_tile(dnxt_ref, 1 - slot)

    def drain(t, _):
        row_copy(0, slot, 0, 0).wait()
        row_copy(0, slot, 1, 0).wait()
        return 0

    lax.fori_loop(0, tm, drain, 0, unroll=DMA_UNROLL)
    g = gate_ref[...]
    y0 = _load_tile_rows(buf_ref.at[slot, 0], tm)
    y1 = _load_tile_rows(buf_ref.at[slot, 1], tm)
    xn = x_ref[...] + g[:, 0:1] * y0 + g[:, 1:2] * y1
    o_ref[...] = _rms(xn, fn_ref[...]) if final_norm else xn


def _combine(dst_c, x2d, gates_col, fnorm, ys, final_norm):
    T, D = x2d.shape
    nt, _, tm = dst_c.shape
    return pl.pallas_call(
        functools.partial(_combine_kernel, final_norm=final_norm),
        grid=(nt,),
        in_specs=[pl.BlockSpec((1, SUBLANES, tm), lambda i: (i, 0, 0), memory_space=pltpu.SMEM),
                  pl.BlockSpec((1, SUBLANES, tm), lambda i: (jnp.minimum(i + 1, nt - 1), 0, 0),
                               memory_space=pltpu.SMEM),
                  pl.BlockSpec((tm, D), lambda i: (i, 0)),
                  pl.BlockSpec((tm, SUBLANES), lambda i: (i, 0)),
                  pl.BlockSpec((1, D), lambda i: (0, 0)),
                  pl.BlockSpec(memory_space=pl.ANY)],
        out_specs=pl.BlockSpec((tm, D), lambda i: (i, 0)),
        scratch_shapes=[pltpu.VMEM((2, 2, tm * SUBLANES, LANES), F32), pltpu.SemaphoreType.DMA((2,))],
        out_shape=jax.ShapeDtypeStruct((T, D), F32),
        compiler_params=pltpu.CompilerParams(dimension_semantics=("arbitrary",)),
        name="moe_combine",
    )(dst_c, dst_c, x2d, gates_col, fnorm, ys)


def _swap_halves_cols(w):
    half = w.shape[-1] // 2
    return jnp.concatenate([w[..., half:], w[..., :half]], axis=-1)


def _pad_cols(w, width):
    return jnp.pad(w, [(0, 0)] * (w.ndim - 1) + [(0, width - w.shape[-1])])


def _prep_w_in(w):
    kpe0 = _C_CKV + MLA_KV_RANK - 0
    kpe = w[:, kpe0:kpe0 + MLA_ROPE]
    return jnp.concatenate([
        w[:, :kpe0],
        _pad_cols(kpe, LANES), _pad_cols(_swap_halves_cols(kpe), LANES),
        w[:, kpe0 + MLA_ROPE:],
    ], axis=1).astype(BF16)


def _prep_w_uq(w):
    w = w.reshape(MLA_Q_RANK, MLA_HEADS, MLA_NOPE + MLA_ROPE)
    nope, pe = w[..., :MLA_NOPE], w[..., MLA_NOPE:]
    out = jnp.concatenate([nope, _pad_cols(pe, LANES), _pad_cols(_swap_halves_cols(pe), LANES)], axis=-1)
    return out.reshape(MLA_Q_RANK, MLA_HEADS * _Q_HEAD_COLS).astype(BF16)


def _prep_w_ukv(w):
    w = w.reshape(MLA_KV_RANK, MLA_HEADS, MLA_NOPE + MLA_V)
    wukt = jnp.transpose(w[..., :MLA_NOPE], (1, 2, 0)).reshape(MLA_HEADS * MLA_NOPE, MLA_KV_RANK)
    wuv = w[..., MLA_NOPE:].reshape(MLA_KV_RANK, MLA_HEADS * MLA_V)
    return wukt.astype(BF16), wuv.astype(BF16)


def _prep_w_kpe_t(w_ext):
    return w_ext[:, _C_KPE:_C_XP].T


def _rope_tables(seq):
    pos = jnp.arange(seq, dtype=F32)
    inv_freq = 1.0 / (ROPE_THETA ** (jnp.arange(0, MLA_ROPE, 2, dtype=F32) / MLA_ROPE))
    ang = pos[:, None] * inv_freq[None, :]
    cos, sin = jnp.cos(ang), jnp.sin(ang)
    cpad = _pad_cols(jnp.concatenate([cos, cos], axis=-1), LANES)
    spad = _pad_cols(jnp.concatenate([-sin, sin], axis=-1), LANES)
    return cpad, spad


def _block_diag_pool(w):
    G, c, _ = w.shape
    eye = jnp.eye(G, dtype=w.dtype)
    return (eye[:, None, :, None] * w[:, :, None, :]).reshape(G * c, G * c).astype(BF16)


def _moe_layout(counts, n_tiles, tile):
    tiles_per = (counts + tile - 1) // tile
    ends = jnp.cumsum(tiles_per)
    starts = (ends - tiles_per) * tile
    tile_expert = jnp.sum((jnp.arange(n_tiles)[:, None] >= ends[None, :]).astype(jnp.int32), axis=1)
    tile_expert = jnp.minimum(tile_expert, N_EXPERTS - 1)
    zero_rows = jnp.minimum(starts + counts, (n_tiles - 1) * tile)
    zero_info = jnp.concatenate([zero_rows, ends[-1:]])
    return (starts.astype(jnp.int32), tile_expert.astype(jnp.int32), ends[-1:].astype(jnp.int32),
            zero_info.astype(jnp.int32))


def kernel(x, attn_norm, w_in, hgrn_lower_bounds, hgrn_out_norm, mla_q_norm, mla_w_uq, mla_kv_norm,
           mla_w_ukv, pool_w, pool_scale, w_o, ffn_norm, dense_w_gate, dense_w_up, dense_w_down,
           moe_router, moe_w_gate, moe_w_up, moe_w_down, final_norm):
    B, S, D = x.shape
    T = B * S
    depth = w_in.shape[0]
    cpad, spad = _rope_tables(S)
    p_lb = jax.nn.softmax(hgrn_lower_bounds.astype(F32), axis=0)
    lbs = jnp.cumsum(p_lb, axis=0) - p_lb[0:1]

    assert depth % 2 == 0, "the final RMSNorm is fused into the last (MoE) layer's combine kernel"
    for l in range(depth):
        wukt, wuv = _prep_w_ukv(mla_w_ukv[l])
        w_ext = _prep_w_in(w_in[l])
        hg, xp, q, kt, v = _in_proj(
            x, attn_norm[l][None], w_ext, mla_q_norm[l][None], _prep_w_uq(mla_w_uq[l]),
            mla_kv_norm[l][None], wukt, wuv, _prep_w_kpe_t(w_ext), cpad, spad)
        o_a = _hgrn(hg, lbs[l][None], hgrn_out_norm[l][None])
        o_b = _attention(q, kt, v)
        moe_layer = (l % 2 == 1)
        x, h = _mix_out(x, o_a, o_b, xp, _block_diag_pool(pool_w[l]), pool_scale[l][None],
                        w_o[l].astype(BF16), ffn_norm[l][None], moe_layer)
        j = l // 2
        if not moe_layer:
            x = _dense_ffn(x.reshape(T, D), h.reshape(T, D), dense_w_gate[j].astype(BF16),
                           dense_w_up[j].astype(BF16), dense_w_down[j].astype(BF16)).reshape(B, S, D)
        else:
            meta, gates, counts = _router(h, moe_router[j].T)
            n_tiles = (2 * T) // MOE_TILE + N_EXPERTS
            starts, tile_expert, n_used, zero_rows = _moe_layout(counts[:, 0], n_tiles, MOE_TILE)
            dst = _dest_rows(starts, meta)
            xs = _scatter_rows(zero_rows, dst, h, n_tiles * MOE_TILE)
            ys = _expert_ffn(tile_expert, n_used, xs, moe_w_gate[j].astype(BF16), moe_w_up[j].astype(BF16),
                             moe_w_down[j].astype(BF16))
            ct = min(COMBINE_TILE, T)
            dst_c = dst.transpose(1, 0, 2).reshape(SUBLANES, T // ct, ct).transpose(1, 0, 2)
            gates_col = gates.transpose(0, 2, 1).reshape(T, SUBLANES)
            last = (l == depth - 1)
            y = _combine(dst_c, x.reshape(T, D), gates_col, final_norm[None], ys, last)
            x = y.reshape(B, S, D)
    return x
```

```python
import functools
import math

import jax
import jax.numpy as jnp
import numpy as np
from jax import lax
from jax.experimental import pallas as pl
from jax.experimental.pallas import tpu as pltpu

F32 = jnp.float32
BF16 = jnp.bfloat16

HG_HEADS = 4
HG_KEY_DIM = 128
HG_VAL_DIM = 64
HG_KEY_WIDTH = HG_HEADS * HG_KEY_DIM
HG_WIDTH = HG_HEADS * HG_VAL_DIM
MIN_FORGET = 1e-20
MLA_HEADS = 4
MLA_Q_RANK = 256
MLA_KV_RANK = 128
MLA_NOPE = 128
MLA_ROPE = 64
MLA_V = 128
MLA_WIDTH = MLA_HEADS * MLA_V
ROPE_THETA = 10000.0
MASK_VALUE = -1e30
POOL_GROUPS = 4
POOL_WINDOWS = (2, 4, 8, 16)
POOL_WIDTH = 256
POOL_GROUP_DIM = POOL_WIDTH // POOL_GROUPS
N_EXPERTS = 8
EPS = 1e-6

LANES = 128
SUBLANES = 8
QK_PAD = 256
V_EXT = 256

TOKEN_TILE = 1024
IN_PROJ_CHAINS = 1
HGRN_CHUNK = 256
HGRN_CHUNKS_PER_STEP = 2
ATTN_BK = 512
ATTN_HEADS_PER_STEP = 1
ATTN_PAIRS_PER_TRIP = 8
FF_CHUNK = 256
MOE_FF_CHUNK = 256
MOE_TILE = 1024
ROUTE_TILE = 512
COMBINE_TILE = 256
POOL_HALO = 16
DMA_UNROLL = 8

_C_HG = 0
_C_CQ = 2 * HG_KEY_WIDTH + 2 * HG_WIDTH
_C_CKV = _C_CQ + MLA_Q_RANK
_C_KPE = _C_CKV + MLA_KV_RANK
_C_KPES = _C_KPE + LANES
_C_XP = _C_KPES + LANES
_C_END = _C_XP + POOL_WIDTH
_Q_HEAD_COLS = 3 * LANES


def _rms(x, g):
    return x * lax.rsqrt(jnp.mean(x * x, axis=-1, keepdims=True) + EPS) * g


def _dot(a, b):
    return jnp.dot(a, b, preferred_element_type=F32)


def _dot_nt(a, b):
    return lax.dot_general(a, b, (((1,), (1,)), ((), ())), preferred_element_type=F32)


def _dot_tn(a, b):
    return lax.dot_general(a, b, (((0,), (0,)), ((), ())), preferred_element_type=F32)


def _const_spec(shape):
    nd = len(shape)
    return pl.BlockSpec(shape, lambda *_: (0,) * nd)


def _load_tile_rows(ref, n):
    return jnp.concatenate([ref[pl.ds(s, n, stride=SUBLANES), :] for s in range(SUBLANES)], axis=1)


def _store_tile_rows(ref, val):
    n = val.shape[0]
    for s in range(SUBLANES):
        ref[pl.ds(s, n, stride=SUBLANES), :] = val[:, s * LANES:(s + 1) * LANES]


def _in_proj_kernel(x_ref, g_ref, w_ref, qn_ref, wuq_ref, kvn_ref, wukt_ref, wuv_ref, wkpet_ref,
                    cpad_ref, spad_ref, cpadt_ref, spadt_ref,
                    hg_ref, xp_ref, q_ref, kt_ref, v_ref):
    scale = (MLA_NOPE + MLA_ROPE) ** -0.5 * math.log2(math.e)
    tm = x_ref.shape[1]
    rows = tm // IN_PROJ_CHAINS
    ones_col = (lax.broadcasted_iota(jnp.int32, (rows, LANES), 1) == 0).astype(BF16)

    for r0 in range(0, tm, rows):
        rs = slice(r0, r0 + rows)
        h = _rms(x_ref[0, rs, :], g_ref[...]).astype(BF16)
        hg_ref[0, rs, :] = _dot(h, w_ref[:, _C_HG:_C_CQ])
        xp_ref[0, rs, :] = _dot(h, w_ref[:, _C_XP:_C_END])
        cpad = cpad_ref[rs, :]
        spad = spad_ref[rs, :]

        cq = _dot(h, w_ref[:, _C_CQ:_C_CKV])
        cqn = _rms(cq, qn_ref[...]).astype(BF16)
        for hd in range(MLA_HEADS):
            qh = _dot(cqn, wuq_ref[:, hd * _Q_HEAD_COLS:(hd + 1) * _Q_HEAD_COLS])
            q_ref[0, hd, rs, 0:LANES] = (qh[:, 0:LANES] * scale).astype(BF16)
            pe = qh[:, LANES:2 * LANES] * cpad + qh[:, 2 * LANES:3 * LANES] * spad
            q_ref[0, hd, rs, LANES:QK_PAD] = (pe * scale).astype(BF16)

        ckv = _dot(h, w_ref[:, _C_CKV:_C_KPE])
        ckvn = _rms(ckv, kvn_ref[...]).astype(BF16)
        kpet = (_dot_nt(wkpet_ref[0:LANES, :], h) * cpadt_ref[:, rs]
                + _dot_nt(wkpet_ref[LANES:2 * LANES, :], h) * spadt_ref[:, rs]).astype(BF16)
        v_all = _dot(ckvn, wuv_ref[...]).astype(BF16)
        for hd in range(MLA_HEADS):
            kt_ref[0, hd, 0:LANES, rs] = _dot_nt(wukt_ref[hd * LANES:(hd + 1) * LANES, :], ckvn).astype(BF16)
            kt_ref[0, hd, LANES:QK_PAD, rs] = kpet
            v_ref[0, hd, rs, 0:MLA_V] = v_all[:, hd * MLA_V:(hd + 1) * MLA_V]
            v_ref[0, hd, rs, MLA_V:V_EXT] = ones_col


def _in_proj(x, g, w_ext, qn, wuq_ext, kvn, wukt, wuv, wkpet, cpad, spad):
    B, S, D = x.shape
    tm = min(TOKEN_TILE, S)
    grid = (B, S // tm)
    n_hg = _C_CQ
    out_shape = (
        jax.ShapeDtypeStruct((B, S, n_hg), F32),
        jax.ShapeDtypeStruct((B, S, POOL_WIDTH), F32),
        jax.ShapeDtypeStruct((B, MLA_HEADS, S, QK_PAD), BF16),
        jax.ShapeDtypeStruct((B, MLA_HEADS, QK_PAD, S), BF16),
        jax.ShapeDtypeStruct((B, MLA_HEADS, S, V_EXT), BF16),
    )
    return pl.pallas_call(
        _in_proj_kernel,
        grid=grid,
        in_specs=[
            pl.BlockSpec((1, tm, D), lambda b, i: (b, i, 0)),
            _const_spec((1, D)),
            _const_spec(w_ext.shape),
            _const_spec((1, MLA_Q_RANK)),
            _const_spec(wuq_ext.shape),
            _const_spec((1, MLA_KV_RANK)),
            _const_spec(wukt.shape),
            _const_spec(wuv.shape),
            _const_spec(wkpet.shape),
            pl.BlockSpec((tm, LANES), lambda b, i: (i, 0)),
            pl.BlockSpec((tm, LANES), lambda b, i: (i, 0)),
            pl.BlockSpec((LANES, tm), lambda b, i: (0, i)),
            pl.BlockSpec((LANES, tm), lambda b, i: (0, i)),
        ],
        out_specs=(
            pl.BlockSpec((1, tm, n_hg), lambda b, i: (b, i, 0)),
            pl.BlockSpec((1, tm, POOL_WIDTH), lambda b, i: (b, i, 0)),
            pl.BlockSpec((1, MLA_HEADS, tm, QK_PAD), lambda b, i: (b, 0, i, 0)),
            pl.BlockSpec((1, MLA_HEADS, QK_PAD, tm), lambda b, i: (b, 0, 0, i)),
            pl.BlockSpec((1, MLA_HEADS, tm, V_EXT), lambda b, i: (b, 0, i, 0)),
        ),
        out_shape=out_shape,
        compiler_params=pltpu.CompilerParams(dimension_semantics=("arbitrary", "arbitrary")),
        name="in_proj",
    )(x, g, w_ext, qn, wuq_ext, kvn, wukt, wuv, wkpet, cpad, spad, cpad.T, spad.T)


def _split2(x):
    hi = x.astype(BF16)
    return hi, (x - hi.astype(F32)).astype(BF16)


def _hgrn_kernel(q_ref, f_ref, i_ref, g_ref, lb_ref, on_ref, tril_ref, grp_ref, o_ref, st_ref):
    C = tril_ref.shape[0]

    @pl.when(pl.program_id(1) == 0)
    def _():
        st_ref[...] = jnp.zeros_like(st_ref)

    for c0 in range(0, q_ref.shape[1], C):
        rs = slice(c0, c0 + C)
        out = _hgrn_chunk(q_ref[0, rs, :], f_ref[0, rs, :], i_ref[0, rs, :], g_ref[0, rs, :],
                          lb_ref, on_ref, tril_ref, grp_ref, st_ref)
        o_ref[0, rs, :] = out.astype(o_ref.dtype)


def _hgrn_chunk(q_in, z, vv, g_in, lb_ref, on_ref, tril_ref, grp_ref, st_ref):
    C = z.shape[0]
    KW = HG_KEY_WIDTH
    lb = lb_ref[...]
    sig = jax.nn.sigmoid(z)
    forget = lb + (1.0 - lb) * sig
    lg = jnp.log2(jnp.maximum(forget, MIN_FORGET))
    kk = (1.0 - lb) * (1.0 - sig)
    qq = jax.nn.silu(q_in)

    tril = tril_ref[...]
    b = sum(_dot(tril, part) for part in _split2(lg))

    sides, masks = [], []
    row = lax.broadcasted_iota(jnp.int32, (C, C), 0)
    col = lax.broadcasted_iota(jnp.int32, (C, C), 1)
    sub = lax.broadcasted_iota(jnp.int32, (C, KW), 0)
    half = C // 2
    while half >= 4:
        blk = 2 * half
        b3 = b.reshape(C // blk, blk, KW)
        ref_row = jnp.broadcast_to(b3[:, half - 1:half, :], b3.shape).reshape(C, KW)
        e = jnp.exp2(-jnp.abs(b - ref_row))
        x = (jnp.where((sub & half) != 0, qq, kk) * e).astype(BF16)
        sides.append((x, x))
        shift = int(math.log2(blk))
        masks.append(((row >> shift) == (col >> shift)) & ((row & half) != 0) & ((col & half) == 0))
        half //= 2
    b3 = b.reshape(C // 8, 8, KW)
    mid_lo = 0.5 * (b3[:, 0:1, :] + b3[:, 3:4, :])
    mid_hi = 0.5 * (b3[:, 4:5, :] + b3[:, 7:8, :])
    sub8 = lax.broadcasted_iota(jnp.int32, b3.shape, 1)
    mid = jnp.where(sub8 < 4, mid_lo, mid_hi).reshape(C, KW)
    sides.append(((qq * jnp.exp2(b - mid)).astype(BF16), (kk * jnp.exp2(mid - b)).astype(BF16)))
    masks.append(((row >> 2) == (col >> 2)) & (col <= row))

    lane_v = lax.broadcasted_iota(jnp.int32, (C, HG_WIDTH), 1)
    o = _dot_nt((qq * jnp.exp2(b)).astype(BF16), st_ref[...].astype(BF16))
    for hd in range(HG_HEADS):
        ks = slice(hd * HG_KEY_DIM, (hd + 1) * HG_KEY_DIM)
        a = jnp.zeros((C, C), F32)
        for (qt, kt), m in zip(sides, masks):
            a = jnp.where(m, _dot_nt(qt[:, ks], kt[:, ks]), a)
        v_h = jnp.where((lane_v >> 6) == hd, vv, 0.0).astype(BF16)
        o = o + _dot(a.astype(BF16), v_h)

    b_last = b[C - 1:C, :]
    khat = (kk * jnp.exp2(b_last - b)).astype(BF16)
    st_row = lax.broadcasted_iota(jnp.int32, (HG_WIDTH, KW), 0)
    st_col = lax.broadcasted_iota(jnp.int32, (HG_WIDTH, KW), 1)
    new_st = st_ref[...] * jnp.exp2(b_last) + _dot_tn(vv.astype(BF16), khat)
    st_ref[...] = jnp.where((st_row >> 6) == (st_col >> 7), new_st, 0.0)

    grp = grp_ref[...]
    ssq = sum(_dot(part, grp) for part in _split2(o * o))
    on = o * lax.rsqrt(ssq * (1.0 / HG_VAL_DIM) + EPS) * on_ref[...]
    return on * jax.nn.silu(g_in)


def _hgrn(hg, lb, out_norm):
    B, S, _ = hg.shape
    C = min(HGRN_CHUNK, S)
    kb = HG_KEY_WIDTH // HG_KEY_WIDTH
    tril = jnp.asarray(np.tril(np.ones((C, C), np.float32)), BF16)
    lane_group = np.arange(HG_WIDTH) // HG_VAL_DIM
    grp = jnp.asarray((lane_group[:, None] == lane_group[None, :]).astype(np.float32), BF16)
    rows = min(HGRN_CHUNKS_PER_STEP * C, S)
    return pl.pallas_call(
        _hgrn_kernel,
        grid=(B, S // rows),
        in_specs=[
            pl.BlockSpec((1, rows, HG_KEY_WIDTH), lambda b, c: (b, c, 0)),
            pl.BlockSpec((1, rows, HG_KEY_WIDTH), lambda b, c: (b, c, kb)),
            pl.BlockSpec((1, rows, HG_WIDTH), lambda b, c: (b, c, 2 * HG_KEY_WIDTH // HG_WIDTH)),
            pl.BlockSpec((1, rows, HG_WIDTH), lambda b, c: (b, c, 2 * HG_KEY_WIDTH // HG_WIDTH + 1)),
            _const_spec((1, HG_KEY_WIDTH)),
            _const_spec((1, HG_WIDTH)),
            _const_spec(tril.shape),
            _const_spec(grp.shape),
        ],
        out_specs=pl.BlockSpec((1, rows, HG_WIDTH), lambda b, c: (b, c, 0)),
        out_shape=jax.ShapeDtypeStruct((B, S, HG_WIDTH), BF16),
        scratch_shapes=[pltpu.VMEM((HG_WIDTH, HG_KEY_WIDTH), F32)],
        compiler_params=pltpu.CompilerParams(dimension_semantics=("arbitrary", "arbitrary")),
        name="hgrn2",
    )(hg, hg, hg, hg, lb, out_norm, tril, grp)


def _attn_kernel(q_ref, kt_ref, v_ref, o_ref, s0_ref, s1_ref, acc_ref, *, bk):
    bq = q_ref.shape[2]
    nh = q_ref.shape[1]
    i = pl.program_id(2)
    qs = [q_ref[0, hd] for hd in range(nh)]

    def causal(s, key0):
        qry = lax.broadcasted_iota(jnp.int32, (bq, bk), 0)
        key = key0 + lax.broadcasted_iota(jnp.int32, (bq, bk), 1)
        return jnp.where(key <= qry, s, MASK_VALUE)

    def scores(blk, s_ref):
        start = pl.multiple_of(blk * bk, bk)
        for hd in range(nh):
            s_ref[hd] = _dot(qs[hd], kt_ref[0, hd, :, pl.ds(start, bk)])

    def softmax_pv(blk, s_ref, ms, key0=None):
        start = pl.multiple_of(blk * bk, bk)
        out = []
        for hd in range(nh):
            s = s_ref[hd]
            if key0 is not None:
                s = causal(s, key0)
            m_new = jnp.maximum(ms[hd], jnp.max(s, axis=1, keepdims=True))
            p = jnp.exp2(s - m_new).astype(BF16)
            acc_ref[hd] = jnp.exp2(ms[hd] - m_new) * acc_ref[hd] + _dot(p, v_ref[0, hd, pl.ds(start, bk), :])
            out.append(m_new)
        return tuple(out)

    acc_ref[...] = jnp.zeros_like(acc_ref)
    scores(0, s0_ref)

    def pair(t, ms):
        scores(2 * t + 1, s1_ref)
        ms = softmax_pv(2 * t, s0_ref, ms)
        scores(2 * t + 2, s0_ref)
        return softmax_pv(2 * t + 1, s1_ref, ms)

    def pairs(t0, n, c):
        for k in range(n):
            c = pair(t0 + k, c)
        return c

    ms = tuple(jnp.full((bq, 1), -jnp.inf, F32) for _ in range(nh))
    group = ATTN_PAIRS_PER_TRIP
    ms = lax.fori_loop(0, i // group, lambda t, c: pairs(group * t, group, c), ms)
    done = (i // group) * group
    while group > 1:
        group //= 2
        ms = lax.cond((i & group) != 0, functools.partial(pairs, done, group), lambda c: c, ms)
        done = done + (i & group)
    start1 = pl.multiple_of((2 * i + 1) * bk, bk)
    lower = [_dot(qs[hd][bk:, :], kt_ref[0, hd, :, pl.ds(start1, bk)]) for hd in range(nh)]
    ms = softmax_pv(2 * i, s0_ref, ms, 0)
    qry = lax.broadcasted_iota(jnp.int32, (bk, bk), 0)
    key = lax.broadcasted_iota(jnp.int32, (bk, bk), 1)
    for hd in range(nh):
        s = jnp.where(key <= qry, lower[hd], MASK_VALUE)
        m_old = ms[hd][bk:, :]
        m_new = jnp.maximum(m_old, jnp.max(s, axis=1, keepdims=True))
        p = jnp.exp2(s - m_new).astype(BF16)
        acc_ref[hd, bk:, :] = (jnp.exp2(m_old - m_new) * acc_ref[hd, bk:, :]
                               + _dot(p, v_ref[0, hd, pl.ds(start1, bk), :]))
    for hd in range(nh):
        acc = acc_ref[hd]
        o_ref[0, :, hd * MLA_V:(hd + 1) * MLA_V] = (acc[:, 0:MLA_V] / acc[:, MLA_V:MLA_V + 1]).astype(o_ref.dtype)


def _attention(q, kt, v):
    B, H, S, _ = q.shape
    bk = min(ATTN_BK, S // 2)
    bq = 2 * bk
    nh = ATTN_HEADS_PER_STEP
    resident = dict(pipeline_mode=pl.Buffered(1)) if nh > 1 else {}
    return pl.pallas_call(
        functools.partial(_attn_kernel, bk=bk),
        grid=(B, H // nh, S // bq),
        in_specs=[
            pl.BlockSpec((1, nh, bq, QK_PAD), lambda b, h, i: (b, h, i, 0)),
            pl.BlockSpec((1, nh, QK_PAD, S), lambda b, h, i: (b, h, 0, 0), **resident),
            pl.BlockSpec((1, nh, S, V_EXT), lambda b, h, i: (b, h, 0, 0), **resident),
        ],
        out_specs=pl.BlockSpec((1, bq, nh * MLA_V), lambda b, h, i: (b, i, h)),
        out_shape=jax.ShapeDtypeStruct((B, S, H * MLA_V), BF16),
        scratch_shapes=[pltpu.VMEM((nh, bq, bk), F32), pltpu.VMEM((nh, bq, bk), F32),
                        pltpu.VMEM((nh, bq, V_EXT), F32)],
        compiler_params=pltpu.CompilerParams(dimension_semantics=("arbitrary", "arbitrary", "arbitrary")),
        name="mla_attention",
    )(q, kt, v)


def _mix_out_kernel(x_ref, oa_ref, ob_ref, xp_ref, wpool_ref, pscale_ref, wo_ref, fn_ref,
                    xo_ref, h_ref, halo_ref, *, tile_rows):
    tm = xp_ref.shape[1]
    i = pl.program_id(1)

    @pl.when(i == 0)
    def _():
        halo_ref[...] = jnp.zeros_like(halo_ref)

    xp = xp_ref[0]
    xx = jnp.concatenate([halo_ref[...], xp], axis=0)
    halo_ref[...] = xp[tm - POOL_HALO:, :]

    w2 = xx[1:, :] + xx[:-1, :]
    w4 = w2[2:, :] + w2[:-2, :]
    w8 = w4[4:, :] + w4[:-4, :]
    w16 = w8[8:, :] + w8[:-8, :]
    sums = (w2[POOL_HALO - 1:, :], w4[POOL_HALO - 3:, :], w8[POOL_HALO - 7:, :], w16[POOL_HALO - 15:, :])
    t = i * tm + lax.broadcasted_iota(jnp.int32, (tm, POOL_WIDTH), 0)
    lane = lax.broadcasted_iota(jnp.int32, (tm, POOL_WIDTH), 1)
    pooled = jnp.zeros((tm, POOL_WIDTH), F32)
    for gi, w in enumerate(POOL_WINDOWS):
        cnt = jnp.minimum(t + 1, w).astype(F32)
        pooled = jnp.where((lane >> 6) == gi, sums[gi] / cnt, pooled)
    pooled = pooled - xp
    oc = _dot(pooled.astype(BF16), wpool_ref[...]) * pscale_ref[...]

    y = _dot(oa_ref[0], wo_ref[0:HG_WIDTH, :])
    y = y + _dot(ob_ref[0], wo_ref[HG_WIDTH:HG_WIDTH + MLA_WIDTH, :])
    y = y + _dot(oc.astype(BF16), wo_ref[HG_WIDTH + MLA_WIDTH:, :])
    xn = x_ref[0] + y
    xo_ref[0] = xn
    h = _rms(xn, fn_ref[...])
    if tile_rows:
        _store_tile_rows(h_ref, h)
    else:
        h_ref[0] = h.astype(h_ref.dtype)


def _mix_out(x, oa, ob, xp, wpool_bd, pscale, wo, fnorm, tile_rows):
    B, S, D = x.shape
    tm = min(TOKEN_TILE, S)
    nt = S // tm
    tok = lambda w: pl.BlockSpec((1, tm, w), lambda b, i: (b, i, 0))
    if tile_rows:
        assert D == SUBLANES * LANES
        h_spec = pl.BlockSpec((tm * SUBLANES, LANES), lambda b, i: (b * nt + i, 0))
        h_shape = jax.ShapeDtypeStruct((B * S * SUBLANES, LANES), F32)
    else:
        h_spec, h_shape = tok(D), jax.ShapeDtypeStruct((B, S, D), BF16)
    return pl.pallas_call(
        functools.partial(_mix_out_kernel, tile_rows=tile_rows),
        grid=(B, nt),
        in_specs=[tok(D), tok(HG_WIDTH), tok(MLA_WIDTH), tok(POOL_WIDTH),
                  _const_spec(wpool_bd.shape), _const_spec((1, POOL_WIDTH)), _const_spec(wo.shape),
                  _const_spec((1, D))],
        out_specs=(tok(D), h_spec),
        out_shape=(jax.ShapeDtypeStruct((B, S, D), F32), h_shape),
        scratch_shapes=[pltpu.VMEM((POOL_HALO, POOL_WIDTH), F32)],
        compiler_params=pltpu.CompilerParams(dimension_semantics=("arbitrary", "arbitrary")),
        name="mix_out",
    )(x, oa, ob, xp, wpool_bd, pscale, wo, fnorm)


def _dense_ffn_kernel(x_ref, h_ref, wg_ref, wu_ref, wd_ref, o_ref):
    h = h_ref[...]
    acc = x_ref[...]
    dff = wg_ref.shape[1]
    for c0 in range(0, dff, FF_CHUNK):
        g = _dot(h, wg_ref[:, c0:c0 + FF_CHUNK])
        u = _dot(h, wu_ref[:, c0:c0 + FF_CHUNK])
        acc = acc + _dot((jax.nn.silu(g) * u).astype(BF16), wd_ref[c0:c0 + FF_CHUNK, :])
    o_ref[...] = acc


def _dense_ffn(x2d, h2d, wg, wu, wd):
    T, D = x2d.shape
    tm = min(TOKEN_TILE, T)
    tok = pl.BlockSpec((tm, D), lambda i: (i, 0))
    return pl.pallas_call(
        _dense_ffn_kernel,
        grid=(T // tm,),
        in_specs=[tok, tok, _const_spec(wg.shape), _const_spec(wu.shape), _const_spec(wd.shape)],
        out_specs=tok,
        out_shape=jax.ShapeDtypeStruct((T, D), F32),
        compiler_params=pltpu.CompilerParams(dimension_semantics=("arbitrary",)),
        name="dense_ffn",
    )(x2d, h2d, wg, wu, wd)


def _router_kernel(h_ref, rt_ref, meta_ref, gate_ref, cnt_ref, run_ref):
    tm = h_ref.shape[0] // SUBLANES
    E = N_EXPERTS

    @pl.when(pl.program_id(0) == 0)
    def _():
        run_ref[...] = jnp.zeros_like(run_ref)

    logits = lax.dot_general(rt_ref[...], _load_tile_rows(h_ref, tm), (((1,), (1,)), ((), ())),
                             precision=lax.Precision.HIGHEST, preferred_element_type=F32)
    eid = lax.broadcasted_iota(jnp.int32, (E, tm), 0)
    m1 = jnp.max(logits, axis=0, keepdims=True)
    i1 = jnp.min(jnp.where(logits == m1, eid, E), axis=0, keepdims=True)
    rest = jnp.where(eid == i1, -jnp.inf, logits)
    m2 = jnp.max(rest, axis=0, keepdims=True)
    i2 = jnp.min(jnp.where(rest == m2, eid, E), axis=0, keepdims=True)
    e2 = jnp.exp(m2 - m1)
    g1 = 1.0 / (1.0 + e2)
    g2 = e2 / (1.0 + e2)

    sel = ((eid == i1) | (eid == i2))
    r = lax.broadcasted_iota(jnp.int32, (tm, tm), 0)
    c = lax.broadcasted_iota(jnp.int32, (tm, tm), 1)
    before = (r < c).astype(BF16)
    excl = _dot(sel.astype(BF16), before) + run_ref[:, 0:1]
    rank1 = jnp.sum(jnp.where(eid == i1, excl, 0.0), axis=0, keepdims=True).astype(jnp.int32)
    rank2 = jnp.sum(jnp.where(eid == i2, excl, 0.0), axis=0, keepdims=True).astype(jnp.int32)
    run_ref[...] = run_ref[...] + jnp.sum(sel.astype(F32), axis=1, keepdims=True)
    cnt_ref[...] = run_ref[...].astype(jnp.int32)

    zi = jnp.zeros((1, tm), jnp.int32)
    meta_ref[0] = jnp.concatenate([i1, i2, rank1, rank2, zi, zi, zi, zi], axis=0)
    zf = jnp.zeros((1, tm), F32)
    gate_ref[0] = jnp.concatenate([g1, g2, zf, zf, zf, zf, zf, zf], axis=0)


def _router(h_rows, router_t):
    T = h_rows.shape[0] // SUBLANES
    tm = min(ROUTE_TILE, T)
    nt = T // tm
    return pl.pallas_call(
        _router_kernel,
        grid=(nt,),
        in_specs=[pl.BlockSpec((tm * SUBLANES, LANES), lambda i: (i, 0)), _const_spec(router_t.shape)],
        out_specs=(pl.BlockSpec((1, SUBLANES, tm), lambda i: (i, 0, 0)),
                   pl.BlockSpec((1, SUBLANES, tm), lambda i: (i, 0, 0)),
                   _const_spec((N_EXPERTS, LANES))),
        out_shape=(jax.ShapeDtypeStruct((nt, SUBLANES, tm), jnp.int32),
                   jax.ShapeDtypeStruct((nt, SUBLANES, tm), F32),
                   jax.ShapeDtypeStruct((N_EXPERTS, LANES), jnp.int32)),
        scratch_shapes=[pltpu.VMEM((N_EXPERTS, LANES), F32)],
        compiler_params=pltpu.CompilerParams(dimension_semantics=("arbitrary",)),
        name="moe_router",
    )(h_rows, router_t)


def _dest_kernel(start_ref, meta_ref, dst_ref):
    meta = meta_ref[0]
    rows = []
    for k in range(2):
        e = meta[k:k + 1, :]
        base = jnp.zeros_like(e)
        for ex in range(N_EXPERTS):
            base = jnp.where(e == ex, start_ref[ex], base)
        rows.append(base + meta[2 + k:3 + k, :])
    dst_ref[0] = jnp.concatenate(rows + [jnp.zeros_like(rows[0])] * (SUBLANES - 2), axis=0)


def _dest_rows(starts, meta):
    nt, _, tm = meta.shape
    spec = pl.BlockSpec((1, SUBLANES, tm), lambda i, s: (i, 0, 0))
    return pl.pallas_call(
        _dest_kernel,
        grid_spec=pltpu.PrefetchScalarGridSpec(num_scalar_prefetch=1, grid=(nt,), in_specs=[spec], out_specs=spec),
        out_shape=jax.ShapeDtypeStruct(meta.shape, jnp.int32),
        compiler_params=pltpu.CompilerParams(dimension_semantics=("arbitrary",)),
        name="moe_dest",
    )(starts, meta)


def _scatter_kernel(zrow_ref, dst_ref, h_ref, xs_hbm, zero_ref, sem, zsem):
    tm = dst_ref.shape[2]

    @pl.when(pl.program_id(0) == 0)
    def _():
        zero_ref[...] = jnp.zeros_like(zero_ref)
        tile_rows = zero_ref.shape[0]

        def fill(row):
            start = pl.multiple_of(row * SUBLANES, SUBLANES)
            return pltpu.make_async_copy(zero_ref, xs_hbm.at[pl.ds(start, tile_rows), :], zsem)

        fills = [fill(zrow_ref[e]) for e in range(N_EXPERTS)]
        for f in fills:
            f.start()
        for f in fills:
            f.wait()
        n_tiles = xs_hbm.shape[0] // tile_rows
        for j in range(N_EXPERTS):
            @pl.when(n_tiles - 1 - j >= zrow_ref[N_EXPERTS])
            def _():
                f = fill((n_tiles - 1 - j) * (tile_rows // SUBLANES))
                f.start()
                f.wait()

    def row_copy(src_row, dst_row):
        src = pl.multiple_of(src_row * SUBLANES, SUBLANES)
        dst = pl.multiple_of(dst_row * SUBLANES, SUBLANES)
        return pltpu.make_async_copy(h_ref.at[pl.ds(src, SUBLANES), :], xs_hbm.at[pl.ds(dst, SUBLANES), :], sem)

    def issue(t, _):
        for k in range(2):
            row_copy(t, dst_ref[0, k, t]).start(priority=k)
        return 0

    lax.fori_loop(0, tm, issue, 0, unroll=DMA_UNROLL)

    def drain(t, _):
        row_copy(0, 0).wait()
        row_copy(0, 0).wait()
        return 0

    lax.fori_loop(0, tm, drain, 0, unroll=DMA_UNROLL)


def _scatter_rows(zero_rows, dst, h_rows, n_rows):
    nt, _, tm = dst.shape
    grid_spec = pltpu.PrefetchScalarGridSpec(
        num_scalar_prefetch=1,
        grid=(nt,),
        in_specs=[pl.BlockSpec((1, SUBLANES, tm), lambda i, z: (i, 0, 0), memory_space=pltpu.SMEM),
                  pl.BlockSpec((tm * SUBLANES, LANES), lambda i, z: (i, 0))],
        out_specs=pl.BlockSpec(memory_space=pl.ANY),
        scratch_shapes=[pltpu.VMEM((MOE_TILE * SUBLANES, LANES), F32), pltpu.SemaphoreType.DMA(()),
                        pltpu.SemaphoreType.DMA(())],
    )
    return pl.pallas_call(
        _scatter_kernel,
        grid_spec=grid_spec,
        out_shape=jax.ShapeDtypeStruct((n_rows * SUBLANES, LANES), F32),
        compiler_params=pltpu.CompilerParams(dimension_semantics=("arbitrary",), has_side_effects=True),
        name="moe_scatter",
    )(zero_rows, dst, h_rows)


def _expert_kernel(te_ref, nu_ref, x_ref, wg_ref, wu_ref, wd_ref, y_ref, *, fc):
    m = pl.program_id(0)
    tr = x_ref.shape[0] // SUBLANES
    dff = wg_ref.shape[2]

    @pl.when(m < nu_ref[0])
    def _():
        xb = _load_tile_rows(x_ref, tr).astype(BF16)
        acc = jnp.zeros((tr, wd_ref.shape[2]), F32)
        for c0 in range(0, dff, fc):
            g = _dot(xb, wg_ref[0, :, c0:c0 + fc])
            u = _dot(xb, wu_ref[0, :, c0:c0 + fc])
            acc = acc + _dot((jax.nn.silu(g) * u).astype(BF16), wd_ref[0, c0:c0 + fc, :])
        _store_tile_rows(y_ref, acc)

    @pl.when(m >= nu_ref[0])
    def _():
        y_ref[...] = jnp.zeros_like(y_ref)


def _expert_ffn(tile_expert, n_used, xs, wg, wu, wd):
    E, D, F = wg.shape
    R = xs.shape[0] // SUBLANES
    tr = MOE_TILE
    n_tiles = R // tr
    fc = min(MOE_FF_CHUNK, F)

    def row_map(m, te, nu):
        return (jnp.minimum(m, nu[0] - 1), 0)

    def w_map(m, te, nu):
        return (te[jnp.minimum(m, nu[0] - 1)], 0, 0)

    resident = dict(pipeline_mode=pl.Buffered(1))
    grid_spec = pltpu.PrefetchScalarGridSpec(
        num_scalar_prefetch=2,
        grid=(n_tiles,),
        in_specs=[
            pl.BlockSpec((tr * SUBLANES, LANES), row_map),
            pl.BlockSpec((1, D, F), w_map, **resident),
            pl.BlockSpec((1, D, F), w_map, **resident),
            pl.BlockSpec((1, F, D), w_map, **resident),
        ],
        out_specs=pl.BlockSpec((tr * SUBLANES, LANES), lambda m, te, nu: (m, 0)),
    )
    return pl.pallas_call(
        functools.partial(_expert_kernel, fc=fc),
        grid_spec=grid_spec,
        out_shape=jax.ShapeDtypeStruct(xs.shape, F32),
        compiler_params=pltpu.CompilerParams(dimension_semantics=("arbitrary",)),
        name="moe_experts",
    )(tile_expert, n_used, xs, wg, wu, wd)


def _combine_kernel(dcur_ref, dnxt_ref, x_ref, gate_ref, fn_ref, ys_hbm, o_ref, buf_ref, sems, *, final_norm):
    tm = x_ref.shape[0]
    i = pl.program_id(0)
    slot = i % 2

    def row_copy(src_row, s, k, t):
        src = pl.multiple_of(src_row * SUBLANES, SUBLANES)
        dst = pl.multiple_of(t * SUBLANES, SUBLANES)
        return pltpu.make_async_copy(ys_hbm.at[pl.ds(src, SUBLANES), :],
                                     buf_ref.at[s, k, pl.ds(dst, SUBLANES), :], sems.at[s])

    def issue_tile(d_ref, s):
        def issue(t, _):
            for k in range(2):
                row_copy(d_ref[0, k, t], s, k, t).start(priority=k)
            return 0

        lax.fori_loop(0, tm, issue, 0, unroll=DMA_UNROLL)

    @pl.when(i == 0)
    def _():
        issue_tile(dcur_ref, 0)

    @pl.when(i + 1 < pl.num_programs(0))
    def _():
        issue_tile(dnxt_ref, 1 - slot)

    def drain(t, _):
        row_copy(0, slot, 0, 0).wait()
        row_copy(0, slot, 1, 0).wait()
        return 0

    lax.fori_loop(0, tm, drain, 0, unroll=DMA_UNROLL)
    g = gate_ref[...]
    y0 = _load_tile_rows(buf_ref.at[slot, 0], tm)
    y1 = _load_tile_rows(buf_ref.at[slot, 1], tm)
    xn = x_ref[...] + g[:, 0:1] * y0 + g[:, 1:2] * y1
    o_ref[...] = _rms(xn, fn_ref[...]) if final_norm else xn


def _combine(dst_c, x2d, gates_col, fnorm, ys, final_norm):
    T, D = x2d.shape
    nt, _, tm = dst_c.shape
    return pl.pallas_call(
        functools.partial(_combine_kernel, final_norm=final_norm),
        grid=(nt,),
        in_specs=[pl.BlockSpec((1, SUBLANES, tm), lambda i: (i, 0, 0), memory_space=pltpu.SMEM),
                  pl.BlockSpec((1, SUBLANES, tm), lambda i: (jnp.minimum(i + 1, nt - 1), 0, 0),
                               memory_space=pltpu.SMEM),
                  pl.BlockSpec((tm, D), lambda i: (i, 0)),
                  pl.BlockSpec((tm, SUBLANES), lambda i: (i, 0)),
                  pl.BlockSpec((1, D), lambda i: (0, 0)),
                  pl.BlockSpec(memory_space=pl.ANY)],
        out_specs=pl.BlockSpec((tm, D), lambda i: (i, 0)),
        scratch_shapes=[pltpu.VMEM((2, 2, tm * SUBLANES, LANES), F32), pltpu.SemaphoreType.DMA((2,))],
        out_shape=jax.ShapeDtypeStruct((T, D), F32),
        compiler_params=pltpu.CompilerParams(dimension_semantics=("arbitrary",)),
        name="moe_combine",
    )(dst_c, dst_c, x2d, gates_col, fnorm, ys)


def _swap_halves_cols(w):
    half = w.shape[-1] // 2
    return jnp.concatenate([w[..., half:], w[..., :half]], axis=-1)


def _pad_cols(w, width):
    return jnp.pad(w, [(0, 0)] * (w.ndim - 1) + [(0, width - w.shape[-1])])


def _prep_w_in(w):
    kpe0 = _C_CKV + MLA_KV_RANK - 0
    kpe = w[:, kpe0:kpe0 + MLA_ROPE]
    return jnp.concatenate([
        w[:, :kpe0],
        _pad_cols(kpe, LANES), _pad_cols(_swap_halves_cols(kpe), LANES),
        w[:, kpe0 + MLA_ROPE:],
    ], axis=1).astype(BF16)


def _prep_w_uq(w):
    w = w.reshape(MLA_Q_RANK, MLA_HEADS, MLA_NOPE + MLA_ROPE)
    nope, pe = w[..., :MLA_NOPE], w[..., MLA_NOPE:]
    out = jnp.concatenate([nope, _pad_cols(pe, LANES), _pad_cols(_swap_halves_cols(pe), LANES)], axis=-1)
    return out.reshape(MLA_Q_RANK, MLA_HEADS * _Q_HEAD_COLS).astype(BF16)


def _prep_w_ukv(w):
    w = w.reshape(MLA_KV_RANK, MLA_HEADS, MLA_NOPE + MLA_V)
    wukt = jnp.transpose(w[..., :MLA_NOPE], (1, 2, 0)).reshape(MLA_HEADS * MLA_NOPE, MLA_KV_RANK)
    wuv = w[..., MLA_NOPE:].reshape(MLA_KV_RANK, MLA_HEADS * MLA_V)
    return wukt.astype(BF16), wuv.astype(BF16)


def _prep_w_kpe_t(w_ext):
    return w_ext[:, _C_KPE:_C_XP].T


def _rope_tables(seq):
    pos = jnp.arange(seq, dtype=F32)
    inv_freq = 1.0 / (ROPE_THETA ** (jnp.arange(0, MLA_ROPE, 2, dtype=F32) / MLA_ROPE))
    ang = pos[:, None] * inv_freq[None, :]
    cos, sin = jnp.cos(ang), jnp.sin(ang)
    cpad = _pad_cols(jnp.concatenate([cos, cos], axis=-1), LANES)
    spad = _pad_cols(jnp.concatenate([-sin, sin], axis=-1), LANES)
    return cpad, spad


def _block_diag_pool(w):
    G, c, _ = w.shape
    eye = jnp.eye(G, dtype=w.dtype)
    return (eye[:, None, :, None] * w[:, :, None, :]).reshape(G * c, G * c).astype(BF16)


def _moe_layout(counts, n_tiles, tile):
    tiles_per = (counts + tile - 1) // tile
    ends = jnp.cumsum(tiles_per)
    starts = (ends - tiles_per) * tile
    tile_expert = jnp.sum((jnp.arange(n_tiles)[:, None] >= ends[None, :]).astype(jnp.int32), axis=1)
    tile_expert = jnp.minimum(tile_expert, N_EXPERTS - 1)
    zero_rows = jnp.minimum(starts + counts, (n_tiles - 1) * tile)
    zero_info = jnp.concatenate([zero_rows, ends[-1:]])
    return (starts.astype(jnp.int32), tile_expert.astype(jnp.int32), ends[-1:].astype(jnp.int32),
            zero_info.astype(jnp.int32))


def kernel(x, attn_norm, w_in, hgrn_lower_bounds, hgrn_out_norm, mla_q_norm, mla_w_uq, mla_kv_norm,
           mla_w_ukv, pool_w, pool_scale, w_o, ffn_norm, dense_w_gate, dense_w_up, dense_w_down,
           moe_router, moe_w_gate, moe_w_up, moe_w_down, final_norm):
    B, S, D = x.shape
    T = B * S
    depth = w_in.shape[0]
    cpad, spad = _rope_tables(S)
    p_lb = jax.nn.softmax(hgrn_lower_bounds.astype(F32), axis=0)
    lbs = jnp.cumsum(p_lb, axis=0) - p_lb[0:1]

    assert depth % 2 == 0, "the final RMSNorm is fused into the last (MoE) layer's combine kernel"
    for l in range(depth):
        wukt, wuv = _prep_w_ukv(mla_w_ukv[l])
        w_ext = _prep_w_in(w_in[l])
        hg, xp, q, kt, v = _in_proj(
            x, attn_norm[l][None], w_ext, mla_q_norm[l][None], _prep_w_uq(mla_w_uq[l]),
            mla_kv_norm[l][None], wukt, wuv, _prep_w_kpe_t(w_ext), cpad, spad)
        o_a = _hgrn(hg, lbs[l][None], hgrn_out_norm[l][None])
        o_b = _attention(q, kt, v)
        moe_layer = (l % 2 == 1)
        x, h = _mix_out(x, o_a, o_b, xp, _block_diag_pool(pool_w[l]), pool_scale[l][None],
                        w_o[l].astype(BF16), ffn_norm[l][None], moe_layer)
        j = l // 2
        if not moe_layer:
            x = _dense_ffn(x.reshape(T, D), h.reshape(T, D), dense_w_gate[j].astype(BF16),
                           dense_w_up[j].astype(BF16), dense_w_down[j].astype(BF16)).reshape(B, S, D)
        else:
            meta, gates, counts = _router(h, moe_router[j].T)
            n_tiles = (2 * T) // MOE_TILE + N_EXPERTS
            starts, tile_expert, n_used, zero_rows = _moe_layout(counts[:, 0], n_tiles, MOE_TILE)
            dst = _dest_rows(starts, meta)
            xs = _scatter_rows(zero_rows, dst, h, n_tiles * MOE_TILE)
            ys = _expert_ffn(tile_expert, n_used, xs, moe_w_gate[j].astype(BF16), moe_w_up[j].astype(BF16),
                             moe_w_down[j].astype(BF16))
            ct = min(COMBINE_TILE, T)
            dst_c = dst.transpose(1, 0, 2).reshape(SUBLANES, T // ct, ct).transpose(1, 0, 2)
            gates_col = gates.transpose(0, 2, 1).reshape(T, SUBLANES)
            last = (l == depth - 1)
            y = _combine(dst_c, x.reshape(T, D), gates_col, final_norm[None], ys, last)
            x = y.reshape(B, S, D)
    return x
```

```python
import functools
import math

import jax
import jax.numpy as jnp
import numpy as np
from jax import lax
from jax.experimental import pallas as pl
from jax.experimental.pallas import tpu as pltpu

F32 = jnp.float32
BF16 = jnp.bfloat16

HG_HEADS = 4
HG_KEY_DIM = 128
HG_VAL_DIM = 64
HG_KEY_WIDTH = HG_HEADS * HG_KEY_DIM
HG_WIDTH = HG_HEADS * HG_VAL_DIM
MIN_FORGET = 1e-20
MLA_HEADS = 4
MLA_Q_RANK = 256
MLA_KV_RANK = 128
MLA_NOPE = 128
MLA_ROPE = 64
MLA_V = 128
MLA_WIDTH = MLA_HEADS * MLA_V
ROPE_THETA = 10000.0
MASK_VALUE = -1e30
POOL_GROUPS = 4
POOL_WINDOWS = (2, 4, 8, 16)
POOL_WIDTH = 256
POOL_GROUP_DIM = POOL_WIDTH // POOL_GROUPS
N_EXPERTS = 8
EPS = 1e-6

LANES = 128
SUBLANES = 8
QK_PAD = 256
V_EXT = 256

TOKEN_TILE = 1024
IN_PROJ_CHAINS = 1
HGRN_CHUNK = 256
HGRN_CHUNKS_PER_STEP = 4
ATTN_BK = 512
ATTN_HEADS_PER_STEP = 1
ATTN_PAIRS_PER_TRIP = 8
FF_CHUNK = 256
MOE_FF_CHUNK = 256
MOE_TILE = 1024
ROUTE_TILE = 512
COMBINE_TILE = 256
POOL_HALO = 16
DMA_UNROLL = 8

_C_HG = 0
_C_CQ = 2 * HG_KEY_WIDTH + 2 * HG_WIDTH
_C_CKV = _C_CQ + MLA_Q_RANK
_C_KPE = _C_CKV + MLA_KV_RANK
_C_KPES = _C_KPE + LANES
_C_XP = _C_KPES + LANES
_C_END = _C_XP + POOL_WIDTH
_Q_HEAD_COLS = 3 * LANES


def _rms(x, g):
    return x * lax.rsqrt(jnp.mean(x * x, axis=-1, keepdims=True) + EPS) * g


def _dot(a, b):
    return jnp.dot(a, b, preferred_element_type=F32)


def _dot_nt(a, b):
    return lax.dot_general(a, b, (((1,), (1,)), ((), ())), preferred_element_type=F32)


def _dot_tn(a, b):
    return lax.dot_general(a, b, (((0,), (0,)), ((), ())), preferred_element_type=F32)


def _const_spec(shape):
    nd = len(shape)
    return pl.BlockSpec(shape, lambda *_: (0,) * nd)


def _load_tile_rows(ref, n):
    return jnp.concatenate([ref[pl.ds(s, n, stride=SUBLANES), :] for s in range(SUBLANES)], axis=1)


def _store_tile_rows(ref, val):
    n = val.shape[0]
    for s in range(SUBLANES):
        ref[pl.ds(s, n, stride=SUBLANES), :] = val[:, s * LANES:(s + 1) * LANES]


def _in_proj_kernel(x_ref, g_ref, w_ref, qn_ref, wuq_ref, kvn_ref, wukt_ref, wuv_ref, wkpet_ref,
                    cpad_ref, spad_ref, cpadt_ref, spadt_ref,
                    hg_ref, xp_ref, q_ref, kt_ref, v_ref):
    scale = (MLA_NOPE + MLA_ROPE) ** -0.5 * math.log2(math.e)
    tm = x_ref.shape[1]
    rows = tm // IN_PROJ_CHAINS
    ones_col = (lax.broadcasted_iota(jnp.int32, (rows, LANES), 1) == 0).astype(BF16)

    for r0 in range(0, tm, rows):
        rs = slice(r0, r0 + rows)
        h = _rms(x_ref[0, rs, :], g_ref[...]).astype(BF16)
        hg_ref[0, rs, :] = _dot(h, w_ref[:, _C_HG:_C_CQ])
        xp_ref[0, rs, :] = _dot(h, w_ref[:, _C_XP:_C_END])
        cpad = cpad_ref[rs, :]
        spad = spad_ref[rs, :]

        cq = _dot(h, w_ref[:, _C_CQ:_C_CKV])
        cqn = _rms(cq, qn_ref[...]).astype(BF16)
        for hd in range(MLA_HEADS):
            qh = _dot(cqn, wuq_ref[:, hd * _Q_HEAD_COLS:(hd + 1) * _Q_HEAD_COLS])
            q_ref[0, hd, rs, 0:LANES] = (qh[:, 0:LANES] * scale).astype(BF16)
            pe = qh[:, LANES:2 * LANES] * cpad + qh[:, 2 * LANES:3 * LANES] * spad
            q_ref[0, hd, rs, LANES:QK_PAD] = (pe * scale).astype(BF16)

        ckv = _dot(h, w_ref[:, _C_CKV:_C_KPE])
        ckvn = _rms(ckv, kvn_ref[...]).astype(BF16)
        kpet = (_dot_nt(wkpet_ref[0:LANES, :], h) * cpadt_ref[:, rs]
                + _dot_nt(wkpet_ref[LANES:2 * LANES, :], h) * spadt_ref[:, rs]).astype(BF16)
        v_all = _dot(ckvn, wuv_ref[...]).astype(BF16)
        for hd in range(MLA_HEADS):
            kt_ref[0, hd, 0:LANES, rs] = _dot_nt(wukt_ref[hd * LANES:(hd + 1) * LANES, :], ckvn).astype(BF16)
            kt_ref[0, hd, LANES:QK_PAD, rs] = kpet
            v_ref[0, hd, rs, 0:MLA_V] = v_all[:, hd * MLA_V:(hd + 1) * MLA_V]
            v_ref[0, hd, rs, MLA_V:V_EXT] = ones_col


def _in_proj(x, g, w_ext, qn, wuq_ext, kvn, wukt, wuv, wkpet, cpad, spad):
    B, S, D = x.shape
    tm = min(TOKEN_TILE, S)
    grid = (B, S // tm)
    n_hg = _C_CQ
    out_shape = (
        jax.ShapeDtypeStruct((B, S, n_hg), F32),
        jax.ShapeDtypeStruct((B, S, POOL_WIDTH), F32),
        jax.ShapeDtypeStruct((B, MLA_HEADS, S, QK_PAD), BF16),
        jax.ShapeDtypeStruct((B, MLA_HEADS, QK_PAD, S), BF16),
        jax.ShapeDtypeStruct((B, MLA_HEADS, S, V_EXT), BF16),
    )
    return pl.pallas_call(
        _in_proj_kernel,
        grid=grid,
        in_specs=[
            pl.BlockSpec((1, tm, D), lambda b, i: (b, i, 0)),
            _const_spec((1, D)),
            _const_spec(w_ext.shape),
            _const_spec((1, MLA_Q_RANK)),
            _const_spec(wuq_ext.shape),
            _const_spec((1, MLA_KV_RANK)),
            _const_spec(wukt.shape),
            _const_spec(wuv.shape),
            _const_spec(wkpet.shape),
            pl.BlockSpec((tm, LANES), lambda b, i: (i, 0)),
            pl.BlockSpec((tm, LANES), lambda b, i: (i, 0)),
            pl.BlockSpec((LANES, tm), lambda b, i: (0, i)),
            pl.BlockSpec((LANES, tm), lambda b, i: (0, i)),
        ],
        out_specs=(
            pl.BlockSpec((1, tm, n_hg), lambda b, i: (b, i, 0)),
            pl.BlockSpec((1, tm, POOL_WIDTH), lambda b, i: (b, i, 0)),
            pl.BlockSpec((1, MLA_HEADS, tm, QK_PAD), lambda b, i: (b, 0, i, 0)),
            pl.BlockSpec((1, MLA_HEADS, QK_PAD, tm), lambda b, i: (b, 0, 0, i)),
            pl.BlockSpec((1, MLA_HEADS, tm, V_EXT), lambda b, i: (b, 0, i, 0)),
        ),
        out_shape=out_shape,
        compiler_params=pltpu.CompilerParams(dimension_semantics=("arbitrary", "arbitrary")),
        name="in_proj",
    )(x, g, w_ext, qn, wuq_ext, kvn, wukt, wuv, wkpet, cpad, spad, cpad.T, spad.T)


def _split2(x):
    hi = x.astype(BF16)
    return hi, (x - hi.astype(F32)).astype(BF16)


def _hgrn_kernel(q_ref, f_ref, i_ref, g_ref, lb_ref, on_ref, tril_ref, grp_ref, o_ref, st_ref):
    C = tril_ref.shape[0]

    @pl.when(pl.program_id(1) == 0)
    def _():
        st_ref[...] = jnp.zeros_like(st_ref)

    for c0 in range(0, q_ref.shape[1], C):
        rs = slice(c0, c0 + C)
        out = _hgrn_chunk(q_ref[0, rs, :], f_ref[0, rs, :], i_ref[0, rs, :], g_ref[0, rs, :],
                          lb_ref, on_ref, tril_ref, grp_ref, st_ref)
        o_ref[0, rs, :] = out.astype(o_ref.dtype)


def _hgrn_chunk(q_in, z, vv, g_in, lb_ref, on_ref, tril_ref, grp_ref, st_ref):
    C = z.shape[0]
    KW = HG_KEY_WIDTH
    lb = lb_ref[...]
    sig = jax.nn.sigmoid(z)
    forget = lb + (1.0 - lb) * sig
    lg = jnp.log2(jnp.maximum(forget, MIN_FORGET))
    kk = (1.0 - lb) * (1.0 - sig)
    qq = jax.nn.silu(q_in)

    tril = tril_ref[...]
    b = sum(_dot(tril, part) for part in _split2(lg))

    sides, masks = [], []
    row = lax.broadcasted_iota(jnp.int32, (C, C), 0)
    col = lax.broadcasted_iota(jnp.int32, (C, C), 1)
    sub = lax.broadcasted_iota(jnp.int32, (C, KW), 0)
    half = C // 2
    while half >= 4:
        blk = 2 * half
        b3 = b.reshape(C // blk, blk, KW)
        ref_row = jnp.broadcast_to(b3[:, half - 1:half, :], b3.shape).reshape(C, KW)
        e = jnp.exp2(-jnp.abs(b - ref_row))
        x = (jnp.where((sub & half) != 0, qq, kk) * e).astype(BF16)
        sides.append((x, x))
        shift = int(math.log2(blk))
        masks.append(((row >> shift) == (col >> shift)) & ((row & half) != 0) & ((col & half) == 0))
        half //= 2
    b3 = b.reshape(C // 8, 8, KW)
    mid_lo = 0.5 * (b3[:, 0:1, :] + b3[:, 3:4, :])
    mid_hi = 0.5 * (b3[:, 4:5, :] + b3[:, 7:8, :])
    sub8 = lax.broadcasted_iota(jnp.int32, b3.shape, 1)
    mid = jnp.where(sub8 < 4, mid_lo, mid_hi).reshape(C, KW)
    sides.append(((qq * jnp.exp2(b - mid)).astype(BF16), (kk * jnp.exp2(mid - b)).astype(BF16)))
    masks.append(((row >> 2) == (col >> 2)) & (col <= row))

    lane_v = lax.broadcasted_iota(jnp.int32, (C, HG_WIDTH), 1)
    o = _dot_nt((qq * jnp.exp2(b)).astype(BF16), st_ref[...].astype(BF16))
    for hd in range(HG_HEADS):
        ks = slice(hd * HG_KEY_DIM, (hd + 1) * HG_KEY_DIM)
        a = jnp.zeros((C, C), F32)
        for (qt, kt), m in zip(sides, masks):
            a = jnp.where(m, _dot_nt(qt[:, ks], kt[:, ks]), a)
        v_h = jnp.where((lane_v >> 6) == hd, vv, 0.0).astype(BF16)
        o = o + _dot(a.astype(BF16), v_h)

    b_last = b[C - 1:C, :]
    khat = (kk * jnp.exp2(b_last - b)).astype(BF16)
    st_row = lax.broadcasted_iota(jnp.int32, (HG_WIDTH, KW), 0)
    st_col = lax.broadcasted_iota(jnp.int32, (HG_WIDTH, KW), 1)
    new_st = st_ref[...] * jnp.exp2(b_last) + _dot_tn(vv.astype(BF16), khat)
    st_ref[...] = jnp.where((st_row >> 6) == (st_col >> 7), new_st, 0.0)

    grp = grp_ref[...]
    ssq = sum(_dot(part, grp) for part in _split2(o * o))
    on = o * lax.rsqrt(ssq * (1.0 / HG_VAL_DIM) + EPS) * on_ref[...]
    return on * jax.nn.silu(g_in)


def _hgrn(hg, lb, out_norm):
    B, S, _ = hg.shape
    C = min(HGRN_CHUNK, S)
    kb = HG_KEY_WIDTH // HG_KEY_WIDTH
    tril = jnp.asarray(np.tril(np.ones((C, C), np.float32)), BF16)
    lane_group = np.arange(HG_WIDTH) // HG_VAL_DIM
    grp = jnp.asarray((lane_group[:, None] == lane_group[None, :]).astype(np.float32), BF16)
    rows = min(HGRN_CHUNKS_PER_STEP * C, S)
    return pl.pallas_call(
        _hgrn_kernel,
        grid=(B, S // rows),
        in_specs=[
            pl.BlockSpec((1, rows, HG_KEY_WIDTH), lambda b, c: (b, c, 0)),
            pl.BlockSpec((1, rows, HG_KEY_WIDTH), lambda b, c: (b, c, kb)),
            pl.BlockSpec((1, rows, HG_WIDTH), lambda b, c: (b, c, 2 * HG_KEY_WIDTH // HG_WIDTH)),
            pl.BlockSpec((1, rows, HG_WIDTH), lambda b, c: (b, c, 2 * HG_KEY_WIDTH // HG_WIDTH + 1)),
            _const_spec((1, HG_KEY_WIDTH)),
            _const_spec((1, HG_WIDTH)),
            _const_spec(tril.shape),
            _const_spec(grp.shape),
        ],
        out_specs=pl.BlockSpec((1, rows, HG_WIDTH), lambda b, c: (b, c, 0)),
        out_shape=jax.ShapeDtypeStruct((B, S, HG_WIDTH), BF16),
        scratch_shapes=[pltpu.VMEM((HG_WIDTH, HG_KEY_WIDTH), F32)],
        compiler_params=pltpu.CompilerParams(dimension_semantics=("arbitrary", "arbitrary")),
        name="hgrn2",
    )(hg, hg, hg, hg, lb, out_norm, tril, grp)


def _attn_kernel(q_ref, kt_ref, v_ref, o_ref, s0_ref, s1_ref, acc_ref, *, bk):
    bq = q_ref.shape[2]
    nh = q_ref.shape[1]
    i = pl.program_id(2)
    qs = [q_ref[0, hd] for hd in range(nh)]

    def causal(s, key0):
        qry = lax.broadcasted_iota(jnp.int32, (bq, bk), 0)
        key = key0 + lax.broadcasted_iota(jnp.int32, (bq, bk), 1)
        return jnp.where(key <= qry, s, MASK_VALUE)

    def scores(blk, s_ref):
        start = pl.multiple_of(blk * bk, bk)
        for hd in range(nh):
            s_ref[hd] = _dot(qs[hd], kt_ref[0, hd, :, pl.ds(start, bk)])

    def softmax_pv(blk, s_ref, ms, key0=None):
        start = pl.multiple_of(blk * bk, bk)
        out = []
        for hd in range(nh):
            s = s_ref[hd]
            if key0 is not None:
                s = causal(s, key0)
            m_new = jnp.maximum(ms[hd], jnp.max(s, axis=1, keepdims=True))
            p = jnp.exp2(s - m_new).astype(BF16)
            acc_ref[hd] = jnp.exp2(ms[hd] - m_new) * acc_ref[hd] + _dot(p, v_ref[0, hd, pl.ds(start, bk), :])
            out.append(m_new)
        return tuple(out)

    acc_ref[...] = jnp.zeros_like(acc_ref)
    scores(0, s0_ref)

    def pair(t, ms):
        scores(2 * t + 1, s1_ref)
        ms = softmax_pv(2 * t, s0_ref, ms)
        scores(2 * t + 2, s0_ref)
        return softmax_pv(2 * t + 1, s1_ref, ms)

    def pairs(t0, n, c):
        for k in range(n):
            c = pair(t0 + k, c)
        return c

    ms = tuple(jnp.full((bq, 1), -jnp.inf, F32) for _ in range(nh))
    group = ATTN_PAIRS_PER_TRIP
    ms = lax.fori_loop(0, i // group, lambda t, c: pairs(group * t, group, c), ms)
    done = (i // group) * group
    while group > 1:
        group //= 2
        ms = lax.cond((i & group) != 0, functools.partial(pairs, done, group), lambda c: c, ms)
        done = done + (i & group)
    start1 = pl.multiple_of((2 * i + 1) * bk, bk)
    lower = [_dot(qs[hd][bk:, :], kt_ref[0, hd, :, pl.ds(start1, bk)]) for hd in range(nh)]
    ms = softmax_pv(2 * i, s0_ref, ms, 0)
    qry = lax.broadcasted_iota(jnp.int32, (bk, bk), 0)
    key = lax.broadcasted_iota(jnp.int32, (bk, bk), 1)
    for hd in range(nh):
        s = jnp.where(key <= qry, lower[hd], MASK_VALUE)
        m_old = ms[hd][bk:, :]
        m_new = jnp.maximum(m_old, jnp.max(s, axis=1, keepdims=True))
        p = jnp.exp2(s - m_new).astype(BF16)
        acc_ref[hd, bk:, :] = (jnp.exp2(m_old - m_new) * acc_ref[hd, bk:, :]
                               + _dot(p, v_ref[0, hd, pl.ds(start1, bk), :]))
    for hd in range(nh):
        acc = acc_ref[hd]
        o_ref[0, :, hd * MLA_V:(hd + 1) * MLA_V] = (acc[:, 0:MLA_V] / acc[:, MLA_V:MLA_V + 1]).astype(o_ref.dtype)


def _attention(q, kt, v):
    B, H, S, _ = q.shape
    bk = min(ATTN_BK, S // 2)
    bq = 2 * bk
    nh = ATTN_HEADS_PER_STEP
    resident = dict(pipeline_mode=pl.Buffered(1)) if nh > 1 else {}
    return pl.pallas_call(
        functools.partial(_attn_kernel, bk=bk),
        grid=(B, H // nh, S // bq),
        in_specs=[
            pl.BlockSpec((1, nh, bq, QK_PAD), lambda b, h, i: (b, h, i, 0)),
            pl.BlockSpec((1, nh, QK_PAD, S), lambda b, h, i: (b, h, 0, 0), **resident),
            pl.BlockSpec((1, nh, S, V_EXT), lambda b, h, i: (b, h, 0, 0), **resident),
        ],
        out_specs=pl.BlockSpec((1, bq, nh * MLA_V), lambda b, h, i: (b, i, h)),
        out_shape=jax.ShapeDtypeStruct((B, S, H * MLA_V), BF16),
        scratch_shapes=[pltpu.VMEM((nh, bq, bk), F32), pltpu.VMEM((nh, bq, bk), F32),
                        pltpu.VMEM((nh, bq, V_EXT), F32)],
        compiler_params=pltpu.CompilerParams(dimension_semantics=("arbitrary", "arbitrary", "arbitrary")),
        name="mla_attention",
    )(q, kt, v)


def _mix_out_kernel(x_ref, oa_ref, ob_ref, xp_ref, wpool_ref, pscale_ref, wo_ref, fn_ref,
                    xo_ref, h_ref, halo_ref, *, tile_rows):
    tm = xp_ref.shape[1]
    i = pl.program_id(1)

    @pl.when(i == 0)
    def _():
        halo_ref[...] = jnp.zeros_like(halo_ref)

    xp = xp_ref[0]
    xx = jnp.concatenate([halo_ref[...], xp], axis=0)
    halo_ref[...] = xp[tm - POOL_HALO:, :]

    w2 = xx[1:, :] + xx[:-1, :]
    w4 = w2[2:, :] + w2[:-2, :]
    w8 = w4[4:, :] + w4[:-4, :]
    w16 = w8[8:, :] + w8[:-8, :]
    sums = (w2[POOL_HALO - 1:, :], w4[POOL_HALO - 3:, :], w8[POOL_HALO - 7:, :], w16[POOL_HALO - 15:, :])
    t = i * tm + lax.broadcasted_iota(jnp.int32, (tm, POOL_WIDTH), 0)
    lane = lax.broadcasted_iota(jnp.int32, (tm, POOL_WIDTH), 1)
    pooled = jnp.zeros((tm, POOL_WIDTH), F32)
    for gi, w in enumerate(POOL_WINDOWS):
        cnt = jnp.minimum(t + 1, w).astype(F32)
        pooled = jnp.where((lane >> 6) == gi, sums[gi] / cnt, pooled)
    pooled = pooled - xp
    oc = _dot(pooled.astype(BF16), wpool_ref[...]) * pscale_ref[...]

    y = _dot(oa_ref[0], wo_ref[0:HG_WIDTH, :])
    y = y + _dot(ob_ref[0], wo_ref[HG_WIDTH:HG_WIDTH + MLA_WIDTH, :])
    y = y + _dot(oc.astype(BF16), wo_ref[HG_WIDTH + MLA_WIDTH:, :])
    xn = x_ref[0] + y
    xo_ref[0] = xn
    h = _rms(xn, fn_ref[...])
    if tile_rows:
        _store_tile_rows(h_ref, h)
    else:
        h_ref[0] = h.astype(h_ref.dtype)


def _mix_out(x, oa, ob, xp, wpool_bd, pscale, wo, fnorm, tile_rows):
    B, S, D = x.shape
    tm = min(TOKEN_TILE, S)
    nt = S // tm
    tok = lambda w: pl.BlockSpec((1, tm, w), lambda b, i: (b, i, 0))
    if tile_rows:
        assert D == SUBLANES * LANES
        h_spec = pl.BlockSpec((tm * SUBLANES, LANES), lambda b, i: (b * nt + i, 0))
        h_shape = jax.ShapeDtypeStruct((B * S * SUBLANES, LANES), F32)
    else:
        h_spec, h_shape = tok(D), jax.ShapeDtypeStruct((B, S, D), BF16)
    return pl.pallas_call(
        functools.partial(_mix_out_kernel, tile_rows=tile_rows),
        grid=(B, nt),
        in_specs=[tok(D), tok(HG_WIDTH), tok(MLA_WIDTH), tok(POOL_WIDTH),
                  _const_spec(wpool_bd.shape), _const_spec((1, POOL_WIDTH)), _const_spec(wo.shape),
                  _const_spec((1, D))],
        out_specs=(tok(D), h_spec),
        out_shape=(jax.ShapeDtypeStruct((B, S, D), F32), h_shape),
        scratch_shapes=[pltpu.VMEM((POOL_HALO, POOL_WIDTH), F32)],
        compiler_params=pltpu.CompilerParams(dimension_semantics=("arbitrary", "arbitrary")),
        name="mix_out",
    )(x, oa, ob, xp, wpool_bd, pscale, wo, fnorm)


def _dense_ffn_kernel(x_ref, h_ref, wg_ref, wu_ref, wd_ref, o_ref):
    h = h_ref[...]
    acc = x_ref[...]
    dff = wg_ref.shape[1]
    for c0 in range(0, dff, FF_CHUNK):
        g = _dot(h, wg_ref[:, c0:c0 + FF_CHUNK])
        u = _dot(h, wu_ref[:, c0:c0 + FF_CHUNK])
        acc = acc + _dot((jax.nn.silu(g) * u).astype(BF16), wd_ref[c0:c0 + FF_CHUNK, :])
    o_ref[...] = acc


def _dense_ffn(x2d, h2d, wg, wu, wd):
    T, D = x2d.shape
    tm = min(TOKEN_TILE, T)
    tok = pl.BlockSpec((tm, D), lambda i: (i, 0))
    return pl.pallas_call(
        _dense_ffn_kernel,
        grid=(T // tm,),
        in_specs=[tok, tok, _const_spec(wg.shape), _const_spec(wu.shape), _const_spec(wd.shape)],
        out_specs=tok,
        out_shape=jax.ShapeDtypeStruct((T, D), F32),
        compiler_params=pltpu.CompilerParams(dimension_semantics=("arbitrary",)),
        name="dense_ffn",
    )(x2d, h2d, wg, wu, wd)


def _router_kernel(h_ref, rt_ref, meta_ref, gate_ref, cnt_ref, run_ref):
    tm = h_ref.shape[0] // SUBLANES
    E = N_EXPERTS

    @pl.when(pl.program_id(0) == 0)
    def _():
        run_ref[...] = jnp.zeros_like(run_ref)

    logits = lax.dot_general(rt_ref[...], _load_tile_rows(h_ref, tm), (((1,), (1,)), ((), ())),
                             precision=lax.Precision.HIGHEST, preferred_element_type=F32)
    eid = lax.broadcasted_iota(jnp.int32, (E, tm), 0)
    m1 = jnp.max(logits, axis=0, keepdims=True)
    i1 = jnp.min(jnp.where(logits == m1, eid, E), axis=0, keepdims=True)
    rest = jnp.where(eid == i1, -jnp.inf, logits)
    m2 = jnp.max(rest, axis=0, keepdims=True)
    i2 = jnp.min(jnp.where(rest == m2, eid, E), axis=0, keepdims=True)
    e2 = jnp.exp(m2 - m1)
    g1 = 1.0 / (1.0 + e2)
    g2 = e2 / (1.0 + e2)

    sel = ((eid == i1) | (eid == i2))
    r = lax.broadcasted_iota(jnp.int32, (tm, tm), 0)
    c = lax.broadcasted_iota(jnp.int32, (tm, tm), 1)
    before = (r < c).astype(BF16)
    excl = _dot(sel.astype(BF16), before) + run_ref[:, 0:1]
    rank1 = jnp.sum(jnp.where(eid == i1, excl, 0.0), axis=0, keepdims=True).astype(jnp.int32)
    rank2 = jnp.sum(jnp.where(eid == i2, excl, 0.0), axis=0, keepdims=True).astype(jnp.int32)
    run_ref[...] = run_ref[...] + jnp.sum(sel.astype(F32), axis=1, keepdims=True)
    cnt_ref[...] = run_ref[...].astype(jnp.int32)

    zi = jnp.zeros((1, tm), jnp.int32)
    meta_ref[0] = jnp.concatenate([i1, i2, rank1, rank2, zi, zi, zi, zi], axis=0)
    zf = jnp.zeros((1, tm), F32)
    gate_ref[0] = jnp.concatenate([g1, g2, zf, zf, zf, zf, zf, zf], axis=0)


def _router(h_rows, router_t):
    T = h_rows.shape[0] // SUBLANES
    tm = min(ROUTE_TILE, T)
    nt = T // tm
    return pl.pallas_call(
        _router_kernel,
        grid=(nt,),
        in_specs=[pl.BlockSpec((tm * SUBLANES, LANES), lambda i: (i, 0)), _const_spec(router_t.shape)],
        out_specs=(pl.BlockSpec((1, SUBLANES, tm), lambda i: (i, 0, 0)),
                   pl.BlockSpec((1, SUBLANES, tm), lambda i: (i, 0, 0)),
                   _const_spec((N_EXPERTS, LANES))),
        out_shape=(jax.ShapeDtypeStruct((nt, SUBLANES, tm), jnp.int32),
                   jax.ShapeDtypeStruct((nt, SUBLANES, tm), F32),
                   jax.ShapeDtypeStruct((N_EXPERTS, LANES), jnp.int32)),
        scratch_shapes=[pltpu.VMEM((N_EXPERTS, LANES), F32)],
        compiler_params=pltpu.CompilerParams(dimension_semantics=("arbitrary",)),
        name="moe_router",
    )(h_rows, router_t)


def _dest_kernel(start_ref, meta_ref, dst_ref):
    meta = meta_ref[0]
    rows = []
    for k in range(2):
        e = meta[k:k + 1, :]
        base = jnp.zeros_like(e)
        for ex in range(N_EXPERTS):
            base = jnp.where(e == ex, start_ref[ex], base)
        rows.append(base + meta[2 + k:3 + k, :])
    dst_ref[0] = jnp.concatenate(rows + [jnp.zeros_like(rows[0])] * (SUBLANES - 2), axis=0)


def _dest_rows(starts, meta):
    nt, _, tm = meta.shape
    spec = pl.BlockSpec((1, SUBLANES, tm), lambda i, s: (i, 0, 0))
    return pl.pallas_call(
        _dest_kernel,
        grid_spec=pltpu.PrefetchScalarGridSpec(num_scalar_prefetch=1, grid=(nt,), in_specs=[spec], out_specs=spec),
        out_shape=jax.ShapeDtypeStruct(meta.shape, jnp.int32),
        compiler_params=pltpu.CompilerParams(dimension_semantics=("arbitrary",)),
        name="moe_dest",
    )(starts, meta)


def _scatter_kernel(zrow_ref, dst_ref, h_ref, xs_hbm, zero_ref, sem, zsem):
    tm = dst_ref.shape[2]

    @pl.when(pl.program_id(0) == 0)
    def _():
        zero_ref[...] = jnp.zeros_like(zero_ref)
        tile_rows = zero_ref.shape[0]

        def fill(row):
            start = pl.multiple_of(row * SUBLANES, SUBLANES)
            return pltpu.make_async_copy(zero_ref, xs_hbm.at[pl.ds(start, tile_rows), :], zsem)

        fills = [fill(zrow_ref[e]) for e in range(N_EXPERTS)]
        for f in fills:
            f.start()
        for f in fills:
            f.wait()
        n_tiles = xs_hbm.shape[0] // tile_rows
        for j in range(N_EXPERTS):
            @pl.when(n_tiles - 1 - j >= zrow_ref[N_EXPERTS])
            def _():
                f = fill((n_tiles - 1 - j) * (tile_rows // SUBLANES))
                f.start()
                f.wait()

    def row_copy(src_row, dst_row):
        src = pl.multiple_of(src_row * SUBLANES, SUBLANES)
        dst = pl.multiple_of(dst_row * SUBLANES, SUBLANES)
        return pltpu.make_async_copy(h_ref.at[pl.ds(src, SUBLANES), :], xs_hbm.at[pl.ds(dst, SUBLANES), :], sem)

    def issue(t, _):
        for k in range(2):
            row_copy(t, dst_ref[0, k, t]).start(priority=k)
        return 0

    lax.fori_loop(0, tm, issue, 0, unroll=DMA_UNROLL)

    def drain(t, _):
        row_copy(0, 0).wait()
        row_copy(0, 0).wait()
        return 0

    lax.fori_loop(0, tm, drain, 0, unroll=DMA_UNROLL)


def _scatter_rows(zero_rows, dst, h_rows, n_rows):
    nt, _, tm = dst.shape
    grid_spec = pltpu.PrefetchScalarGridSpec(
        num_scalar_prefetch=1,
        grid=(nt,),
        in_specs=[pl.BlockSpec((1, SUBLANES, tm), lambda i, z: (i, 0, 0), memory_space=pltpu.SMEM),
                  pl.BlockSpec((tm * SUBLANES, LANES), lambda i, z: (i, 0))],
        out_specs=pl.BlockSpec(memory_space=pl.ANY),
        scratch_shapes=[pltpu.VMEM((MOE_TILE * SUBLANES, LANES), F32), pltpu.SemaphoreType.DMA(()),
                        pltpu.SemaphoreType.DMA(())],
    )
    return pl.pallas_call(
        _scatter_kernel,
        grid_spec=grid_spec,
        out_shape=jax.ShapeDtypeStruct((n_rows * SUBLANES, LANES), F32),
        compiler_params=pltpu.CompilerParams(dimension_semantics=("arbitrary",), has_side_effects=True),
        name="moe_scatter",
    )(zero_rows, dst, h_rows)


def _expert_kernel(te_ref, nu_ref, x_ref, wg_ref, wu_ref, wd_ref, y_ref, *, fc):
    m = pl.program_id(0)
    tr = x_ref.shape[0] // SUBLANES
    dff = wg_ref.shape[2]

    @pl.when(m < nu_ref[0])
    def _():
        xb = _load_tile_rows(x_ref, tr).astype(BF16)
        acc = jnp.zeros((tr, wd_ref.shape[2]), F32)
        for c0 in range(0, dff, fc):
            g = _dot(xb, wg_ref[0, :, c0:c0 + fc])
            u = _dot(xb, wu_ref[0, :, c0:c0 + fc])
            acc = acc + _dot((jax.nn.silu(g) * u).astype(BF16), wd_ref[0, c0:c0 + fc, :])
        _store_tile_rows(y_ref, acc)

    @pl.when(m >= nu_ref[0])
    def _():
        y_ref[...] = jnp.zeros_like(y_ref)


def _expert_ffn(tile_expert, n_used, xs, wg, wu, wd):
    E, D, F = wg.shape
    R = xs.shape[0] // SUBLANES
    tr = MOE_TILE
    n_tiles = R // tr
    fc = min(MOE_FF_CHUNK, F)

    def row_map(m, te, nu):
        return (jnp.minimum(m, nu[0] - 1), 0)

    def w_map(m, te, nu):
        return (te[jnp.minimum(m, nu[0] - 1)], 0, 0)

    resident = dict(pipeline_mode=pl.Buffered(1))
    grid_spec = pltpu.PrefetchScalarGridSpec(
        num_scalar_prefetch=2,
        grid=(n_tiles,),
        in_specs=[
            pl.BlockSpec((tr * SUBLANES, LANES), row_map),
            pl.BlockSpec((1, D, F), w_map, **resident),
            pl.BlockSpec((1, D, F), w_map, **resident),
            pl.BlockSpec((1, F, D), w_map, **resident),
        ],
        out_specs=pl.BlockSpec((tr * SUBLANES, LANES), lambda m, te, nu: (m, 0)),
    )
    return pl.pallas_call(
        functools.partial(_expert_kernel, fc=fc),
        grid_spec=grid_spec,
        out_shape=jax.ShapeDtypeStruct(xs.shape, F32),
        compiler_params=pltpu.CompilerParams(dimension_semantics=("arbitrary",)),
        name="moe_experts",
    )(tile_expert, n_used, xs, wg, wu, wd)


def _combine_kernel(dcur_ref, dnxt_ref, x_ref, gate_ref, fn_ref, ys_hbm, o_ref, buf_ref, sems, *, final_norm):
    tm = x_ref.shape[0]
    i = pl.program_id(0)
    slot = i % 2

    def row_copy(src_row, s, k, t):
        src = pl.multiple_of(src_row * SUBLANES, SUBLANES)
        dst = pl.multiple_of(t * SUBLANES, SUBLANES)
        return pltpu.make_async_copy(ys_hbm.at[pl.ds(src, SUBLANES), :],
                                     buf_ref.at[s, k, pl.ds(dst, SUBLANES), :], sems.at[s])

    def issue_tile(d_ref, s):
        def issue(t, _):
            for k in range(2):
                row_copy(d_ref[0, k, t], s, k, t).start(priority=k)
            return 0

        lax.fori_loop(0, tm, issue, 0, unroll=DMA_UNROLL)

    @pl.when(i == 0)
    def _():
        issue_tile(dcur_ref, 0)

    @pl.when(i + 1 < pl.num_programs(0))
    def _():
        issue_tile(dnxt_ref, 1 - slot)

    def drain(t, _):
        row_copy(0, slot, 0, 0).wait()
        row_copy(0, slot, 1, 0).wait()
        return 0

    lax.fori_loop(0, tm, drain, 0, unroll=DMA_UNROLL)
    g = gate_ref[...]
    y0 = _load_tile_rows(buf_ref.at[slot, 0], tm)
    y1 = _load_tile_rows(buf_ref.at[slot, 1], tm)
    xn = x_ref[...] + g[:, 0:1] * y0 + g[:, 1:2] * y1
    o_ref[...] = _rms(xn, fn_ref[...]) if final_norm else xn


def _combine(dst_c, x2d, gates_col, fnorm, ys, final_norm):
    T, D = x2d.shape
    nt, _, tm = dst_c.shape
    return pl.pallas_call(
        functools.partial(_combine_kernel, final_norm=final_norm),
        grid=(nt,),
        in_specs=[pl.BlockSpec((1, SUBLANES, tm), lambda i: (i, 0, 0), memory_space=pltpu.SMEM),
                  pl.BlockSpec((1, SUBLANES, tm), lambda i: (jnp.minimum(i + 1, nt - 1), 0, 0),
                               memory_space=pltpu.SMEM),
                  pl.BlockSpec((tm, D), lambda i: (i, 0)),
                  pl.BlockSpec((tm, SUBLANES), lambda i: (i, 0)),
                  pl.BlockSpec((1, D), lambda i: (0, 0)),
                  pl.BlockSpec(memory_space=pl.ANY)],
        out_specs=pl.BlockSpec((tm, D), lambda i: (i, 0)),
        scratch_shapes=[pltpu.VMEM((2, 2, tm * SUBLANES, LANES), F32), pltpu.SemaphoreType.DMA((2,))],
        out_shape=jax.ShapeDtypeStruct((T, D), F32),
        compiler_params=pltpu.CompilerParams(dimension_semantics=("arbitrary",)),
        name="moe_combine",
    )(dst_c, dst_c, x2d, gates_col, fnorm, ys)


def _swap_halves_cols(w):
    half = w.shape[-1] // 2
    return jnp.concatenate([w[..., half:], w[..., :half]], axis=-1)


def _pad_cols(w, width):
    return jnp.pad(w, [(0, 0)] * (w.ndim - 1) + [(0, width - w.shape[-1])])


def _prep_w_in(w):
    kpe0 = _C_CKV + MLA_KV_RANK - 0
    kpe = w[:, kpe0:kpe0 + MLA_ROPE]
    return jnp.concatenate([
        w[:, :kpe0],
        _pad_cols(kpe, LANES), _pad_cols(_swap_halves_cols(kpe), LANES),
        w[:, kpe0 + MLA_ROPE:],
    ], axis=1).astype(BF16)


def _prep_w_uq(w):
    w = w.reshape(MLA_Q_RANK, MLA_HEADS, MLA_NOPE + MLA_ROPE)
    nope, pe = w[..., :MLA_NOPE], w[..., MLA_NOPE:]
    out = jnp.concatenate([nope, _pad_cols(pe, LANES), _pad_cols(_swap_halves_cols(pe), LANES)], axis=-1)
    return out.reshape(MLA_Q_RANK, MLA_HEADS * _Q_HEAD_COLS).astype(BF16)


def _prep_w_ukv(w):
    w = w.reshape(MLA_KV_RANK, MLA_HEADS, MLA_NOPE + MLA_V)
    wukt = jnp.transpose(w[..., :MLA_NOPE], (1, 2, 0)).reshape(MLA_HEADS * MLA_NOPE, MLA_KV_RANK)
    wuv = w[..., MLA_NOPE:].reshape(MLA_KV_RANK, MLA_HEADS * MLA_V)
    return wukt.astype(BF16), wuv.astype(BF16)


def _prep_w_kpe_t(w_ext):
    return w_ext[:, _C_KPE:_C_XP].T


def _rope_tables(seq):
    pos = jnp.arange(seq, dtype=F32)
    inv_freq = 1.0 / (ROPE_THETA ** (jnp.arange(0, MLA_ROPE, 2, dtype=F32) / MLA_ROPE))
    ang = pos[:, None] * inv_freq[None, :]
    cos, sin = jnp.cos(ang), jnp.sin(ang)
    cpad = _pad_cols(jnp.concatenate([cos, cos], axis=-1), LANES)
    spad = _pad_cols(jnp.concatenate([-sin, sin], axis=-1), LANES)
    return cpad, spad


def _block_diag_pool(w):
    G, c, _ = w.shape
    eye = jnp.eye(G, dtype=w.dtype)
    return (eye[:, None, :, None] * w[:, :, None, :]).reshape(G * c, G * c).astype(BF16)


def _moe_layout(counts, n_tiles, tile):
    tiles_per = (counts + tile - 1) // tile
    ends = jnp.cumsum(tiles_per)
    starts = (ends - tiles_per) * tile
    tile_expert = jnp.sum((jnp.arange(n_tiles)[:, None] >= ends[None, :]).astype(jnp.int32), axis=1)
    tile_expert = jnp.minimum(tile_expert, N_EXPERTS - 1)
    zero_rows = jnp.minimum(starts + counts, (n_tiles - 1) * tile)
    zero_info = jnp.concatenate([zero_rows, ends[-1:]])
    return (starts.astype(jnp.int32), tile_expert.astype(jnp.int32), ends[-1:].astype(jnp.int32),
            zero_info.astype(jnp.int32))


def kernel(x, attn_norm, w_in, hgrn_lower_bounds, hgrn_out_norm, mla_q_norm, mla_w_uq, mla_kv_norm,
           mla_w_ukv, pool_w, pool_scale, w_o, ffn_norm, dense_w_gate, dense_w_up, dense_w_down,
           moe_router, moe_w_gate, moe_w_up, moe_w_down, final_norm):
    B, S, D = x.shape
    T = B * S
    depth = w_in.shape[0]
    cpad, spad = _rope_tables(S)
    p_lb = jax.nn.softmax(hgrn_lower_bounds.astype(F32), axis=0)
    lbs = jnp.cumsum(p_lb, axis=0) - p_lb[0:1]

    assert depth % 2 == 0, "the final RMSNorm is fused into the last (MoE) layer's combine kernel"
    for l in range(depth):
        wukt, wuv = _prep_w_ukv(mla_w_ukv[l])
        w_ext = _prep_w_in(w_in[l])
        hg, xp, q, kt, v = _in_proj(
            x, attn_norm[l][None], w_ext, mla_q_norm[l][None], _prep_w_uq(mla_w_uq[l]),
            mla_kv_norm[l][None], wukt, wuv, _prep_w_kpe_t(w_ext), cpad, spad)
        o_a = _hgrn(hg, lbs[l][None], hgrn_out_norm[l][None])
        o_b = _attention(q, kt, v)
        moe_layer = (l % 2 == 1)
        x, h = _mix_out(x, o_a, o_b, xp, _block_diag_pool(pool_w[l]), pool_scale[l][None],
                        w_o[l].astype(BF16), ffn_norm[l][None], moe_layer)
        j = l // 2
        if not moe_layer:
            x = _dense_ffn(x.reshape(T, D), h.reshape(T, D), dense_w_gate[j].astype(BF16),
                           dense_w_up[j].astype(BF16), dense_w_down[j].astype(BF16)).reshape(B, S, D)
        else:
            meta, gates, counts = _router(h, moe_router[j].T)
            n_tiles = (2 * T) // MOE_TILE + N_EXPERTS
            starts, tile_expert, n_used, zero_rows = _moe_layout(counts[:, 0], n_tiles, MOE_TILE)
            dst = _dest_rows(starts, meta)
            xs = _scatter_rows(zero_rows, dst, h, n_tiles * MOE_TILE)
            ys = _expert_ffn(tile_expert, n_used, xs, moe_w_gate[j].astype(BF16), moe_w_up[j].astype(BF16),
                             moe_w_down[j].astype(BF16))
            ct = min(COMBINE_TILE, T)
            dst_c = dst.transpose(1, 0, 2).reshape(SUBLANES, T // ct, ct).transpose(1, 0, 2)
            gates_col = gates.transpose(0, 2, 1).reshape(T, SUBLANES)
            last = (l == depth - 1)
            y = _combine(dst_c, x.reshape(T, D), gates_col, final_norm[None], ys, last)
            x = y.reshape(B, S, D)
    return x
```
